```python
import math
import jax
import jax.numpy as jnp
from jax import lax
import numpy as np


D_MODEL = 1024
BATCH = 8
SEQ = 2048
DEPTH = 2

F32 = jnp.float32
GRID_W = 64
CTX_LEN = 256
Q_BLOCK = 128
ROPE_THETA = 10000.0
LN_EPS = 1e-5
DEEPNORM_ALPHA = (2 * DEPTH) ** 0.25
DEEPNORM_BETA = (8 * DEPTH) ** -0.25
N_EVEN = (DEPTH + 1) // 2
N_ODD = DEPTH // 2
MIX_WIDTH = D_MODEL

A_HEAD_DIM = 64
A_HEADS = (MIX_WIDTH // 2) // A_HEAD_DIM
A_WIDTH = A_HEADS * A_HEAD_DIM
A_DECAY_LORA = 64
A_ICLR_LORA = 64
A_GATE_LORA = 128
A_GN_EPS = 64e-5
A_IN = 3 * A_WIDTH + 2 * A_DECAY_LORA + 2 * A_ICLR_LORA + A_GATE_LORA

B_HEAD_DIM = 64
B_V_DIM = 2 * B_HEAD_DIM
B_HEADS = (MIX_WIDTH - A_WIDTH) // B_V_DIM
B_WIDTH = B_HEADS * B_V_DIM
B_QK = B_HEADS * 2 * B_HEAD_DIM
B_IN = 2 * B_QK + B_WIDTH
B_SUBLN_EPS = 1e-5
EVEN_IN = A_IN + B_IN

C_HEAD_DIM = 128
C_HEADS = MIX_WIDTH // C_HEAD_DIM
C_KV_HEADS = 2
C_GROUP = C_HEADS // C_KV_HEADS
C_Q = C_HEADS * C_HEAD_DIM
C_KV = C_KV_HEADS * C_HEAD_DIM
ODD_IN = C_Q + 2 * C_KV
QK_NORM_EPS = 1e-6

N_EXPERTS = 16
N_GROUPS = 4
EXPERTS_PER_GROUP = N_EXPERTS // N_GROUPS
TOP_K = 2
EXPERT_FF = 512

kernel_name = 'hybrid_rwkv7_diffattn_gqa_grouped_moe'


def _layer_norm(x, g, b):
    xf = x.astype(F32)
    mu = jnp.mean(xf, -1, keepdims=True)
    var = jnp.mean(jnp.square(xf - mu), -1, keepdims=True)
    return ((xf - mu) * lax.rsqrt(var + LN_EPS) * g + b).astype(x.dtype)


def _rms_norm(x, g, eps):
    xf = x.astype(F32)
    y = xf * lax.rsqrt(jnp.mean(jnp.square(xf), -1, keepdims=True) + eps)
    return (y * g).astype(x.dtype)


def _modulate(x, shift, scale):
    return x * (1.0 + scale) + shift


def _rope_tables(row_pos, col_pos, head_dim):
    axis_dim = head_dim // 2
    inv = ROPE_THETA ** (-jnp.arange(0, axis_dim, 2, dtype=F32) / axis_dim)
    ang = jnp.concatenate([row_pos[:, None] * inv, col_pos[:, None] * inv], -1)
    return jnp.cos(ang), jnp.sin(ang)


def _apply_rope(x, cos, sin):
    half = x.shape[-1] // 2
    shape = (1, x.shape[1]) + (1,) * (x.ndim - 3) + (half,)
    cs, sn = cos.reshape(shape), sin.reshape(shape)
    xf = x.astype(F32)
    x1, x2 = xf[..., :half], xf[..., half:]
    return jnp.concatenate([x1 * cs - x2 * sn, x1 * sn + x2 * cs], -1).astype(x.dtype)


def _sweep_query_blocks(fn, q):
    bsz, s = q.shape[:2]
    nb = s // Q_BLOCK
    qb = jnp.moveaxis(q.reshape((bsz, nb, Q_BLOCK) + q.shape[2:]), 1, 0)
    out = jnp.moveaxis(lax.map(fn, qb), 0, 1)
    return out.reshape((bsz, s) + out.shape[3:])


def _centred_shift(p):
    zero = jnp.zeros_like(p[:, :1])
    prev = jnp.concatenate([zero, p[:, :-1]], axis=1)
    nxt = jnp.concatenate([p[:, 1:], zero], axis=1)
    return 0.5 * (prev + nxt)


def _heads_a(z):
    return z.reshape(z.shape[:-1] + (A_HEADS, A_HEAD_DIM))


def _rwkv_features(pa, mu, w0, w2, a0, a2, g2, k_k, k_a):
    bsz, t, _ = pa.shape
    u = pa + (_centred_shift(pa) - pa) * mu
    o1, o2, o3 = A_WIDTH, 2 * A_WIDTH, 3 * A_WIDTH
    o4 = o3 + 2 * A_DECAY_LORA
    o5 = o4 + 2 * A_ICLR_LORA
    r, k, v = u[..., :o1], u[..., o1:o2], u[..., o2:o3]
    wd = u[..., o3:o4].reshape(bsz, t, 2, A_DECAY_LORA)
    ad = u[..., o4:o5].reshape(bsz, t, 2, A_ICLR_LORA)
    gd = u[..., o5:]
    w_log = -jax.nn.softplus(-(w0 + jnp.einsum('btdl,dlc->btdc', jnp.tanh(wd), w2))) - 0.5
    decay = jnp.exp(-jnp.exp(w_log.astype(F32)))
    a = jax.nn.sigmoid(a0 + jnp.einsum('btdl,dlc->btdc', ad, a2))
    g = jax.nn.sigmoid(gd) @ g2
    kk = _heads_a(k * k_k).astype(F32)
    kk = kk / jnp.maximum(jnp.sqrt(jnp.sum(jnp.square(kk), -1, keepdims=True)), 1e-12)
    k_dir = k[:, :, None, :] * (1.0 + (a - 1.0) * k_a)
    return _heads_a(r), _heads_a(k_dir), _heads_a(v), _heads_a(decay), kk, _heads_a(a), g


def _wkv7_scan(feats, d, s0, reverse):
    r, k_dir, v, decay, kk, a, _ = feats
    tm = lambda z: jnp.moveaxis(z.astype(F32), 1, 0)
    xs = (tm(r), tm(decay[:, :, d]), tm(k_dir[:, :, d]), tm(v), tm(-kk), tm(kk * a[:, :, d]))

    def step(s, inp):
        r_t, w_t, k_t, v_t, a_t, b_t = inp
        sa = jnp.einsum('bhvk,bhk->bhv', s, a_t)
        s = s * w_t[:, :, None, :] + sa[..., :, None] * b_t[..., None, :] + v_t[..., :, None] * k_t[..., None, :]
        return s, jnp.einsum('bhvk,bhk->bhv', s, r_t)

    s_final, ys = lax.scan(step, s0, xs, reverse=reverse)
    return jnp.moveaxis(ys, 0, 1), s_final


def _rwkv_readout(y, feats, r_k, lnx_g, lnx_b):
    r, k_dir, v, _, _, _, g = feats
    bsz, t = y.shape[:2]
    mu = jnp.mean(y, -1, keepdims=True)
    var = jnp.mean(jnp.square(y - mu), -1, keepdims=True)
    yn = ((y - mu) * lax.rsqrt(var + A_GN_EPS)).reshape(bsz, t, A_WIDTH) * lnx_g + lnx_b
    bonus = jnp.sum(r * (k_dir[:, :, 0] + k_dir[:, :, 1]) * r_k, -1, keepdims=True) * v
    return ((yn + bonus.reshape(bsz, t, A_WIDTH)) * g).astype(g.dtype)


def _rwkv7_group(pa_c, pa_l, mu, w0, w2, a0, a2, g2, k_k, k_a, r_k, lnx_g, lnx_b):
    feats_c = _rwkv_features(pa_c, mu, w0, w2, a0, a2, g2, k_k, k_a)
    feats_l = _rwkv_features(pa_l, mu, w0, w2, a0, a2, g2, k_k, k_a)
    s0 = jnp.zeros((pa_l.shape[0], A_HEADS, A_HEAD_DIM, A_HEAD_DIM), F32)
    ys_c, ys_l = [], []
    for d, rev in enumerate((False, True)):
        y_c, s_ctx = _wkv7_scan(feats_c, d, s0, rev)
        y_l, _ = _wkv7_scan(feats_l, d, s_ctx, rev)
        ys_c.append(y_c)
        ys_l.append(y_l)
    out_c = _rwkv_readout(ys_c[0] + ys_c[1], feats_c, r_k, lnx_g, lnx_b)
    out_l = _rwkv_readout(ys_l[0] + ys_l[1], feats_l, r_k, lnx_g, lnx_b)
    return out_c, out_l


def _diff_core(q, k, v, lam):
    s = jnp.einsum('bqhmd,bkhmd->bhmqk', q, k).astype(F32) * (B_HEAD_DIM ** -0.5)
    p = jax.nn.softmax(s, axis=-1)
    w = p[:, :, 0] - lam * p[:, :, 1]
    return jnp.einsum('bhqk,bkhe->bqhe', w.astype(v.dtype), v)


def _diff_attn_group(pb_c, pb_l, lam_vecs, subln_g, lambda_init, cos, sin):
    def split(p):
        bsz, t, _ = p.shape
        q = p[..., :B_QK].reshape(bsz, t, B_HEADS, 2, B_HEAD_DIM)
        k = p[..., B_QK:2 * B_QK].reshape(bsz, t, B_HEADS, 2, B_HEAD_DIM)
        v = p[..., 2 * B_QK:].reshape(bsz, t, B_HEADS, B_V_DIM)
        return q, k, v

    q_c, k_c, v_c = split(pb_c)
    q_l, k_l, v_l = split(pb_l)
    q_l = _apply_rope(q_l, cos, sin)
    k_l = _apply_rope(k_l, cos, sin)
    lv = lam_vecs.astype(F32)
    lam = jnp.exp(jnp.sum(lv[0] * lv[1])) - jnp.exp(jnp.sum(lv[2] * lv[3])) + lambda_init
    k_all = jnp.concatenate([k_c, k_l], axis=1)
    v_all = jnp.concatenate([v_c, v_l], axis=1)
    o_l = _sweep_query_blocks(lambda qb: _diff_core(qb, k_all, v_all, lam), q_l)
    o_c = _diff_core(q_c, k_c, v_c, lam)

    def post(o):
        bsz, t = o.shape[:2]
        return (_rms_norm(o, subln_g, B_SUBLN_EPS) * (1.0 - lambda_init)).reshape(bsz, t, B_WIDTH)

    return post(o_c), post(o_l)


def _even_mixer(hc, hl, w_in, w_out, a_mu, a_w0, a_w2, a_a0, a_a2, a_g2, a_kk, a_ka, a_rk,
                a_lnx_g, a_lnx_b, b_lam, b_subln_g, lambda_init, cos, sin):
    pc = hc @ w_in
    pl = hl @ w_in
    ya_c, ya_l = _rwkv7_group(pc[..., :A_IN], pl[..., :A_IN], a_mu, a_w0, a_w2, a_a0, a_a2, a_g2,
                              a_kk, a_ka, a_rk, a_lnx_g, a_lnx_b)
    yb_c, yb_l = _diff_attn_group(pc[..., A_IN:], pl[..., A_IN:], b_lam, b_subln_g, lambda_init, cos, sin)
    out_c = jnp.concatenate([ya_c, yb_c], -1) @ w_out
    out_l = jnp.concatenate([ya_l, yb_l], -1) @ w_out
    return out_c, out_l


def _gqa_core(q, k, v):
    s = jnp.einsum('bqhgd,bkhd->bhgqk', q, k).astype(F32) * (C_HEAD_DIM ** -0.5)
    p = jax.nn.softmax(s, axis=-1).astype(v.dtype)
    return jnp.einsum('bhgqk,bkhd->bqhgd', p, v)


def _odd_mixer(hc, hl, w_in, w_out, qn_g, kn_g, cos, sin, need_ctx):
    def proj(h, with_q):
        bsz, t, _ = h.shape
        p = h @ (w_in if with_q else w_in[:, C_Q:])
        off = C_Q if with_q else 0
        q = None
        if with_q:
            q = _rms_norm(p[..., :C_Q].reshape(bsz, t, C_KV_HEADS, C_GROUP, C_HEAD_DIM), qn_g, QK_NORM_EPS)
        k = _rms_norm(p[..., off:off + C_KV].reshape(bsz, t, C_KV_HEADS, C_HEAD_DIM), kn_g, QK_NORM_EPS)
        v = p[..., off + C_KV:].reshape(bsz, t, C_KV_HEADS, C_HEAD_DIM)
        return q, k, v

    q_l, k_l, v_l = proj(hl, True)
    q_l = _apply_rope(q_l, cos, sin)
    k_l = _apply_rope(k_l, cos, sin)
    q_c, k_c, v_c = proj(hc, need_ctx)
    k_all = jnp.concatenate([k_c, k_l], axis=1)
    v_all = jnp.concatenate([v_c, v_l], axis=1)
    bsz, s = hl.shape[:2]
    o_l = _sweep_query_blocks(lambda qb: _gqa_core(qb, k_all, v_all), q_l)
    out_l = o_l.reshape(bsz, s, C_Q) @ w_out
    out_c = None
    if need_ctx:
        out_c = _gqa_core(q_c, k_c, v_c).reshape(hc.shape[0], hc.shape[1], C_Q) @ w_out
    return out_c, out_l


def _moe(h, router_w, router_b, w1, w3, w2):
    t = h.shape[0]
    logits = (h @ router_w).astype(F32) + router_b.astype(F32)
    probs = jax.nn.softmax(logits, axis=-1)
    gscore = jnp.sum(lax.top_k(probs.reshape(t, N_GROUPS, EXPERTS_PER_GROUP), TOP_K)[0], -1)
    best = jnp.argmax(gscore, axis=-1)
    in_grp = (jnp.arange(N_EXPERTS) // EXPERTS_PER_GROUP)[None, :] == best[:, None]
    top_p, top_i = lax.top_k(jnp.where(in_grp, probs, -jnp.inf), TOP_K)
    top_w = top_p / jnp.sum(top_p, -1, keepdims=True)
    gates = jnp.sum(jax.nn.one_hot(top_i, N_EXPERTS, dtype=F32) * top_w[..., None], axis=1).astype(h.dtype)
    out = jnp.zeros_like(h)
    for e in range(N_EXPERTS):
        hid = jax.nn.silu(h @ w1[e]) * (h @ w3[e])
        out = out + gates[:, e:e + 1] * (hid @ w2[e])
    return out


def setup_inputs(seed: int = 0) -> dict:
    key = jax.random.key(seed)
    ks = iter(jax.random.split(key, 48))

    def nrm(shape, scale):
        return jax.random.normal(next(ks), shape, F32) * scale

    def unif(shape, lo, hi):
        return jax.random.uniform(next(ks), shape, F32, lo, hi)

    d = D_MODEL
    ev_out = A_WIDTH + B_WIDTH
    return {
        'x': nrm((BATCH, SEQ, d), 1.0),
        'c': nrm((BATCH, d), 1.0),
        'ctx': nrm((BATCH, CTX_LEN, d), 1.0),
        'c_ctx': nrm((d,), 1.0),
        'router_w': nrm((d, N_EXPERTS), d ** -0.5),
        'router_b': nrm((N_EXPERTS,), 0.01),
        'ada_w': nrm((DEPTH, d, 6 * d), 0.5 * d ** -0.5),
        'ada_b': nrm((DEPTH, 6 * d), 0.01),
        'ln1_g': 1.0 + nrm((DEPTH, d), 0.02),
        'ln1_b': nrm((DEPTH, d), 0.02),
        'ln2_g': 1.0 + nrm((DEPTH, d), 0.02),
        'ln2_b': nrm((DEPTH, d), 0.02),
        'moe_w1': nrm((DEPTH, N_EXPERTS, d, EXPERT_FF), d ** -0.5),
        'moe_w3': nrm((DEPTH, N_EXPERTS, d, EXPERT_FF), d ** -0.5),
        'moe_w2': nrm((DEPTH, N_EXPERTS, EXPERT_FF, d), DEEPNORM_BETA * EXPERT_FF ** -0.5),
        'ev_w_in': nrm((N_EVEN, d, EVEN_IN), d ** -0.5),
        'ev_w_out': nrm((N_EVEN, ev_out, d), DEEPNORM_BETA * ev_out ** -0.5),
        'ev_a_mu': unif((N_EVEN, A_IN), 0.0, 1.0),
        'ev_a_w0': unif((N_EVEN, 2, A_WIDTH), -6.0, -1.0),
        'ev_a_w2': nrm((N_EVEN, 2, A_DECAY_LORA, A_WIDTH), 0.5 * A_DECAY_LORA ** -0.5),
        'ev_a_a0': nrm((N_EVEN, 2, A_WIDTH), 0.1),
        'ev_a_a2': nrm((N_EVEN, 2, A_ICLR_LORA, A_WIDTH), 0.5 * A_ICLR_LORA ** -0.5),
        'ev_a_g2': nrm((N_EVEN, A_GATE_LORA, A_WIDTH), A_GATE_LORA ** -0.5),
        'ev_a_kk': 0.85 + nrm((N_EVEN, A_WIDTH), 0.05),
        'ev_a_ka': 1.0 + nrm((N_EVEN, A_WIDTH), 0.05),
        'ev_a_rk': nrm((N_EVEN, A_HEADS, A_HEAD_DIM), 0.1),
        'ev_a_lnx_g': 1.0 + nrm((N_EVEN, A_WIDTH), 0.02),
        'ev_a_lnx_b': nrm((N_EVEN, A_WIDTH), 0.02),
        'ev_b_lam': nrm((N_EVEN, 4, B_HEAD_DIM), 0.1),
        'ev_b_subln_g': 1.0 + nrm((N_EVEN, B_V_DIM), 0.02),
        'od_w_in': nrm((N_ODD, d, ODD_IN), d ** -0.5),
        'od_w_out': nrm((N_ODD, C_Q, d), DEEPNORM_BETA * C_Q ** -0.5),
        'od_qn_g': 1.0 + nrm((N_ODD, C_HEAD_DIM), 0.02),
        'od_kn_g': 1.0 + nrm((N_ODD, C_HEAD_DIM), 0.02),
    }


def reference(x, c, ctx, c_ctx, router_w, router_b, ada_w, ada_b, ln1_g, ln1_b, ln2_g, ln2_b,
              moe_w1, moe_w3, moe_w2, ev_w_in, ev_w_out, ev_a_mu, ev_a_w0, ev_a_w2, ev_a_a0, ev_a_a2,
              ev_a_g2, ev_a_kk, ev_a_ka, ev_a_rk, ev_a_lnx_g, ev_a_lnx_b, ev_b_lam, ev_b_subln_g,
              od_w_in, od_w_out, od_qn_g, od_kn_g):
    bsz, n_lat, d = x.shape
    rows = n_lat // GRID_W
    rr, cc = jnp.meshgrid(jnp.arange(rows), jnp.arange(GRID_W), indexing='ij')
    row_pos = rr.reshape(-1).astype(F32)
    col_pos = cc.reshape(-1).astype(F32)
    cos_b, sin_b = _rope_tables(row_pos, col_pos, B_HEAD_DIM)
    cos_c, sin_c = _rope_tables(row_pos, col_pos, C_HEAD_DIM)
    s_lat = jax.nn.silu(c)
    s_ctx = jax.nn.silu(c_ctx)
    xl, xc = x, ctx
    for i in range(DEPTH):
        last = i == DEPTH - 1
        j = i // 2
        mod_l = jnp.split((s_lat @ ada_w[i] + ada_b[i])[:, None, :], 6, axis=-1)
        mod_c = jnp.split(s_ctx @ ada_w[i] + ada_b[i], 6, axis=-1)
        hl = _modulate(xl, mod_l[0], mod_l[1])
        hc = _modulate(xc, mod_c[0], mod_c[1])
        if i % 2 == 0:
            lambda_init = 0.8 - 0.6 * math.exp(-0.3 * i)
            mix_c, mix_l = _even_mixer(hc, hl, ev_w_in[j], ev_w_out[j], ev_a_mu[j], ev_a_w0[j], ev_a_w2[j],
                                       ev_a_a0[j], ev_a_a2[j], ev_a_g2[j], ev_a_kk[j], ev_a_ka[j], ev_a_rk[j],
                                       ev_a_lnx_g[j], ev_a_lnx_b[j], ev_b_lam[j], ev_b_subln_g[j],
                                       lambda_init, cos_b, sin_b)
        else:
            mix_c, mix_l = _odd_mixer(hc, hl, od_w_in[j], od_w_out[j], od_qn_g[j], od_kn_g[j],
                                      cos_c, sin_c, not last)
        xl = _layer_norm(DEEPNORM_ALPHA * xl + mod_l[2] * mix_l, ln1_g[i], ln1_b[i])
        hl = _modulate(xl, mod_l[3], mod_l[4])
        if last:
            ffn_l = _moe(hl.reshape(-1, d), router_w, router_b, moe_w1[i], moe_w3[i], moe_w2[i]).reshape(hl.shape)
        else:
            xc = _layer_norm(DEEPNORM_ALPHA * xc + mod_c[2] * mix_c, ln1_g[i], ln1_b[i])
            hc = _modulate(xc, mod_c[3], mod_c[4])
            n_c = hc.shape[0] * hc.shape[1]
            ffn = _moe(jnp.concatenate([hc.reshape(-1, d), hl.reshape(-1, d)], axis=0),
                       router_w, router_b, moe_w1[i], moe_w3[i], moe_w2[i])
            ffn_c = ffn[:n_c].reshape(hc.shape)
            ffn_l = ffn[n_c:].reshape(hl.shape)
            xc = _layer_norm(DEEPNORM_ALPHA * xc + mod_c[5] * ffn_c, ln2_g[i], ln2_b[i])
        xl = _layer_norm(DEEPNORM_ALPHA * xl + mod_l[5] * ffn_l, ln2_g[i], ln2_b[i])
    return xl
```

```python
import functools
import math

import jax
import jax.numpy as jnp
from jax import lax
from jax.experimental import pallas as pl
from jax.experimental.pallas import tpu as pltpu

F32 = jnp.float32
BF16 = jnp.bfloat16

D_MODEL = 1024
DEPTH = 2
GRID_W = 64
ROPE_THETA = 10000.0
LN_EPS = 1e-5
DEEPNORM_ALPHA = (2 * DEPTH) ** 0.25

A_HEAD_DIM = 64
A_HEADS = 8
A_WIDTH = 512
A_LORA = 64
A_GATE_LORA = 128
A_GN_EPS = 64e-5
A_IN = 3 * A_WIDTH + 4 * A_LORA + A_GATE_LORA

B_HEAD_DIM = 64
B_V_DIM = 128
B_HEADS = 4
B_WIDTH = 512
B_QK = 512
B_SUBLN_EPS = 1e-5
EVEN_IN = A_IN + 2 * B_QK + B_WIDTH

C_HEAD_DIM = 128
C_HEADS = 8
C_KV_HEADS = 2
C_GROUP = 4
C_Q = 1024
C_KV = 256
ODD_IN = C_Q + 2 * C_KV
QK_NORM_EPS = 1e-6

N_EXPERTS = 16
N_GROUPS = 4
EXPERTS_PER_GROUP = 4
EXPERT_FF = 512

VMEM_LIMIT_BYTES = 56 * 1024 * 1024
LANES = 128
SUBLANES = 8


def _cparams(sem):
    return pltpu.CompilerParams(dimension_semantics=sem, vmem_limit_bytes=VMEM_LIMIT_BYTES)


def _dot(a, b):
    return jnp.dot(a.astype(BF16), b.astype(BF16), preferred_element_type=F32)


def _dot_nt(a, b):
    return lax.dot_general(a.astype(BF16), b.astype(BF16), (((1,), (1,)), ((), ())),
                           preferred_element_type=F32)


def _split(a):
    hi = a.astype(BF16)
    lo = (a - hi.astype(F32)).astype(BF16)
    return hi, lo


def _dot3(a, b):
    ah, al = _split(a)
    bh, bl = _split(b)
    return (jnp.dot(ah, bh, preferred_element_type=F32)
            + (jnp.dot(ah, bl, preferred_element_type=F32)
               + jnp.dot(al, bh, preferred_element_type=F32)))


def _dot2_exact_rhs(a, b_bf16):
    ah, al = _split(a)
    return jnp.dot(ah, b_bf16, preferred_element_type=F32) + jnp.dot(al, b_bf16, preferred_element_type=F32)


def _dot3_nt(a, b):
    ah, al = _split(a)
    bh, bl = _split(b)
    dn = (((1,), (1,)), ((), ()))
    return (lax.dot_general(ah, bh, dn, preferred_element_type=F32)
            + (lax.dot_general(ah, bl, dn, preferred_element_type=F32)
               + lax.dot_general(al, bh, dn, preferred_element_type=F32)))


def _sigmoid(x):
    return 1.0 / (1.0 + jnp.exp(-x))


def _layer_norm_rows(z, g, b):
    mu = jnp.mean(z, axis=-1, keepdims=True)
    zc = z - mu
    var = jnp.mean(zc * zc, axis=-1, keepdims=True)
    return zc * lax.rsqrt(var + LN_EPS) * g + b


class _Layout:
    def __init__(self, bsz, ctx_len, seq):
        self.B, self.CTX, self.S = bsz, ctx_len, seq
        self.T = ctx_len + seq
        self.tm = math.gcd(256, math.gcd(ctx_len, seq))
        self.nct = ctx_len // self.tm
        self.nlt = seq // self.tm
        self.rows_c = bsz * ctx_len
        self.rows_l = bsz * seq
        self.rows = self.rows_c + self.rows_l
        self.ntiles_c = bsz * self.nct
        self.ntiles = self.rows // self.tm
        assert self.rows_c % seq == 0, "latent K/V blocks are addressed in units of S rows"
        self.tmm = math.gcd(1024, math.gcd(self.rows_c, seq))
        self.mod_rows = -(-(bsz + 1) // SUBLANES) * SUBLANES

    def seq_tile(self, b, j):
        return jnp.where(j < self.nct, b * self.nct + j, self.ntiles_c + b * self.nlt + (j - self.nct))

    def mod_row(self, i, tile, row_offset=0):
        r = i * tile + row_offset
        return jnp.where(r < self.rows_c, self.B, (r - self.rows_c) // self.S)

    def pos_tile(self, i):
        il = i - self.ntiles_c
        return jnp.where(i < self.ntiles_c, i % self.nct, self.nct + il % self.nlt)


def _ada_body(cv_ref, w_ref, b_ref, o_ref):
    cv = cv_ref[...]
    s = cv * _sigmoid(cv)
    o_ref[0] = _dot3(s, w_ref[0]) + b_ref[0]


def _ada_mods(cvec, ada_w, ada_b):
    depth, d, n = ada_w.shape
    r = cvec.shape[0]
    tn = 512
    return pl.pallas_call(
        _ada_body,
        out_shape=jax.ShapeDtypeStruct((depth, r, n), F32),
        grid=(depth, n // tn),
        in_specs=[pl.BlockSpec((r, d), lambda l, j: (0, 0)),
                  pl.BlockSpec((1, d, tn), lambda l, j: (l, 0, j)),
                  pl.BlockSpec((1, 1, tn), lambda l, j: (l, 0, j))],
        out_specs=pl.BlockSpec((1, r, tn), lambda l, j: (l, 0, j)),
        compiler_params=_cparams(("arbitrary", "arbitrary")),
        name="ada_mods",
    )(cvec, ada_w, ada_b.reshape(depth, 1, n))


def _rope_tables(lay, head_dim, width):
    rows = lay.S // GRID_W
    rr, cc = jnp.meshgrid(jnp.arange(rows), jnp.arange(GRID_W), indexing="ij")
    row_pos = rr.reshape(-1).astype(F32)
    col_pos = cc.reshape(-1).astype(F32)
    axis_dim = head_dim // 2
    inv = ROPE_THETA ** (-jnp.arange(0, axis_dim, 2, dtype=F32) / axis_dim)
    ang = jnp.concatenate([row_pos[:, None] * inv, col_pos[:, None] * inv], -1)
    cos, sin = jnp.cos(ang), jnp.sin(ang)
    cos = jnp.concatenate([jnp.ones((lay.CTX, head_dim // 2), F32), cos], 0)
    sin = jnp.concatenate([jnp.zeros((lay.CTX, head_dim // 2), F32), sin], 0)
    cos_h = jnp.concatenate([cos, cos], -1)
    sin_h = jnp.concatenate([-sin, sin], -1)
    reps = width // head_dim
    return jnp.tile(cos_h, (1, reps)), jnp.tile(sin_h, (1, reps))


def _rope_lanes(x, cos, sin, head_dim):
    w = x.shape[-1]
    half = head_dim // 2
    if head_dim == LANES and w == LANES:
        rot = pltpu.roll(x, half, 1)
    else:
        fwd = pltpu.roll(x, w - half, 1)
        bwd = pltpu.roll(x, half, 1)
        lane = lax.broadcasted_iota(jnp.int32, x.shape, 1)
        rot = jnp.where((lane % head_dim) < half, fwd, bwd)
    return x * cos + rot * sin


def _even_in_body(x_ref, mod_ref, w_ref, cos_ref, sin_ref, pa_ref, q_ref, k_ref, v_ref):
    m = mod_ref[0]
    h = (x_ref[...] * (1.0 + m[1:2]) + m[0:1]).astype(BF16)
    pa_ref[...] = jnp.dot(h, w_ref[:, :A_IN], preferred_element_type=F32)
    cos, sin = cos_ref[...], sin_ref[...]
    o = A_IN
    q = jnp.dot(h, w_ref[:, o:o + B_QK], preferred_element_type=F32)
    q_ref[...] = (_rope_lanes(q, cos, sin, B_HEAD_DIM) * (B_HEAD_DIM ** -0.5)).astype(BF16)
    o += B_QK
    k = jnp.dot(h, w_ref[:, o:o + B_QK], preferred_element_type=F32)
    k_ref[...] = _rope_lanes(k, cos, sin, B_HEAD_DIM).astype(BF16)
    o += B_QK
    v_ref[...] = jnp.dot(h, w_ref[:, o:o + B_WIDTH], preferred_element_type=F32).astype(BF16)


def _even_in_proj(lay, x, mod, w_bf16, cos, sin):
    tm, d = lay.tm, D_MODEL
    row = lambda i: (i, 0)
    return pl.pallas_call(
        _even_in_body,
        out_shape=(jax.ShapeDtypeStruct((lay.rows, A_IN), F32),
                   jax.ShapeDtypeStruct((lay.rows, B_QK), BF16),
                   jax.ShapeDtypeStruct((lay.rows, B_QK), BF16),
                   jax.ShapeDtypeStruct((lay.rows, B_WIDTH), BF16)),
        grid=(lay.ntiles,),
        in_specs=[pl.BlockSpec((tm, d), row),
                  pl.BlockSpec((1, 6, d), lambda i: (lay.mod_row(i, tm), 0, 0)),
                  pl.BlockSpec((d, EVEN_IN), lambda i: (0, 0)),
                  pl.BlockSpec((tm, B_QK), lambda i: (lay.pos_tile(i), 0)),
                  pl.BlockSpec((tm, B_QK), lambda i: (lay.pos_tile(i), 0))],
        out_specs=(pl.BlockSpec((tm, A_IN), row), pl.BlockSpec((tm, B_QK), row),
                   pl.BlockSpec((tm, B_QK), row), pl.BlockSpec((tm, B_WIDTH), row)),
        compiler_params=_cparams(("arbitrary",)),
        name="even_in_proj",
    )(x, mod, w_bf16, cos, sin)


def _rwkv_feat_body(lay, pa_ref, prev_ref, next_ref, mu_ref, w0_ref, w2_ref, a0_ref, a2_ref, g2_ref,
                    kk_ref, ka_ref, rk_ref, bd_ref,
                    r_ref, v_ref, nkk_ref, g_ref, bonus_ref, w_f_ref, w_b_ref, kd_f_ref, kd_b_ref,
                    b_f_ref, b_b_ref):
    i = pl.program_id(0)
    tm = lay.tm
    il = i - lay.ntiles_c
    in_ctx = i < lay.ntiles_c
    seg_first = jnp.where(in_ctx, i % lay.nct == 0, il % lay.nlt == 0)
    seg_last = jnp.where(in_ctx, i % lay.nct == lay.nct - 1, il % lay.nlt == lay.nlt - 1)

    pa = pa_ref[...]
    row = lax.broadcasted_iota(jnp.int32, pa.shape, 0)
    prev_edge = jnp.where(seg_first, 0.0, 1.0) * prev_ref[SUBLANES - 1:SUBLANES, :]
    next_edge = jnp.where(seg_last, 0.0, 1.0) * next_ref[0:1, :]
    prev = jnp.where(row == 0, prev_edge, pltpu.roll(pa, 1, 0))
    nxt = jnp.where(row == tm - 1, next_edge, pltpu.roll(pa, tm - 1, 0))
    u = pa + (0.5 * (prev + nxt) - pa) * mu_ref[...]

    o1, o2, o3 = A_WIDTH, 2 * A_WIDTH, 3 * A_WIDTH
    o4 = o3 + 2 * A_LORA
    o5 = o4 + 2 * A_LORA
    r, k, v = u[:, :o1], u[:, o1:o2], u[:, o2:o3]
    bd = bd_ref[...]

    kk = k * kk_ref[...]
    ss = _dot2_exact_rhs(kk * kk, bd)
    kkn = kk / jnp.maximum(jnp.sqrt(ss), 1e-12)
    g = _dot3(_sigmoid(u[:, o5:]), g2_ref[...])

    r_ref[...] = r
    v_ref[...] = v
    nkk_ref[...] = -kkn
    g_ref[...] = g

    kd_sum = jnp.zeros_like(k)
    for d, (w_ref_o, kd_ref_o, b_ref_o) in enumerate(((w_f_ref, kd_f_ref, b_f_ref), (w_b_ref, kd_b_ref, b_b_ref))):
        wd = u[:, o3 + d * A_LORA:o3 + (d + 1) * A_LORA]
        ad = u[:, o4 + d * A_LORA:o4 + (d + 1) * A_LORA]
        z = -(w0_ref[d:d + 1, :] + _dot3(jnp.tanh(wd), w2_ref[d]))
        softplus = jnp.maximum(z, 0.0) + jnp.log(1.0 + jnp.exp(-jnp.abs(z)))
        w_log = -softplus - 0.5
        w_ref_o[...] = jnp.exp(-jnp.exp(w_log))
        a = _sigmoid(a0_ref[d:d + 1, :] + _dot3(ad, a2_ref[d]))
        kd = k * (1.0 + (a - 1.0) * ka_ref[...])
        kd_ref_o[...] = kd
        b_ref_o[...] = kkn * a
        kd_sum = kd_sum + kd
    bonus_ref[...] = _dot2_exact_rhs(r * kd_sum * rk_ref[...], bd) * v


def _head_block_diag(width, head_dim):
    h = jnp.arange(width) // head_dim
    return (h[:, None] == h[None, :]).astype(BF16)


def _rwkv_features(lay, pa, mu, w0, w2, a0, a2, g2, k_k, k_a, r_k):
    tm = lay.tm
    hb = tm // SUBLANES
    nb8 = lay.rows // SUBLANES
    row = lambda i: (i, 0)
    full2 = lambda i: (0, 0)
    full3 = lambda i: (0, 0, 0)
    w = A_WIDTH
    out = jax.ShapeDtypeStruct((lay.rows, w), F32)
    return pl.pallas_call(
        functools.partial(_rwkv_feat_body, lay),
        out_shape=(out,) * 11,
        grid=(lay.ntiles,),
        in_specs=[pl.BlockSpec((tm, A_IN), row),
                  pl.BlockSpec((SUBLANES, A_IN), lambda i: (jnp.maximum(i * hb - 1, 0), 0)),
                  pl.BlockSpec((SUBLANES, A_IN), lambda i: (jnp.minimum((i + 1) * hb, nb8 - 1), 0)),
                  pl.BlockSpec((1, A_IN), full2),
                  pl.BlockSpec((2, w), full2),
                  pl.BlockSpec((2, A_LORA, w), full3),
                  pl.BlockSpec((2, w), full2),
                  pl.BlockSpec((2, A_LORA, w), full3),
                  pl.BlockSpec((A_GATE_LORA, w), full2),
                  pl.BlockSpec((1, w), full2),
                  pl.BlockSpec((1, w), full2),
                  pl.BlockSpec((1, w), full2),
                  pl.BlockSpec((w, w), full2)],
        out_specs=(pl.BlockSpec((tm, w), row),) * 11,
        compiler_params=_cparams(("arbitrary",)),
        name="rwkv_features",
    )(pa, pa, pa, mu.reshape(1, A_IN), w0, w2, a0, a2, g2, k_k.reshape(1, w), k_a.reshape(1, w),
      r_k.reshape(1, w), _head_block_diag(w, A_HEAD_DIM))


def _wkv_scan_body(tb, r_ref, w_ref, k_ref, v_ref, a_ref, b_ref, y_ref, s_ref):
    n = A_HEAD_DIM
    chains = r_ref.shape[-1]

    @pl.when(pl.program_id(0) == 0)
    def _():
        s_ref[...] = jnp.zeros_like(s_ref)

    def step(t, carry):
        sa = jnp.zeros((n, chains), F32)
        for kk in range(n):
            sa = sa + s_ref[kk] * a_ref[t, kk:kk + 1, :]
        v_t = v_ref[t]
        y = jnp.zeros((n, chains), F32)
        for kk in range(n):
            s_new = (s_ref[kk] * w_ref[t, kk:kk + 1, :] + sa * b_ref[t, kk:kk + 1, :]
                     + v_t * k_ref[t, kk:kk + 1, :])
            s_ref[kk] = s_new
            y = y + s_new * r_ref[t, kk:kk + 1, :]
        y_ref[t] = y
        return carry

    lax.fori_loop(0, tb, step, 0)


def _wkv_scan(r, w, k, v, a, b):
    t, n, lanes = r.shape
    tb = math.gcd(32, t)
    blk = pl.BlockSpec((tb, n, lanes), lambda i: (i, 0, 0))
    return pl.pallas_call(
        functools.partial(_wkv_scan_body, tb),
        out_shape=jax.ShapeDtypeStruct((t, n, lanes), F32),
        grid=(t // tb,),
        in_specs=[blk] * 6,
        out_specs=blk,
        scratch_shapes=[pltpu.VMEM((n, n, lanes), F32)],
        compiler_params=_cparams(("arbitrary",)),
        name="wkv7_scan",
    )(r, w, k, v, a, b)


def _to_chains(lay, x_rows, reverse):
    h, n = A_HEADS, A_HEAD_DIM
    xc = x_rows[:lay.rows_c].reshape(lay.B, lay.CTX, h, n).transpose(1, 3, 0, 2).reshape(lay.CTX, n, lay.B * h)
    xl = x_rows[lay.rows_c:].reshape(lay.B, lay.S, h, n).transpose(1, 3, 0, 2).reshape(lay.S, n, lay.B * h)
    if reverse:
        xc, xl = xc[::-1], xl[::-1]
    return jnp.concatenate([xc, xl], axis=0)


def _from_chains(lay, y, reverse):
    h, n = A_HEADS, A_HEAD_DIM
    yc, yl = y[:lay.CTX], y[lay.CTX:]
    if reverse:
        yc, yl = yc[::-1], yl[::-1]
    yc = yc.reshape(lay.CTX, n, lay.B, h).transpose(2, 0, 3, 1).reshape(lay.rows_c, h * n)
    yl = yl.reshape(lay.S, n, lay.B, h).transpose(2, 0, 3, 1).reshape(lay.rows_l, h * n)
    return jnp.concatenate([yc, yl], axis=0)


def _rwkv_readout_body(yf_ref, yb_ref, bonus_ref, g_ref, lg_ref, lb_ref, bd_ref, o_ref):
    y = yf_ref[...] + yb_ref[...]
    bd = bd_ref[...]
    inv_n = 1.0 / A_HEAD_DIM
    mu = _dot2_exact_rhs(y, bd) * inv_n
    yc = y - mu
    var = _dot2_exact_rhs(yc * yc, bd) * inv_n
    yn = yc * lax.rsqrt(var + A_GN_EPS) * lg_ref[...] + lb_ref[...]
    o_ref[...] = ((yn + bonus_ref[...]) * g_ref[...]).astype(BF16)


def _rwkv_readout(lay, yf, yb, bonus, g, lnx_g, lnx_b):
    tm, w = lay.tm, A_WIDTH
    row = lambda i: (i, 0)
    full = lambda i: (0, 0)
    return pl.pallas_call(
        _rwkv_readout_body,
        out_shape=jax.ShapeDtypeStruct((lay.rows, w), BF16),
        grid=(lay.ntiles,),
        in_specs=[pl.BlockSpec((tm, w), row)] * 4 + [pl.BlockSpec((1, w), full)] * 2 + [pl.BlockSpec((w, w), full)],
        out_specs=pl.BlockSpec((tm, w), row),
        compiler_params=_cparams(("arbitrary",)),
        name="rwkv_readout",
    )(yf, yb, bonus, g, lnx_g.reshape(1, w), lnx_b.reshape(1, w), _head_block_diag(w, A_HEAD_DIM))


def _softmax_pv(q, keys, vals):
    scores = [_dot_nt(q, kk) for kk in keys]
    m = scores[0].max(axis=-1, keepdims=True)
    for s in scores[1:]:
        m = jnp.maximum(m, s.max(axis=-1, keepdims=True))
    l = None
    o = None
    for s, vv in zip(scores, vals):
        e = jnp.exp(s - m)
        ls = e.sum(axis=-1, keepdims=True)
        os_ = jnp.dot(e.astype(BF16), vv, preferred_element_type=F32)
        l = ls if l is None else l + ls
        o = os_ if o is None else o + os_
    return o / l


def _diff_attn_body(lay, lambda_init, q_ref, kc_ref, kl_ref, vc_ref, vl_ref, lam_ref, g_ref, o_ref):
    j = pl.program_id(1)
    lv = lam_ref[...]
    lam = (jnp.exp(jnp.sum(lv[0:1] * lv[1:2], axis=1, keepdims=True))
           - jnp.exp(jnp.sum(lv[2:3] * lv[3:4], axis=1, keepdims=True)) + lambda_init)

    def run(with_latent):
        for h in range(B_HEADS):
            vs = slice(h * B_V_DIM, (h + 1) * B_V_DIM)
            vals = [vc_ref[:, vs]] + ([vl_ref[:, vs]] if with_latent else [])
            outs = []
            for mi in range(2):
                cs = slice(h * B_V_DIM + mi * B_HEAD_DIM, h * B_V_DIM + (mi + 1) * B_HEAD_DIM)
                keys = [kc_ref[:, cs]] + ([kl_ref[:, cs]] if with_latent else [])
                outs.append(_softmax_pv(q_ref[:, cs], keys, vals))
            o = outs[0] - lam * outs[1]
            ms = jnp.mean(o * o, axis=-1, keepdims=True)
            o = o * lax.rsqrt(ms + B_SUBLN_EPS) * g_ref[...] * (1.0 - lambda_init)
            o_ref[:, vs] = o.astype(BF16)

    @pl.when(j < lay.nct)
    def _():
        run(False)

    @pl.when(j >= lay.nct)
    def _():
        run(True)


def _diff_attention(lay, q, k, v, lam_vecs, subln_g, lambda_init):
    tm = lay.tm
    w = B_WIDTH
    lat0 = lay.rows_c // lay.S
    return pl.pallas_call(
        functools.partial(_diff_attn_body, lay, lambda_init),
        out_shape=jax.ShapeDtypeStruct((lay.rows, w), BF16),
        grid=(lay.B, lay.nct + lay.nlt),
        in_specs=[pl.BlockSpec((tm, w), lambda b, j: (lay.seq_tile(b, j), 0)),
                  pl.BlockSpec((lay.CTX, w), lambda b, j: (b, 0)),
                  pl.BlockSpec((lay.S, w), lambda b, j: (lat0 + b, 0)),
                  pl.BlockSpec((lay.CTX, w), lambda b, j: (b, 0)),
                  pl.BlockSpec((lay.S, w), lambda b, j: (lat0 + b, 0)),
                  pl.BlockSpec((4, B_HEAD_DIM), lambda b, j: (0, 0)),
                  pl.BlockSpec((1, B_V_DIM), lambda b, j: (0, 0))],
        out_specs=pl.BlockSpec((tm, w), lambda b, j: (lay.seq_tile(b, j), 0)),
        compiler_params=_cparams(("arbitrary", "arbitrary")),
        name="diff_attention",
    )(q, k, k, v, v, lam_vecs, subln_g.reshape(1, B_V_DIM))


def _odd_in_body(x_ref, mod_ref, w_ref, cos_ref, sin_ref, qn_ref, kn_ref, q_ref, k_ref, v_ref):
    m = mod_ref[0]
    h = (x_ref[...] * (1.0 + m[1:2]) + m[0:1]).astype(BF16)
    cos, sin = cos_ref[...], sin_ref[...]

    def norm_rope(p, g, scale):
        ms = jnp.mean(p * p, axis=-1, keepdims=True)
        y = p * lax.rsqrt(ms + QK_NORM_EPS) * g
        return (_rope_lanes(y, cos, sin, C_HEAD_DIM) * scale).astype(BF16)

    for hd in range(C_HEADS):
        cs = slice(hd * C_HEAD_DIM, (hd + 1) * C_HEAD_DIM)
        p = jnp.dot(h, w_ref[:, cs], preferred_element_type=F32)
        q_ref[:, cs] = norm_rope(p, qn_ref[...], C_HEAD_DIM ** -0.5)
    for hd in range(C_KV_HEADS):
        cs = slice(hd * C_HEAD_DIM, (hd + 1) * C_HEAD_DIM)
        p = jnp.dot(h, w_ref[:, C_Q + hd * C_HEAD_DIM:C_Q + (hd + 1) * C_HEAD_DIM], preferred_element_type=F32)
        k_ref[:, cs] = norm_rope(p, kn_ref[...], 1.0)
    v_ref[...] = jnp.dot(h, w_ref[:, C_Q + C_KV:], preferred_element_type=F32).astype(BF16)


def _odd_in_proj(lay, x, mod, w_bf16, cos, sin, qn_g, kn_g):
    tm, d = lay.tm, D_MODEL
    row = lambda i: (i, 0)
    full = lambda i: (0, 0)
    return pl.pallas_call(
        _odd_in_body,
        out_shape=(jax.ShapeDtypeStruct((lay.rows, C_Q), BF16),
                   jax.ShapeDtypeStruct((lay.rows, C_KV), BF16),
                   jax.ShapeDtypeStruct((lay.rows, C_KV), BF16)),
        grid=(lay.ntiles,),
        in_specs=[pl.BlockSpec((tm, d), row),
                  pl.BlockSpec((1, 6, d), lambda i: (lay.mod_row(i, tm), 0, 0)),
                  pl.BlockSpec((d, ODD_IN), full),
                  pl.BlockSpec((tm, C_HEAD_DIM), lambda i: (lay.pos_tile(i), 0)),
                  pl.BlockSpec((tm, C_HEAD_DIM), lambda i: (lay.pos_tile(i), 0)),
                  pl.BlockSpec((1, C_HEAD_DIM), full),
                  pl.BlockSpec((1, C_HEAD_DIM), full)],
        out_specs=(pl.BlockSpec((tm, C_Q), row), pl.BlockSpec((tm, C_KV), row), pl.BlockSpec((tm, C_KV), row)),
        compiler_params=_cparams(("arbitrary",)),
        name="odd_in_proj",
    )(x, mod, w_bf16, cos, sin, qn_g.reshape(1, C_HEAD_DIM), kn_g.reshape(1, C_HEAD_DIM))


def _gqa_body(q_ref, kc_ref, kl_ref, vc_ref, vl_ref, o_ref):
    keys = [kc_ref[...], kl_ref[...]]
    vals = [vc_ref[...], vl_ref[...]]
    for g in range(C_GROUP):
        cs = slice(g * C_HEAD_DIM, (g + 1) * C_HEAD_DIM)
        o_ref[:, cs] = _softmax_pv(q_ref[:, cs], keys, vals).astype(BF16)


def _gqa_attention(lay, q, k, v):
    tm = lay.tm
    gw = C_GROUP * C_HEAD_DIM
    lat0 = lay.rows_c // lay.S
    hd = C_HEAD_DIM
    return pl.pallas_call(
        _gqa_body,
        out_shape=jax.ShapeDtypeStruct((lay.rows_l, C_Q), BF16),
        grid=(lay.B, C_KV_HEADS, lay.nlt),
        in_specs=[pl.BlockSpec((tm, gw), lambda b, h, j: (lay.ntiles_c + b * lay.nlt + j, h)),
                  pl.BlockSpec((lay.CTX, hd), lambda b, h, j: (b, h)),
                  pl.BlockSpec((lay.S, hd), lambda b, h, j: (lat0 + b, h)),
                  pl.BlockSpec((lay.CTX, hd), lambda b, h, j: (b, h)),
                  pl.BlockSpec((lay.S, hd), lambda b, h, j: (lat0 + b, h))],
        out_specs=pl.BlockSpec((tm, gw), lambda b, h, j: (b * lay.nlt + j, h)),
        compiler_params=_cparams(("arbitrary", "arbitrary", "arbitrary")),
        name="gqa_attention",
    )(q, k, k, v, v)


def _out_proj_body(n_mix, *refs):
    mix_refs = refs[:n_mix]
    w_ref, x_ref, mod_ref, lg_ref, lb_ref, rw_ref, xo_ref, h_ref, lt_ref = refs[n_mix:]
    m = mod_ref[0]
    off = 0
    mix = None
    for mr in mix_refs:
        kw = mr.shape[-1]
        part = jnp.dot(mr[...], w_ref[off:off + kw, :], preferred_element_type=F32)
        mix = part if mix is None else mix + part
        off += kw
    z = DEEPNORM_ALPHA * x_ref[...] + m[2:3] * mix
    xn = _layer_norm_rows(z, lg_ref[...], lb_ref[...])
    xo_ref[...] = xn
    h2 = xn * (1.0 + m[4:5]) + m[3:4]
    h_ref[...] = h2.astype(BF16)
    lt_ref[...] = _dot3_nt(rw_ref[...], h2)


def _out_proj(lay, mixes, w_bf16, x, x_tile_off, mod, ln_g, ln_b, router_wt, n_rows):
    tm, d = lay.tm, D_MODEL
    row = lambda i: (i, 0)
    full = lambda i: (0, 0)
    row_off = x_tile_off * tm
    in_specs = [pl.BlockSpec((tm, mx.shape[-1]), row) for mx in mixes]
    in_specs += [pl.BlockSpec((d, d), full),
                 pl.BlockSpec((tm, d), lambda i: (i + x_tile_off, 0)),
                 pl.BlockSpec((1, 6, d), lambda i: (lay.mod_row(i, tm, row_off), 0, 0)),
                 pl.BlockSpec((1, d), full), pl.BlockSpec((1, d), full),
                 pl.BlockSpec((N_EXPERTS, d), full)]
    return pl.pallas_call(
        functools.partial(_out_proj_body, len(mixes)),
        out_shape=(jax.ShapeDtypeStruct((n_rows, d), F32),
                   jax.ShapeDtypeStruct((n_rows, d), BF16),
                   jax.ShapeDtypeStruct((N_EXPERTS, n_rows), F32)),
        grid=(n_rows // tm,),
        in_specs=in_specs,
        out_specs=(pl.BlockSpec((tm, d), row), pl.BlockSpec((tm, d), row),
                   pl.BlockSpec((N_EXPERTS, tm), lambda i: (0, i))),
        compiler_params=_cparams(("arbitrary",)),
        name="out_proj_ln",
    )(*mixes, w_bf16, x, mod, ln_g.reshape(1, d), ln_b.reshape(1, d), router_wt)


def _router_body(lt_ref, rb_ref, g_ref):
    logits = lt_ref[...] + rb_ref[...]
    rows = [logits[e:e + 1, :] for e in range(N_EXPERTS)]
    m = rows[0]
    for x in rows[1:]:
        m = jnp.maximum(m, x)
    ex = [jnp.exp(x - m) for x in rows]
    z = ex[0]
    for x in ex[1:]:
        z = z + x
    p = [x / z for x in ex]

    gscore = []
    for g in range(N_GROUPS):
        a, b, c, d = p[4 * g:4 * g + 4]
        hi1, lo1 = jnp.maximum(a, b), jnp.minimum(a, b)
        hi2, lo2 = jnp.maximum(c, d), jnp.minimum(c, d)
        top1 = jnp.maximum(hi1, hi2)
        top2 = jnp.maximum(jnp.minimum(hi1, hi2), jnp.maximum(lo1, lo2))
        gscore.append(top1 + top2)
    best = []
    for g in range(N_GROUPS):
        ok = None
        for o in range(N_GROUPS):
            if o == g:
                continue
            c = (gscore[g] > gscore[o]) if o < g else (gscore[g] >= gscore[o])
            ok = c if ok is None else jnp.logical_and(ok, c)
        best.append(ok)
    sel = []
    for e in range(N_EXPERTS):
        g = e // EXPERTS_PER_GROUP
        rank = jnp.zeros_like(p[e])
        for o in range(4 * g, 4 * g + 4):
            if o == e:
                continue
            ahead = (p[o] > p[e]) if o > e else (p[o] >= p[e])
            rank = rank + jnp.where(ahead, 1.0, 0.0)
        sel.append(jnp.where(jnp.logical_and(best[g], rank < 1.5), p[e], 0.0))
    tot = sel[0]
    for x in sel[1:]:
        tot = tot + x
    for e in range(N_EXPERTS):
        g_ref[e:e + 1, :] = sel[e] / tot


def _router(logits_t, router_b):
    e, n = logits_t.shape
    tr = math.gcd(2048, n)
    return pl.pallas_call(
        _router_body,
        out_shape=jax.ShapeDtypeStruct((e, n), F32),
        grid=(n // tr,),
        in_specs=[pl.BlockSpec((e, tr), lambda i: (0, i)), pl.BlockSpec((e, 1), lambda i: (0, 0))],
        out_specs=pl.BlockSpec((e, tr), lambda i: (0, i)),
        compiler_params=_cparams(("arbitrary",)),
        name="router_gates",
    )(logits_t, router_b.reshape(e, 1))


def _moe_body(sub, h_ref, gate_ref, w1_ref, w3_ref, w2_ref, x_ref, mod_ref, lg_ref, lb_ref, o_ref, acc_ref):
    e = pl.program_id(1)
    tmm = h_ref.shape[0]

    @pl.when(e == 0)
    def _():
        acc_ref[...] = jnp.zeros_like(acc_ref)

    onehot = jnp.where(lax.broadcasted_iota(jnp.int32, (1, N_EXPERTS), 1) == e, 1.0, 0.0)
    w1, w3, w2 = w1_ref[0], w3_ref[0], w2_ref[0]

    def blk(r, carry):
        rs = pl.ds(pl.multiple_of(r * sub, sub), sub)
        h = h_ref[rs, :]
        gcol = jnp.sum(gate_ref[rs, :] * onehot, axis=1, keepdims=True)
        h1 = jnp.dot(h, w1, preferred_element_type=F32)
        h3 = jnp.dot(h, w3, preferred_element_type=F32)
        hid = (h1 * _sigmoid(h1) * h3 * gcol).astype(BF16)
        acc_ref[rs, :] += jnp.dot(hid, w2, preferred_element_type=F32)
        return carry

    lax.fori_loop(0, tmm // sub, blk, 0)

    @pl.when(e == N_EXPERTS - 1)
    def _():
        m = mod_ref[0]
        z = DEEPNORM_ALPHA * x_ref[...] + m[5:6] * acc_ref[...]
        o_ref[...] = _layer_norm_rows(z, lg_ref[...], lb_ref[...])


def _moe(lay, h2, gates, w1, w3, w2, x, mod, ln_g, ln_b, row_off):
    n_rows, d = h2.shape
    tmm = lay.tmm
    sub = math.gcd(256, tmm)
    row = lambda i, e: (i, 0)
    full = lambda i, e: (0, 0)
    return pl.pallas_call(
        functools.partial(_moe_body, sub),
        out_shape=jax.ShapeDtypeStruct((n_rows, d), F32),
        grid=(n_rows // tmm, N_EXPERTS),
        in_specs=[pl.BlockSpec((tmm, d), row),
                  pl.BlockSpec((tmm, N_EXPERTS), row),
                  pl.BlockSpec((1, d, EXPERT_FF), lambda i, e: (e, 0, 0)),
                  pl.BlockSpec((1, d, EXPERT_FF), lambda i, e: (e, 0, 0)),
                  pl.BlockSpec((1, EXPERT_FF, d), lambda i, e: (e, 0, 0)),
                  pl.BlockSpec((tmm, d), row),
                  pl.BlockSpec((1, 6, d), lambda i, e: (lay.mod_row(i, tmm, row_off), 0, 0)),
                  pl.BlockSpec((1, d), full), pl.BlockSpec((1, d), full)],
        out_specs=pl.BlockSpec((tmm, d), row),
        scratch_shapes=[pltpu.VMEM((tmm, d), F32)],
        compiler_params=_cparams(("arbitrary", "arbitrary")),
        name="moe_experts",
    )(h2, gates, w1, w3, w2, x, mod, ln_g.reshape(1, d), ln_b.reshape(1, d))


def kernel(x, c, ctx, c_ctx, router_w, router_b, ada_w, ada_b, ln1_g, ln1_b, ln2_g, ln2_b, moe_w1, moe_w3, moe_w2, ev_w_in, ev_w_out, ev_a_mu, ev_a_w0, ev_a_w2, ev_a_a0, ev_a_a2, ev_a_g2, ev_a_kk, ev_a_ka, ev_a_rk, ev_a_lnx_g, ev_a_lnx_b, ev_b_lam, ev_b_subln_g, od_w_in, od_w_out, od_qn_g, od_kn_g):
    bsz, seq, d = x.shape
    ctx_len = ctx.shape[1]
    assert d == D_MODEL and ada_w.shape[0] == DEPTH and seq % GRID_W == 0
    lay = _Layout(bsz, ctx_len, seq)

    cvec = jnp.zeros((lay.mod_rows, d), F32).at[:bsz].set(c).at[bsz].set(c_ctx)
    mods = _ada_mods(cvec, ada_w, ada_b).reshape(DEPTH, lay.mod_rows, 6, d)

    xs = jnp.concatenate([ctx.reshape(lay.rows_c, d), x.reshape(lay.rows_l, d)], axis=0)
    router_wt = router_w.T

    cos_b, sin_b = _rope_tables(lay, B_HEAD_DIM, B_QK)
    cos_c, sin_c = _rope_tables(lay, C_HEAD_DIM, C_HEAD_DIM)

    for i in range(DEPTH):
        last = i == DEPTH - 1
        j = i // 2
        mod = mods[i]
        if i % 2 == 0:
            lambda_init = 0.8 - 0.6 * math.exp(-0.3 * i)
            pa, q, k, v = _even_in_proj(lay, xs, mod, ev_w_in[j].astype(BF16), cos_b, sin_b)
            (r_, v_, nkk, g_, bonus, w_f, w_b, kd_f, kd_b, b_f, b_b) = _rwkv_features(
                lay, pa, ev_a_mu[j], ev_a_w0[j], ev_a_w2[j], ev_a_a0[j], ev_a_a2[j], ev_a_g2[j],
                ev_a_kk[j], ev_a_ka[j], ev_a_rk[j].reshape(-1))

            def both(fw, bw):
                return jnp.concatenate([_to_chains(lay, fw, False), _to_chains(lay, bw, True)], axis=-1)

            y = _wkv_scan(both(r_, r_), both(w_f, w_b), both(kd_f, kd_b), both(v_, v_),
                          both(nkk, nkk), both(b_f, b_b))
            nl = lay.B * A_HEADS
            yf = _from_chains(lay, y[:, :, :nl], False)
            yb = _from_chains(lay, y[:, :, nl:], True)
            ya = _rwkv_readout(lay, yf, yb, bonus, g_, ev_a_lnx_g[j], ev_a_lnx_b[j])
            yd = _diff_attention(lay, q, k, v, ev_b_lam[j], ev_b_subln_g[j], lambda_init)
            mixes, w_out = [ya, yd], ev_w_out[j]
            n_rows, tile_off = (lay.rows_l, lay.ntiles_c) if last else (lay.rows, 0)
            if last:
                mixes = [mx[lay.rows_c:] for mx in mixes]
        else:
            q, k, v = _odd_in_proj(lay, xs, mod, od_w_in[j].astype(BF16), cos_c, sin_c, od_qn_g[j], od_kn_g[j])
            assert last, "odd layers other than the last are not supported"
            o = _gqa_attention(lay, q, k, v)
            mixes, w_out = [o], od_w_out[j]
            n_rows, tile_off = lay.rows_l, lay.ntiles_c
        row_off = tile_off * lay.tm
        x_new, h2, logits_t = _out_proj(lay, mixes, w_out.astype(BF16), xs, tile_off, mod,
                                        ln1_g[i], ln1_b[i], router_wt, n_rows)
        gates = _router(logits_t, router_b).T
        xs = _moe(lay, h2, gates, moe_w1[i].astype(BF16), moe_w3[i].astype(BF16), moe_w2[i].astype(BF16),
                  x_new, mod, ln2_g[i], ln2_b[i], row_off)
    return xs.reshape(bsz, seq, d)
```

```python
import functools
import math

import jax
import jax.numpy as jnp
from jax import lax
from jax.experimental import pallas as pl
from jax.experimental.pallas import tpu as pltpu

F32 = jnp.float32
BF16 = jnp.bfloat16

D_MODEL = 1024
DEPTH = 2
GRID_W = 64
ROPE_THETA = 10000.0
LN_EPS = 1e-5
DEEPNORM_ALPHA = (2 * DEPTH) ** 0.25

A_HEAD_DIM = 64
A_HEADS = 8
A_WIDTH = 512
A_LORA = 64
A_GATE_LORA = 128
A_GN_EPS = 64e-5
A_IN = 3 * A_WIDTH + 4 * A_LORA + A_GATE_LORA

B_HEAD_DIM = 64
B_V_DIM = 128
B_HEADS = 4
B_WIDTH = 512
B_QK = 512
B_SUBLN_EPS = 1e-5
EVEN_IN = A_IN + 2 * B_QK + B_WIDTH

C_HEAD_DIM = 128
C_HEADS = 8
C_KV_HEADS = 2
C_GROUP = 4
C_Q = 1024
C_KV = 256
ODD_IN = C_Q + 2 * C_KV
QK_NORM_EPS = 1e-6

N_EXPERTS = 16
N_GROUPS = 4
EXPERTS_PER_GROUP = 4
EXPERT_FF = 512

VMEM_LIMIT_BYTES = 56 * 1024 * 1024
LANES = 128
SUBLANES = 8


def _cparams(sem):
    return pltpu.CompilerParams(dimension_semantics=sem, vmem_limit_bytes=VMEM_LIMIT_BYTES)


def _dot(a, b):
    return jnp.dot(a.astype(BF16), b.astype(BF16), preferred_element_type=F32)


def _dot_nt(a, b):
    return lax.dot_general(a.astype(BF16), b.astype(BF16), (((1,), (1,)), ((), ())),
                           preferred_element_type=F32)


def _split(a):
    hi = a.astype(BF16)
    lo = (a - hi.astype(F32)).astype(BF16)
    return hi, lo


def _dot3(a, b):
    ah, al = _split(a)
    bh, bl = _split(b)
    return (jnp.dot(ah, bh, preferred_element_type=F32)
            + (jnp.dot(ah, bl, preferred_element_type=F32)
               + jnp.dot(al, bh, preferred_element_type=F32)))


def _dot2_exact_rhs(a, b_bf16):
    ah, al = _split(a)
    return jnp.dot(ah, b_bf16, preferred_element_type=F32) + jnp.dot(al, b_bf16, preferred_element_type=F32)


def _dot3_nt(a, b):
    ah, al = _split(a)
    bh, bl = _split(b)
    dn = (((1,), (1,)), ((), ()))
    return (lax.dot_general(ah, bh, dn, preferred_element_type=F32)
            + (lax.dot_general(ah, bl, dn, preferred_element_type=F32)
               + lax.dot_general(al, bh, dn, preferred_element_type=F32)))


def _sigmoid(x):
    return 1.0 / (1.0 + jnp.exp(-x))


def _layer_norm_rows(z, g, b):
    mu = jnp.mean(z, axis=-1, keepdims=True)
    zc = z - mu
    var = jnp.mean(zc * zc, axis=-1, keepdims=True)
    return zc * lax.rsqrt(var + LN_EPS) * g + b


class _Layout:
    def __init__(self, bsz, ctx_len, seq):
        self.B, self.CTX, self.S = bsz, ctx_len, seq
        self.T = ctx_len + seq
        self.tm = math.gcd(256, math.gcd(ctx_len, seq))
        self.nct = ctx_len // self.tm
        self.nlt = seq // self.tm
        self.rows_c = bsz * ctx_len
        self.rows_l = bsz * seq
        self.rows = self.rows_c + self.rows_l
        self.ntiles_c = bsz * self.nct
        self.ntiles = self.rows // self.tm
        assert self.rows_c % seq == 0, "latent K/V blocks are addressed in units of S rows"
        self.tmm = math.gcd(1024, math.gcd(self.rows_c, seq))
        self.mod_rows = -(-(bsz + 1) // SUBLANES) * SUBLANES

    def seq_tile(self, b, j):
        return jnp.where(j < self.nct, b * self.nct + j, self.ntiles_c + b * self.nlt + (j - self.nct))

    def seq_block(self, i):
        il = i - self.ntiles_c
        return (jnp.where(i < self.ntiles_c, i // self.nct, il // self.nlt),
                jnp.where(i < self.ntiles_c, i % self.nct, self.nct + il % self.nlt))

    def mod_row(self, i, tile, row_offset=0):
        r = i * tile + row_offset
        return jnp.where(r < self.rows_c, self.B, (r - self.rows_c) // self.S)

    def pos_tile(self, i):
        il = i - self.ntiles_c
        return jnp.where(i < self.ntiles_c, i % self.nct, self.nct + il % self.nlt)


def _ada_body(cv_ref, w_ref, b_ref, o_ref):
    cv = cv_ref[...]
    s = cv * _sigmoid(cv)
    o_ref[0] = _dot3(s, w_ref[0]) + b_ref[0]


def _ada_mods(cvec, ada_w, ada_b):
    depth, d, n = ada_w.shape
    r = cvec.shape[0]
    tn = 512
    return pl.pallas_call(
        _ada_body,
        out_shape=jax.ShapeDtypeStruct((depth, r, n), F32),
        grid=(depth, n // tn),
        in_specs=[pl.BlockSpec((r, d), lambda l, j: (0, 0)),
                  pl.BlockSpec((1, d, tn), lambda l, j: (l, 0, j)),
                  pl.BlockSpec((1, 1, tn), lambda l, j: (l, 0, j))],
        out_specs=pl.BlockSpec((1, r, tn), lambda l, j: (l, 0, j)),
        compiler_params=_cparams(("arbitrary", "arbitrary")),
        name="ada_mods",
    )(cvec, ada_w, ada_b.reshape(depth, 1, n))


def _rope_tables(lay, head_dim, width):
    rows = lay.S // GRID_W
    rr, cc = jnp.meshgrid(jnp.arange(rows), jnp.arange(GRID_W), indexing="ij")
    row_pos = rr.reshape(-1).astype(F32)
    col_pos = cc.reshape(-1).astype(F32)
    axis_dim = head_dim // 2
    inv = ROPE_THETA ** (-jnp.arange(0, axis_dim, 2, dtype=F32) / axis_dim)
    ang = jnp.concatenate([row_pos[:, None] * inv, col_pos[:, None] * inv], -1)
    cos, sin = jnp.cos(ang), jnp.sin(ang)
    cos = jnp.concatenate([jnp.ones((lay.CTX, head_dim // 2), F32), cos], 0)
    sin = jnp.concatenate([jnp.zeros((lay.CTX, head_dim // 2), F32), sin], 0)
    cos_h = jnp.concatenate([cos, cos], -1)
    sin_h = jnp.concatenate([-sin, sin], -1)
    reps = width // head_dim
    return jnp.tile(cos_h, (1, reps)), jnp.tile(sin_h, (1, reps))


def _rope_lanes(x, cos, sin, head_dim):
    w = x.shape[-1]
    half = head_dim // 2
    if head_dim == LANES and w == LANES:
        rot = pltpu.roll(x, half, 1)
    else:
        fwd = pltpu.roll(x, w - half, 1)
        bwd = pltpu.roll(x, half, 1)
        lane = lax.broadcasted_iota(jnp.int32, x.shape, 1)
        rot = jnp.where((lane % head_dim) < half, fwd, bwd)
    return x * cos + rot * sin


def _even_in_body(x_ref, mod_ref, w_ref, cos_ref, sin_ref, pa_ref, q_ref, k_ref, v_ref):
    m = mod_ref[0]
    h = (x_ref[...] * (1.0 + m[1:2]) + m[0:1]).astype(BF16)
    pa_ref[...] = jnp.dot(h, w_ref[:, :A_IN], preferred_element_type=F32)
    cos, sin = cos_ref[...], sin_ref[...]
    o = A_IN
    q = jnp.dot(h, w_ref[:, o:o + B_QK], preferred_element_type=F32)
    q_ref[...] = (_rope_lanes(q, cos, sin, B_HEAD_DIM) * (B_HEAD_DIM ** -0.5)).astype(BF16)
    o += B_QK
    k = jnp.dot(h, w_ref[:, o:o + B_QK], preferred_element_type=F32)
    k_ref[...] = _rope_lanes(k, cos, sin, B_HEAD_DIM).astype(BF16)
    o += B_QK
    v_ref[...] = jnp.dot(h, w_ref[:, o:o + B_WIDTH], preferred_element_type=F32).astype(BF16)


def _even_in_proj(lay, x, mod, w_bf16, cos, sin):
    tm, d = lay.tm, D_MODEL
    row = lambda i: (i, 0)
    return pl.pallas_call(
        _even_in_body,
        out_shape=(jax.ShapeDtypeStruct((lay.rows, A_IN), F32),
                   jax.ShapeDtypeStruct((lay.rows, B_QK), BF16),
                   jax.ShapeDtypeStruct((lay.rows, B_QK), BF16),
                   jax.ShapeDtypeStruct((lay.rows, B_WIDTH), BF16)),
        grid=(lay.ntiles,),
        in_specs=[pl.BlockSpec((tm, d), row),
                  pl.BlockSpec((1, 6, d), lambda i: (lay.mod_row(i, tm), 0, 0)),
                  pl.BlockSpec((d, EVEN_IN), lambda i: (0, 0)),
                  pl.BlockSpec((tm, B_QK), lambda i: (lay.pos_tile(i), 0)),
                  pl.BlockSpec((tm, B_QK), lambda i: (lay.pos_tile(i), 0))],
        out_specs=(pl.BlockSpec((tm, A_IN), row), pl.BlockSpec((tm, B_QK), row),
                   pl.BlockSpec((tm, B_QK), row), pl.BlockSpec((tm, B_WIDTH), row)),
        compiler_params=_cparams(("arbitrary",)),
        name="even_in_proj",
    )(x, mod, w_bf16, cos, sin)


def _rwkv_feat_body(lay, pa_ref, prev_ref, next_ref, mu_ref, w0_ref, w2_ref, a0_ref, a2_ref, g2_ref,
                    kk_ref, ka_ref, rk_ref, bd_ref,
                    r_ref, v_ref, nkk_ref, g_ref, bonus_ref, w_f_ref, w_b_ref, kd_f_ref, kd_b_ref,
                    b_f_ref, b_b_ref):
    i = pl.program_id(0)
    tm = lay.tm
    il = i - lay.ntiles_c
    in_ctx = i < lay.ntiles_c
    seg_first = jnp.where(in_ctx, i % lay.nct == 0, il % lay.nlt == 0)
    seg_last = jnp.where(in_ctx, i % lay.nct == lay.nct - 1, il % lay.nlt == lay.nlt - 1)

    pa = pa_ref[...]
    row = lax.broadcasted_iota(jnp.int32, pa.shape, 0)
    prev_edge = jnp.where(seg_first, 0.0, 1.0) * prev_ref[SUBLANES - 1:SUBLANES, :]
    next_edge = jnp.where(seg_last, 0.0, 1.0) * next_ref[0:1, :]
    prev = jnp.where(row == 0, prev_edge, pltpu.roll(pa, 1, 0))
    nxt = jnp.where(row == tm - 1, next_edge, pltpu.roll(pa, tm - 1, 0))
    u = pa + (0.5 * (prev + nxt) - pa) * mu_ref[...]

    o1, o2, o3 = A_WIDTH, 2 * A_WIDTH, 3 * A_WIDTH
    o4 = o3 + 2 * A_LORA
    o5 = o4 + 2 * A_LORA
    r, k, v = u[:, :o1], u[:, o1:o2], u[:, o2:o3]
    bd = bd_ref[...]

    kk = k * kk_ref[...]
    ss = _dot2_exact_rhs(kk * kk, bd)
    kkn = kk / jnp.maximum(jnp.sqrt(ss), 1e-12)
    g = _dot3(_sigmoid(u[:, o5:]), g2_ref[...])

    r_ref[0] = r
    v_ref[0] = v
    nkk_ref[0] = -kkn
    g_ref[0] = g

    kd_sum = jnp.zeros_like(k)
    for d, (w_ref_o, kd_ref_o, b_ref_o) in enumerate(((w_f_ref, kd_f_ref, b_f_ref), (w_b_ref, kd_b_ref, b_b_ref))):
        wd = u[:, o3 + d * A_LORA:o3 + (d + 1) * A_LORA]
        ad = u[:, o4 + d * A_LORA:o4 + (d + 1) * A_LORA]
        z = -(w0_ref[d:d + 1, :] + _dot3(jnp.tanh(wd), w2_ref[d]))
        softplus = jnp.maximum(z, 0.0) + jnp.log(1.0 + jnp.exp(-jnp.abs(z)))
        w_log = -softplus - 0.5
        w_ref_o[0] = jnp.exp(-jnp.exp(w_log))
        a = _sigmoid(a0_ref[d:d + 1, :] + _dot3(ad, a2_ref[d]))
        kd = k * (1.0 + (a - 1.0) * ka_ref[...])
        kd_ref_o[0] = kd
        b_ref_o[0] = kkn * a
        kd_sum = kd_sum + kd
    bonus_ref[0] = _dot2_exact_rhs(r * kd_sum * rk_ref[...], bd) * v


def _head_block_diag(width, head_dim):
    h = jnp.arange(width) // head_dim
    return (h[:, None] == h[None, :]).astype(BF16)


def _rwkv_features(lay, pa, mu, w0, w2, a0, a2, g2, k_k, k_a, r_k):
    tm = lay.tm
    hb = tm // SUBLANES
    nb8 = lay.rows // SUBLANES
    row = lambda i: (i, 0)
    full2 = lambda i: (0, 0)
    full3 = lambda i: (0, 0, 0)
    w = A_WIDTH
    out = jax.ShapeDtypeStruct((lay.B, lay.T, w), F32)
    seq = lambda i: lay.seq_block(i) + (0,)
    return pl.pallas_call(
        functools.partial(_rwkv_feat_body, lay),
        out_shape=(out,) * 11,
        grid=(lay.ntiles,),
        in_specs=[pl.BlockSpec((tm, A_IN), row),
                  pl.BlockSpec((SUBLANES, A_IN), lambda i: (jnp.maximum(i * hb - 1, 0), 0)),
                  pl.BlockSpec((SUBLANES, A_IN), lambda i: (jnp.minimum((i + 1) * hb, nb8 - 1), 0)),
                  pl.BlockSpec((1, A_IN), full2),
                  pl.BlockSpec((2, w), full2),
                  pl.BlockSpec((2, A_LORA, w), full3),
                  pl.BlockSpec((2, w), full2),
                  pl.BlockSpec((2, A_LORA, w), full3),
                  pl.BlockSpec((A_GATE_LORA, w), full2),
                  pl.BlockSpec((1, w), full2),
                  pl.BlockSpec((1, w), full2),
                  pl.BlockSpec((1, w), full2),
                  pl.BlockSpec((w, w), full2)],
        out_specs=(pl.BlockSpec((1, tm, w), seq),) * 11,
        compiler_params=_cparams(("arbitrary",)),
        name="rwkv_features",
    )(pa, pa, pa, mu.reshape(1, A_IN), w0, w2, a0, a2, g2, k_k.reshape(1, w), k_a.reshape(1, w),
      r_k.reshape(1, w), _head_block_diag(w, A_HEAD_DIM))


_SCAN_R, _SCAN_W, _SCAN_K, _SCAN_V, _SCAN_A, _SCAN_B = range(6)


def _wkv_scan_body(tb, *refs):
    fwd, bwd = refs[:6], refs[6:12]
    yf_ref, yb_ref, s_ref, m_ref = refs[12:]
    n = A_HEAD_DIM
    chains = s_ref.shape[-1]
    is_fwd = lax.broadcasted_iota(jnp.int32, (n, chains), 1) < chains // 2

    @pl.when(pl.program_id(0) == 0)
    def _():
        s_ref[...] = jnp.zeros_like(s_ref)

    def step(t, carry):
        tr = tb - 1 - t
        for idx in range(6):
            m_ref[idx] = jnp.where(is_fwd, fwd[idx][t], bwd[idx][tr])
        sa = jnp.zeros((n, chains), F32)
        for kk in range(n):
            sa = sa + s_ref[kk] * m_ref[_SCAN_A, kk:kk + 1, :]
        v_t = m_ref[_SCAN_V]
        y = jnp.zeros((n, chains), F32)
        for kk in range(n):
            s_new = (s_ref[kk] * m_ref[_SCAN_W, kk:kk + 1, :] + sa * m_ref[_SCAN_B, kk:kk + 1, :]
                     + v_t * m_ref[_SCAN_K, kk:kk + 1, :])
            s_ref[kk] = s_new
            y = y + s_new * m_ref[_SCAN_R, kk:kk + 1, :]
        yf_ref[t] = y
        yb_ref[tr] = y
        return carry

    lax.fori_loop(0, tb, step, 0)


def _wkv_scan(lay, ops):
    t, n, lanes = ops[0].shape
    tb = math.gcd(32, math.gcd(lay.CTX, lay.S))
    nctb, nt = lay.CTX // tb, t // tb
    fwd_map = lambda g: (g, 0, 0)
    bwd_map = lambda g: (jnp.where(g < nctb, nctb - 1 - g, nt - 1 - (g - nctb)), 0, 0)
    fblk = pl.BlockSpec((tb, n, lanes), fwd_map)
    bblk = pl.BlockSpec((tb, n, lanes), bwd_map)
    out = jax.ShapeDtypeStruct((t, n, lanes), F32)
    return pl.pallas_call(
        functools.partial(_wkv_scan_body, tb),
        out_shape=(out, out),
        grid=(nt,),
        in_specs=[fblk] * 6 + [bblk] * 6,
        out_specs=(fblk, bblk),
        scratch_shapes=[pltpu.VMEM((n, n, lanes), F32), pltpu.VMEM((6, n, lanes), F32)],
        compiler_params=_cparams(("arbitrary",)),
        name="wkv7_scan",
    )(*ops, *ops)


def _to_chains(lay, x_fwd, x_bwd):
    h, n = A_HEADS, A_HEAD_DIM
    tr = lambda x: x.reshape(lay.B, lay.T, h, n).transpose(1, 3, 0, 2).reshape(lay.T, n, lay.B * h)
    return jnp.concatenate([tr(x_fwd), tr(x_bwd)], axis=-1)


def _from_chains(lay, y):
    h, n = A_HEADS, A_HEAD_DIM
    return y.reshape(lay.T, n, lay.B, h).transpose(2, 0, 3, 1).reshape(lay.B, lay.T, h * n)


def _rwkv_readout_body(yf_ref, yb_ref, bonus_ref, g_ref, lg_ref, lb_ref, bd_ref, o_ref):
    y = yf_ref[0] + yb_ref[0]
    bd = bd_ref[...]
    inv_n = 1.0 / A_HEAD_DIM
    mu = _dot2_exact_rhs(y, bd) * inv_n
    yc = y - mu
    var = _dot2_exact_rhs(yc * yc, bd) * inv_n
    yn = yc * lax.rsqrt(var + A_GN_EPS) * lg_ref[...] + lb_ref[...]
    o_ref[...] = ((yn + bonus_ref[0]) * g_ref[0]).astype(BF16)


def _rwkv_readout(lay, yf, yb, bonus, g, lnx_g, lnx_b):
    tm, w = lay.tm, A_WIDTH
    row = lambda i: (i, 0)
    full = lambda i: (0, 0)
    seq = lambda i: lay.seq_block(i) + (0,)
    return pl.pallas_call(
        _rwkv_readout_body,
        out_shape=jax.ShapeDtypeStruct((lay.rows, w), BF16),
        grid=(lay.ntiles,),
        in_specs=[pl.BlockSpec((1, tm, w), seq)] * 4 + [pl.BlockSpec((1, w), full)] * 2 + [pl.BlockSpec((w, w), full)],
        out_specs=pl.BlockSpec((tm, w), row),
        compiler_params=_cparams(("arbitrary",)),
        name="rwkv_readout",
    )(yf, yb, bonus, g, lnx_g.reshape(1, w), lnx_b.reshape(1, w), _head_block_diag(w, A_HEAD_DIM))


def _softmax_pv(q, keys, vals):
    scores = [_dot_nt(q, kk) for kk in keys]
    m = scores[0].max(axis=-1, keepdims=True)
    for s in scores[1:]:
        m = jnp.maximum(m, s.max(axis=-1, keepdims=True))
    l = None
    o = None
    for s, vv in zip(scores, vals):
        e = jnp.exp(s - m)
        ls = e.sum(axis=-1, keepdims=True)
        os_ = jnp.dot(e.astype(BF16), vv, preferred_element_type=F32)
        l = ls if l is None else l + ls
        o = os_ if o is None else o + os_
    return o / l


def _diff_attn_body(lay, lambda_init, q_ref, kc_ref, kl_ref, vc_ref, vl_ref, lam_ref, g_ref, o_ref):
    j = pl.program_id(1)
    lv = lam_ref[...]
    lam = (jnp.exp(jnp.sum(lv[0:1] * lv[1:2], axis=1, keepdims=True))
           - jnp.exp(jnp.sum(lv[2:3] * lv[3:4], axis=1, keepdims=True)) + lambda_init)

    def run(with_latent):
        for h in range(B_HEADS):
            vs = slice(h * B_V_DIM, (h + 1) * B_V_DIM)
            vals = [vc_ref[:, vs]] + ([vl_ref[:, vs]] if with_latent else [])
            outs = []
            for mi in range(2):
                cs = slice(h * B_V_DIM + mi * B_HEAD_DIM, h * B_V_DIM + (mi + 1) * B_HEAD_DIM)
                keys = [kc_ref[:, cs]] + ([kl_ref[:, cs]] if with_latent else [])
                outs.append(_softmax_pv(q_ref[:, cs], keys, vals))
            o = outs[0] - lam * outs[1]
            ms = jnp.mean(o * o, axis=-1, keepdims=True)
            o = o * lax.rsqrt(ms + B_SUBLN_EPS) * g_ref[...] * (1.0 - lambda_init)
            o_ref[:, vs] = o.astype(BF16)

    @pl.when(j < lay.nct)
    def _():
        run(False)

    @pl.when(j >= lay.nct)
    def _():
        run(True)


def _diff_attention(lay, q, k, v, lam_vecs, subln_g, lambda_init):
    tm = lay.tm
    w = B_WIDTH
    lat0 = lay.rows_c // lay.S
    return pl.pallas_call(
        functools.partial(_diff_attn_body, lay, lambda_init),
        out_shape=jax.ShapeDtypeStruct((lay.rows, w), BF16),
        grid=(lay.B, lay.nct + lay.nlt),
        in_specs=[pl.BlockSpec((tm, w), lambda b, j: (lay.seq_tile(b, j), 0)),
                  pl.BlockSpec((lay.CTX, w), lambda b, j: (b, 0)),
                  pl.BlockSpec((lay.S, w), lambda b, j: (lat0 + b, 0)),
                  pl.BlockSpec((lay.CTX, w), lambda b, j: (b, 0)),
                  pl.BlockSpec((lay.S, w), lambda b, j: (lat0 + b, 0)),
                  pl.BlockSpec((4, B_HEAD_DIM), lambda b, j: (0, 0)),
                  pl.BlockSpec((1, B_V_DIM), lambda b, j: (0, 0))],
        out_specs=pl.BlockSpec((tm, w), lambda b, j: (lay.seq_tile(b, j), 0)),
        compiler_params=_cparams(("arbitrary", "arbitrary")),
        name="diff_attention",
    )(q, k, k, v, v, lam_vecs, subln_g.reshape(1, B_V_DIM))


def _odd_in_body(x_ref, mod_ref, w_ref, cos_ref, sin_ref, qn_ref, kn_ref, q_ref, k_ref, v_ref):
    m = mod_ref[0]
    h = (x_ref[...] * (1.0 + m[1:2]) + m[0:1]).astype(BF16)
    cos, sin = cos_ref[...], sin_ref[...]

    def norm_rope(p, g, scale):
        ms = jnp.mean(p * p, axis=-1, keepdims=True)
        y = p * lax.rsqrt(ms + QK_NORM_EPS) * g
        return (_rope_lanes(y, cos, sin, C_HEAD_DIM) * scale).astype(BF16)

    for hd in range(C_HEADS):
        cs = slice(hd * C_HEAD_DIM, (hd + 1) * C_HEAD_DIM)
        p = jnp.dot(h, w_ref[:, cs], preferred_element_type=F32)
        q_ref[:, cs] = norm_rope(p, qn_ref[...], C_HEAD_DIM ** -0.5)
    for hd in range(C_KV_HEADS):
        cs = slice(hd * C_HEAD_DIM, (hd + 1) * C_HEAD_DIM)
        p = jnp.dot(h, w_ref[:, C_Q + hd * C_HEAD_DIM:C_Q + (hd + 1) * C_HEAD_DIM], preferred_element_type=F32)
        k_ref[:, cs] = norm_rope(p, kn_ref[...], 1.0)
    v_ref[...] = jnp.dot(h, w_ref[:, C_Q + C_KV:], preferred_element_type=F32).astype(BF16)


def _odd_in_proj(lay, x, mod, w_bf16, cos, sin, qn_g, kn_g):
    tm, d = lay.tm, D_MODEL
    row = lambda i: (i, 0)
    full = lambda i: (0, 0)
    return pl.pallas_call(
        _odd_in_body,
        out_shape=(jax.ShapeDtypeStruct((lay.rows, C_Q), BF16),
                   jax.ShapeDtypeStruct((lay.rows, C_KV), BF16),
                   jax.ShapeDtypeStruct((lay.rows, C_KV), BF16)),
        grid=(lay.ntiles,),
        in_specs=[pl.BlockSpec((tm, d), row),
                  pl.BlockSpec((1, 6, d), lambda i: (lay.mod_row(i, tm), 0, 0)),
                  pl.BlockSpec((d, ODD_IN), full),
                  pl.BlockSpec((tm, C_HEAD_DIM), lambda i: (lay.pos_tile(i), 0)),
                  pl.BlockSpec((tm, C_HEAD_DIM), lambda i: (lay.pos_tile(i), 0)),
                  pl.BlockSpec((1, C_HEAD_DIM), full),
                  pl.BlockSpec((1, C_HEAD_DIM), full)],
        out_specs=(pl.BlockSpec((tm, C_Q), row), pl.BlockSpec((tm, C_KV), row), pl.BlockSpec((tm, C_KV), row)),
        compiler_params=_cparams(("arbitrary",)),
        name="odd_in_proj",
    )(x, mod, w_bf16, cos, sin, qn_g.reshape(1, C_HEAD_DIM), kn_g.reshape(1, C_HEAD_DIM))


def _gqa_body(q_ref, kc_ref, kl_ref, vc_ref, vl_ref, o_ref):
    keys = [kc_ref[...], kl_ref[...]]
    vals = [vc_ref[...], vl_ref[...]]
    for g in range(C_GROUP):
        cs = slice(g * C_HEAD_DIM, (g + 1) * C_HEAD_DIM)
        o_ref[:, cs] = _softmax_pv(q_ref[:, cs], keys, vals).astype(BF16)


def _gqa_attention(lay, q, k, v):
    tm = lay.tm
    gw = C_GROUP * C_HEAD_DIM
    lat0 = lay.rows_c // lay.S
    hd = C_HEAD_DIM
    return pl.pallas_call(
        _gqa_body,
        out_shape=jax.ShapeDtypeStruct((lay.rows_l, C_Q), BF16),
        grid=(lay.B, C_KV_HEADS, lay.nlt),
        in_specs=[pl.BlockSpec((tm, gw), lambda b, h, j: (lay.ntiles_c + b * lay.nlt + j, h)),
                  pl.BlockSpec((lay.CTX, hd), lambda b, h, j: (b, h)),
                  pl.BlockSpec((lay.S, hd), lambda b, h, j: (lat0 + b, h)),
                  pl.BlockSpec((lay.CTX, hd), lambda b, h, j: (b, h)),
                  pl.BlockSpec((lay.S, hd), lambda b, h, j: (lat0 + b, h))],
        out_specs=pl.BlockSpec((tm, gw), lambda b, h, j: (b * lay.nlt + j, h)),
        compiler_params=_cparams(("arbitrary", "arbitrary", "arbitrary")),
        name="gqa_attention",
    )(q, k, k, v, v)


def _out_proj_body(n_mix, *refs):
    mix_refs = refs[:n_mix]
    w_ref, x_ref, mod_ref, lg_ref, lb_ref, rw_ref, xo_ref, h_ref, lt_ref = refs[n_mix:]
    m = mod_ref[0]
    off = 0
    mix = None
    for mr in mix_refs:
        kw = mr.shape[-1]
        part = jnp.dot(mr[...], w_ref[off:off + kw, :], preferred_element_type=F32)
        mix = part if mix is None else mix + part
        off += kw
    z = DEEPNORM_ALPHA * x_ref[...] + m[2:3] * mix
    xn = _layer_norm_rows(z, lg_ref[...], lb_ref[...])
    xo_ref[...] = xn
    h2 = xn * (1.0 + m[4:5]) + m[3:4]
    h_ref[...] = h2.astype(BF16)
    lt_ref[...] = _dot3_nt(rw_ref[...], h2)


def _out_proj(lay, mixes, w_bf16, x, x_tile_off, mod, ln_g, ln_b, router_wt, n_rows):
    tm, d = lay.tm, D_MODEL
    row = lambda i: (i, 0)
    full = lambda i: (0, 0)
    row_off = x_tile_off * tm
    in_specs = [pl.BlockSpec((tm, mx.shape[-1]), row) for mx in mixes]
    in_specs += [pl.BlockSpec((d, d), full),
                 pl.BlockSpec((tm, d), lambda i: (i + x_tile_off, 0)),
                 pl.BlockSpec((1, 6, d), lambda i: (lay.mod_row(i, tm, row_off), 0, 0)),
                 pl.BlockSpec((1, d), full), pl.BlockSpec((1, d), full),
                 pl.BlockSpec((N_EXPERTS, d), full)]
    return pl.pallas_call(
        functools.partial(_out_proj_body, len(mixes)),
        out_shape=(jax.ShapeDtypeStruct((n_rows, d), F32),
                   jax.ShapeDtypeStruct((n_rows, d), BF16),
                   jax.ShapeDtypeStruct((N_EXPERTS, n_rows), F32)),
        grid=(n_rows // tm,),
        in_specs=in_specs,
        out_specs=(pl.BlockSpec((tm, d), row), pl.BlockSpec((tm, d), row),
                   pl.BlockSpec((N_EXPERTS, tm), lambda i: (0, i))),
        compiler_params=_cparams(("arbitrary",)),
        name="out_proj_ln",
    )(*mixes, w_bf16, x, mod, ln_g.reshape(1, d), ln_b.reshape(1, d), router_wt)


def _router_body(lt_ref, rb_ref, g_ref):
    logits = lt_ref[...] + rb_ref[...]
    rows = [logits[e:e + 1, :] for e in range(N_EXPERTS)]
    m = rows[0]
    for x in rows[1:]:
        m = jnp.maximum(m, x)
    ex = [jnp.exp(x - m) for x in rows]
    z = ex[0]
    for x in ex[1:]:
        z = z + x
    p = [x / z for x in ex]

    gscore = []
    for g in range(N_GROUPS):
        a, b, c, d = p[4 * g:4 * g + 4]
        hi1, lo1 = jnp.maximum(a, b), jnp.minimum(a, b)
        hi2, lo2 = jnp.maximum(c, d), jnp.minimum(c, d)
        top1 = jnp.maximum(hi1, hi2)
        top2 = jnp.maximum(jnp.minimum(hi1, hi2), jnp.maximum(lo1, lo2))
        gscore.append(top1 + top2)
    best = []
    for g in range(N_GROUPS):
        ok = None
        for o in range(N_GROUPS):
            if o == g:
                continue
            c = (gscore[g] > gscore[o]) if o < g else (gscore[g] >= gscore[o])
            ok = c if ok is None else jnp.logical_and(ok, c)
        best.append(ok)
    sel = []
    for e in range(N_EXPERTS):
        g = e // EXPERTS_PER_GROUP
        rank = jnp.zeros_like(p[e])
        for o in range(4 * g, 4 * g + 4):
            if o == e:
                continue
            ahead = (p[o] > p[e]) if o > e else (p[o] >= p[e])
            rank = rank + jnp.where(ahead, 1.0, 0.0)
        sel.append(jnp.where(jnp.logical_and(best[g], rank < 1.5), p[e], 0.0))
    tot = sel[0]
    for x in sel[1:]:
        tot = tot + x
    for e in range(N_EXPERTS):
        g_ref[e:e + 1, :] = sel[e] / tot


def _router(logits_t, router_b):
    e, n = logits_t.shape
    tr = math.gcd(2048, n)
    return pl.pallas_call(
        _router_body,
        out_shape=jax.ShapeDtypeStruct((e, n), F32),
        grid=(n // tr,),
        in_specs=[pl.BlockSpec((e, tr), lambda i: (0, i)), pl.BlockSpec((e, 1), lambda i: (0, 0))],
        out_specs=pl.BlockSpec((e, tr), lambda i: (0, i)),
        compiler_params=_cparams(("arbitrary",)),
        name="router_gates",
    )(logits_t, router_b.reshape(e, 1))


def _moe_body(sub, h_ref, gate_ref, w1_ref, w3_ref, w2_ref, x_ref, mod_ref, lg_ref, lb_ref, o_ref, acc_ref):
    e = pl.program_id(1)
    tmm = h_ref.shape[0]

    @pl.when(e == 0)
    def _():
        acc_ref[...] = jnp.zeros_like(acc_ref)

    onehot = jnp.where(lax.broadcasted_iota(jnp.int32, (1, N_EXPERTS), 1) == e, 1.0, 0.0)
    w1, w3, w2 = w1_ref[0], w3_ref[0], w2_ref[0]

    def blk(r, carry):
        rs = pl.ds(pl.multiple_of(r * sub, sub), sub)
        h = h_ref[rs, :]
        gcol = jnp.sum(gate_ref[rs, :] * onehot, axis=1, keepdims=True)
        h1 = jnp.dot(h, w1, preferred_element_type=F32)
        h3 = jnp.dot(h, w3, preferred_element_type=F32)
        hid = (h1 * _sigmoid(h1) * h3 * gcol).astype(BF16)
        acc_ref[rs, :] += jnp.dot(hid, w2, preferred_element_type=F32)
        return carry

    lax.fori_loop(0, tmm // sub, blk, 0)

    @pl.when(e == N_EXPERTS - 1)
    def _():
        m = mod_ref[0]
        z = DEEPNORM_ALPHA * x_ref[...] + m[5:6] * acc_ref[...]
        o_ref[...] = _layer_norm_rows(z, lg_ref[...], lb_ref[...])


def _moe(lay, h2, gates, w1, w3, w2, x, mod, ln_g, ln_b, row_off):
    n_rows, d = h2.shape
    tmm = lay.tmm
    sub = math.gcd(256, tmm)
    row = lambda i, e: (i, 0)
    full = lambda i, e: (0, 0)
    return pl.pallas_call(
        functools.partial(_moe_body, sub),
        out_shape=jax.ShapeDtypeStruct((n_rows, d), F32),
        grid=(n_rows // tmm, N_EXPERTS),
        in_specs=[pl.BlockSpec((tmm, d), row),
                  pl.BlockSpec((tmm, N_EXPERTS), row),
                  pl.BlockSpec((1, d, EXPERT_FF), lambda i, e: (e, 0, 0)),
                  pl.BlockSpec((1, d, EXPERT_FF), lambda i, e: (e, 0, 0)),
                  pl.BlockSpec((1, EXPERT_FF, d), lambda i, e: (e, 0, 0)),
                  pl.BlockSpec((tmm, d), row),
                  pl.BlockSpec((1, 6, d), lambda i, e: (lay.mod_row(i, tmm, row_off), 0, 0)),
                  pl.BlockSpec((1, d), full), pl.BlockSpec((1, d), full)],
        out_specs=pl.BlockSpec((tmm, d), row),
        scratch_shapes=[pltpu.VMEM((tmm, d), F32)],
        compiler_params=_cparams(("arbitrary", "arbitrary")),
        name="moe_experts",
    )(h2, gates, w1, w3, w2, x, mod, ln_g.reshape(1, d), ln_b.reshape(1, d))


def kernel(x, c, ctx, c_ctx, router_w, router_b, ada_w, ada_b, ln1_g, ln1_b, ln2_g, ln2_b, moe_w1, moe_w3, moe_w2, ev_w_in, ev_w_out, ev_a_mu, ev_a_w0, ev_a_w2, ev_a_a0, ev_a_a2, ev_a_g2, ev_a_kk, ev_a_ka, ev_a_rk, ev_a_lnx_g, ev_a_lnx_b, ev_b_lam, ev_b_subln_g, od_w_in, od_w_out, od_qn_g, od_kn_g):
    bsz, seq, d = x.shape
    ctx_len = ctx.shape[1]
    assert d == D_MODEL and ada_w.shape[0] == DEPTH and seq % GRID_W == 0
    lay = _Layout(bsz, ctx_len, seq)

    cvec = jnp.zeros((lay.mod_rows, d), F32).at[:bsz].set(c).at[bsz].set(c_ctx)
    mods = _ada_mods(cvec, ada_w, ada_b).reshape(DEPTH, lay.mod_rows, 6, d)

    xs = jnp.concatenate([ctx.reshape(lay.rows_c, d), x.reshape(lay.rows_l, d)], axis=0)
    router_wt = router_w.T

    cos_b, sin_b = _rope_tables(lay, B_HEAD_DIM, B_QK)
    cos_c, sin_c = _rope_tables(lay, C_HEAD_DIM, C_HEAD_DIM)

    for i in range(DEPTH):
        last = i == DEPTH - 1
        j = i // 2
        mod = mods[i]
        if i % 2 == 0:
            lambda_init = 0.8 - 0.6 * math.exp(-0.3 * i)
            pa, q, k, v = _even_in_proj(lay, xs, mod, ev_w_in[j].astype(BF16), cos_b, sin_b)
            (r_, v_, nkk, g_, bonus, w_f, w_b, kd_f, kd_b, b_f, b_b) = _rwkv_features(
                lay, pa, ev_a_mu[j], ev_a_w0[j], ev_a_w2[j], ev_a_a0[j], ev_a_a2[j], ev_a_g2[j],
                ev_a_kk[j], ev_a_ka[j], ev_a_rk[j].reshape(-1))

            both = functools.partial(_to_chains, lay)
            y_f, y_b = _wkv_scan(lay, (both(r_, r_), both(w_f, w_b), both(kd_f, kd_b), both(v_, v_),
                                       both(nkk, nkk), both(b_f, b_b)))
            nl = lay.B * A_HEADS
            yf = _from_chains(lay, y_f[:, :, :nl])
            yb = _from_chains(lay, y_b[:, :, nl:])
            ya = _rwkv_readout(lay, yf, yb, bonus, g_, ev_a_lnx_g[j], ev_a_lnx_b[j])
            yd = _diff_attention(lay, q, k, v, ev_b_lam[j], ev_b_subln_g[j], lambda_init)
            mixes, w_out = [ya, yd], ev_w_out[j]
            n_rows, tile_off = (lay.rows_l, lay.ntiles_c) if last else (lay.rows, 0)
            if last:
                mixes = [mx[lay.rows_c:] for mx in mixes]
        else:
            q, k, v = _odd_in_proj(lay, xs, mod, od_w_in[j].astype(BF16), cos_c, sin_c, od_qn_g[j], od_kn_g[j])
            assert last, "an odd layer that must also update the context stream is not supported"
            o = _gqa_attention(lay, q, k, v)
            mixes, w_out = [o], od_w_out[j]
            n_rows, tile_off = lay.rows_l, lay.ntiles_c
        row_off = tile_off * lay.tm
        x_new, h2, logits_t = _out_proj(lay, mixes, w_out.astype(BF16), xs, tile_off, mod,
                                        ln1_g[i], ln1_b[i], router_wt, n_rows)
        gates = _router(logits_t, router_b).T
        xs = _moe(lay, h2, gates, moe_w1[i].astype(BF16), moe_w3[i].astype(BF16), moe_w2[i].astype(BF16),
                  x_new, mod, ln2_g[i], ln2_b[i], row_off)
    return xs.reshape(bsz, seq, d)
```

```python
import functools
import math

import jax
import jax.numpy as jnp
from jax import lax
from jax.experimental import pallas as pl
from jax.experimental.pallas import tpu as pltpu

F32 = jnp.float32
BF16 = jnp.bfloat16

D_MODEL = 1024
DEPTH = 2
GRID_W = 64
ROPE_THETA = 10000.0
LN_EPS = 1e-5
DEEPNORM_ALPHA = (2 * DEPTH) ** 0.25

A_HEAD_DIM = 64
A_HEADS = 8
A_WIDTH = 512
A_LORA = 64
A_GATE_LORA = 128
A_GN_EPS = 64e-5
A_IN = 3 * A_WIDTH + 4 * A_LORA + A_GATE_LORA

B_HEAD_DIM = 64
B_V_DIM = 128
B_HEADS = 4
B_WIDTH = 512
B_QK = 512
B_SUBLN_EPS = 1e-5
EVEN_IN = A_IN + 2 * B_QK + B_WIDTH

C_HEAD_DIM = 128
C_HEADS = 8
C_KV_HEADS = 2
C_GROUP = 4
C_Q = 1024
C_KV = 256
ODD_IN = C_Q + 2 * C_KV
QK_NORM_EPS = 1e-6

N_EXPERTS = 16
N_GROUPS = 4
EXPERTS_PER_GROUP = 4
EXPERT_FF = 512

VMEM_LIMIT_BYTES = 56 * 1024 * 1024
LANES = 128
SUBLANES = 8


def _cparams(sem):
    return pltpu.CompilerParams(dimension_semantics=sem, vmem_limit_bytes=VMEM_LIMIT_BYTES)


def _dot(a, b):
    return jnp.dot(a.astype(BF16), b.astype(BF16), preferred_element_type=F32)


def _dot_nt(a, b):
    return lax.dot_general(a.astype(BF16), b.astype(BF16), (((1,), (1,)), ((), ())),
                           preferred_element_type=F32)


def _split(a):
    hi = a.astype(BF16)
    lo = (a - hi.astype(F32)).astype(BF16)
    return hi, lo


def _dot3(a, b):
    ah, al = _split(a)
    bh, bl = _split(b)
    return (jnp.dot(ah, bh, preferred_element_type=F32)
            + (jnp.dot(ah, bl, preferred_element_type=F32)
               + jnp.dot(al, bh, preferred_element_type=F32)))


def _dot2_exact_rhs(a, b_bf16):
    ah, al = _split(a)
    return jnp.dot(ah, b_bf16, preferred_element_type=F32) + jnp.dot(al, b_bf16, preferred_element_type=F32)


def _dot3_nt(a, b):
    ah, al = _split(a)
    bh, bl = _split(b)
    dn = (((1,), (1,)), ((), ()))
    return (lax.dot_general(ah, bh, dn, preferred_element_type=F32)
            + (lax.dot_general(ah, bl, dn, preferred_element_type=F32)
               + lax.dot_general(al, bh, dn, preferred_element_type=F32)))


def _sigmoid(x):
    return 1.0 / (1.0 + jnp.exp(-x))


def _layer_norm_rows(z, g, b):
    mu = jnp.mean(z, axis=-1, keepdims=True)
    zc = z - mu
    var = jnp.mean(zc * zc, axis=-1, keepdims=True)
    return zc * lax.rsqrt(var + LN_EPS) * g + b


class _Layout:
    def __init__(self, bsz, ctx_len, seq):
        self.B, self.CTX, self.S = bsz, ctx_len, seq
        self.T = ctx_len + seq
        self.tm = math.gcd(256, math.gcd(ctx_len, seq))
        self.nct = ctx_len // self.tm
        self.nlt = seq // self.tm
        self.rows_c = bsz * ctx_len
        self.rows_l = bsz * seq
        self.rows = self.rows_c + self.rows_l
        self.ntiles_c = bsz * self.nct
        self.ntiles = self.rows // self.tm
        assert self.rows_c % seq == 0, "latent K/V blocks are addressed in units of S rows"
        self.tmm = math.gcd(1024, math.gcd(self.rows_c, seq))
        self.mod_rows = -(-(bsz + 1) // SUBLANES) * SUBLANES

    def seq_tile(self, b, j):
        return jnp.where(j < self.nct, b * self.nct + j, self.ntiles_c + b * self.nlt + (j - self.nct))

    def seq_block(self, i):
        il = i - self.ntiles_c
        return (jnp.where(i < self.ntiles_c, i // self.nct, il // self.nlt),
                jnp.where(i < self.ntiles_c, i % self.nct, self.nct + il % self.nlt))

    def mod_row(self, i, tile, row_offset=0):
        r = i * tile + row_offset
        return jnp.where(r < self.rows_c, self.B, (r - self.rows_c) // self.S)

    def pos_tile(self, i):
        il = i - self.ntiles_c
        return jnp.where(i < self.ntiles_c, i % self.nct, self.nct + il % self.nlt)


def _ada_body(cv_ref, w_ref, b_ref, o_ref):
    cv = cv_ref[...]
    s = cv * _sigmoid(cv)
    o_ref[0] = _dot3(s, w_ref[0]) + b_ref[0]


def _ada_mods(cvec, ada_w, ada_b):
    depth, d, n = ada_w.shape
    r = cvec.shape[0]
    tn = 512
    return pl.pallas_call(
        _ada_body,
        out_shape=jax.ShapeDtypeStruct((depth, r, n), F32),
        grid=(depth, n // tn),
        in_specs=[pl.BlockSpec((r, d), lambda l, j: (0, 0)),
                  pl.BlockSpec((1, d, tn), lambda l, j: (l, 0, j)),
                  pl.BlockSpec((1, 1, tn), lambda l, j: (l, 0, j))],
        out_specs=pl.BlockSpec((1, r, tn), lambda l, j: (l, 0, j)),
        compiler_params=_cparams(("arbitrary", "arbitrary")),
        name="ada_mods",
    )(cvec, ada_w, ada_b.reshape(depth, 1, n))


def _rope_tables(lay, head_dim, width):
    rows = lay.S // GRID_W
    rr, cc = jnp.meshgrid(jnp.arange(rows), jnp.arange(GRID_W), indexing="ij")
    row_pos = rr.reshape(-1).astype(F32)
    col_pos = cc.reshape(-1).astype(F32)
    axis_dim = head_dim // 2
    inv = ROPE_THETA ** (-jnp.arange(0, axis_dim, 2, dtype=F32) / axis_dim)
    ang = jnp.concatenate([row_pos[:, None] * inv, col_pos[:, None] * inv], -1)
    cos, sin = jnp.cos(ang), jnp.sin(ang)
    cos = jnp.concatenate([jnp.ones((lay.CTX, head_dim // 2), F32), cos], 0)
    sin = jnp.concatenate([jnp.zeros((lay.CTX, head_dim // 2), F32), sin], 0)
    cos_h = jnp.concatenate([cos, cos], -1)
    sin_h = jnp.concatenate([-sin, sin], -1)
    reps = width // head_dim
    return jnp.tile(cos_h, (1, reps)), jnp.tile(sin_h, (1, reps))


def _rope_lanes(x, cos, sin, head_dim):
    w = x.shape[-1]
    half = head_dim // 2
    if head_dim == LANES and w == LANES:
        rot = pltpu.roll(x, half, 1)
    else:
        fwd = pltpu.roll(x, w - half, 1)
        bwd = pltpu.roll(x, half, 1)
        lane = lax.broadcasted_iota(jnp.int32, x.shape, 1)
        rot = jnp.where((lane % head_dim) < half, fwd, bwd)
    return x * cos + rot * sin


def _even_in_body(x_ref, mod_ref, w_ref, cos_ref, sin_ref, pa_ref, q_ref, k_ref, v_ref):
    m = mod_ref[0]
    h = (x_ref[...] * (1.0 + m[1:2]) + m[0:1]).astype(BF16)
    pa_ref[...] = jnp.dot(h, w_ref[:, :A_IN], preferred_element_type=F32)
    cos, sin = cos_ref[...], sin_ref[...]
    o = A_IN
    q = jnp.dot(h, w_ref[:, o:o + B_QK], preferred_element_type=F32)
    q_ref[...] = (_rope_lanes(q, cos, sin, B_HEAD_DIM) * (B_HEAD_DIM ** -0.5)).astype(BF16)
    o += B_QK
    k = jnp.dot(h, w_ref[:, o:o + B_QK], preferred_element_type=F32)
    k_ref[...] = _rope_lanes(k, cos, sin, B_HEAD_DIM).astype(BF16)
    o += B_QK
    v_ref[...] = jnp.dot(h, w_ref[:, o:o + B_WIDTH], preferred_element_type=F32).astype(BF16)


def _even_in_proj(lay, x, mod, w_bf16, cos, sin):
    tm, d = lay.tm, D_MODEL
    row = lambda i: (i, 0)
    return pl.pallas_call(
        _even_in_body,
        out_shape=(jax.ShapeDtypeStruct((lay.rows, A_IN), F32),
                   jax.ShapeDtypeStruct((lay.rows, B_QK), BF16),
                   jax.ShapeDtypeStruct((lay.rows, B_QK), BF16),
                   jax.ShapeDtypeStruct((lay.rows, B_WIDTH), BF16)),
        grid=(lay.ntiles,),
        in_specs=[pl.BlockSpec((tm, d), row),
                  pl.BlockSpec((1, 6, d), lambda i: (lay.mod_row(i, tm), 0, 0)),
                  pl.BlockSpec((d, EVEN_IN), lambda i: (0, 0)),
                  pl.BlockSpec((tm, B_QK), lambda i: (lay.pos_tile(i), 0)),
                  pl.BlockSpec((tm, B_QK), lambda i: (lay.pos_tile(i), 0))],
        out_specs=(pl.BlockSpec((tm, A_IN), row), pl.BlockSpec((tm, B_QK), row),
                   pl.BlockSpec((tm, B_QK), row), pl.BlockSpec((tm, B_WIDTH), row)),
        compiler_params=_cparams(("arbitrary",)),
        name="even_in_proj",
    )(x, mod, w_bf16, cos, sin)


def _rwkv_feat_body(lay, pa_ref, prev_ref, next_ref, mu_ref, w0_ref, w2_ref, a0_ref, a2_ref, g2_ref,
                    kk_ref, ka_ref, rk_ref, bd_ref,
                    r_ref, v_ref, nkk_ref, g_ref, bonus_ref, w_f_ref, w_b_ref, kd_f_ref, kd_b_ref,
                    b_f_ref, b_b_ref):
    i = pl.program_id(0)
    tm = lay.tm
    il = i - lay.ntiles_c
    in_ctx = i < lay.ntiles_c
    seg_first = jnp.where(in_ctx, i % lay.nct == 0, il % lay.nlt == 0)
    seg_last = jnp.where(in_ctx, i % lay.nct == lay.nct - 1, il % lay.nlt == lay.nlt - 1)

    pa = pa_ref[...]
    row = lax.broadcasted_iota(jnp.int32, pa.shape, 0)
    prev_edge = jnp.where(seg_first, 0.0, 1.0) * prev_ref[SUBLANES - 1:SUBLANES, :]
    next_edge = jnp.where(seg_last, 0.0, 1.0) * next_ref[0:1, :]
    prev = jnp.where(row == 0, prev_edge, pltpu.roll(pa, 1, 0))
    nxt = jnp.where(row == tm - 1, next_edge, pltpu.roll(pa, tm - 1, 0))
    u = pa + (0.5 * (prev + nxt) - pa) * mu_ref[...]

    o1, o2, o3 = A_WIDTH, 2 * A_WIDTH, 3 * A_WIDTH
    o4 = o3 + 2 * A_LORA
    o5 = o4 + 2 * A_LORA
    r, k, v = u[:, :o1], u[:, o1:o2], u[:, o2:o3]
    bd = bd_ref[...]

    kk = k * kk_ref[...]
    ss = _dot2_exact_rhs(kk * kk, bd)
    kkn = kk / jnp.maximum(jnp.sqrt(ss), 1e-12)
    g = _dot3(_sigmoid(u[:, o5:]), g2_ref[...])

    r_ref[0] = r
    v_ref[0] = v
    nkk_ref[0] = -kkn
    g_ref[0] = g

    kd_sum = jnp.zeros_like(k)
    for d, (w_ref_o, kd_ref_o, b_ref_o) in enumerate(((w_f_ref, kd_f_ref, b_f_ref), (w_b_ref, kd_b_ref, b_b_ref))):
        wd = u[:, o3 + d * A_LORA:o3 + (d + 1) * A_LORA]
        ad = u[:, o4 + d * A_LORA:o4 + (d + 1) * A_LORA]
        z = -(w0_ref[d:d + 1, :] + _dot3(jnp.tanh(wd), w2_ref[d]))
        softplus = jnp.maximum(z, 0.0) + jnp.log(1.0 + jnp.exp(-jnp.abs(z)))
        w_log = -softplus - 0.5
        w_ref_o[0] = jnp.exp(-jnp.exp(w_log))
        a = _sigmoid(a0_ref[d:d + 1, :] + _dot3(ad, a2_ref[d]))
        kd = k * (1.0 + (a - 1.0) * ka_ref[...])
        kd_ref_o[0] = kd
        b_ref_o[0] = kkn * a
        kd_sum = kd_sum + kd
    bonus_ref[0] = _dot2_exact_rhs(r * kd_sum * rk_ref[...], bd) * v


def _head_block_diag(width, head_dim):
    h = jnp.arange(width) // head_dim
    return (h[:, None] == h[None, :]).astype(BF16)


def _rwkv_features(lay, pa, mu, w0, w2, a0, a2, g2, k_k, k_a, r_k):
    tm = lay.tm
    hb = tm // SUBLANES
    nb8 = lay.rows // SUBLANES
    row = lambda i: (i, 0)
    full2 = lambda i: (0, 0)
    full3 = lambda i: (0, 0, 0)
    w = A_WIDTH
    out = jax.ShapeDtypeStruct((lay.B, lay.T, w), F32)
    seq = lambda i: lay.seq_block(i) + (0,)
    return pl.pallas_call(
        functools.partial(_rwkv_feat_body, lay),
        out_shape=(out,) * 11,
        grid=(lay.ntiles,),
        in_specs=[pl.BlockSpec((tm, A_IN), row),
                  pl.BlockSpec((SUBLANES, A_IN), lambda i: (jnp.maximum(i * hb - 1, 0), 0)),
                  pl.BlockSpec((SUBLANES, A_IN), lambda i: (jnp.minimum((i + 1) * hb, nb8 - 1), 0)),
                  pl.BlockSpec((1, A_IN), full2),
                  pl.BlockSpec((2, w), full2),
                  pl.BlockSpec((2, A_LORA, w), full3),
                  pl.BlockSpec((2, w), full2),
                  pl.BlockSpec((2, A_LORA, w), full3),
                  pl.BlockSpec((A_GATE_LORA, w), full2),
                  pl.BlockSpec((1, w), full2),
                  pl.BlockSpec((1, w), full2),
                  pl.BlockSpec((1, w), full2),
                  pl.BlockSpec((w, w), full2)],
        out_specs=(pl.BlockSpec((1, tm, w), seq),) * 11,
        compiler_params=_cparams(("arbitrary",)),
        name="rwkv_features",
    )(pa, pa, pa, mu.reshape(1, A_IN), w0, w2, a0, a2, g2, k_k.reshape(1, w), k_a.reshape(1, w),
      r_k.reshape(1, w), _head_block_diag(w, A_HEAD_DIM))


_SCAN_R, _SCAN_W, _SCAN_K, _SCAN_V, _SCAN_A, _SCAN_B = range(6)


def _wkv_scan_body(tb, *refs):
    fwd, bwd = refs[:6], refs[6:12]
    yf_ref, yb_ref, s_ref, m_ref = refs[12:]
    n = A_HEAD_DIM
    chains = s_ref.shape[-1]
    is_fwd = lax.broadcasted_iota(jnp.int32, (n, chains), 1) < chains // 2

    @pl.when(pl.program_id(0) == 0)
    def _():
        s_ref[...] = jnp.zeros_like(s_ref)

    def step(t, carry):
        tr = tb - 1 - t
        for idx in range(6):
            m_ref[idx] = jnp.where(is_fwd, fwd[idx][t], bwd[idx][tr])
        sa = jnp.zeros((n, chains), F32)
        for kk in range(n):
            sa = sa + s_ref[kk] * m_ref[_SCAN_A, kk:kk + 1, :]
        v_t = m_ref[_SCAN_V]
        y = jnp.zeros((n, chains), F32)
        for kk in range(n):
            s_new = (s_ref[kk] * m_ref[_SCAN_W, kk:kk + 1, :] + sa * m_ref[_SCAN_B, kk:kk + 1, :]
                     + v_t * m_ref[_SCAN_K, kk:kk + 1, :])
            s_ref[kk] = s_new
            y = y + s_new * m_ref[_SCAN_R, kk:kk + 1, :]
        yf_ref[t] = y
        yb_ref[tr] = y
        return carry

    lax.fori_loop(0, tb, step, 0)


def _wkv_scan(lay, ops):
    t, n, lanes = ops[0].shape
    tb = math.gcd(32, math.gcd(lay.CTX, lay.S))
    nctb, nt = lay.CTX // tb, t // tb
    fwd_map = lambda g: (g, 0, 0)
    bwd_map = lambda g: (jnp.where(g < nctb, nctb - 1 - g, nt - 1 - (g - nctb)), 0, 0)
    fblk = pl.BlockSpec((tb, n, lanes), fwd_map)
    bblk = pl.BlockSpec((tb, n, lanes), bwd_map)
    out = jax.ShapeDtypeStruct((t, n, lanes), F32)
    return pl.pallas_call(
        functools.partial(_wkv_scan_body, tb),
        out_shape=(out, out),
        grid=(nt,),
        in_specs=[fblk] * 6 + [bblk] * 6,
        out_specs=(fblk, bblk),
        scratch_shapes=[pltpu.VMEM((n, n, lanes), F32), pltpu.VMEM((6, n, lanes), F32)],
        compiler_params=_cparams(("arbitrary",)),
        name="wkv7_scan",
    )(*ops, *ops)


def _to_chains(lay, x_fwd, x_bwd):
    h, n = A_HEADS, A_HEAD_DIM
    tr = lambda x: x.reshape(lay.B, lay.T, h, n).transpose(1, 3, 0, 2).reshape(lay.T, n, lay.B * h)
    return jnp.concatenate([tr(x_fwd), tr(x_bwd)], axis=-1)


def _from_chains(lay, y):
    h, n = A_HEADS, A_HEAD_DIM
    return y.reshape(lay.T, n, lay.B, h).transpose(2, 0, 3, 1).reshape(lay.B, lay.T, h * n)


def _rwkv_readout_body(yf_ref, yb_ref, bonus_ref, g_ref, lg_ref, lb_ref, bd_ref, o_ref):
    y = yf_ref[0] + yb_ref[0]
    bd = bd_ref[...]
    inv_n = 1.0 / A_HEAD_DIM
    mu = _dot2_exact_rhs(y, bd) * inv_n
    yc = y - mu
    var = _dot2_exact_rhs(yc * yc, bd) * inv_n
    yn = yc * lax.rsqrt(var + A_GN_EPS) * lg_ref[...] + lb_ref[...]
    o_ref[...] = ((yn + bonus_ref[0]) * g_ref[0]).astype(BF16)


def _rwkv_readout(lay, yf, yb, bonus, g, lnx_g, lnx_b):
    tm, w = lay.tm, A_WIDTH
    row = lambda i: (i, 0)
    full = lambda i: (0, 0)
    seq = lambda i: lay.seq_block(i) + (0,)
    return pl.pallas_call(
        _rwkv_readout_body,
        out_shape=jax.ShapeDtypeStruct((lay.rows, w), BF16),
        grid=(lay.ntiles,),
        in_specs=[pl.BlockSpec((1, tm, w), seq)] * 4 + [pl.BlockSpec((1, w), full)] * 2 + [pl.BlockSpec((w, w), full)],
        out_specs=pl.BlockSpec((tm, w), row),
        compiler_params=_cparams(("arbitrary",)),
        name="rwkv_readout",
    )(yf, yb, bonus, g, lnx_g.reshape(1, w), lnx_b.reshape(1, w), _head_block_diag(w, A_HEAD_DIM))


def _softmax_pv(q, keys, vals):
    scores = [_dot_nt(q, kk) for kk in keys]
    m = scores[0].max(axis=-1, keepdims=True)
    for s in scores[1:]:
        m = jnp.maximum(m, s.max(axis=-1, keepdims=True))
    l = None
    o = None
    for s, vv in zip(scores, vals):
        e = jnp.exp(s - m)
        ls = e.sum(axis=-1, keepdims=True)
        os_ = jnp.dot(e.astype(BF16), vv, preferred_element_type=F32)
        l = ls if l is None else l + ls
        o = os_ if o is None else o + os_
    return o / l


def _diff_attn_body(lay, lambda_init, q_ref, kc_ref, kl_ref, vc_ref, vl_ref, lam_ref, g_ref, o_ref):
    j = pl.program_id(1)
    lv = lam_ref[...]
    lam = (jnp.exp(jnp.sum(lv[0:1] * lv[1:2], axis=1, keepdims=True))
           - jnp.exp(jnp.sum(lv[2:3] * lv[3:4], axis=1, keepdims=True)) + lambda_init)

    def run(with_latent):
        for h in range(B_HEADS):
            vs = slice(h * B_V_DIM, (h + 1) * B_V_DIM)
            vals = [vc_ref[:, vs]] + ([vl_ref[:, vs]] if with_latent else [])
            outs = []
            for mi in range(2):
                cs = slice(h * B_V_DIM + mi * B_HEAD_DIM, h * B_V_DIM + (mi + 1) * B_HEAD_DIM)
                keys = [kc_ref[:, cs]] + ([kl_ref[:, cs]] if with_latent else [])
                outs.append(_softmax_pv(q_ref[:, cs], keys, vals))
            o = outs[0] - lam * outs[1]
            ms = jnp.mean(o * o, axis=-1, keepdims=True)
            o = o * lax.rsqrt(ms + B_SUBLN_EPS) * g_ref[...] * (1.0 - lambda_init)
            o_ref[:, vs] = o.astype(BF16)

    @pl.when(j < lay.nct)
    def _():
        run(False)

    @pl.when(j >= lay.nct)
    def _():
        run(True)


def _diff_attention(lay, q, k, v, lam_vecs, subln_g, lambda_init):
    tm = lay.tm
    w = B_WIDTH
    lat0 = lay.rows_c // lay.S
    return pl.pallas_call(
        functools.partial(_diff_attn_body, lay, lambda_init),
        out_shape=jax.ShapeDtypeStruct((lay.rows, w), BF16),
        grid=(lay.B, lay.nct + lay.nlt),
        in_specs=[pl.BlockSpec((tm, w), lambda b, j: (lay.seq_tile(b, j), 0)),
                  pl.BlockSpec((lay.CTX, w), lambda b, j: (b, 0)),
                  pl.BlockSpec((lay.S, w), lambda b, j: (lat0 + b, 0)),
                  pl.BlockSpec((lay.CTX, w), lambda b, j: (b, 0)),
                  pl.BlockSpec((lay.S, w), lambda b, j: (lat0 + b, 0)),
                  pl.BlockSpec((4, B_HEAD_DIM), lambda b, j: (0, 0)),
                  pl.BlockSpec((1, B_V_DIM), lambda b, j: (0, 0))],
        out_specs=pl.BlockSpec((tm, w), lambda b, j: (lay.seq_tile(b, j), 0)),
        compiler_params=_cparams(("arbitrary", "arbitrary")),
        name="diff_attention",
    )(q, k, k, v, v, lam_vecs, subln_g.reshape(1, B_V_DIM))


def _odd_in_body(x_ref, mod_ref, w_ref, cos_ref, sin_ref, qn_ref, kn_ref, q_ref, k_ref, v_ref):
    m = mod_ref[0]
    h = (x_ref[...] * (1.0 + m[1:2]) + m[0:1]).astype(BF16)
    cos, sin = cos_ref[...], sin_ref[...]

    def norm_rope(p, g, scale):
        ms = jnp.mean(p * p, axis=-1, keepdims=True)
        y = p * lax.rsqrt(ms + QK_NORM_EPS) * g
        return (_rope_lanes(y, cos, sin, C_HEAD_DIM) * scale).astype(BF16)

    for hd in range(C_HEADS):
        cs = slice(hd * C_HEAD_DIM, (hd + 1) * C_HEAD_DIM)
        p = jnp.dot(h, w_ref[:, cs], preferred_element_type=F32)
        q_ref[:, cs] = norm_rope(p, qn_ref[...], C_HEAD_DIM ** -0.5)
    for hd in range(C_KV_HEADS):
        cs = slice(hd * C_HEAD_DIM, (hd + 1) * C_HEAD_DIM)
        p = jnp.dot(h, w_ref[:, C_Q + hd * C_HEAD_DIM:C_Q + (hd + 1) * C_HEAD_DIM], preferred_element_type=F32)
        k_ref[:, cs] = norm_rope(p, kn_ref[...], 1.0)
    v_ref[...] = jnp.dot(h, w_ref[:, C_Q + C_KV:], preferred_element_type=F32).astype(BF16)


def _odd_in_proj(lay, x, mod, w_bf16, cos, sin, qn_g, kn_g):
    tm, d = lay.tm, D_MODEL
    row = lambda i: (i, 0)
    full = lambda i: (0, 0)
    return pl.pallas_call(
        _odd_in_body,
        out_shape=(jax.ShapeDtypeStruct((lay.rows, C_Q), BF16),
                   jax.ShapeDtypeStruct((lay.rows, C_KV), BF16),
                   jax.ShapeDtypeStruct((lay.rows, C_KV), BF16)),
        grid=(lay.ntiles,),
        in_specs=[pl.BlockSpec((tm, d), row),
                  pl.BlockSpec((1, 6, d), lambda i: (lay.mod_row(i, tm), 0, 0)),
                  pl.BlockSpec((d, ODD_IN), full),
                  pl.BlockSpec((tm, C_HEAD_DIM), lambda i: (lay.pos_tile(i), 0)),
                  pl.BlockSpec((tm, C_HEAD_DIM), lambda i: (lay.pos_tile(i), 0)),
                  pl.BlockSpec((1, C_HEAD_DIM), full),
                  pl.BlockSpec((1, C_HEAD_DIM), full)],
        out_specs=(pl.BlockSpec((tm, C_Q), row), pl.BlockSpec((tm, C_KV), row), pl.BlockSpec((tm, C_KV), row)),
        compiler_params=_cparams(("arbitrary",)),
        name="odd_in_proj",
    )(x, mod, w_bf16, cos, sin, qn_g.reshape(1, C_HEAD_DIM), kn_g.reshape(1, C_HEAD_DIM))


def _gqa_body(q_ref, kc_ref, kl_ref, vc_ref, vl_ref, o_ref):
    keys = [kc_ref[...], kl_ref[...]]
    vals = [vc_ref[...], vl_ref[...]]
    for g in range(C_GROUP):
        cs = slice(g * C_HEAD_DIM, (g + 1) * C_HEAD_DIM)
        o_ref[:, cs] = _softmax_pv(q_ref[:, cs], keys, vals).astype(BF16)


def _gqa_attention(lay, q, k, v):
    tm = lay.tm
    gw = C_GROUP * C_HEAD_DIM
    lat0 = lay.rows_c // lay.S
    hd = C_HEAD_DIM
    return pl.pallas_call(
        _gqa_body,
        out_shape=jax.ShapeDtypeStruct((lay.rows_l, C_Q), BF16),
        grid=(lay.B, C_KV_HEADS, lay.nlt),
        in_specs=[pl.BlockSpec((tm, gw), lambda b, h, j: (lay.ntiles_c + b * lay.nlt + j, h)),
                  pl.BlockSpec((lay.CTX, hd), lambda b, h, j: (b, h)),
                  pl.BlockSpec((lay.S, hd), lambda b, h, j: (lat0 + b, h)),
                  pl.BlockSpec((lay.CTX, hd), lambda b, h, j: (b, h)),
                  pl.BlockSpec((lay.S, hd), lambda b, h, j: (lat0 + b, h))],
        out_specs=pl.BlockSpec((tm, gw), lambda b, h, j: (b * lay.nlt + j, h)),
        compiler_params=_cparams(("arbitrary", "arbitrary", "arbitrary")),
        name="gqa_attention",
    )(q, k, k, v, v)


def _out_proj_body(n_mix, *refs):
    mix_refs = refs[:n_mix]
    w_ref, x_ref, mod_ref, lg_ref, lb_ref, rw_ref, xo_ref, h_ref, lt_ref = refs[n_mix:]
    m = mod_ref[0]
    off = 0
    mix = None
    for mr in mix_refs:
        kw = mr.shape[-1]
        part = jnp.dot(mr[...], w_ref[off:off + kw, :], preferred_element_type=F32)
        mix = part if mix is None else mix + part
        off += kw
    z = DEEPNORM_ALPHA * x_ref[...] + m[2:3] * mix
    xn = _layer_norm_rows(z, lg_ref[...], lb_ref[...])
    xo_ref[...] = xn
    h2 = xn * (1.0 + m[4:5]) + m[3:4]
    h_ref[...] = h2
    lt_ref[...] = _dot3_nt(rw_ref[...], h2)


def _out_proj(lay, mixes, w_bf16, x, x_tile_off, mod, ln_g, ln_b, router_wt, n_rows):
    tm, d = lay.tm, D_MODEL
    row = lambda i: (i, 0)
    full = lambda i: (0, 0)
    row_off = x_tile_off * tm
    in_specs = [pl.BlockSpec((tm, mx.shape[-1]), row) for mx in mixes]
    in_specs += [pl.BlockSpec((d, d), full),
                 pl.BlockSpec((tm, d), lambda i: (i + x_tile_off, 0)),
                 pl.BlockSpec((1, 6, d), lambda i: (lay.mod_row(i, tm, row_off), 0, 0)),
                 pl.BlockSpec((1, d), full), pl.BlockSpec((1, d), full),
                 pl.BlockSpec((N_EXPERTS, d), full)]
    return pl.pallas_call(
        functools.partial(_out_proj_body, len(mixes)),
        out_shape=(jax.ShapeDtypeStruct((n_rows, d), F32),
                   jax.ShapeDtypeStruct((n_rows, d), F32),
                   jax.ShapeDtypeStruct((N_EXPERTS, n_rows), F32)),
        grid=(n_rows // tm,),
        in_specs=in_specs,
        out_specs=(pl.BlockSpec((tm, d), row), pl.BlockSpec((tm, d), row),
                   pl.BlockSpec((N_EXPERTS, tm), lambda i: (0, i))),
        compiler_params=_cparams(("arbitrary",)),
        name="out_proj_ln",
    )(*mixes, w_bf16, x, mod, ln_g.reshape(1, d), ln_b.reshape(1, d), router_wt)


_EXPERT_PAIRS = ((0, 1), (0, 2), (0, 3), (1, 2), (1, 3), (2, 3))
N_BUCKETS = N_GROUPS * len(_EXPERT_PAIRS)


def _router_body(lt_ref, rb_ref, o_ref):
    logits = lt_ref[...] + rb_ref[...]
    rows = [logits[e:e + 1, :] for e in range(N_EXPERTS)]
    m = rows[0]
    for x in rows[1:]:
        m = jnp.maximum(m, x)
    ex = [jnp.exp(x - m) for x in rows]
    z = ex[0]
    for x in ex[1:]:
        z = z + x
    p = [x / z for x in ex]

    gscore = []
    for g in range(N_GROUPS):
        a, b, c, d = p[4 * g:4 * g + 4]
        hi1, lo1 = jnp.maximum(a, b), jnp.minimum(a, b)
        hi2, lo2 = jnp.maximum(c, d), jnp.minimum(c, d)
        top1 = jnp.maximum(hi1, hi2)
        top2 = jnp.maximum(jnp.minimum(hi1, hi2), jnp.maximum(lo1, lo2))
        gscore.append(top1 + top2)
    best = []
    for g in range(N_GROUPS):
        ok = None
        for o in range(N_GROUPS):
            if o == g:
                continue
            c = (gscore[g] > gscore[o]) if o < g else (gscore[g] >= gscore[o])
            ok = c if ok is None else jnp.logical_and(ok, c)
        best.append(ok)
    won = []
    for e in range(N_EXPERTS):
        g = e // EXPERTS_PER_GROUP
        rank = jnp.zeros_like(p[e])
        for o in range(4 * g, 4 * g + 4):
            if o == e:
                continue
            ahead = (p[o] > p[e]) if o > e else (p[o] >= p[e])
            rank = rank + jnp.where(ahead, 1.0, 0.0)
        won.append(jnp.where(jnp.logical_and(best[g], rank < 1.5), 1.0, 0.0))
    tot = won[0] * p[0]
    for e in range(1, N_EXPERTS):
        tot = tot + won[e] * p[e]
    bucket = jnp.zeros_like(tot)
    gate_a = jnp.zeros_like(tot)
    gate_b = jnp.zeros_like(tot)
    for g in range(N_GROUPS):
        for pid, (a, b) in enumerate(_EXPERT_PAIRS):
            ind = won[4 * g + a] * won[4 * g + b]
            bucket = bucket + ind * float(len(_EXPERT_PAIRS) * g + pid)
            gate_a = gate_a + ind * p[4 * g + a]
            gate_b = gate_b + ind * p[4 * g + b]
    o_ref[...] = jnp.zeros_like(o_ref)
    o_ref[0:1, :] = bucket
    o_ref[1:2, :] = gate_a / tot
    o_ref[2:3, :] = gate_b / tot


def _router(logits_t, router_b):
    e, n = logits_t.shape
    tr = math.gcd(2048, n)
    return pl.pallas_call(
        _router_body,
        out_shape=jax.ShapeDtypeStruct((SUBLANES, n), F32),
        grid=(n // tr,),
        in_specs=[pl.BlockSpec((e, tr), lambda i: (0, i)), pl.BlockSpec((e, 1), lambda i: (0, 0))],
        out_specs=pl.BlockSpec((SUBLANES, tr), lambda i: (0, i)),
        compiler_params=_cparams(("arbitrary",)),
        name="router_gates",
    )(logits_t, router_b.reshape(e, 1))


GATE_LANES = LANES


def _moe_plan(bucket, tg):
    n = bucket.shape[0]
    n_tiles = n // tg + N_BUCKETS
    ids = jnp.arange(N_BUCKETS, dtype=jnp.int32)
    onehot = (bucket[:, None] == ids[None, :]).astype(jnp.int32)
    csum = jnp.cumsum(onehot, axis=0)
    counts = csum[-1]
    tiles = (counts + tg - 1) // tg
    tile_end = jnp.cumsum(tiles)
    row_start = (tile_end - tiles) * tg
    pos = jnp.sum(onehot * (row_start[None, :] + csum - 1), axis=1).astype(jnp.int32)
    n_used = tile_end[-1]
    tile_ids = jnp.minimum(jnp.arange(n_tiles, dtype=jnp.int32), n_used - 1)
    tile_bucket = jnp.sum((tile_ids[:, None] >= tile_end[None, :]).astype(jnp.int32), axis=1)
    pair_a = jnp.asarray([a for a, _ in _EXPERT_PAIRS], jnp.int32)
    pair_b = jnp.asarray([b for _, b in _EXPERT_PAIRS], jnp.int32)
    grp, pid = tile_bucket // len(_EXPERT_PAIRS), tile_bucket % len(_EXPERT_PAIRS)
    ea = grp * EXPERTS_PER_GROUP + pair_a[pid]
    eb = grp * EXPERTS_PER_GROUP + pair_b[pid]
    return pos, ea.astype(jnp.int32), eb.astype(jnp.int32), n_used.reshape(1).astype(jnp.int32), n_tiles


def _row_copy(src_ref, src_row, dst_ref, dst_row, sem):
    return pltpu.make_async_copy(src_ref.at[pl.ds(src_row, 1), :], dst_ref.at[pl.ds(dst_row, 1), :], sem)


def _dispatch_body(tg, pos_ref, h_ref, g_ref, xs_in_ref, xs_ref, aug_ref, sem):
    del xs_in_ref
    base = pl.program_id(0) * tg
    aug_ref[:, :D_MODEL] = h_ref[...]
    aug_ref[:, D_MODEL:] = g_ref[...]

    def issue(r, carry):
        _row_copy(aug_ref, r, xs_ref, pos_ref[base + r], sem).start()
        return carry

    lax.fori_loop(0, tg, issue, 0)

    def drain(r, carry):
        _row_copy(aug_ref, r, xs_ref, pos_ref[base + r], sem).wait()
        return carry

    lax.fori_loop(0, tg, drain, 0)


def _moe_dispatch(pos, h2, gate_rows, n_tiles, tg):
    n, d = h2.shape
    wide = d + GATE_LANES
    xs0 = jnp.zeros((n_tiles * tg, wide), F32)
    grid_spec = pltpu.PrefetchScalarGridSpec(
        num_scalar_prefetch=1,
        grid=(n // tg,),
        in_specs=[pl.BlockSpec((tg, d), lambda i, pos_ref: (i, 0)),
                  pl.BlockSpec((tg, GATE_LANES), lambda i, pos_ref: (i, 0)),
                  pl.BlockSpec(memory_space=pl.ANY)],
        out_specs=pl.BlockSpec(memory_space=pl.ANY),
        scratch_shapes=[pltpu.VMEM((tg, wide), F32), pltpu.SemaphoreType.DMA(())],
    )
    return pl.pallas_call(
        functools.partial(_dispatch_body, tg),
        out_shape=jax.ShapeDtypeStruct(xs0.shape, F32),
        grid_spec=grid_spec,
        input_output_aliases={3: 0},
        compiler_params=_cparams(("arbitrary",)),
        name="moe_dispatch",
    )(pos, h2, gate_rows, xs0)


def _grouped_body(ea_ref, eb_ref, nu_ref, xs_ref, w1a_ref, w3a_ref, w2a_ref, w1b_ref, w3b_ref, w2b_ref, y_ref):
    del ea_ref, eb_ref

    @pl.when(pl.program_id(0) < nu_ref[0])
    def _():
        x = xs_ref[:, :D_MODEL].astype(BF16)

        def expert(w1_ref, w3_ref, w2_ref, gate):
            h1 = jnp.dot(x, w1_ref[0], preferred_element_type=F32)
            h3 = jnp.dot(x, w3_ref[0], preferred_element_type=F32)
            hid = (h1 * _sigmoid(h1) * h3 * gate).astype(BF16)
            return jnp.dot(hid, w2_ref[0], preferred_element_type=F32)

        y_ref[...] = (expert(w1a_ref, w3a_ref, w2a_ref, xs_ref[:, D_MODEL:D_MODEL + 1])
                      + expert(w1b_ref, w3b_ref, w2b_ref, xs_ref[:, D_MODEL + 1:D_MODEL + 2]))

    @pl.when(pl.program_id(0) >= nu_ref[0])
    def _():
        y_ref[...] = jnp.zeros_like(y_ref)


def _moe_grouped(xs, ea, eb, n_used, w1, w3, w2, tg):
    rows, wide = xs.shape
    d, ff = D_MODEL, EXPERT_FF
    n_tiles = rows // tg
    tile = lambda i, ea_r, eb_r, nu_r: (jnp.minimum(i, nu_r[0] - 1), 0)
    up_a = pl.BlockSpec((1, d, ff), lambda i, ea_r, eb_r, nu_r: (ea_r[i], 0, 0))
    up_b = pl.BlockSpec((1, d, ff), lambda i, ea_r, eb_r, nu_r: (eb_r[i], 0, 0))
    dn_a = pl.BlockSpec((1, ff, d), lambda i, ea_r, eb_r, nu_r: (ea_r[i], 0, 0))
    dn_b = pl.BlockSpec((1, ff, d), lambda i, ea_r, eb_r, nu_r: (eb_r[i], 0, 0))
    grid_spec = pltpu.PrefetchScalarGridSpec(
        num_scalar_prefetch=3,
        grid=(n_tiles,),
        in_specs=[pl.BlockSpec((tg, wide), tile), up_a, up_a, dn_a, up_b, up_b, dn_b],
        out_specs=pl.BlockSpec((tg, d), lambda i, ea_r, eb_r, nu_r: (i, 0)),
    )
    return pl.pallas_call(
        _grouped_body,
        out_shape=jax.ShapeDtypeStruct((rows, d), F32),
        grid_spec=grid_spec,
        compiler_params=_cparams(("arbitrary",)),
        name="moe_grouped",
    )(ea, eb, n_used, xs, w1, w3, w2, w1, w3, w2)


def _combine_body(tg, pos_ref, ys_ref, x_ref, mod_ref, lg_ref, lb_ref, o_ref, buf_ref, sem):
    base = pl.program_id(0) * tg

    def issue(r, carry):
        _row_copy(ys_ref, pos_ref[base + r], buf_ref, r, sem).start()
        return carry

    lax.fori_loop(0, tg, issue, 0)

    def drain(r, carry):
        _row_copy(ys_ref, pos_ref[base + r], buf_ref, r, sem).wait()
        return carry

    lax.fori_loop(0, tg, drain, 0)
    m = mod_ref[0]
    z = DEEPNORM_ALPHA * x_ref[...] + m[5:6] * buf_ref[...]
    o_ref[...] = _layer_norm_rows(z, lg_ref[...], lb_ref[...])


def _moe_combine(lay, pos, ys, x, mod, ln_g, ln_b, row_off, tg):
    n, d = x.shape
    grid_spec = pltpu.PrefetchScalarGridSpec(
        num_scalar_prefetch=1,
        grid=(n // tg,),
        in_specs=[pl.BlockSpec(memory_space=pl.ANY),
                  pl.BlockSpec((tg, d), lambda i, pos_ref: (i, 0)),
                  pl.BlockSpec((1, 6, d), lambda i, pos_ref: (lay.mod_row(i, tg, row_off), 0, 0)),
                  pl.BlockSpec((1, d), lambda i, pos_ref: (0, 0)),
                  pl.BlockSpec((1, d), lambda i, pos_ref: (0, 0))],
        out_specs=pl.BlockSpec((tg, d), lambda i, pos_ref: (i, 0)),
        scratch_shapes=[pltpu.VMEM((tg, d), F32), pltpu.SemaphoreType.DMA(())],
    )
    return pl.pallas_call(
        functools.partial(_combine_body, tg),
        out_shape=jax.ShapeDtypeStruct((n, d), F32),
        grid_spec=grid_spec,
        compiler_params=_cparams(("arbitrary",)),
        name="moe_combine_ln",
    )(pos, ys, x, mod, ln_g.reshape(1, d), ln_b.reshape(1, d))


def _moe(lay, h2, routed, w1, w3, w2, x, mod, ln_g, ln_b, row_off):
    tg = lay.tm
    bucket = routed[0].astype(jnp.int32)
    gate_rows = jnp.pad(routed[1:3].T, ((0, 0), (0, GATE_LANES - 2)))
    pos, ea, eb, n_used, n_tiles = _moe_plan(bucket, tg)
    xs = _moe_dispatch(pos, h2, gate_rows, n_tiles, tg)
    ys = _moe_grouped(xs, ea, eb, n_used, w1, w3, w2, tg)
    return _moe_combine(lay, pos, ys, x, mod, ln_g, ln_b, row_off, tg)


def kernel(x, c, ctx, c_ctx, router_w, router_b, ada_w, ada_b, ln1_g, ln1_b, ln2_g, ln2_b, moe_w1, moe_w3, moe_w2, ev_w_in, ev_w_out, ev_a_mu, ev_a_w0, ev_a_w2, ev_a_a0, ev_a_a2, ev_a_g2, ev_a_kk, ev_a_ka, ev_a_rk, ev_a_lnx_g, ev_a_lnx_b, ev_b_lam, ev_b_subln_g, od_w_in, od_w_out, od_qn_g, od_kn_g):
    bsz, seq, d = x.shape
    ctx_len = ctx.shape[1]
    assert d == D_MODEL and ada_w.shape[0] == DEPTH and seq % GRID_W == 0
    lay = _Layout(bsz, ctx_len, seq)

    cvec = jnp.zeros((lay.mod_rows, d), F32).at[:bsz].set(c).at[bsz].set(c_ctx)
    mods = _ada_mods(cvec, ada_w, ada_b).reshape(DEPTH, lay.mod_rows, 6, d)

    xs = jnp.concatenate([ctx.reshape(lay.rows_c, d), x.reshape(lay.rows_l, d)], axis=0)
    router_wt = router_w.T

    cos_b, sin_b = _rope_tables(lay, B_HEAD_DIM, B_QK)
    cos_c, sin_c = _rope_tables(lay, C_HEAD_DIM, C_HEAD_DIM)

    for i in range(DEPTH):
        last = i == DEPTH - 1
        j = i // 2
        mod = mods[i]
        if i % 2 == 0:
            lambda_init = 0.8 - 0.6 * math.exp(-0.3 * i)
            pa, q, k, v = _even_in_proj(lay, xs, mod, ev_w_in[j].astype(BF16), cos_b, sin_b)
            (r_, v_, nkk, g_, bonus, w_f, w_b, kd_f, kd_b, b_f, b_b) = _rwkv_features(
                lay, pa, ev_a_mu[j], ev_a_w0[j], ev_a_w2[j], ev_a_a0[j], ev_a_a2[j], ev_a_g2[j],
                ev_a_kk[j], ev_a_ka[j], ev_a_rk[j].reshape(-1))

            both = functools.partial(_to_chains, lay)
            y_f, y_b = _wkv_scan(lay, (both(r_, r_), both(w_f, w_b), both(kd_f, kd_b), both(v_, v_),
                                       both(nkk, nkk), both(b_f, b_b)))
            nl = lay.B * A_HEADS
            yf = _from_chains(lay, y_f[:, :, :nl])
            yb = _from_chains(lay, y_b[:, :, nl:])
            ya = _rwkv_readout(lay, yf, yb, bonus, g_, ev_a_lnx_g[j], ev_a_lnx_b[j])
            yd = _diff_attention(lay, q, k, v, ev_b_lam[j], ev_b_subln_g[j], lambda_init)
            mixes, w_out = [ya, yd], ev_w_out[j]
            n_rows, tile_off = (lay.rows_l, lay.ntiles_c) if last else (lay.rows, 0)
            if last:
                mixes = [mx[lay.rows_c:] for mx in mixes]
        else:
            q, k, v = _odd_in_proj(lay, xs, mod, od_w_in[j].astype(BF16), cos_c, sin_c, od_qn_g[j], od_kn_g[j])
            assert last, "an odd layer that must also update the context stream is not supported"
            o = _gqa_attention(lay, q, k, v)
            mixes, w_out = [o], od_w_out[j]
            n_rows, tile_off = lay.rows_l, lay.ntiles_c
        row_off = tile_off * lay.tm
        x_new, h2, logits_t = _out_proj(lay, mixes, w_out.astype(BF16), xs, tile_off, mod,
                                        ln1_g[i], ln1_b[i], router_wt, n_rows)
        routed = _router(logits_t, router_b)
        xs = _moe(lay, h2, routed, moe_w1[i].astype(BF16), moe_w3[i].astype(BF16), moe_w2[i].astype(BF16),
                  x_new, mod, ln2_g[i], ln2_b[i], row_off)
    return xs.reshape(bsz, seq, d)
```

```python
import functools
import math

import jax
import jax.numpy as jnp
from jax import lax
from jax.experimental import pallas as pl
from jax.experimental.pallas import tpu as pltpu

F32 = jnp.float32
BF16 = jnp.bfloat16

D_MODEL = 1024
DEPTH = 2
GRID_W = 64
ROPE_THETA = 10000.0
LN_EPS = 1e-5
DEEPNORM_ALPHA = (2 * DEPTH) ** 0.25

A_HEAD_DIM = 64
A_HEADS = 8
A_WIDTH = 512
A_LORA = 64
A_GATE_LORA = 128
A_GN_EPS = 64e-5
A_IN = 3 * A_WIDTH + 4 * A_LORA + A_GATE_LORA
_SCAN_R, _SCAN_W, _SCAN_K, _SCAN_V, _SCAN_A, _SCAN_B = range(6)

B_HEAD_DIM = 64
B_V_DIM = 128
B_HEADS = 4
B_WIDTH = 512
B_QK = 512
B_SUBLN_EPS = 1e-5
EVEN_IN = A_IN + 2 * B_QK + B_WIDTH

C_HEAD_DIM = 128
C_HEADS = 8
C_KV_HEADS = 2
C_GROUP = 4
C_Q = 1024
C_KV = 256
ODD_IN = C_Q + 2 * C_KV
QK_NORM_EPS = 1e-6

N_EXPERTS = 16
N_GROUPS = 4
EXPERTS_PER_GROUP = 4
EXPERT_FF = 512

VMEM_LIMIT_BYTES = 56 * 1024 * 1024
LANES = 128
SUBLANES = 8


def _cparams(sem):
    return pltpu.CompilerParams(dimension_semantics=sem, vmem_limit_bytes=VMEM_LIMIT_BYTES)


def _dot(a, b):
    return jnp.dot(a.astype(BF16), b.astype(BF16), preferred_element_type=F32)


def _dot_nt(a, b):
    return lax.dot_general(a.astype(BF16), b.astype(BF16), (((1,), (1,)), ((), ())),
                           preferred_element_type=F32)


def _split(a):
    hi = a.astype(BF16)
    lo = (a - hi.astype(F32)).astype(BF16)
    return hi, lo


def _dot3(a, b):
    ah, al = _split(a)
    bh, bl = _split(b)
    return (jnp.dot(ah, bh, preferred_element_type=F32)
            + (jnp.dot(ah, bl, preferred_element_type=F32)
               + jnp.dot(al, bh, preferred_element_type=F32)))


def _dot2_exact_rhs(a, b_bf16):
    ah, al = _split(a)
    return jnp.dot(ah, b_bf16, preferred_element_type=F32) + jnp.dot(al, b_bf16, preferred_element_type=F32)


def _dot3_nt(a, b):
    ah, al = _split(a)
    bh, bl = _split(b)
    dn = (((1,), (1,)), ((), ()))
    return (lax.dot_general(ah, bh, dn, preferred_element_type=F32)
            + (lax.dot_general(ah, bl, dn, preferred_element_type=F32)
               + lax.dot_general(al, bh, dn, preferred_element_type=F32)))


def _sigmoid(x):
    return 1.0 / (1.0 + jnp.exp(-x))


def _layer_norm_rows(z, g, b):
    mu = jnp.mean(z, axis=-1, keepdims=True)
    zc = z - mu
    var = jnp.mean(zc * zc, axis=-1, keepdims=True)
    return zc * lax.rsqrt(var + LN_EPS) * g + b


class _Layout:
    def __init__(self, bsz, ctx_len, seq):
        self.B, self.CTX, self.S = bsz, ctx_len, seq
        self.T = ctx_len + seq
        self.tm = math.gcd(256, math.gcd(ctx_len, seq))
        self.nct = ctx_len // self.tm
        self.nlt = seq // self.tm
        self.rows_c = bsz * ctx_len
        self.rows_l = bsz * seq
        self.rows = self.rows_c + self.rows_l
        self.ntiles_c = bsz * self.nct
        self.ntiles = self.rows // self.tm
        assert self.rows_c % seq == 0, "latent K/V blocks are addressed in units of S rows"
        self.tmm = math.gcd(1024, math.gcd(self.rows_c, seq))
        self.mod_rows = -(-(bsz + 1) // SUBLANES) * SUBLANES

    def seq_tile(self, b, j):
        return jnp.where(j < self.nct, b * self.nct + j, self.ntiles_c + b * self.nlt + (j - self.nct))

    def seq_block(self, i):
        il = i - self.ntiles_c
        return (jnp.where(i < self.ntiles_c, i // self.nct, il // self.nlt),
                jnp.where(i < self.ntiles_c, i % self.nct, self.nct + il % self.nlt))

    def mod_row(self, i, tile, row_offset=0):
        r = i * tile + row_offset
        return jnp.where(r < self.rows_c, self.B, (r - self.rows_c) // self.S)

    def pos_tile(self, i):
        il = i - self.ntiles_c
        return jnp.where(i < self.ntiles_c, i % self.nct, self.nct + il % self.nlt)


def _ada_body(cv_ref, w_ref, b_ref, o_ref):
    cv = cv_ref[...]
    s = cv * _sigmoid(cv)
    o_ref[0] = _dot3(s, w_ref[0]) + b_ref[0]


def _ada_mods(cvec, ada_w, ada_b):
    depth, d, n = ada_w.shape
    r = cvec.shape[0]
    tn = 512
    return pl.pallas_call(
        _ada_body,
        out_shape=jax.ShapeDtypeStruct((depth, r, n), F32),
        grid=(depth, n // tn),
        in_specs=[pl.BlockSpec((r, d), lambda l, j: (0, 0)),
                  pl.BlockSpec((1, d, tn), lambda l, j: (l, 0, j)),
                  pl.BlockSpec((1, 1, tn), lambda l, j: (l, 0, j))],
        out_specs=pl.BlockSpec((1, r, tn), lambda l, j: (l, 0, j)),
        compiler_params=_cparams(("arbitrary", "arbitrary")),
        name="ada_mods",
    )(cvec, ada_w, ada_b.reshape(depth, 1, n))


def _rope_tables(lay, head_dim, width):
    rows = lay.S // GRID_W
    rr, cc = jnp.meshgrid(jnp.arange(rows), jnp.arange(GRID_W), indexing="ij")
    row_pos = rr.reshape(-1).astype(F32)
    col_pos = cc.reshape(-1).astype(F32)
    axis_dim = head_dim // 2
    inv = ROPE_THETA ** (-jnp.arange(0, axis_dim, 2, dtype=F32) / axis_dim)
    ang = jnp.concatenate([row_pos[:, None] * inv, col_pos[:, None] * inv], -1)
    cos, sin = jnp.cos(ang), jnp.sin(ang)
    cos = jnp.concatenate([jnp.ones((lay.CTX, head_dim // 2), F32), cos], 0)
    sin = jnp.concatenate([jnp.zeros((lay.CTX, head_dim // 2), F32), sin], 0)
    cos_h = jnp.concatenate([cos, cos], -1)
    sin_h = jnp.concatenate([-sin, sin], -1)
    reps = width // head_dim
    return jnp.tile(cos_h, (1, reps)), jnp.tile(sin_h, (1, reps))


def _rope_lanes(x, cos, sin, head_dim):
    w = x.shape[-1]
    half = head_dim // 2
    if head_dim == LANES and w == LANES:
        rot = pltpu.roll(x, half, 1)
    else:
        fwd = pltpu.roll(x, w - half, 1)
        bwd = pltpu.roll(x, half, 1)
        lane = lax.broadcasted_iota(jnp.int32, x.shape, 1)
        rot = jnp.where((lane % head_dim) < half, fwd, bwd)
    return x * cos + rot * sin


def _even_in_body(x_ref, mod_ref, w_ref, cos_ref, sin_ref, pa_ref, q_ref, k_ref, v_ref):
    m = mod_ref[0]
    h = (x_ref[...] * (1.0 + m[1:2]) + m[0:1]).astype(BF16)
    pa_ref[...] = jnp.dot(h, w_ref[:, :A_IN], preferred_element_type=F32)
    cos, sin = cos_ref[...], sin_ref[...]
    o = A_IN
    q = jnp.dot(h, w_ref[:, o:o + B_QK], preferred_element_type=F32)
    q_ref[...] = (_rope_lanes(q, cos, sin, B_HEAD_DIM) * (B_HEAD_DIM ** -0.5)).astype(BF16)
    o += B_QK
    k = jnp.dot(h, w_ref[:, o:o + B_QK], preferred_element_type=F32)
    k_ref[...] = _rope_lanes(k, cos, sin, B_HEAD_DIM).astype(BF16)
    o += B_QK
    v_ref[...] = jnp.dot(h, w_ref[:, o:o + B_WIDTH], preferred_element_type=F32).astype(BF16)


def _even_in_proj(lay, x, mod, w_bf16, cos, sin):
    tm, d = lay.tm, D_MODEL
    row = lambda i: (i, 0)
    return pl.pallas_call(
        _even_in_body,
        out_shape=(jax.ShapeDtypeStruct((lay.rows, A_IN), F32),
                   jax.ShapeDtypeStruct((lay.rows, B_QK), BF16),
                   jax.ShapeDtypeStruct((lay.rows, B_QK), BF16),
                   jax.ShapeDtypeStruct((lay.rows, B_WIDTH), BF16)),
        grid=(lay.ntiles,),
        in_specs=[pl.BlockSpec((tm, d), row),
                  pl.BlockSpec((1, 6, d), lambda i: (lay.mod_row(i, tm), 0, 0)),
                  pl.BlockSpec((d, EVEN_IN), lambda i: (0, 0)),
                  pl.BlockSpec((tm, B_QK), lambda i: (lay.pos_tile(i), 0)),
                  pl.BlockSpec((tm, B_QK), lambda i: (lay.pos_tile(i), 0))],
        out_specs=(pl.BlockSpec((tm, A_IN), row), pl.BlockSpec((tm, B_QK), row),
                   pl.BlockSpec((tm, B_QK), row), pl.BlockSpec((tm, B_WIDTH), row)),
        compiler_params=_cparams(("arbitrary",)),
        name="even_in_proj",
    )(x, mod, w_bf16, cos, sin)


def _rwkv_feat_body(lay, pa_ref, prev_ref, next_ref, mu_ref, w0_ref, w2_ref, a0_ref, a2_ref, g2_ref,
                    kk_ref, ka_ref, rk_ref, bd_ref,
                    ops_ref, g_ref, bonus_ref):
    i = pl.program_id(0)
    tm = lay.tm
    il = i - lay.ntiles_c
    in_ctx = i < lay.ntiles_c
    seg_first = jnp.where(in_ctx, i % lay.nct == 0, il % lay.nlt == 0)
    seg_last = jnp.where(in_ctx, i % lay.nct == lay.nct - 1, il % lay.nlt == lay.nlt - 1)

    pa = pa_ref[...]
    row = lax.broadcasted_iota(jnp.int32, pa.shape, 0)
    prev_edge = jnp.where(seg_first, 0.0, 1.0) * prev_ref[SUBLANES - 1:SUBLANES, :]
    next_edge = jnp.where(seg_last, 0.0, 1.0) * next_ref[0:1, :]
    prev = jnp.where(row == 0, prev_edge, pltpu.roll(pa, 1, 0))
    nxt = jnp.where(row == tm - 1, next_edge, pltpu.roll(pa, tm - 1, 0))
    u = pa + (0.5 * (prev + nxt) - pa) * mu_ref[...]

    o1, o2, o3 = A_WIDTH, 2 * A_WIDTH, 3 * A_WIDTH
    o4 = o3 + 2 * A_LORA
    o5 = o4 + 2 * A_LORA
    r, k, v = u[:, :o1], u[:, o1:o2], u[:, o2:o3]
    bd = bd_ref[...]

    kk = k * kk_ref[...]
    ss = _dot2_exact_rhs(kk * kk, bd)
    kkn = kk / jnp.maximum(jnp.sqrt(ss), 1e-12)
    g = _dot3(_sigmoid(u[:, o5:]), g2_ref[...])

    g_ref[0] = g

    kd_sum = jnp.zeros_like(k)
    for d in range(2):
        ops_ref[_SCAN_R, d, 0] = r
        ops_ref[_SCAN_V, d, 0] = v
        ops_ref[_SCAN_A, d, 0] = -kkn
        wd = u[:, o3 + d * A_LORA:o3 + (d + 1) * A_LORA]
        ad = u[:, o4 + d * A_LORA:o4 + (d + 1) * A_LORA]
        z = -(w0_ref[d:d + 1, :] + _dot3(jnp.tanh(wd), w2_ref[d]))
        softplus = jnp.maximum(z, 0.0) + jnp.log(1.0 + jnp.exp(-jnp.abs(z)))
        w_log = -softplus - 0.5
        ops_ref[_SCAN_W, d, 0] = jnp.exp(-jnp.exp(w_log))
        a = _sigmoid(a0_ref[d:d + 1, :] + _dot3(ad, a2_ref[d]))
        kd = k * (1.0 + (a - 1.0) * ka_ref[...])
        ops_ref[_SCAN_K, d, 0] = kd
        ops_ref[_SCAN_B, d, 0] = kkn * a
        kd_sum = kd_sum + kd
    bonus_ref[0] = _dot2_exact_rhs(r * kd_sum * rk_ref[...], bd) * v


def _head_block_diag(width, head_dim):
    h = jnp.arange(width) // head_dim
    return (h[:, None] == h[None, :]).astype(BF16)


def _rwkv_features(lay, pa, mu, w0, w2, a0, a2, g2, k_k, k_a, r_k):
    tm = lay.tm
    hb = tm // SUBLANES
    nb8 = lay.rows // SUBLANES
    row = lambda i: (i, 0)
    full2 = lambda i: (0, 0)
    full3 = lambda i: (0, 0, 0)
    w = A_WIDTH
    out = jax.ShapeDtypeStruct((lay.B, lay.T, w), F32)
    seq = lambda i: lay.seq_block(i) + (0,)
    return pl.pallas_call(
        functools.partial(_rwkv_feat_body, lay),
        out_shape=(jax.ShapeDtypeStruct((6, 2, lay.B, lay.T, w), F32), out, out),
        grid=(lay.ntiles,),
        in_specs=[pl.BlockSpec((tm, A_IN), row),
                  pl.BlockSpec((SUBLANES, A_IN), lambda i: (jnp.maximum(i * hb - 1, 0), 0)),
                  pl.BlockSpec((SUBLANES, A_IN), lambda i: (jnp.minimum((i + 1) * hb, nb8 - 1), 0)),
                  pl.BlockSpec((1, A_IN), full2),
                  pl.BlockSpec((2, w), full2),
                  pl.BlockSpec((2, A_LORA, w), full3),
                  pl.BlockSpec((2, w), full2),
                  pl.BlockSpec((2, A_LORA, w), full3),
                  pl.BlockSpec((A_GATE_LORA, w), full2),
                  pl.BlockSpec((1, w), full2),
                  pl.BlockSpec((1, w), full2),
                  pl.BlockSpec((1, w), full2),
                  pl.BlockSpec((w, w), full2)],
        out_specs=(pl.BlockSpec((6, 2, 1, tm, w), lambda i: (0, 0) + lay.seq_block(i) + (0,)),
                   pl.BlockSpec((1, tm, w), seq), pl.BlockSpec((1, tm, w), seq)),
        compiler_params=_cparams(("arbitrary",)),
        name="rwkv_features",
    )(pa, pa, pa, mu.reshape(1, A_IN), w0, w2, a0, a2, g2, k_k.reshape(1, w), k_a.reshape(1, w),
      r_k.reshape(1, w), _head_block_diag(w, A_HEAD_DIM))


def _wkv_scan_body(tb, fwd_ref, bwd_ref, yf_ref, yb_ref, s_ref, m_ref):
    n = A_HEAD_DIM
    chains = s_ref.shape[-1]
    is_fwd = lax.broadcasted_iota(jnp.int32, (n, chains), 1) < chains // 2

    @pl.when(pl.program_id(0) == 0)
    def _():
        s_ref[...] = jnp.zeros_like(s_ref)

    def step(t, carry):
        tr = tb - 1 - t
        for idx in range(6):
            m_ref[idx] = jnp.where(is_fwd, fwd_ref[t, idx], bwd_ref[tr, idx])
        sa = jnp.zeros((n, chains), F32)
        for kk in range(n):
            sa = sa + s_ref[kk] * m_ref[_SCAN_A, kk:kk + 1, :]
        v_t = m_ref[_SCAN_V]
        y = jnp.zeros((n, chains), F32)
        for kk in range(n):
            s_new = (s_ref[kk] * m_ref[_SCAN_W, kk:kk + 1, :] + sa * m_ref[_SCAN_B, kk:kk + 1, :]
                     + v_t * m_ref[_SCAN_K, kk:kk + 1, :])
            s_ref[kk] = s_new
            y = y + s_new * m_ref[_SCAN_R, kk:kk + 1, :]
        yf_ref[t] = y
        yb_ref[tr] = y
        return carry

    lax.fori_loop(0, tb, step, 0)


def _wkv_scan(lay, ops):
    t, nops, n, lanes = ops.shape
    tb = math.gcd(32, math.gcd(lay.CTX, lay.S))
    nctb, nt = lay.CTX // tb, t // tb
    bwd_tile = lambda g: jnp.where(g < nctb, nctb - 1 - g, nt - 1 - (g - nctb))
    out = jax.ShapeDtypeStruct((t, n, lanes), F32)
    return pl.pallas_call(
        functools.partial(_wkv_scan_body, tb),
        out_shape=(out, out),
        grid=(nt,),
        in_specs=[pl.BlockSpec((tb, nops, n, lanes), lambda g: (g, 0, 0, 0)),
                  pl.BlockSpec((tb, nops, n, lanes), lambda g: (bwd_tile(g), 0, 0, 0))],
        out_specs=(pl.BlockSpec((tb, n, lanes), lambda g: (g, 0, 0)),
                   pl.BlockSpec((tb, n, lanes), lambda g: (bwd_tile(g), 0, 0))),
        scratch_shapes=[pltpu.VMEM((n, n, lanes), F32), pltpu.VMEM((nops, n, lanes), F32)],
        compiler_params=_cparams(("arbitrary",)),
        name="wkv7_scan",
    )(ops, ops)


def _to_chains(lay, ops):
    h, n = A_HEADS, A_HEAD_DIM
    nops = ops.shape[0]
    x = ops.reshape(nops, 2, lay.B, lay.T, h, n).transpose(3, 0, 5, 1, 2, 4)
    return x.reshape(lay.T, nops, n, 2 * lay.B * h)


def _from_chains(lay, y):
    h, n = A_HEADS, A_HEAD_DIM
    return y.reshape(lay.T, n, lay.B, h).transpose(2, 0, 3, 1).reshape(lay.B, lay.T, h * n)


def _rwkv_readout_body(yf_ref, yb_ref, bonus_ref, g_ref, lg_ref, lb_ref, bd_ref, o_ref):
    y = yf_ref[0] + yb_ref[0]
    bd = bd_ref[...]
    inv_n = 1.0 / A_HEAD_DIM
    mu = _dot2_exact_rhs(y, bd) * inv_n
    yc = y - mu
    var = _dot2_exact_rhs(yc * yc, bd) * inv_n
    yn = yc * lax.rsqrt(var + A_GN_EPS) * lg_ref[...] + lb_ref[...]
    o_ref[...] = ((yn + bonus_ref[0]) * g_ref[0]).astype(BF16)


def _rwkv_readout(lay, yf, yb, bonus, g, lnx_g, lnx_b):
    tm, w = lay.tm, A_WIDTH
    row = lambda i: (i, 0)
    full = lambda i: (0, 0)
    seq = lambda i: lay.seq_block(i) + (0,)
    return pl.pallas_call(
        _rwkv_readout_body,
        out_shape=jax.ShapeDtypeStruct((lay.rows, w), BF16),
        grid=(lay.ntiles,),
        in_specs=[pl.BlockSpec((1, tm, w), seq)] * 4 + [pl.BlockSpec((1, w), full)] * 2 + [pl.BlockSpec((w, w), full)],
        out_specs=pl.BlockSpec((tm, w), row),
        compiler_params=_cparams(("arbitrary",)),
        name="rwkv_readout",
    )(yf, yb, bonus, g, lnx_g.reshape(1, w), lnx_b.reshape(1, w), _head_block_diag(w, A_HEAD_DIM))


def _softmax_pv(q, keys, vals):
    scores = [_dot_nt(q, kk) for kk in keys]
    m = scores[0].max(axis=-1, keepdims=True)
    for s in scores[1:]:
        m = jnp.maximum(m, s.max(axis=-1, keepdims=True))
    l = None
    o = None
    for s, vv in zip(scores, vals):
        e = jnp.exp(s - m)
        ls = e.sum(axis=-1, keepdims=True)
        os_ = jnp.dot(e.astype(BF16), vv, preferred_element_type=F32)
        l = ls if l is None else l + ls
        o = os_ if o is None else o + os_
    return o / l


def _diff_attn_body(lay, lambda_init, q_ref, kc_ref, kl_ref, vc_ref, vl_ref, lam_ref, g_ref, o_ref):
    j = pl.program_id(1)
    lv = lam_ref[...]
    lam = (jnp.exp(jnp.sum(lv[0:1] * lv[1:2], axis=1, keepdims=True))
           - jnp.exp(jnp.sum(lv[2:3] * lv[3:4], axis=1, keepdims=True)) + lambda_init)

    def run(with_latent):
        for h in range(B_HEADS):
            vs = slice(h * B_V_DIM, (h + 1) * B_V_DIM)
            vals = [vc_ref[:, vs]] + ([vl_ref[:, vs]] if with_latent else [])
            outs = []
            for mi in range(2):
                cs = slice(h * B_V_DIM + mi * B_HEAD_DIM, h * B_V_DIM + (mi + 1) * B_HEAD_DIM)
                keys = [kc_ref[:, cs]] + ([kl_ref[:, cs]] if with_latent else [])
                outs.append(_softmax_pv(q_ref[:, cs], keys, vals))
            o = outs[0] - lam * outs[1]
            ms = jnp.mean(o * o, axis=-1, keepdims=True)
            o = o * lax.rsqrt(ms + B_SUBLN_EPS) * g_ref[...] * (1.0 - lambda_init)
            o_ref[:, vs] = o.astype(BF16)

    @pl.when(j < lay.nct)
    def _():
        run(False)

    @pl.when(j >= lay.nct)
    def _():
        run(True)


def _diff_attention(lay, q, k, v, lam_vecs, subln_g, lambda_init):
    tm = lay.tm
    w = B_WIDTH
    lat0 = lay.rows_c // lay.S
    return pl.pallas_call(
        functools.partial(_diff_attn_body, lay, lambda_init),
        out_shape=jax.ShapeDtypeStruct((lay.rows, w), BF16),
        grid=(lay.B, lay.nct + lay.nlt),
        in_specs=[pl.BlockSpec((tm, w), lambda b, j: (lay.seq_tile(b, j), 0)),
                  pl.BlockSpec((lay.CTX, w), lambda b, j: (b, 0)),
                  pl.BlockSpec((lay.S, w), lambda b, j: (lat0 + b, 0)),
                  pl.BlockSpec((lay.CTX, w), lambda b, j: (b, 0)),
                  pl.BlockSpec((lay.S, w), lambda b, j: (lat0 + b, 0)),
                  pl.BlockSpec((4, B_HEAD_DIM), lambda b, j: (0, 0)),
                  pl.BlockSpec((1, B_V_DIM), lambda b, j: (0, 0))],
        out_specs=pl.BlockSpec((tm, w), lambda b, j: (lay.seq_tile(b, j), 0)),
        compiler_params=_cparams(("arbitrary", "arbitrary")),
        name="diff_attention",
    )(q, k, k, v, v, lam_vecs, subln_g.reshape(1, B_V_DIM))


def _odd_in_body(x_ref, mod_ref, w_ref, cos_ref, sin_ref, qn_ref, kn_ref, q_ref, k_ref, v_ref):
    m = mod_ref[0]
    h = (x_ref[...] * (1.0 + m[1:2]) + m[0:1]).astype(BF16)
    cos, sin = cos_ref[...], sin_ref[...]

    def norm_rope(p, g, scale):
        ms = jnp.mean(p * p, axis=-1, keepdims=True)
        y = p * lax.rsqrt(ms + QK_NORM_EPS) * g
        return (_rope_lanes(y, cos, sin, C_HEAD_DIM) * scale).astype(BF16)

    for hd in range(C_HEADS):
        cs = slice(hd * C_HEAD_DIM, (hd + 1) * C_HEAD_DIM)
        p = jnp.dot(h, w_ref[:, cs], preferred_element_type=F32)
        q_ref[:, cs] = norm_rope(p, qn_ref[...], C_HEAD_DIM ** -0.5)
    for hd in range(C_KV_HEADS):
        cs = slice(hd * C_HEAD_DIM, (hd + 1) * C_HEAD_DIM)
        p = jnp.dot(h, w_ref[:, C_Q + hd * C_HEAD_DIM:C_Q + (hd + 1) * C_HEAD_DIM], preferred_element_type=F32)
        k_ref[:, cs] = norm_rope(p, kn_ref[...], 1.0)
    v_ref[...] = jnp.dot(h, w_ref[:, C_Q + C_KV:], preferred_element_type=F32).astype(BF16)


def _odd_in_proj(lay, x, mod, w_bf16, cos, sin, qn_g, kn_g):
    tm, d = lay.tm, D_MODEL
    row = lambda i: (i, 0)
    full = lambda i: (0, 0)
    return pl.pallas_call(
        _odd_in_body,
        out_shape=(jax.ShapeDtypeStruct((lay.rows, C_Q), BF16),
                   jax.ShapeDtypeStruct((lay.rows, C_KV), BF16),
                   jax.ShapeDtypeStruct((lay.rows, C_KV), BF16)),
        grid=(lay.ntiles,),
        in_specs=[pl.BlockSpec((tm, d), row),
                  pl.BlockSpec((1, 6, d), lambda i: (lay.mod_row(i, tm), 0, 0)),
                  pl.BlockSpec((d, ODD_IN), full),
                  pl.BlockSpec((tm, C_HEAD_DIM), lambda i: (lay.pos_tile(i), 0)),
                  pl.BlockSpec((tm, C_HEAD_DIM), lambda i: (lay.pos_tile(i), 0)),
                  pl.BlockSpec((1, C_HEAD_DIM), full),
                  pl.BlockSpec((1, C_HEAD_DIM), full)],
        out_specs=(pl.BlockSpec((tm, C_Q), row), pl.BlockSpec((tm, C_KV), row), pl.BlockSpec((tm, C_KV), row)),
        compiler_params=_cparams(("arbitrary",)),
        name="odd_in_proj",
    )(x, mod, w_bf16, cos, sin, qn_g.reshape(1, C_HEAD_DIM), kn_g.reshape(1, C_HEAD_DIM))


def _gqa_body(q_ref, kc_ref, kl_ref, vc_ref, vl_ref, o_ref):
    keys = [kc_ref[...], kl_ref[...]]
    vals = [vc_ref[...], vl_ref[...]]
    for g in range(C_GROUP):
        cs = slice(g * C_HEAD_DIM, (g + 1) * C_HEAD_DIM)
        o_ref[:, cs] = _softmax_pv(q_ref[:, cs], keys, vals).astype(BF16)


def _gqa_attention(lay, q, k, v):
    tm = lay.tm
    gw = C_GROUP * C_HEAD_DIM
    lat0 = lay.rows_c // lay.S
    hd = C_HEAD_DIM
    return pl.pallas_call(
        _gqa_body,
        out_shape=jax.ShapeDtypeStruct((lay.rows_l, C_Q), BF16),
        grid=(lay.B, C_KV_HEADS, lay.nlt),
        in_specs=[pl.BlockSpec((tm, gw), lambda b, h, j: (lay.ntiles_c + b * lay.nlt + j, h)),
                  pl.BlockSpec((lay.CTX, hd), lambda b, h, j: (b, h)),
                  pl.BlockSpec((lay.S, hd), lambda b, h, j: (lat0 + b, h)),
                  pl.BlockSpec((lay.CTX, hd), lambda b, h, j: (b, h)),
                  pl.BlockSpec((lay.S, hd), lambda b, h, j: (lat0 + b, h))],
        out_specs=pl.BlockSpec((tm, gw), lambda b, h, j: (b * lay.nlt + j, h)),
        compiler_params=_cparams(("arbitrary", "arbitrary", "arbitrary")),
        name="gqa_attention",
    )(q, k, k, v, v)


def _out_proj_body(n_mix, *refs):
    mix_refs = refs[:n_mix]
    w_ref, x_ref, mod_ref, lg_ref, lb_ref, rw_ref, xo_ref, h_ref, lt_ref = refs[n_mix:]
    m = mod_ref[0]
    off = 0
    mix = None
    for mr in mix_refs:
        kw = mr.shape[-1]
        part = jnp.dot(mr[...], w_ref[off:off + kw, :], preferred_element_type=F32)
        mix = part if mix is None else mix + part
        off += kw
    z = DEEPNORM_ALPHA * x_ref[...] + m[2:3] * mix
    xn = _layer_norm_rows(z, lg_ref[...], lb_ref[...])
    xo_ref[...] = xn
    h2 = xn * (1.0 + m[4:5]) + m[3:4]
    h_ref[...] = h2
    lt_ref[...] = _dot3_nt(rw_ref[...], h2)


def _out_proj(lay, mixes, w_bf16, x, x_tile_off, mod, ln_g, ln_b, router_wt, n_rows):
    tm, d = lay.tm, D_MODEL
    row = lambda i: (i, 0)
    full = lambda i: (0, 0)
    row_off = x_tile_off * tm
    in_specs = [pl.BlockSpec((tm, mx.shape[-1]), row) for mx in mixes]
    in_specs += [pl.BlockSpec((d, d), full),
                 pl.BlockSpec((tm, d), lambda i: (i + x_tile_off, 0)),
                 pl.BlockSpec((1, 6, d), lambda i: (lay.mod_row(i, tm, row_off), 0, 0)),
                 pl.BlockSpec((1, d), full), pl.BlockSpec((1, d), full),
                 pl.BlockSpec((N_EXPERTS, d), full)]
    return pl.pallas_call(
        functools.partial(_out_proj_body, len(mixes)),
        out_shape=(jax.ShapeDtypeStruct((n_rows, d), F32),
                   jax.ShapeDtypeStruct((n_rows, d), F32),
                   jax.ShapeDtypeStruct((N_EXPERTS, n_rows), F32)),
        grid=(n_rows // tm,),
        in_specs=in_specs,
        out_specs=(pl.BlockSpec((tm, d), row), pl.BlockSpec((tm, d), row),
                   pl.BlockSpec((N_EXPERTS, tm), lambda i: (0, i))),
        compiler_params=_cparams(("arbitrary",)),
        name="out_proj_ln",
    )(*mixes, w_bf16, x, mod, ln_g.reshape(1, d), ln_b.reshape(1, d), router_wt)


_EXPERT_PAIRS = ((0, 1), (0, 2), (0, 3), (1, 2), (1, 3), (2, 3))
N_BUCKETS = N_GROUPS * len(_EXPERT_PAIRS)


def _router_body(lt_ref, rb_ref, o_ref):
    logits = lt_ref[...] + rb_ref[...]
    rows = [logits[e:e + 1, :] for e in range(N_EXPERTS)]
    m = rows[0]
    for x in rows[1:]:
        m = jnp.maximum(m, x)
    ex = [jnp.exp(x - m) for x in rows]
    z = ex[0]
    for x in ex[1:]:
        z = z + x
    p = [x / z for x in ex]

    gscore = []
    for g in range(N_GROUPS):
        a, b, c, d = p[4 * g:4 * g + 4]
        hi1, lo1 = jnp.maximum(a, b), jnp.minimum(a, b)
        hi2, lo2 = jnp.maximum(c, d), jnp.minimum(c, d)
        top1 = jnp.maximum(hi1, hi2)
        top2 = jnp.maximum(jnp.minimum(hi1, hi2), jnp.maximum(lo1, lo2))
        gscore.append(top1 + top2)
    best = []
    for g in range(N_GROUPS):
        ok = None
        for o in range(N_GROUPS):
            if o == g:
                continue
            c = (gscore[g] > gscore[o]) if o < g else (gscore[g] >= gscore[o])
            ok = c if ok is None else jnp.logical_and(ok, c)
        best.append(ok)
    won = []
    for e in range(N_EXPERTS):
        g = e // EXPERTS_PER_GROUP
        rank = jnp.zeros_like(p[e])
        for o in range(4 * g, 4 * g + 4):
            if o == e:
                continue
            ahead = (p[o] > p[e]) if o > e else (p[o] >= p[e])
            rank = rank + jnp.where(ahead, 1.0, 0.0)
        won.append(jnp.where(jnp.logical_and(best[g], rank < 1.5), 1.0, 0.0))
    tot = won[0] * p[0]
    for e in range(1, N_EXPERTS):
        tot = tot + won[e] * p[e]
    bucket = jnp.zeros_like(tot)
    gate_a = jnp.zeros_like(tot)
    gate_b = jnp.zeros_like(tot)
    for g in range(N_GROUPS):
        for pid, (a, b) in enumerate(_EXPERT_PAIRS):
            ind = won[4 * g + a] * won[4 * g + b]
            bucket = bucket + ind * float(len(_EXPERT_PAIRS) * g + pid)
            gate_a = gate_a + ind * p[4 * g + a]
            gate_b = gate_b + ind * p[4 * g + b]
    o_ref[...] = jnp.zeros_like(o_ref)
    o_ref[0:1, :] = bucket
    o_ref[1:2, :] = gate_a / tot
    o_ref[2:3, :] = gate_b / tot


def _router(logits_t, router_b):
    e, n = logits_t.shape
    tr = math.gcd(2048, n)
    return pl.pallas_call(
        _router_body,
        out_shape=jax.ShapeDtypeStruct((SUBLANES, n), F32),
        grid=(n // tr,),
        in_specs=[pl.BlockSpec((e, tr), lambda i: (0, i)), pl.BlockSpec((e, 1), lambda i: (0, 0))],
        out_specs=pl.BlockSpec((SUBLANES, tr), lambda i: (0, i)),
        compiler_params=_cparams(("arbitrary",)),
        name="router_gates",
    )(logits_t, router_b.reshape(e, 1))


GATE_LANES = LANES


def _moe_plan(bucket, tg):
    n = bucket.shape[0]
    n_tiles = n // tg + N_BUCKETS
    ids = jnp.arange(N_BUCKETS, dtype=jnp.int32)
    onehot = (bucket[:, None] == ids[None, :]).astype(jnp.int32)
    csum = jnp.cumsum(onehot, axis=0)
    counts = csum[-1]
    tiles = (counts + tg - 1) // tg
    tile_end = jnp.cumsum(tiles)
    row_start = (tile_end - tiles) * tg
    pos = jnp.sum(onehot * (row_start[None, :] + csum - 1), axis=1).astype(jnp.int32)
    n_used = tile_end[-1]
    tile_ids = jnp.minimum(jnp.arange(n_tiles, dtype=jnp.int32), n_used - 1)
    tile_bucket = jnp.sum((tile_ids[:, None] >= tile_end[None, :]).astype(jnp.int32), axis=1)
    pair_a = jnp.asarray([a for a, _ in _EXPERT_PAIRS], jnp.int32)
    pair_b = jnp.asarray([b for _, b in _EXPERT_PAIRS], jnp.int32)
    grp, pid = tile_bucket // len(_EXPERT_PAIRS), tile_bucket % len(_EXPERT_PAIRS)
    ea = grp * EXPERTS_PER_GROUP + pair_a[pid]
    eb = grp * EXPERTS_PER_GROUP + pair_b[pid]
    return pos, ea.astype(jnp.int32), eb.astype(jnp.int32), n_used.reshape(1).astype(jnp.int32), n_tiles


def _row_copy(src_ref, src_row, dst_ref, dst_row, sem):
    return pltpu.make_async_copy(src_ref.at[pl.ds(src_row, 1), :], dst_ref.at[pl.ds(dst_row, 1), :], sem)


def _dispatch_body(tg, pos_ref, h_ref, g_ref, xs_in_ref, xs_ref, aug_ref, sem):
    del xs_in_ref
    base = pl.program_id(0) * tg
    aug_ref[:, :D_MODEL] = h_ref[...]
    aug_ref[:, D_MODEL:] = g_ref[...]

    def issue(r, carry):
        _row_copy(aug_ref, r, xs_ref, pos_ref[base + r], sem).start()
        return carry

    lax.fori_loop(0, tg, issue, 0)

    def drain(r, carry):
        _row_copy(aug_ref, r, xs_ref, pos_ref[base + r], sem).wait()
        return carry

    lax.fori_loop(0, tg, drain, 0)


def _moe_dispatch(pos, h2, gate_rows, n_tiles, tg):
    n, d = h2.shape
    wide = d + GATE_LANES
    xs0 = jnp.zeros((n_tiles * tg, wide), F32)
    grid_spec = pltpu.PrefetchScalarGridSpec(
        num_scalar_prefetch=1,
        grid=(n // tg,),
        in_specs=[pl.BlockSpec((tg, d), lambda i, pos_ref: (i, 0)),
                  pl.BlockSpec((tg, GATE_LANES), lambda i, pos_ref: (i, 0)),
                  pl.BlockSpec(memory_space=pl.ANY)],
        out_specs=pl.BlockSpec(memory_space=pl.ANY),
        scratch_shapes=[pltpu.VMEM((tg, wide), F32), pltpu.SemaphoreType.DMA(())],
    )
    return pl.pallas_call(
        functools.partial(_dispatch_body, tg),
        out_shape=jax.ShapeDtypeStruct(xs0.shape, F32),
        grid_spec=grid_spec,
        input_output_aliases={3: 0},
        compiler_params=_cparams(("arbitrary",)),
        name="moe_dispatch",
    )(pos, h2, gate_rows, xs0)


def _grouped_body(ea_ref, eb_ref, nu_ref, xs_ref, w1a_ref, w3a_ref, w2a_ref, w1b_ref, w3b_ref, w2b_ref, y_ref):
    del ea_ref, eb_ref

    @pl.when(pl.program_id(0) < nu_ref[0])
    def _():
        x = xs_ref[:, :D_MODEL].astype(BF16)

        def expert(w1_ref, w3_ref, w2_ref, gate):
            h1 = jnp.dot(x, w1_ref[0], preferred_element_type=F32)
            h3 = jnp.dot(x, w3_ref[0], preferred_element_type=F32)
            hid = (h1 * _sigmoid(h1) * h3 * gate).astype(BF16)
            return jnp.dot(hid, w2_ref[0], preferred_element_type=F32)

        y_ref[...] = (expert(w1a_ref, w3a_ref, w2a_ref, xs_ref[:, D_MODEL:D_MODEL + 1])
                      + expert(w1b_ref, w3b_ref, w2b_ref, xs_ref[:, D_MODEL + 1:D_MODEL + 2]))

    @pl.when(pl.program_id(0) >= nu_ref[0])
    def _():
        y_ref[...] = jnp.zeros_like(y_ref)


def _moe_grouped(xs, ea, eb, n_used, w1, w3, w2, tg):
    rows, wide = xs.shape
    d, ff = D_MODEL, EXPERT_FF
    n_tiles = rows // tg
    tile = lambda i, ea_r, eb_r, nu_r: (jnp.minimum(i, nu_r[0] - 1), 0)
    up_a = pl.BlockSpec((1, d, ff), lambda i, ea_r, eb_r, nu_r: (ea_r[i], 0, 0))
    up_b = pl.BlockSpec((1, d, ff), lambda i, ea_r, eb_r, nu_r: (eb_r[i], 0, 0))
    dn_a = pl.BlockSpec((1, ff, d), lambda i, ea_r, eb_r, nu_r: (ea_r[i], 0, 0))
    dn_b = pl.BlockSpec((1, ff, d), lambda i, ea_r, eb_r, nu_r: (eb_r[i], 0, 0))
    grid_spec = pltpu.PrefetchScalarGridSpec(
        num_scalar_prefetch=3,
        grid=(n_tiles,),
        in_specs=[pl.BlockSpec((tg, wide), tile), up_a, up_a, dn_a, up_b, up_b, dn_b],
        out_specs=pl.BlockSpec((tg, d), lambda i, ea_r, eb_r, nu_r: (i, 0)),
    )
    return pl.pallas_call(
        _grouped_body,
        out_shape=jax.ShapeDtypeStruct((rows, d), F32),
        grid_spec=grid_spec,
        compiler_params=_cparams(("arbitrary",)),
        name="moe_grouped",
    )(ea, eb, n_used, xs, w1, w3, w2, w1, w3, w2)


def _combine_body(tg, pos_ref, ys_ref, x_ref, mod_ref, lg_ref, lb_ref, o_ref, buf_ref, sem):
    base = pl.program_id(0) * tg

    def issue(r, carry):
        _row_copy(ys_ref, pos_ref[base + r], buf_ref, r, sem).start()
        return carry

    lax.fori_loop(0, tg, issue, 0)

    def drain(r, carry):
        _row_copy(ys_ref, pos_ref[base + r], buf_ref, r, sem).wait()
        return carry

    lax.fori_loop(0, tg, drain, 0)
    m = mod_ref[0]
    z = DEEPNORM_ALPHA * x_ref[...] + m[5:6] * buf_ref[...]
    o_ref[...] = _layer_norm_rows(z, lg_ref[...], lb_ref[...])


def _moe_combine(lay, pos, ys, x, mod, ln_g, ln_b, row_off, tg):
    n, d = x.shape
    grid_spec = pltpu.PrefetchScalarGridSpec(
        num_scalar_prefetch=1,
        grid=(n // tg,),
        in_specs=[pl.BlockSpec(memory_space=pl.ANY),
                  pl.BlockSpec((tg, d), lambda i, pos_ref: (i, 0)),
                  pl.BlockSpec((1, 6, d), lambda i, pos_ref: (lay.mod_row(i, tg, row_off), 0, 0)),
                  pl.BlockSpec((1, d), lambda i, pos_ref: (0, 0)),
                  pl.BlockSpec((1, d), lambda i, pos_ref: (0, 0))],
        out_specs=pl.BlockSpec((tg, d), lambda i, pos_ref: (i, 0)),
        scratch_shapes=[pltpu.VMEM((tg, d), F32), pltpu.SemaphoreType.DMA(())],
    )
    return pl.pallas_call(
        functools.partial(_combine_body, tg),
        out_shape=jax.ShapeDtypeStruct((n, d), F32),
        grid_spec=grid_spec,
        compiler_params=_cparams(("arbitrary",)),
        name="moe_combine_ln",
    )(pos, ys, x, mod, ln_g.reshape(1, d), ln_b.reshape(1, d))


def _moe(lay, h2, routed, w1, w3, w2, x, mod, ln_g, ln_b, row_off):
    tg = lay.tm
    bucket = routed[0].astype(jnp.int32)
    gate_rows = jnp.pad(routed[1:3].T, ((0, 0), (0, GATE_LANES - 2)))
    pos, ea, eb, n_used, n_tiles = _moe_plan(bucket, tg)
    xs = _moe_dispatch(pos, h2, gate_rows, n_tiles, tg)
    ys = _moe_grouped(xs, ea, eb, n_used, w1, w3, w2, tg)
    return _moe_combine(lay, pos, ys, x, mod, ln_g, ln_b, row_off, tg)


def kernel(x, c, ctx, c_ctx, router_w, router_b, ada_w, ada_b, ln1_g, ln1_b, ln2_g, ln2_b, moe_w1, moe_w3, moe_w2, ev_w_in, ev_w_out, ev_a_mu, ev_a_w0, ev_a_w2, ev_a_a0, ev_a_a2, ev_a_g2, ev_a_kk, ev_a_ka, ev_a_rk, ev_a_lnx_g, ev_a_lnx_b, ev_b_lam, ev_b_subln_g, od_w_in, od_w_out, od_qn_g, od_kn_g):
    bsz, seq, d = x.shape
    ctx_len = ctx.shape[1]
    assert d == D_MODEL and ada_w.shape[0] == DEPTH and seq % GRID_W == 0
    lay = _Layout(bsz, ctx_len, seq)

    cvec = jnp.zeros((lay.mod_rows, d), F32).at[:bsz].set(c).at[bsz].set(c_ctx)
    mods = _ada_mods(cvec, ada_w, ada_b).reshape(DEPTH, lay.mod_rows, 6, d)

    xs = jnp.concatenate([ctx.reshape(lay.rows_c, d), x.reshape(lay.rows_l, d)], axis=0)
    router_wt = router_w.T

    cos_b, sin_b = _rope_tables(lay, B_HEAD_DIM, B_QK)
    cos_c, sin_c = _rope_tables(lay, C_HEAD_DIM, C_HEAD_DIM)

    for i in range(DEPTH):
        last = i == DEPTH - 1
        j = i // 2
        mod = mods[i]
        if i % 2 == 0:
            lambda_init = 0.8 - 0.6 * math.exp(-0.3 * i)
            pa, q, k, v = _even_in_proj(lay, xs, mod, ev_w_in[j].astype(BF16), cos_b, sin_b)
            scan_ops, g_, bonus = _rwkv_features(
                lay, pa, ev_a_mu[j], ev_a_w0[j], ev_a_w2[j], ev_a_a0[j], ev_a_a2[j], ev_a_g2[j],
                ev_a_kk[j], ev_a_ka[j], ev_a_rk[j].reshape(-1))
            y_f, y_b = _wkv_scan(lay, _to_chains(lay, scan_ops))
            nl = lay.B * A_HEADS
            yf = _from_chains(lay, y_f[:, :, :nl])
            yb = _from_chains(lay, y_b[:, :, nl:])
            ya = _rwkv_readout(lay, yf, yb, bonus, g_, ev_a_lnx_g[j], ev_a_lnx_b[j])
            yd = _diff_attention(lay, q, k, v, ev_b_lam[j], ev_b_subln_g[j], lambda_init)
            mixes, w_out = [ya, yd], ev_w_out[j]
            n_rows, tile_off = (lay.rows_l, lay.ntiles_c) if last else (lay.rows, 0)
            if last:
                mixes = [mx[lay.rows_c:] for mx in mixes]
        else:
            q, k, v = _odd_in_proj(lay, xs, mod, od_w_in[j].astype(BF16), cos_c, sin_c, od_qn_g[j], od_kn_g[j])
            assert last, "an odd layer that must also update the context stream is not supported"
            o = _gqa_attention(lay, q, k, v)
            mixes, w_out = [o], od_w_out[j]
            n_rows, tile_off = lay.rows_l, lay.ntiles_c
        row_off = tile_off * lay.tm
        x_new, h2, logits_t = _out_proj(lay, mixes, w_out.astype(BF16), xs, tile_off, mod,
                                        ln1_g[i], ln1_b[i], router_wt, n_rows)
        routed = _router(logits_t, router_b)
        xs = _moe(lay, h2, routed, moe_w1[i].astype(BF16), moe_w3[i].astype(BF16), moe_w2[i].astype(BF16),
                  x_new, mod, ln2_g[i], ln2_b[i], row_off)
    return xs.reshape(bsz, seq, d)
```

```python
import functools
import math

import jax
import jax.numpy as jnp
from jax import lax
from jax.experimental import pallas as pl
from jax.experimental.pallas import tpu as pltpu

F32 = jnp.float32
BF16 = jnp.bfloat16

D_MODEL = 1024
DEPTH = 2
GRID_W = 64
ROPE_THETA = 10000.0
LN_EPS = 1e-5
DEEPNORM_ALPHA = (2 * DEPTH) ** 0.25

A_HEAD_DIM = 64
A_HEADS = 8
A_WIDTH = 512
A_LORA = 64
A_GATE_LORA = 128
A_GN_EPS = 64e-5
A_IN = 3 * A_WIDTH + 4 * A_LORA + A_GATE_LORA
_SCAN_R, _SCAN_W, _SCAN_K, _SCAN_V, _SCAN_A, _SCAN_B = range(6)

B_HEAD_DIM = 64
B_V_DIM = 128
B_HEADS = 4
B_WIDTH = 512
B_QK = 512
B_SUBLN_EPS = 1e-5
EVEN_IN = A_IN + 2 * B_QK + B_WIDTH

C_HEAD_DIM = 128
C_HEADS = 8
C_KV_HEADS = 2
C_GROUP = 4
C_Q = 1024
C_KV = 256
ODD_IN = C_Q + 2 * C_KV
QK_NORM_EPS = 1e-6

N_EXPERTS = 16
N_GROUPS = 4
EXPERTS_PER_GROUP = 4
EXPERT_FF = 512

VMEM_LIMIT_BYTES = 56 * 1024 * 1024
LANES = 128
SUBLANES = 8


def _cparams(sem):
    return pltpu.CompilerParams(dimension_semantics=sem, vmem_limit_bytes=VMEM_LIMIT_BYTES)


def _dot(a, b):
    return jnp.dot(a.astype(BF16), b.astype(BF16), preferred_element_type=F32)


def _dot_nt(a, b):
    return lax.dot_general(a.astype(BF16), b.astype(BF16), (((1,), (1,)), ((), ())),
                           preferred_element_type=F32)


def _split(a):
    hi = a.astype(BF16)
    lo = (a - hi.astype(F32)).astype(BF16)
    return hi, lo


def _dot3(a, b):
    ah, al = _split(a)
    bh, bl = _split(b)
    return (jnp.dot(ah, bh, preferred_element_type=F32)
            + (jnp.dot(ah, bl, preferred_element_type=F32)
               + jnp.dot(al, bh, preferred_element_type=F32)))


def _dot2_exact_rhs(a, b_bf16):
    ah, al = _split(a)
    return jnp.dot(ah, b_bf16, preferred_element_type=F32) + jnp.dot(al, b_bf16, preferred_element_type=F32)


def _dot3_nt(a, b):
    ah, al = _split(a)
    bh, bl = _split(b)
    dn = (((1,), (1,)), ((), ()))
    return (lax.dot_general(ah, bh, dn, preferred_element_type=F32)
            + (lax.dot_general(ah, bl, dn, preferred_element_type=F32)
               + lax.dot_general(al, bh, dn, preferred_element_type=F32)))


def _sigmoid(x):
    return 1.0 / (1.0 + jnp.exp(-x))


def _layer_norm_rows(z, g, b):
    mu = jnp.mean(z, axis=-1, keepdims=True)
    zc = z - mu
    var = jnp.mean(zc * zc, axis=-1, keepdims=True)
    return zc * lax.rsqrt(var + LN_EPS) * g + b


class _Layout:
    def __init__(self, bsz, ctx_len, seq):
        self.B, self.CTX, self.S = bsz, ctx_len, seq
        self.T = ctx_len + seq
        self.tm = math.gcd(256, math.gcd(ctx_len, seq))
        self.nct = ctx_len // self.tm
        self.nlt = seq // self.tm
        self.rows_c = bsz * ctx_len
        self.rows_l = bsz * seq
        self.rows = self.rows_c + self.rows_l
        self.ntiles_c = bsz * self.nct
        self.ntiles = self.rows // self.tm
        assert self.rows_c % seq == 0, "latent K/V blocks are addressed in units of S rows"
        self.tmm = math.gcd(1024, math.gcd(self.rows_c, seq))
        self.mod_rows = -(-(bsz + 1) // SUBLANES) * SUBLANES

    def seq_tile(self, b, j):
        return jnp.where(j < self.nct, b * self.nct + j, self.ntiles_c + b * self.nlt + (j - self.nct))

    def seq_block(self, i):
        il = i - self.ntiles_c
        return (jnp.where(i < self.ntiles_c, i // self.nct, il // self.nlt),
                jnp.where(i < self.ntiles_c, i % self.nct, self.nct + il % self.nlt))

    def mod_row(self, i, tile, row_offset=0):
        r = i * tile + row_offset
        return jnp.where(r < self.rows_c, self.B, (r - self.rows_c) // self.S)

    def pos_tile(self, i):
        il = i - self.ntiles_c
        return jnp.where(i < self.ntiles_c, i % self.nct, self.nct + il % self.nlt)


def _ada_body(cv_ref, w_ref, b_ref, o_ref):
    cv = cv_ref[...]
    s = cv * _sigmoid(cv)
    o_ref[0] = _dot3(s, w_ref[0]) + b_ref[0]


def _ada_mods(cvec, ada_w, ada_b):
    depth, d, n = ada_w.shape
    r = cvec.shape[0]
    tn = 512
    return pl.pallas_call(
        _ada_body,
        out_shape=jax.ShapeDtypeStruct((depth, r, n), F32),
        grid=(depth, n // tn),
        in_specs=[pl.BlockSpec((r, d), lambda l, j: (0, 0)),
                  pl.BlockSpec((1, d, tn), lambda l, j: (l, 0, j)),
                  pl.BlockSpec((1, 1, tn), lambda l, j: (l, 0, j))],
        out_specs=pl.BlockSpec((1, r, tn), lambda l, j: (l, 0, j)),
        compiler_params=_cparams(("arbitrary", "arbitrary")),
        name="ada_mods",
    )(cvec, ada_w, ada_b.reshape(depth, 1, n))


def _rope_tables(lay, head_dim, width):
    rows = lay.S // GRID_W
    rr, cc = jnp.meshgrid(jnp.arange(rows), jnp.arange(GRID_W), indexing="ij")
    row_pos = rr.reshape(-1).astype(F32)
    col_pos = cc.reshape(-1).astype(F32)
    axis_dim = head_dim // 2
    inv = ROPE_THETA ** (-jnp.arange(0, axis_dim, 2, dtype=F32) / axis_dim)
    ang = jnp.concatenate([row_pos[:, None] * inv, col_pos[:, None] * inv], -1)
    cos, sin = jnp.cos(ang), jnp.sin(ang)
    cos = jnp.concatenate([jnp.ones((lay.CTX, head_dim // 2), F32), cos], 0)
    sin = jnp.concatenate([jnp.zeros((lay.CTX, head_dim // 2), F32), sin], 0)
    cos_h = jnp.concatenate([cos, cos], -1)
    sin_h = jnp.concatenate([-sin, sin], -1)
    reps = width // head_dim
    return jnp.tile(cos_h, (1, reps)), jnp.tile(sin_h, (1, reps))


def _rope_lanes(x, cos, sin, head_dim):
    w = x.shape[-1]
    half = head_dim // 2
    if head_dim == LANES and w == LANES:
        rot = pltpu.roll(x, half, 1)
    else:
        fwd = pltpu.roll(x, w - half, 1)
        bwd = pltpu.roll(x, half, 1)
        lane = lax.broadcasted_iota(jnp.int32, x.shape, 1)
        rot = jnp.where((lane % head_dim) < half, fwd, bwd)
    return x * cos + rot * sin


def _even_in_body(x_ref, mod_ref, w_ref, cos_ref, sin_ref, pa_ref, q_ref, k_ref, v_ref):
    m = mod_ref[0]
    h = (x_ref[...] * (1.0 + m[1:2]) + m[0:1]).astype(BF16)
    pa_ref[...] = jnp.dot(h, w_ref[:, :A_IN], preferred_element_type=F32)
    cos, sin = cos_ref[...], sin_ref[...]
    o = A_IN
    q = jnp.dot(h, w_ref[:, o:o + B_QK], preferred_element_type=F32)
    q_ref[...] = (_rope_lanes(q, cos, sin, B_HEAD_DIM) * (B_HEAD_DIM ** -0.5)).astype(BF16)
    o += B_QK
    k = jnp.dot(h, w_ref[:, o:o + B_QK], preferred_element_type=F32)
    k_ref[...] = _rope_lanes(k, cos, sin, B_HEAD_DIM).astype(BF16)
    o += B_QK
    v_ref[...] = jnp.dot(h, w_ref[:, o:o + B_WIDTH], preferred_element_type=F32).astype(BF16)


def _even_in_proj(lay, x, mod, w_bf16, cos, sin):
    tm, d = lay.tm, D_MODEL
    row = lambda i: (i, 0)
    return pl.pallas_call(
        _even_in_body,
        out_shape=(jax.ShapeDtypeStruct((lay.rows, A_IN), F32),
                   jax.ShapeDtypeStruct((lay.rows, B_QK), BF16),
                   jax.ShapeDtypeStruct((lay.rows, B_QK), BF16),
                   jax.ShapeDtypeStruct((lay.rows, B_WIDTH), BF16)),
        grid=(lay.ntiles,),
        in_specs=[pl.BlockSpec((tm, d), row),
                  pl.BlockSpec((1, 6, d), lambda i: (lay.mod_row(i, tm), 0, 0)),
                  pl.BlockSpec((d, EVEN_IN), lambda i: (0, 0)),
                  pl.BlockSpec((tm, B_QK), lambda i: (lay.pos_tile(i), 0)),
                  pl.BlockSpec((tm, B_QK), lambda i: (lay.pos_tile(i), 0))],
        out_specs=(pl.BlockSpec((tm, A_IN), row), pl.BlockSpec((tm, B_QK), row),
                   pl.BlockSpec((tm, B_QK), row), pl.BlockSpec((tm, B_WIDTH), row)),
        compiler_params=_cparams(("arbitrary",)),
        name="even_in_proj",
    )(x, mod, w_bf16, cos, sin)


def _rwkv_feat_body(lay, pa_ref, prev_ref, next_ref, mu_ref, w0_ref, w2_ref, a0_ref, a2_ref, g2_ref,
                    kk_ref, ka_ref, rk_ref, bd_ref,
                    ops_ref, g_ref, bonus_ref):
    i = pl.program_id(0)
    tm = lay.tm
    il = i - lay.ntiles_c
    in_ctx = i < lay.ntiles_c
    seg_first = jnp.where(in_ctx, i % lay.nct == 0, il % lay.nlt == 0)
    seg_last = jnp.where(in_ctx, i % lay.nct == lay.nct - 1, il % lay.nlt == lay.nlt - 1)

    pa = pa_ref[...]
    row = lax.broadcasted_iota(jnp.int32, pa.shape, 0)
    prev_edge = jnp.where(seg_first, 0.0, 1.0) * prev_ref[SUBLANES - 1:SUBLANES, :]
    next_edge = jnp.where(seg_last, 0.0, 1.0) * next_ref[0:1, :]
    prev = jnp.where(row == 0, prev_edge, pltpu.roll(pa, 1, 0))
    nxt = jnp.where(row == tm - 1, next_edge, pltpu.roll(pa, tm - 1, 0))
    u = pa + (0.5 * (prev + nxt) - pa) * mu_ref[...]

    o1, o2, o3 = A_WIDTH, 2 * A_WIDTH, 3 * A_WIDTH
    o4 = o3 + 2 * A_LORA
    o5 = o4 + 2 * A_LORA
    r, k, v = u[:, :o1], u[:, o1:o2], u[:, o2:o3]
    bd = bd_ref[...]

    kk = k * kk_ref[...]
    ss = _dot2_exact_rhs(kk * kk, bd)
    kkn = kk / jnp.maximum(jnp.sqrt(ss), 1e-12)
    g = _dot3(_sigmoid(u[:, o5:]), g2_ref[...])

    g_ref[0] = g

    kd_sum = jnp.zeros_like(k)
    for d in range(2):
        ops_ref[_SCAN_R, d, 0] = r
        ops_ref[_SCAN_V, d, 0] = v
        ops_ref[_SCAN_A, d, 0] = -kkn
        wd = u[:, o3 + d * A_LORA:o3 + (d + 1) * A_LORA]
        ad = u[:, o4 + d * A_LORA:o4 + (d + 1) * A_LORA]
        z = -(w0_ref[d:d + 1, :] + _dot3(jnp.tanh(wd), w2_ref[d]))
        softplus = jnp.maximum(z, 0.0) + jnp.log(1.0 + jnp.exp(-jnp.abs(z)))
        w_log = -softplus - 0.5
        ops_ref[_SCAN_W, d, 0] = jnp.exp(-jnp.exp(w_log))
        a = _sigmoid(a0_ref[d:d + 1, :] + _dot3(ad, a2_ref[d]))
        kd = k * (1.0 + (a - 1.0) * ka_ref[...])
        ops_ref[_SCAN_K, d, 0] = kd
        ops_ref[_SCAN_B, d, 0] = kkn * a
        kd_sum = kd_sum + kd
    bonus_ref[0] = _dot2_exact_rhs(r * kd_sum * rk_ref[...], bd) * v


def _head_block_diag(width, head_dim):
    h = jnp.arange(width) // head_dim
    return (h[:, None] == h[None, :]).astype(BF16)


def _rwkv_features(lay, pa, mu, w0, w2, a0, a2, g2, k_k, k_a, r_k):
    tm = lay.tm
    hb = tm // SUBLANES
    nb8 = lay.rows // SUBLANES
    row = lambda i: (i, 0)
    full2 = lambda i: (0, 0)
    full3 = lambda i: (0, 0, 0)
    w = A_WIDTH
    out = jax.ShapeDtypeStruct((lay.B, lay.T, w), F32)
    seq = lambda i: lay.seq_block(i) + (0,)
    return pl.pallas_call(
        functools.partial(_rwkv_feat_body, lay),
        out_shape=(jax.ShapeDtypeStruct((6, 2, lay.B, lay.T, w), F32), out, out),
        grid=(lay.ntiles,),
        in_specs=[pl.BlockSpec((tm, A_IN), row),
                  pl.BlockSpec((SUBLANES, A_IN), lambda i: (jnp.maximum(i * hb - 1, 0), 0)),
                  pl.BlockSpec((SUBLANES, A_IN), lambda i: (jnp.minimum((i + 1) * hb, nb8 - 1), 0)),
                  pl.BlockSpec((1, A_IN), full2),
                  pl.BlockSpec((2, w), full2),
                  pl.BlockSpec((2, A_LORA, w), full3),
                  pl.BlockSpec((2, w), full2),
                  pl.BlockSpec((2, A_LORA, w), full3),
                  pl.BlockSpec((A_GATE_LORA, w), full2),
                  pl.BlockSpec((1, w), full2),
                  pl.BlockSpec((1, w), full2),
                  pl.BlockSpec((1, w), full2),
                  pl.BlockSpec((w, w), full2)],
        out_specs=(pl.BlockSpec((6, 2, 1, tm, w), lambda i: (0, 0) + lay.seq_block(i) + (0,)),
                   pl.BlockSpec((1, tm, w), seq), pl.BlockSpec((1, tm, w), seq)),
        compiler_params=_cparams(("arbitrary",)),
        name="rwkv_features",
    )(pa, pa, pa, mu.reshape(1, A_IN), w0, w2, a0, a2, g2, k_k.reshape(1, w), k_a.reshape(1, w),
      r_k.reshape(1, w), _head_block_diag(w, A_HEAD_DIM))


def _wkv_scan_body(tb, fwd_ref, bwd_ref, yf_ref, yb_ref, s_ref, m_ref):
    n = A_HEAD_DIM
    chains = s_ref.shape[-1]
    is_fwd = lax.broadcasted_iota(jnp.int32, (n, chains), 1) < chains // 2

    @pl.when(pl.program_id(0) == 0)
    def _():
        s_ref[...] = jnp.zeros_like(s_ref)

    def step(t, carry):
        tr = tb - 1 - t
        for idx in range(6):
            m_ref[idx] = jnp.where(is_fwd, fwd_ref[t, idx], bwd_ref[tr, idx])
        sa = jnp.zeros((n, chains), F32)
        for kk in range(n):
            sa = sa + s_ref[kk] * m_ref[_SCAN_A, kk:kk + 1, :]
        v_t = m_ref[_SCAN_V]
        y = jnp.zeros((n, chains), F32)
        for kk in range(n):
            s_new = (s_ref[kk] * m_ref[_SCAN_W, kk:kk + 1, :] + sa * m_ref[_SCAN_B, kk:kk + 1, :]
                     + v_t * m_ref[_SCAN_K, kk:kk + 1, :])
            s_ref[kk] = s_new
            y = y + s_new * m_ref[_SCAN_R, kk:kk + 1, :]
        yf_ref[t] = y
        yb_ref[tr] = y
        return carry

    lax.fori_loop(0, tb, step, 0)


def _wkv_scan(lay, ops):
    t, nops, n, lanes = ops.shape
    tb = math.gcd(32, math.gcd(lay.CTX, lay.S))
    nctb, nt = lay.CTX // tb, t // tb
    bwd_tile = lambda g: jnp.where(g < nctb, nctb - 1 - g, nt - 1 - (g - nctb))
    out = jax.ShapeDtypeStruct((t, n, lanes), F32)
    return pl.pallas_call(
        functools.partial(_wkv_scan_body, tb),
        out_shape=(out, out),
        grid=(nt,),
        in_specs=[pl.BlockSpec((tb, nops, n, lanes), lambda g: (g, 0, 0, 0)),
                  pl.BlockSpec((tb, nops, n, lanes), lambda g: (bwd_tile(g), 0, 0, 0))],
        out_specs=(pl.BlockSpec((tb, n, lanes), lambda g: (g, 0, 0)),
                   pl.BlockSpec((tb, n, lanes), lambda g: (bwd_tile(g), 0, 0))),
        scratch_shapes=[pltpu.VMEM((n, n, lanes), F32), pltpu.VMEM((nops, n, lanes), F32)],
        compiler_params=_cparams(("arbitrary",)),
        name="wkv7_scan",
    )(ops, ops)


def _to_chains(lay, ops):
    h, n = A_HEADS, A_HEAD_DIM
    nops = ops.shape[0]
    x = ops.reshape(nops, 2, lay.B, lay.T, h, n).transpose(3, 0, 5, 1, 2, 4)
    return x.reshape(lay.T, nops, n, 2 * lay.B * h)


def _from_chains(lay, y):
    h, n = A_HEADS, A_HEAD_DIM
    return y.reshape(lay.T, n, lay.B, h).transpose(2, 0, 3, 1).reshape(lay.B, lay.T, h * n)


def _rwkv_readout_body(yf_ref, yb_ref, bonus_ref, g_ref, lg_ref, lb_ref, bd_ref, o_ref):
    y = yf_ref[0] + yb_ref[0]
    bd = bd_ref[...]
    inv_n = 1.0 / A_HEAD_DIM
    mu = _dot2_exact_rhs(y, bd) * inv_n
    yc = y - mu
    var = _dot2_exact_rhs(yc * yc, bd) * inv_n
    yn = yc * lax.rsqrt(var + A_GN_EPS) * lg_ref[...] + lb_ref[...]
    o_ref[...] = ((yn + bonus_ref[0]) * g_ref[0]).astype(BF16)


def _rwkv_readout(lay, yf, yb, bonus, g, lnx_g, lnx_b):
    tm, w = lay.tm, A_WIDTH
    row = lambda i: (i, 0)
    full = lambda i: (0, 0)
    seq = lambda i: lay.seq_block(i) + (0,)
    return pl.pallas_call(
        _rwkv_readout_body,
        out_shape=jax.ShapeDtypeStruct((lay.rows, w), BF16),
        grid=(lay.ntiles,),
        in_specs=[pl.BlockSpec((1, tm, w), seq)] * 4 + [pl.BlockSpec((1, w), full)] * 2 + [pl.BlockSpec((w, w), full)],
        out_specs=pl.BlockSpec((tm, w), row),
        compiler_params=_cparams(("arbitrary",)),
        name="rwkv_readout",
    )(yf, yb, bonus, g, lnx_g.reshape(1, w), lnx_b.reshape(1, w), _head_block_diag(w, A_HEAD_DIM))


def _softmax_pv(q, keys, vals):
    scores = [_dot_nt(q, kk) for kk in keys]
    m = scores[0].max(axis=-1, keepdims=True)
    for s in scores[1:]:
        m = jnp.maximum(m, s.max(axis=-1, keepdims=True))
    l = None
    o = None
    for s, vv in zip(scores, vals):
        e = jnp.exp(s - m)
        ls = e.sum(axis=-1, keepdims=True)
        os_ = jnp.dot(e.astype(BF16), vv, preferred_element_type=F32)
        l = ls if l is None else l + ls
        o = os_ if o is None else o + os_
    return o / l


def _diff_attn_body(lay, lambda_init, q_ref, kc_ref, kl_ref, vc_ref, vl_ref, lam_ref, g_ref, o_ref):
    j = pl.program_id(1)
    lv = lam_ref[...]
    lam = (jnp.exp(jnp.sum(lv[0:1] * lv[1:2], axis=1, keepdims=True))
           - jnp.exp(jnp.sum(lv[2:3] * lv[3:4], axis=1, keepdims=True)) + lambda_init)

    def run(with_latent):
        for h in range(B_HEADS):
            vs = slice(h * B_V_DIM, (h + 1) * B_V_DIM)
            vals = [vc_ref[:, vs]] + ([vl_ref[:, vs]] if with_latent else [])
            outs = []
            for mi in range(2):
                cs = slice(h * B_V_DIM + mi * B_HEAD_DIM, h * B_V_DIM + (mi + 1) * B_HEAD_DIM)
                keys = [kc_ref[:, cs]] + ([kl_ref[:, cs]] if with_latent else [])
                outs.append(_softmax_pv(q_ref[:, cs], keys, vals))
            o = outs[0] - lam * outs[1]
            ms = jnp.mean(o * o, axis=-1, keepdims=True)
            o = o * lax.rsqrt(ms + B_SUBLN_EPS) * g_ref[...] * (1.0 - lambda_init)
            o_ref[:, vs] = o.astype(BF16)

    @pl.when(j < lay.nct)
    def _():
        run(False)

    @pl.when(j >= lay.nct)
    def _():
        run(True)


def _diff_attention(lay, q, k, v, lam_vecs, subln_g, lambda_init):
    tm = lay.tm
    w = B_WIDTH
    lat0 = lay.rows_c // lay.S
    return pl.pallas_call(
        functools.partial(_diff_attn_body, lay, lambda_init),
        out_shape=jax.ShapeDtypeStruct((lay.rows, w), BF16),
        grid=(lay.B, lay.nct + lay.nlt),
        in_specs=[pl.BlockSpec((tm, w), lambda b, j: (lay.seq_tile(b, j), 0)),
                  pl.BlockSpec((lay.CTX, w), lambda b, j: (b, 0)),
                  pl.BlockSpec((lay.S, w), lambda b, j: (lat0 + b, 0)),
                  pl.BlockSpec((lay.CTX, w), lambda b, j: (b, 0)),
                  pl.BlockSpec((lay.S, w), lambda b, j: (lat0 + b, 0)),
                  pl.BlockSpec((4, B_HEAD_DIM), lambda b, j: (0, 0)),
                  pl.BlockSpec((1, B_V_DIM), lambda b, j: (0, 0))],
        out_specs=pl.BlockSpec((tm, w), lambda b, j: (lay.seq_tile(b, j), 0)),
        compiler_params=_cparams(("arbitrary", "arbitrary")),
        name="diff_attention",
    )(q, k, k, v, v, lam_vecs, subln_g.reshape(1, B_V_DIM))


def _odd_in_body(x_ref, mod_ref, w_ref, cos_ref, sin_ref, qn_ref, kn_ref, q_ref, k_ref, v_ref):
    m = mod_ref[0]
    h = (x_ref[...] * (1.0 + m[1:2]) + m[0:1]).astype(BF16)
    cos, sin = cos_ref[...], sin_ref[...]

    def norm_rope(p, g, scale):
        ms = jnp.mean(p * p, axis=-1, keepdims=True)
        y = p * lax.rsqrt(ms + QK_NORM_EPS) * g
        return (_rope_lanes(y, cos, sin, C_HEAD_DIM) * scale).astype(BF16)

    for hd in range(C_HEADS):
        cs = slice(hd * C_HEAD_DIM, (hd + 1) * C_HEAD_DIM)
        p = jnp.dot(h, w_ref[:, cs], preferred_element_type=F32)
        q_ref[:, cs] = norm_rope(p, qn_ref[...], C_HEAD_DIM ** -0.5)
    for hd in range(C_KV_HEADS):
        cs = slice(hd * C_HEAD_DIM, (hd + 1) * C_HEAD_DIM)
        p = jnp.dot(h, w_ref[:, C_Q + hd * C_HEAD_DIM:C_Q + (hd + 1) * C_HEAD_DIM], preferred_element_type=F32)
        k_ref[:, cs] = norm_rope(p, kn_ref[...], 1.0)
    v_ref[...] = jnp.dot(h, w_ref[:, C_Q + C_KV:], preferred_element_type=F32).astype(BF16)


def _odd_in_proj(lay, x, mod, w_bf16, cos, sin, qn_g, kn_g):
    tm, d = lay.tm, D_MODEL
    row = lambda i: (i, 0)
    full = lambda i: (0, 0)
    return pl.pallas_call(
        _odd_in_body,
        out_shape=(jax.ShapeDtypeStruct((lay.rows, C_Q), BF16),
                   jax.ShapeDtypeStruct((lay.rows, C_KV), BF16),
                   jax.ShapeDtypeStruct((lay.rows, C_KV), BF16)),
        grid=(lay.ntiles,),
        in_specs=[pl.BlockSpec((tm, d), row),
                  pl.BlockSpec((1, 6, d), lambda i: (lay.mod_row(i, tm), 0, 0)),
                  pl.BlockSpec((d, ODD_IN), full),
                  pl.BlockSpec((tm, C_HEAD_DIM), lambda i: (lay.pos_tile(i), 0)),
                  pl.BlockSpec((tm, C_HEAD_DIM), lambda i: (lay.pos_tile(i), 0)),
                  pl.BlockSpec((1, C_HEAD_DIM), full),
                  pl.BlockSpec((1, C_HEAD_DIM), full)],
        out_specs=(pl.BlockSpec((tm, C_Q), row), pl.BlockSpec((tm, C_KV), row), pl.BlockSpec((tm, C_KV), row)),
        compiler_params=_cparams(("arbitrary",)),
        name="odd_in_proj",
    )(x, mod, w_bf16, cos, sin, qn_g.reshape(1, C_HEAD_DIM), kn_g.reshape(1, C_HEAD_DIM))


def _gqa_body(q_ref, kc_ref, kl_ref, vc_ref, vl_ref, o_ref):
    keys = [kc_ref[...], kl_ref[...]]
    vals = [vc_ref[...], vl_ref[...]]
    for g in range(C_GROUP):
        cs = slice(g * C_HEAD_DIM, (g + 1) * C_HEAD_DIM)
        o_ref[:, cs] = _softmax_pv(q_ref[:, cs], keys, vals).astype(BF16)


def _gqa_attention(lay, q, k, v):
    tm = lay.tm
    gw = C_GROUP * C_HEAD_DIM
    lat0 = lay.rows_c // lay.S
    hd = C_HEAD_DIM
    return pl.pallas_call(
        _gqa_body,
        out_shape=jax.ShapeDtypeStruct((lay.rows_l, C_Q), BF16),
        grid=(lay.B, C_KV_HEADS, lay.nlt),
        in_specs=[pl.BlockSpec((tm, gw), lambda b, h, j: (lay.ntiles_c + b * lay.nlt + j, h)),
                  pl.BlockSpec((lay.CTX, hd), lambda b, h, j: (b, h)),
                  pl.BlockSpec((lay.S, hd), lambda b, h, j: (lat0 + b, h)),
                  pl.BlockSpec((lay.CTX, hd), lambda b, h, j: (b, h)),
                  pl.BlockSpec((lay.S, hd), lambda b, h, j: (lat0 + b, h))],
        out_specs=pl.BlockSpec((tm, gw), lambda b, h, j: (b * lay.nlt + j, h)),
        compiler_params=_cparams(("arbitrary", "arbitrary", "arbitrary")),
        name="gqa_attention",
    )(q, k, k, v, v)


def _out_proj_body(n_mix, *refs):
    mix_refs = refs[:n_mix]
    w_ref, x_ref, mod_ref, lg_ref, lb_ref, rw_ref, xo_ref, h_ref, lt_ref = refs[n_mix:]
    m = mod_ref[0]
    off = 0
    mix = None
    for mr in mix_refs:
        kw = mr.shape[-1]
        part = jnp.dot(mr[...], w_ref[off:off + kw, :], preferred_element_type=F32)
        mix = part if mix is None else mix + part
        off += kw
    z = DEEPNORM_ALPHA * x_ref[...] + m[2:3] * mix
    xn = _layer_norm_rows(z, lg_ref[...], lb_ref[...])
    xo_ref[...] = xn
    h2 = xn * (1.0 + m[4:5]) + m[3:4]
    h_ref[...] = h2
    lt_ref[...] = _dot3_nt(rw_ref[...], h2)


def _out_proj(lay, mixes, w_bf16, x, x_tile_off, mod, ln_g, ln_b, router_wt, n_rows):
    tm, d = lay.tm, D_MODEL
    row = lambda i: (i, 0)
    full = lambda i: (0, 0)
    row_off = x_tile_off * tm
    in_specs = [pl.BlockSpec((tm, mx.shape[-1]), row) for mx in mixes]
    in_specs += [pl.BlockSpec((d, d), full),
                 pl.BlockSpec((tm, d), lambda i: (i + x_tile_off, 0)),
                 pl.BlockSpec((1, 6, d), lambda i: (lay.mod_row(i, tm, row_off), 0, 0)),
                 pl.BlockSpec((1, d), full), pl.BlockSpec((1, d), full),
                 pl.BlockSpec((N_EXPERTS, d), full)]
    return pl.pallas_call(
        functools.partial(_out_proj_body, len(mixes)),
        out_shape=(jax.ShapeDtypeStruct((n_rows, d), F32),
                   jax.ShapeDtypeStruct((n_rows, d), F32),
                   jax.ShapeDtypeStruct((N_EXPERTS, n_rows), F32)),
        grid=(n_rows // tm,),
        in_specs=in_specs,
        out_specs=(pl.BlockSpec((tm, d), row), pl.BlockSpec((tm, d), row),
                   pl.BlockSpec((N_EXPERTS, tm), lambda i: (0, i))),
        compiler_params=_cparams(("arbitrary",)),
        name="out_proj_ln",
    )(*mixes, w_bf16, x, mod, ln_g.reshape(1, d), ln_b.reshape(1, d), router_wt)


_EXPERT_PAIRS = ((0, 1), (0, 2), (0, 3), (1, 2), (1, 3), (2, 3))
N_BUCKETS = N_GROUPS * len(_EXPERT_PAIRS)


def _router_body(lt_ref, rb_ref, o_ref):
    logits = lt_ref[...] + rb_ref[...]
    rows = [logits[e:e + 1, :] for e in range(N_EXPERTS)]
    m = rows[0]
    for x in rows[1:]:
        m = jnp.maximum(m, x)
    ex = [jnp.exp(x - m) for x in rows]
    z = ex[0]
    for x in ex[1:]:
        z = z + x
    p = [x / z for x in ex]

    gscore = []
    for g in range(N_GROUPS):
        a, b, c, d = p[4 * g:4 * g + 4]
        hi1, lo1 = jnp.maximum(a, b), jnp.minimum(a, b)
        hi2, lo2 = jnp.maximum(c, d), jnp.minimum(c, d)
        top1 = jnp.maximum(hi1, hi2)
        top2 = jnp.maximum(jnp.minimum(hi1, hi2), jnp.maximum(lo1, lo2))
        gscore.append(top1 + top2)
    best = []
    for g in range(N_GROUPS):
        ok = None
        for o in range(N_GROUPS):
            if o == g:
                continue
            c = (gscore[g] > gscore[o]) if o < g else (gscore[g] >= gscore[o])
            ok = c if ok is None else jnp.logical_and(ok, c)
        best.append(ok)
    won = []
    for e in range(N_EXPERTS):
        g = e // EXPERTS_PER_GROUP
        rank = jnp.zeros_like(p[e])
        for o in range(4 * g, 4 * g + 4):
            if o == e:
                continue
            ahead = (p[o] > p[e]) if o > e else (p[o] >= p[e])
            rank = rank + jnp.where(ahead, 1.0, 0.0)
        won.append(jnp.where(jnp.logical_and(best[g], rank < 1.5), 1.0, 0.0))
    tot = won[0] * p[0]
    for e in range(1, N_EXPERTS):
        tot = tot + won[e] * p[e]
    bucket = jnp.zeros_like(tot)
    gate_a = jnp.zeros_like(tot)
    gate_b = jnp.zeros_like(tot)
    for g in range(N_GROUPS):
        for pid, (a, b) in enumerate(_EXPERT_PAIRS):
            ind = won[4 * g + a] * won[4 * g + b]
            bucket = bucket + ind * float(len(_EXPERT_PAIRS) * g + pid)
            gate_a = gate_a + ind * p[4 * g + a]
            gate_b = gate_b + ind * p[4 * g + b]
    o_ref[...] = jnp.zeros_like(o_ref)
    o_ref[0:1, :] = bucket
    o_ref[1:2, :] = gate_a / tot
    o_ref[2:3, :] = gate_b / tot


def _router(logits_t, router_b):
    e, n = logits_t.shape
    tr = math.gcd(2048, n)
    return pl.pallas_call(
        _router_body,
        out_shape=jax.ShapeDtypeStruct((SUBLANES, n), F32),
        grid=(n // tr,),
        in_specs=[pl.BlockSpec((e, tr), lambda i: (0, i)), pl.BlockSpec((e, 1), lambda i: (0, 0))],
        out_specs=pl.BlockSpec((SUBLANES, tr), lambda i: (0, i)),
        compiler_params=_cparams(("arbitrary",)),
        name="router_gates",
    )(logits_t, router_b.reshape(e, 1))


GATE_LANES = LANES


def _moe_plan(bucket, tg):
    n = bucket.shape[0]
    n_tiles = n // tg + N_BUCKETS
    ids = jnp.arange(N_BUCKETS, dtype=jnp.int32)
    onehot = (bucket[:, None] == ids[None, :]).astype(jnp.int32)
    csum = jnp.cumsum(onehot, axis=0)
    counts = csum[-1]
    tiles = (counts + tg - 1) // tg
    tile_end = jnp.cumsum(tiles)
    row_start = (tile_end - tiles) * tg
    pos = jnp.sum(onehot * (row_start[None, :] + csum - 1), axis=1).astype(jnp.int32)
    n_used = tile_end[-1]
    tile_ids = jnp.minimum(jnp.arange(n_tiles, dtype=jnp.int32), n_used - 1)
    tile_bucket = jnp.sum((tile_ids[:, None] >= tile_end[None, :]).astype(jnp.int32), axis=1)
    pair_a = jnp.asarray([a for a, _ in _EXPERT_PAIRS], jnp.int32)
    pair_b = jnp.asarray([b for _, b in _EXPERT_PAIRS], jnp.int32)
    grp, pid = tile_bucket // len(_EXPERT_PAIRS), tile_bucket % len(_EXPERT_PAIRS)
    ea = grp * EXPERTS_PER_GROUP + pair_a[pid]
    eb = grp * EXPERTS_PER_GROUP + pair_b[pid]
    return pos, ea.astype(jnp.int32), eb.astype(jnp.int32), n_used.reshape(1).astype(jnp.int32), n_tiles


def _row_copy(src_ref, src_row, dst_ref, dst_row, sem):
    return pltpu.make_async_copy(src_ref.at[pl.ds(src_row, 1), :], dst_ref.at[pl.ds(dst_row, 1), :], sem)


ROW_DMA_UNROLL = 8


def _row_copies(n_rows, row_copy, whole_tile_copy):
    def issue(blk, carry):
        for u in range(ROW_DMA_UNROLL):
            row_copy(blk * ROW_DMA_UNROLL + u).start(priority=u % 2)
        return carry

    lax.fori_loop(0, n_rows // ROW_DMA_UNROLL, issue, 0)
    whole_tile_copy.wait()


def _dispatch_body(tg, pos_ref, h_ref, g_ref, xs_in_ref, xs_ref, aug_ref, sem):
    del xs_in_ref
    base = pl.program_id(0) * tg
    aug_ref[:, :D_MODEL] = h_ref[...]
    aug_ref[:, D_MODEL:] = g_ref[...]

    _row_copies(tg, lambda r: _row_copy(aug_ref, r, xs_ref, pos_ref[base + r], sem),
                pltpu.make_async_copy(aug_ref, xs_ref.at[pl.ds(0, tg), :], sem))


def _moe_dispatch(pos, h2, gate_rows, n_tiles, tg):
    n, d = h2.shape
    wide = d + GATE_LANES
    xs0 = jnp.zeros((n_tiles * tg, wide), F32)
    grid_spec = pltpu.PrefetchScalarGridSpec(
        num_scalar_prefetch=1,
        grid=(n // tg,),
        in_specs=[pl.BlockSpec((tg, d), lambda i, pos_ref: (i, 0)),
                  pl.BlockSpec((tg, GATE_LANES), lambda i, pos_ref: (i, 0)),
                  pl.BlockSpec(memory_space=pl.ANY)],
        out_specs=pl.BlockSpec(memory_space=pl.ANY),
        scratch_shapes=[pltpu.VMEM((tg, wide), F32), pltpu.SemaphoreType.DMA(())],
    )
    return pl.pallas_call(
        functools.partial(_dispatch_body, tg),
        out_shape=jax.ShapeDtypeStruct(xs0.shape, F32),
        grid_spec=grid_spec,
        input_output_aliases={3: 0},
        compiler_params=_cparams(("arbitrary",)),
        name="moe_dispatch",
    )(pos, h2, gate_rows, xs0)


def _grouped_body(ea_ref, eb_ref, nu_ref, xs_ref, w1a_ref, w3a_ref, w2a_ref, w1b_ref, w3b_ref, w2b_ref, y_ref):
    del ea_ref, eb_ref

    @pl.when(pl.program_id(0) < nu_ref[0])
    def _():
        x = xs_ref[:, :D_MODEL].astype(BF16)

        def expert(w1_ref, w3_ref, w2_ref, gate):
            h1 = jnp.dot(x, w1_ref[0], preferred_element_type=F32)
            h3 = jnp.dot(x, w3_ref[0], preferred_element_type=F32)
            hid = (h1 * _sigmoid(h1) * h3 * gate).astype(BF16)
            return jnp.dot(hid, w2_ref[0], preferred_element_type=F32)

        y_ref[...] = (expert(w1a_ref, w3a_ref, w2a_ref, xs_ref[:, D_MODEL:D_MODEL + 1])
                      + expert(w1b_ref, w3b_ref, w2b_ref, xs_ref[:, D_MODEL + 1:D_MODEL + 2]))

    @pl.when(pl.program_id(0) >= nu_ref[0])
    def _():
        y_ref[...] = jnp.zeros_like(y_ref)


def _moe_grouped(xs, ea, eb, n_used, w1, w3, w2, tg):
    rows, wide = xs.shape
    d, ff = D_MODEL, EXPERT_FF
    n_tiles = rows // tg
    tile = lambda i, ea_r, eb_r, nu_r: (jnp.minimum(i, nu_r[0] - 1), 0)
    up_a = pl.BlockSpec((1, d, ff), lambda i, ea_r, eb_r, nu_r: (ea_r[i], 0, 0))
    up_b = pl.BlockSpec((1, d, ff), lambda i, ea_r, eb_r, nu_r: (eb_r[i], 0, 0))
    dn_a = pl.BlockSpec((1, ff, d), lambda i, ea_r, eb_r, nu_r: (ea_r[i], 0, 0))
    dn_b = pl.BlockSpec((1, ff, d), lambda i, ea_r, eb_r, nu_r: (eb_r[i], 0, 0))
    grid_spec = pltpu.PrefetchScalarGridSpec(
        num_scalar_prefetch=3,
        grid=(n_tiles,),
        in_specs=[pl.BlockSpec((tg, wide), tile), up_a, up_a, dn_a, up_b, up_b, dn_b],
        out_specs=pl.BlockSpec((tg, d), lambda i, ea_r, eb_r, nu_r: (i, 0)),
    )
    return pl.pallas_call(
        _grouped_body,
        out_shape=jax.ShapeDtypeStruct((rows, d), F32),
        grid_spec=grid_spec,
        compiler_params=_cparams(("arbitrary",)),
        name="moe_grouped",
    )(ea, eb, n_used, xs, w1, w3, w2, w1, w3, w2)


def _combine_body(tg, pos_ref, ys_ref, x_ref, mod_ref, lg_ref, lb_ref, o_ref, buf_ref, sem):
    base = pl.program_id(0) * tg

    _row_copies(tg, lambda r: _row_copy(ys_ref, pos_ref[base + r], buf_ref, r, sem),
                pltpu.make_async_copy(ys_ref.at[pl.ds(0, tg), :], buf_ref, sem))
    m = mod_ref[0]
    z = DEEPNORM_ALPHA * x_ref[...] + m[5:6] * buf_ref[...]
    o_ref[...] = _layer_norm_rows(z, lg_ref[...], lb_ref[...])


def _moe_combine(lay, pos, ys, x, mod, ln_g, ln_b, row_off, tg):
    n, d = x.shape
    grid_spec = pltpu.PrefetchScalarGridSpec(
        num_scalar_prefetch=1,
        grid=(n // tg,),
        in_specs=[pl.BlockSpec(memory_space=pl.ANY),
                  pl.BlockSpec((tg, d), lambda i, pos_ref: (i, 0)),
                  pl.BlockSpec((1, 6, d), lambda i, pos_ref: (lay.mod_row(i, tg, row_off), 0, 0)),
                  pl.BlockSpec((1, d), lambda i, pos_ref: (0, 0)),
                  pl.BlockSpec((1, d), lambda i, pos_ref: (0, 0))],
        out_specs=pl.BlockSpec((tg, d), lambda i, pos_ref: (i, 0)),
        scratch_shapes=[pltpu.VMEM((tg, d), F32), pltpu.SemaphoreType.DMA(())],
    )
    return pl.pallas_call(
        functools.partial(_combine_body, tg),
        out_shape=jax.ShapeDtypeStruct((n, d), F32),
        grid_spec=grid_spec,
        compiler_params=_cparams(("arbitrary",)),
        name="moe_combine_ln",
    )(pos, ys, x, mod, ln_g.reshape(1, d), ln_b.reshape(1, d))


def _moe(lay, h2, routed, w1, w3, w2, x, mod, ln_g, ln_b, row_off):
    tg = lay.tm
    bucket = routed[0].astype(jnp.int32)
    gate_rows = jnp.pad(routed[1:3].T, ((0, 0), (0, GATE_LANES - 2)))
    pos, ea, eb, n_used, n_tiles = _moe_plan(bucket, tg)
    xs = _moe_dispatch(pos, h2, gate_rows, n_tiles, tg)
    ys = _moe_grouped(xs, ea, eb, n_used, w1, w3, w2, tg)
    return _moe_combine(lay, pos, ys, x, mod, ln_g, ln_b, row_off, tg)


def kernel(x, c, ctx, c_ctx, router_w, router_b, ada_w, ada_b, ln1_g, ln1_b, ln2_g, ln2_b, moe_w1, moe_w3, moe_w2, ev_w_in, ev_w_out, ev_a_mu, ev_a_w0, ev_a_w2, ev_a_a0, ev_a_a2, ev_a_g2, ev_a_kk, ev_a_ka, ev_a_rk, ev_a_lnx_g, ev_a_lnx_b, ev_b_lam, ev_b_subln_g, od_w_in, od_w_out, od_qn_g, od_kn_g):
    bsz, seq, d = x.shape
    ctx_len = ctx.shape[1]
    assert d == D_MODEL and ada_w.shape[0] == DEPTH and seq % GRID_W == 0
    lay = _Layout(bsz, ctx_len, seq)

    cvec = jnp.zeros((lay.mod_rows, d), F32).at[:bsz].set(c).at[bsz].set(c_ctx)
    mods = _ada_mods(cvec, ada_w, ada_b).reshape(DEPTH, lay.mod_rows, 6, d)

    xs = jnp.concatenate([ctx.reshape(lay.rows_c, d), x.reshape(lay.rows_l, d)], axis=0)
    router_wt = router_w.T

    cos_b, sin_b = _rope_tables(lay, B_HEAD_DIM, B_QK)
    cos_c, sin_c = _rope_tables(lay, C_HEAD_DIM, C_HEAD_DIM)

    for i in range(DEPTH):
        last = i == DEPTH - 1
        j = i // 2
        mod = mods[i]
        if i % 2 == 0:
            lambda_init = 0.8 - 0.6 * math.exp(-0.3 * i)
            pa, q, k, v = _even_in_proj(lay, xs, mod, ev_w_in[j].astype(BF16), cos_b, sin_b)
            scan_ops, g_, bonus = _rwkv_features(
                lay, pa, ev_a_mu[j], ev_a_w0[j], ev_a_w2[j], ev_a_a0[j], ev_a_a2[j], ev_a_g2[j],
                ev_a_kk[j], ev_a_ka[j], ev_a_rk[j].reshape(-1))
            y_f, y_b = _wkv_scan(lay, _to_chains(lay, scan_ops))
            nl = lay.B * A_HEADS
            yf = _from_chains(lay, y_f[:, :, :nl])
            yb = _from_chains(lay, y_b[:, :, nl:])
            ya = _rwkv_readout(lay, yf, yb, bonus, g_, ev_a_lnx_g[j], ev_a_lnx_b[j])
            yd = _diff_attention(lay, q, k, v, ev_b_lam[j], ev_b_subln_g[j], lambda_init)
            mixes, w_out = [ya, yd], ev_w_out[j]
            n_rows, tile_off = (lay.rows_l, lay.ntiles_c) if last else (lay.rows, 0)
            if last:
                mixes = [mx[lay.rows_c:] for mx in mixes]
        else:
            q, k, v = _odd_in_proj(lay, xs, mod, od_w_in[j].astype(BF16), cos_c, sin_c, od_qn_g[j], od_kn_g[j])
            assert last, "an odd layer that must also update the context stream is not supported"
            o = _gqa_attention(lay, q, k, v)
            mixes, w_out = [o], od_w_out[j]
            n_rows, tile_off = lay.rows_l, lay.ntiles_c
        row_off = tile_off * lay.tm
        x_new, h2, logits_t = _out_proj(lay, mixes, w_out.astype(BF16), xs, tile_off, mod,
                                        ln1_g[i], ln1_b[i], router_wt, n_rows)
        routed = _router(logits_t, router_b)
        xs = _moe(lay, h2, routed, moe_w1[i].astype(BF16), moe_w3[i].astype(BF16), moe_w2[i].astype(BF16),
                  x_new, mod, ln2_g[i], ln2_b[i], row_off)
    return xs.reshape(bsz, seq, d)
```

```python
import functools
import math

import jax
import jax.numpy as jnp
from jax import lax
from jax.experimental import pallas as pl
from jax.experimental.pallas import tpu as pltpu

F32 = jnp.float32
BF16 = jnp.bfloat16

D_MODEL = 1024
DEPTH = 2
GRID_W = 64
ROPE_THETA = 10000.0
LN_EPS = 1e-5
LOG2_E = 1.4426950408889634
DEEPNORM_ALPHA = (2 * DEPTH) ** 0.25

A_HEAD_DIM = 64
A_HEADS = 8
A_WIDTH = 512
A_LORA = 64
A_GATE_LORA = 128
A_GN_EPS = 64e-5
A_IN = 3 * A_WIDTH + 4 * A_LORA + A_GATE_LORA
_SCAN_W, _SCAN_R, _SCAN_K, _SCAN_V, _SCAN_A, _SCAN_B = range(6)
N_SCAN_BF16 = 5

B_HEAD_DIM = 64
B_V_DIM = 128
B_HEADS = 4
B_WIDTH = 512
B_QK = 512
B_SUBLN_EPS = 1e-5
EVEN_IN = A_IN + 2 * B_QK + B_WIDTH

C_HEAD_DIM = 128
C_HEADS = 8
C_KV_HEADS = 2
C_GROUP = 4
C_Q = 1024
C_KV = 256
ODD_IN = C_Q + 2 * C_KV
QK_NORM_EPS = 1e-6

N_EXPERTS = 16
N_GROUPS = 4
EXPERTS_PER_GROUP = 4
EXPERT_FF = 512

VMEM_LIMIT_BYTES = 56 * 1024 * 1024
LANES = 128
SUBLANES = 8


def _cparams(sem):
    return pltpu.CompilerParams(dimension_semantics=sem, vmem_limit_bytes=VMEM_LIMIT_BYTES)


def _dot(a, b):
    return jnp.dot(a.astype(BF16), b.astype(BF16), preferred_element_type=F32)


def _dot_nt(a, b):
    return lax.dot_general(a.astype(BF16), b.astype(BF16), (((1,), (1,)), ((), ())),
                           preferred_element_type=F32)


def _split(a):
    hi = a.astype(BF16)
    lo = (a - hi.astype(F32)).astype(BF16)
    return hi, lo


def _dot3(a, b):
    ah, al = _split(a)
    bh, bl = _split(b)
    return (jnp.dot(ah, bh, preferred_element_type=F32)
            + (jnp.dot(ah, bl, preferred_element_type=F32)
               + jnp.dot(al, bh, preferred_element_type=F32)))


def _dot2_exact_rhs(a, b_bf16):
    ah, al = _split(a)
    return jnp.dot(ah, b_bf16, preferred_element_type=F32) + jnp.dot(al, b_bf16, preferred_element_type=F32)


def _dot3_nt(a, b):
    ah, al = _split(a)
    bh, bl = _split(b)
    dn = (((1,), (1,)), ((), ()))
    return (lax.dot_general(ah, bh, dn, preferred_element_type=F32)
            + (lax.dot_general(ah, bl, dn, preferred_element_type=F32)
               + lax.dot_general(al, bh, dn, preferred_element_type=F32)))


def _sigmoid(x):
    return 1.0 / (1.0 + jnp.exp(-x))


def _layer_norm_rows(z, g, b):
    mu = jnp.mean(z, axis=-1, keepdims=True)
    zc = z - mu
    var = jnp.mean(zc * zc, axis=-1, keepdims=True)
    return zc * lax.rsqrt(var + LN_EPS) * g + b


class _Layout:
    def __init__(self, bsz, ctx_len, seq):
        self.B, self.CTX, self.S = bsz, ctx_len, seq
        self.T = ctx_len + seq
        self.tm = math.gcd(256, math.gcd(ctx_len, seq))
        self.nct = ctx_len // self.tm
        self.nlt = seq // self.tm
        self.rows_c = bsz * ctx_len
        self.rows_l = bsz * seq
        self.rows = self.rows_c + self.rows_l
        self.ntiles_c = bsz * self.nct
        self.ntiles = self.rows // self.tm
        assert self.rows_c % seq == 0, "latent K/V blocks are addressed in units of S rows"
        self.tmm = math.gcd(1024, math.gcd(self.rows_c, seq))
        self.mod_rows = -(-(bsz + 1) // SUBLANES) * SUBLANES

    def seq_tile(self, b, j):
        return jnp.where(j < self.nct, b * self.nct + j, self.ntiles_c + b * self.nlt + (j - self.nct))

    def seq_block(self, i):
        il = i - self.ntiles_c
        return (jnp.where(i < self.ntiles_c, i // self.nct, il // self.nlt),
                jnp.where(i < self.ntiles_c, i % self.nct, self.nct + il % self.nlt))

    def mod_row(self, i, tile, row_offset=0):
        r = i * tile + row_offset
        return jnp.where(r < self.rows_c, self.B, (r - self.rows_c) // self.S)

    def pos_tile(self, i):
        il = i - self.ntiles_c
        return jnp.where(i < self.ntiles_c, i % self.nct, self.nct + il % self.nlt)


def _ada_body(cv_ref, w_ref, b_ref, o_ref):
    cv = cv_ref[...]
    s = cv * _sigmoid(cv)
    o_ref[0] = _dot3(s, w_ref[0]) + b_ref[0]


def _ada_mods(cvec, ada_w, ada_b):
    depth, d, n = ada_w.shape
    r = cvec.shape[0]
    tn = 512
    return pl.pallas_call(
        _ada_body,
        out_shape=jax.ShapeDtypeStruct((depth, r, n), F32),
        grid=(depth, n // tn),
        in_specs=[pl.BlockSpec((r, d), lambda l, j: (0, 0)),
                  pl.BlockSpec((1, d, tn), lambda l, j: (l, 0, j)),
                  pl.BlockSpec((1, 1, tn), lambda l, j: (l, 0, j))],
        out_specs=pl.BlockSpec((1, r, tn), lambda l, j: (l, 0, j)),
        compiler_params=_cparams(("arbitrary", "arbitrary")),
        name="ada_mods",
    )(cvec, ada_w, ada_b.reshape(depth, 1, n))


def _rope_tables(lay, head_dim, width):
    rows = lay.S // GRID_W
    rr, cc = jnp.meshgrid(jnp.arange(rows), jnp.arange(GRID_W), indexing="ij")
    row_pos = rr.reshape(-1).astype(F32)
    col_pos = cc.reshape(-1).astype(F32)
    axis_dim = head_dim // 2
    inv = ROPE_THETA ** (-jnp.arange(0, axis_dim, 2, dtype=F32) / axis_dim)
    ang = jnp.concatenate([row_pos[:, None] * inv, col_pos[:, None] * inv], -1)
    cos, sin = jnp.cos(ang), jnp.sin(ang)
    cos = jnp.concatenate([jnp.ones((lay.CTX, head_dim // 2), F32), cos], 0)
    sin = jnp.concatenate([jnp.zeros((lay.CTX, head_dim // 2), F32), sin], 0)
    cos_h = jnp.concatenate([cos, cos], -1)
    sin_h = jnp.concatenate([-sin, sin], -1)
    reps = width // head_dim
    return jnp.tile(cos_h, (1, reps)), jnp.tile(sin_h, (1, reps))


def _rope_lanes(x, cos, sin, head_dim):
    w = x.shape[-1]
    half = head_dim // 2
    if head_dim == LANES and w == LANES:
        rot = pltpu.roll(x, half, 1)
    else:
        fwd = pltpu.roll(x, w - half, 1)
        bwd = pltpu.roll(x, half, 1)
        lane = lax.broadcasted_iota(jnp.int32, x.shape, 1)
        rot = jnp.where((lane % head_dim) < half, fwd, bwd)
    return x * cos + rot * sin


def _even_in_body(x_ref, mod_ref, w_ref, cos_ref, sin_ref, pa_ref, q_ref, k_ref, v_ref):
    m = mod_ref[0]
    h = (x_ref[...] * (1.0 + m[1:2]) + m[0:1]).astype(BF16)
    pa_ref[...] = jnp.dot(h, w_ref[:, :A_IN], preferred_element_type=F32)
    cos, sin = cos_ref[...], sin_ref[...]
    o = A_IN
    q = jnp.dot(h, w_ref[:, o:o + B_QK], preferred_element_type=F32)
    q_ref[...] = (_rope_lanes(q, cos, sin, B_HEAD_DIM) * (LOG2_E * B_HEAD_DIM ** -0.5)).astype(BF16)
    o += B_QK
    k = jnp.dot(h, w_ref[:, o:o + B_QK], preferred_element_type=F32)
    k_ref[...] = _rope_lanes(k, cos, sin, B_HEAD_DIM).astype(BF16)
    o += B_QK
    v_ref[...] = jnp.dot(h, w_ref[:, o:o + B_WIDTH], preferred_element_type=F32).astype(BF16)


def _even_in_proj(lay, x, mod, w_bf16, cos, sin):
    tm, d = lay.tm, D_MODEL
    row = lambda i: (i, 0)
    return pl.pallas_call(
        _even_in_body,
        out_shape=(jax.ShapeDtypeStruct((lay.rows, A_IN), F32),
                   jax.ShapeDtypeStruct((lay.rows, B_QK), BF16),
                   jax.ShapeDtypeStruct((lay.rows, B_QK), BF16),
                   jax.ShapeDtypeStruct((lay.rows, B_WIDTH), BF16)),
        grid=(lay.ntiles,),
        in_specs=[pl.BlockSpec((tm, d), row),
                  pl.BlockSpec((1, 6, d), lambda i: (lay.mod_row(i, tm), 0, 0)),
                  pl.BlockSpec((d, EVEN_IN), lambda i: (0, 0)),
                  pl.BlockSpec((tm, B_QK), lambda i: (lay.pos_tile(i), 0)),
                  pl.BlockSpec((tm, B_QK), lambda i: (lay.pos_tile(i), 0))],
        out_specs=(pl.BlockSpec((tm, A_IN), row), pl.BlockSpec((tm, B_QK), row),
                   pl.BlockSpec((tm, B_QK), row), pl.BlockSpec((tm, B_WIDTH), row)),
        compiler_params=_cparams(("arbitrary",)),
        name="even_in_proj",
    )(x, mod, w_bf16, cos, sin)


def _rwkv_feat_body(lay, pa_ref, prev_ref, next_ref, mu_ref, w0_ref, w2_ref, a0_ref, a2_ref, g2_ref,
                    kk_ref, ka_ref, rk_ref, bd_ref,
                    w_ref, ops_ref, g_ref, bonus_ref):
    def put(idx, d, val):
        ops_ref[idx - 1, d, 0] = val.astype(BF16)

    i = pl.program_id(0)
    tm = lay.tm
    il = i - lay.ntiles_c
    in_ctx = i < lay.ntiles_c
    seg_first = jnp.where(in_ctx, i % lay.nct == 0, il % lay.nlt == 0)
    seg_last = jnp.where(in_ctx, i % lay.nct == lay.nct - 1, il % lay.nlt == lay.nlt - 1)

    pa = pa_ref[...]
    row = lax.broadcasted_iota(jnp.int32, pa.shape, 0)
    prev_edge = jnp.where(seg_first, 0.0, 1.0) * prev_ref[SUBLANES - 1:SUBLANES, :]
    next_edge = jnp.where(seg_last, 0.0, 1.0) * next_ref[0:1, :]
    prev = jnp.where(row == 0, prev_edge, pltpu.roll(pa, 1, 0))
    nxt = jnp.where(row == tm - 1, next_edge, pltpu.roll(pa, tm - 1, 0))
    u = pa + (0.5 * (prev + nxt) - pa) * mu_ref[...]

    o1, o2, o3 = A_WIDTH, 2 * A_WIDTH, 3 * A_WIDTH
    o4 = o3 + 2 * A_LORA
    o5 = o4 + 2 * A_LORA
    r, k, v = u[:, :o1], u[:, o1:o2], u[:, o2:o3]
    bd = bd_ref[...]

    kk = k * kk_ref[...]
    ss = _dot2_exact_rhs(kk * kk, bd)
    kkn = kk / jnp.maximum(jnp.sqrt(ss), 1e-12)
    g = _dot3(_sigmoid(u[:, o5:]), g2_ref[...])

    g_ref[0] = g

    kd_sum = jnp.zeros_like(k)
    for d in range(2):
        put(_SCAN_R, d, r)
        put(_SCAN_V, d, v)
        put(_SCAN_A, d, -kkn)
        wd = u[:, o3 + d * A_LORA:o3 + (d + 1) * A_LORA]
        ad = u[:, o4 + d * A_LORA:o4 + (d + 1) * A_LORA]
        z = -(w0_ref[d:d + 1, :] + _dot3(jnp.tanh(wd), w2_ref[d]))
        softplus = jnp.maximum(z, 0.0) + jnp.log(1.0 + jnp.exp(-jnp.abs(z)))
        w_log = -softplus - 0.5
        w_ref[0, d, 0] = jnp.exp(-jnp.exp(w_log))
        a = _sigmoid(a0_ref[d:d + 1, :] + _dot3(ad, a2_ref[d]))
        kd = k * (1.0 + (a - 1.0) * ka_ref[...])
        put(_SCAN_K, d, kd)
        put(_SCAN_B, d, kkn * a)
        kd_sum = kd_sum + kd
    bonus_ref[0] = _dot2_exact_rhs(r * kd_sum * rk_ref[...], bd) * v


def _head_block_diag(width, head_dim):
    h = jnp.arange(width) // head_dim
    return (h[:, None] == h[None, :]).astype(BF16)


def _rwkv_features(lay, pa, mu, w0, w2, a0, a2, g2, k_k, k_a, r_k):
    tm = lay.tm
    hb = tm // SUBLANES
    nb8 = lay.rows // SUBLANES
    row = lambda i: (i, 0)
    full2 = lambda i: (0, 0)
    full3 = lambda i: (0, 0, 0)
    w = A_WIDTH
    out = jax.ShapeDtypeStruct((lay.B, lay.T, w), F32)
    seq = lambda i: lay.seq_block(i) + (0,)
    return pl.pallas_call(
        functools.partial(_rwkv_feat_body, lay),
        out_shape=(jax.ShapeDtypeStruct((1, 2, lay.B, lay.T, w), F32),
                   jax.ShapeDtypeStruct((N_SCAN_BF16, 2, lay.B, lay.T, w), BF16), out, out),
        grid=(lay.ntiles,),
        in_specs=[pl.BlockSpec((tm, A_IN), row),
                  pl.BlockSpec((SUBLANES, A_IN), lambda i: (jnp.maximum(i * hb - 1, 0), 0)),
                  pl.BlockSpec((SUBLANES, A_IN), lambda i: (jnp.minimum((i + 1) * hb, nb8 - 1), 0)),
                  pl.BlockSpec((1, A_IN), full2),
                  pl.BlockSpec((2, w), full2),
                  pl.BlockSpec((2, A_LORA, w), full3),
                  pl.BlockSpec((2, w), full2),
                  pl.BlockSpec((2, A_LORA, w), full3),
                  pl.BlockSpec((A_GATE_LORA, w), full2),
                  pl.BlockSpec((1, w), full2),
                  pl.BlockSpec((1, w), full2),
                  pl.BlockSpec((1, w), full2),
                  pl.BlockSpec((w, w), full2)],
        out_specs=(pl.BlockSpec((1, 2, 1, tm, w), lambda i: (0, 0) + lay.seq_block(i) + (0,)),
                   pl.BlockSpec((N_SCAN_BF16, 2, 1, tm, w), lambda i: (0, 0) + lay.seq_block(i) + (0,)),
                   pl.BlockSpec((1, tm, w), seq), pl.BlockSpec((1, tm, w), seq)),
        compiler_params=_cparams(("arbitrary",)),
        name="rwkv_features",
    )(pa, pa, pa, mu.reshape(1, A_IN), w0, w2, a0, a2, g2, k_k.reshape(1, w), k_a.reshape(1, w),
      r_k.reshape(1, w), _head_block_diag(w, A_HEAD_DIM))


def _wkv_scan_body(tb, wf_ref, wb_ref, fwd_ref, bwd_ref, yf_ref, yb_ref, s_ref, m_ref):
    n = A_HEAD_DIM
    chains = s_ref.shape[-1]
    is_fwd = lax.broadcasted_iota(jnp.int32, (n, chains), 1) < chains // 2

    @pl.when(pl.program_id(0) == 0)
    def _():
        s_ref[...] = jnp.zeros_like(s_ref)

    def step(t, carry):
        tr = tb - 1 - t
        m_ref[_SCAN_W] = jnp.where(is_fwd, wf_ref[t, 0], wb_ref[tr, 0])
        for idx in range(N_SCAN_BF16):
            m_ref[idx + 1] = jnp.where(is_fwd, fwd_ref[t, idx], bwd_ref[tr, idx]).astype(F32)
        sa = jnp.zeros((n, chains), F32)
        for kk in range(n):
            sa = sa + s_ref[kk] * m_ref[_SCAN_A, kk:kk + 1, :]
        v_t = m_ref[_SCAN_V]
        y = jnp.zeros((n, chains), F32)
        for kk in range(n):
            s_new = (s_ref[kk] * m_ref[_SCAN_W, kk:kk + 1, :] + sa * m_ref[_SCAN_B, kk:kk + 1, :]
                     + v_t * m_ref[_SCAN_K, kk:kk + 1, :])
            s_ref[kk] = s_new
            y = y + s_new * m_ref[_SCAN_R, kk:kk + 1, :]
        yf_ref[t] = y
        yb_ref[tr] = y
        return carry

    lax.fori_loop(0, tb, step, 0)


def _wkv_scan(lay, decay, ops):
    t, nops, n, lanes = ops.shape
    tb = math.gcd(32, math.gcd(lay.CTX, lay.S))
    nctb, nt = lay.CTX // tb, t // tb
    bwd_tile = lambda g: jnp.where(g < nctb, nctb - 1 - g, nt - 1 - (g - nctb))
    out = jax.ShapeDtypeStruct((t, n, lanes), F32)
    fwd_map = lambda g: (g, 0, 0, 0)
    bwd_map = lambda g: (bwd_tile(g), 0, 0, 0)
    return pl.pallas_call(
        functools.partial(_wkv_scan_body, tb),
        out_shape=(out, out),
        grid=(nt,),
        in_specs=[pl.BlockSpec((tb, 1, n, lanes), fwd_map), pl.BlockSpec((tb, 1, n, lanes), bwd_map),
                  pl.BlockSpec((tb, nops, n, lanes), fwd_map), pl.BlockSpec((tb, nops, n, lanes), bwd_map)],
        out_specs=(pl.BlockSpec((tb, n, lanes), lambda g: (g, 0, 0)),
                   pl.BlockSpec((tb, n, lanes), lambda g: (bwd_tile(g), 0, 0))),
        scratch_shapes=[pltpu.VMEM((n, n, lanes), F32), pltpu.VMEM((nops + 1, n, lanes), F32)],
        compiler_params=_cparams(("arbitrary",)),
        name="wkv7_scan",
    )(decay, decay, ops, ops)


def _to_chains(lay, ops):
    h, n = A_HEADS, A_HEAD_DIM
    nops = ops.shape[0]
    x = ops.reshape(nops, 2, lay.B, lay.T, h, n).transpose(3, 0, 5, 1, 2, 4)
    return x.reshape(lay.T, nops, n, 2 * lay.B * h)


def _from_chains(lay, y):
    h, n = A_HEADS, A_HEAD_DIM
    return y.reshape(lay.T, n, lay.B, h).transpose(2, 0, 3, 1).reshape(lay.B, lay.T, h * n)


def _rwkv_readout_body(yf_ref, yb_ref, bonus_ref, g_ref, lg_ref, lb_ref, bd_ref, o_ref):
    y = yf_ref[0] + yb_ref[0]
    bd = bd_ref[...]
    inv_n = 1.0 / A_HEAD_DIM
    mu = _dot2_exact_rhs(y, bd) * inv_n
    yc = y - mu
    var = _dot2_exact_rhs(yc * yc, bd) * inv_n
    yn = yc * lax.rsqrt(var + A_GN_EPS) * lg_ref[...] + lb_ref[...]
    o_ref[...] = ((yn + bonus_ref[0]) * g_ref[0]).astype(BF16)


def _rwkv_readout(lay, yf, yb, bonus, g, lnx_g, lnx_b):
    tm, w = lay.tm, A_WIDTH
    row = lambda i: (i, 0)
    full = lambda i: (0, 0)
    seq = lambda i: lay.seq_block(i) + (0,)
    return pl.pallas_call(
        _rwkv_readout_body,
        out_shape=jax.ShapeDtypeStruct((lay.rows, w), BF16),
        grid=(lay.ntiles,),
        in_specs=[pl.BlockSpec((1, tm, w), seq)] * 4 + [pl.BlockSpec((1, w), full)] * 2 + [pl.BlockSpec((w, w), full)],
        out_specs=pl.BlockSpec((tm, w), row),
        compiler_params=_cparams(("arbitrary",)),
        name="rwkv_readout",
    )(yf, yb, bonus, g, lnx_g.reshape(1, w), lnx_b.reshape(1, w), _head_block_diag(w, A_HEAD_DIM))


def _exp2_scores(q, keys):
    scores = [_dot_nt(q, kk) for kk in keys]
    m = scores[0].max(axis=-1, keepdims=True)
    for s in scores[1:]:
        m = jnp.maximum(m, s.max(axis=-1, keepdims=True))
    return [jnp.exp2(s - m) for s in scores]


def _with_ones_column(v):
    lane = lax.broadcasted_iota(jnp.int32, (v.shape[0], LANES), 1)
    return jnp.concatenate([v, jnp.where(lane == 0, 1.0, 0.0).astype(BF16)], axis=1)


def _softmax_pv(q, keys, vals_aug):
    e_dim = vals_aug[0].shape[-1] - LANES
    o = None
    for p, vv in zip(_exp2_scores(q, keys), vals_aug):
        part = jnp.dot(p.astype(BF16), vv, preferred_element_type=F32)
        o = part if o is None else o + part
    return o[:, :e_dim] / o[:, e_dim:e_dim + 1]


def _diff_attn_body(lay, lambda_init, q_ref, kc_ref, kl_ref, vc_ref, vl_ref, lam_ref, g_ref, o_ref):
    j = pl.program_id(1)
    lv = lam_ref[...]
    lam = (jnp.exp(jnp.sum(lv[0:1] * lv[1:2], axis=1, keepdims=True))
           - jnp.exp(jnp.sum(lv[2:3] * lv[3:4], axis=1, keepdims=True)) + lambda_init)

    def run(with_latent):
        for h in range(B_HEADS):
            vs = slice(h * B_V_DIM, (h + 1) * B_V_DIM)
            vals = [vc_ref[:, vs]] + ([vl_ref[:, vs]] if with_latent else [])
            probs = []
            for mi in range(2):
                cs = slice(h * B_V_DIM + mi * B_HEAD_DIM, h * B_V_DIM + (mi + 1) * B_HEAD_DIM)
                keys = [kc_ref[:, cs]] + ([kl_ref[:, cs]] if with_latent else [])
                ex = _exp2_scores(q_ref[:, cs], keys)
                tot = ex[0].sum(axis=-1, keepdims=True)
                for e in ex[1:]:
                    tot = tot + e.sum(axis=-1, keepdims=True)
                probs.append((ex, tot))
            (ex0, tot0), (ex1, tot1) = probs
            c0, c1 = 1.0 / tot0, lam / tot1
            o = None
            for e0, e1, vv in zip(ex0, ex1, vals):
                part = jnp.dot((e0 * c0 - e1 * c1).astype(BF16), vv, preferred_element_type=F32)
                o = part if o is None else o + part
            ms = jnp.mean(o * o, axis=-1, keepdims=True)
            o = o * lax.rsqrt(ms + B_SUBLN_EPS) * g_ref[...] * (1.0 - lambda_init)
            o_ref[:, vs] = o.astype(BF16)

    @pl.when(j < lay.nct)
    def _():
        run(False)

    @pl.when(j >= lay.nct)
    def _():
        run(True)


def _diff_attention(lay, q, k, v, lam_vecs, subln_g, lambda_init):
    tm = lay.tm
    w = B_WIDTH
    lat0 = lay.rows_c // lay.S
    return pl.pallas_call(
        functools.partial(_diff_attn_body, lay, lambda_init),
        out_shape=jax.ShapeDtypeStruct((lay.rows, w), BF16),
        grid=(lay.B, lay.nct + lay.nlt),
        in_specs=[pl.BlockSpec((tm, w), lambda b, j: (lay.seq_tile(b, j), 0)),
                  pl.BlockSpec((lay.CTX, w), lambda b, j: (b, 0)),
                  pl.BlockSpec((lay.S, w), lambda b, j: (lat0 + b, 0)),
                  pl.BlockSpec((lay.CTX, w), lambda b, j: (b, 0)),
                  pl.BlockSpec((lay.S, w), lambda b, j: (lat0 + b, 0)),
                  pl.BlockSpec((4, B_HEAD_DIM), lambda b, j: (0, 0)),
                  pl.BlockSpec((1, B_V_DIM), lambda b, j: (0, 0))],
        out_specs=pl.BlockSpec((tm, w), lambda b, j: (lay.seq_tile(b, j), 0)),
        compiler_params=_cparams(("arbitrary", "arbitrary")),
        name="diff_attention",
    )(q, k, k, v, v, lam_vecs, subln_g.reshape(1, B_V_DIM))


def _odd_in_body(x_ref, mod_ref, w_ref, cos_ref, sin_ref, qn_ref, kn_ref, q_ref, k_ref, v_ref):
    m = mod_ref[0]
    h = (x_ref[...] * (1.0 + m[1:2]) + m[0:1]).astype(BF16)
    cos, sin = cos_ref[...], sin_ref[...]

    def norm_rope(p, g, scale):
        ms = jnp.mean(p * p, axis=-1, keepdims=True)
        y = p * lax.rsqrt(ms + QK_NORM_EPS) * g
        return (_rope_lanes(y, cos, sin, C_HEAD_DIM) * scale).astype(BF16)

    for hd in range(C_HEADS):
        cs = slice(hd * C_HEAD_DIM, (hd + 1) * C_HEAD_DIM)
        p = jnp.dot(h, w_ref[:, cs], preferred_element_type=F32)
        q_ref[:, cs] = norm_rope(p, qn_ref[...], LOG2_E * C_HEAD_DIM ** -0.5)
    for hd in range(C_KV_HEADS):
        cs = slice(hd * C_HEAD_DIM, (hd + 1) * C_HEAD_DIM)
        p = jnp.dot(h, w_ref[:, C_Q + hd * C_HEAD_DIM:C_Q + (hd + 1) * C_HEAD_DIM], preferred_element_type=F32)
        k_ref[:, cs] = norm_rope(p, kn_ref[...], 1.0)
    v_ref[...] = jnp.dot(h, w_ref[:, C_Q + C_KV:], preferred_element_type=F32).astype(BF16)


def _odd_in_proj(lay, x, mod, w_bf16, cos, sin, qn_g, kn_g):
    tm, d = lay.tm, D_MODEL
    row = lambda i: (i, 0)
    full = lambda i: (0, 0)
    return pl.pallas_call(
        _odd_in_body,
        out_shape=(jax.ShapeDtypeStruct((lay.rows, C_Q), BF16),
                   jax.ShapeDtypeStruct((lay.rows, C_KV), BF16),
                   jax.ShapeDtypeStruct((lay.rows, C_KV), BF16)),
        grid=(lay.ntiles,),
        in_specs=[pl.BlockSpec((tm, d), row),
                  pl.BlockSpec((1, 6, d), lambda i: (lay.mod_row(i, tm), 0, 0)),
                  pl.BlockSpec((d, ODD_IN), full),
                  pl.BlockSpec((tm, C_HEAD_DIM), lambda i: (lay.pos_tile(i), 0)),
                  pl.BlockSpec((tm, C_HEAD_DIM), lambda i: (lay.pos_tile(i), 0)),
                  pl.BlockSpec((1, C_HEAD_DIM), full),
                  pl.BlockSpec((1, C_HEAD_DIM), full)],
        out_specs=(pl.BlockSpec((tm, C_Q), row), pl.BlockSpec((tm, C_KV), row), pl.BlockSpec((tm, C_KV), row)),
        compiler_params=_cparams(("arbitrary",)),
        name="odd_in_proj",
    )(x, mod, w_bf16, cos, sin, qn_g.reshape(1, C_HEAD_DIM), kn_g.reshape(1, C_HEAD_DIM))


def _gqa_body(q_ref, kc_ref, kl_ref, vc_ref, vl_ref, o_ref):
    keys = [kc_ref[...], kl_ref[...]]
    vals = [_with_ones_column(vc_ref[...]), _with_ones_column(vl_ref[...])]
    for g in range(C_GROUP):
        cs = slice(g * C_HEAD_DIM, (g + 1) * C_HEAD_DIM)
        o_ref[:, cs] = _softmax_pv(q_ref[:, cs], keys, vals).astype(BF16)


def _gqa_attention(lay, q, k, v):
    tm = lay.tm
    gw = C_GROUP * C_HEAD_DIM
    lat0 = lay.rows_c // lay.S
    hd = C_HEAD_DIM
    return pl.pallas_call(
        _gqa_body,
        out_shape=jax.ShapeDtypeStruct((lay.rows_l, C_Q), BF16),
        grid=(lay.B, C_KV_HEADS, lay.nlt),
        in_specs=[pl.BlockSpec((tm, gw), lambda b, h, j: (lay.ntiles_c + b * lay.nlt + j, h)),
                  pl.BlockSpec((lay.CTX, hd), lambda b, h, j: (b, h)),
                  pl.BlockSpec((lay.S, hd), lambda b, h, j: (lat0 + b, h)),
                  pl.BlockSpec((lay.CTX, hd), lambda b, h, j: (b, h)),
                  pl.BlockSpec((lay.S, hd), lambda b, h, j: (lat0 + b, h))],
        out_specs=pl.BlockSpec((tm, gw), lambda b, h, j: (b * lay.nlt + j, h)),
        compiler_params=_cparams(("arbitrary", "arbitrary", "arbitrary")),
        name="gqa_attention",
    )(q, k, k, v, v)


def _out_proj_body(n_mix, *refs):
    mix_refs = refs[:n_mix]
    w_ref, x_ref, mod_ref, lg_ref, lb_ref, rw_ref, xo_ref, h_ref, lt_ref = refs[n_mix:]
    m = mod_ref[0]
    off = 0
    mix = None
    for mr in mix_refs:
        kw = mr.shape[-1]
        part = jnp.dot(mr[...], w_ref[off:off + kw, :], preferred_element_type=F32)
        mix = part if mix is None else mix + part
        off += kw
    z = DEEPNORM_ALPHA * x_ref[...] + m[2:3] * mix
    xn = _layer_norm_rows(z, lg_ref[...], lb_ref[...])
    xo_ref[...] = xn
    h2 = xn * (1.0 + m[4:5]) + m[3:4]
    h_ref[...] = h2
    lt_ref[...] = _dot3_nt(rw_ref[...], h2)


def _out_proj(lay, mixes, w_bf16, x, x_tile_off, mod, ln_g, ln_b, router_wt, n_rows):
    tm, d = lay.tm, D_MODEL
    row = lambda i: (i, 0)
    full = lambda i: (0, 0)
    row_off = x_tile_off * tm
    in_specs = [pl.BlockSpec((tm, mx.shape[-1]), row) for mx in mixes]
    in_specs += [pl.BlockSpec((d, d), full),
                 pl.BlockSpec((tm, d), lambda i: (i + x_tile_off, 0)),
                 pl.BlockSpec((1, 6, d), lambda i: (lay.mod_row(i, tm, row_off), 0, 0)),
                 pl.BlockSpec((1, d), full), pl.BlockSpec((1, d), full),
                 pl.BlockSpec((N_EXPERTS, d), full)]
    return pl.pallas_call(
        functools.partial(_out_proj_body, len(mixes)),
        out_shape=(jax.ShapeDtypeStruct((n_rows, d), F32),
                   jax.ShapeDtypeStruct((n_rows, d), F32),
                   jax.ShapeDtypeStruct((N_EXPERTS, n_rows), F32)),
        grid=(n_rows // tm,),
        in_specs=in_specs,
        out_specs=(pl.BlockSpec((tm, d), row), pl.BlockSpec((tm, d), row),
                   pl.BlockSpec((N_EXPERTS, tm), lambda i: (0, i))),
        compiler_params=_cparams(("arbitrary",)),
        name="out_proj_ln",
    )(*mixes, w_bf16, x, mod, ln_g.reshape(1, d), ln_b.reshape(1, d), router_wt)


_EXPERT_PAIRS = ((0, 1), (0, 2), (0, 3), (1, 2), (1, 3), (2, 3))
N_BUCKETS = N_GROUPS * len(_EXPERT_PAIRS)


def _router_body(lt_ref, rb_ref, o_ref):
    logits = lt_ref[...] + rb_ref[...]
    rows = [logits[e:e + 1, :] for e in range(N_EXPERTS)]
    m = rows[0]
    for x in rows[1:]:
        m = jnp.maximum(m, x)
    ex = [jnp.exp(x - m) for x in rows]
    z = ex[0]
    for x in ex[1:]:
        z = z + x
    p = [x / z for x in ex]

    gscore = []
    for g in range(N_GROUPS):
        a, b, c, d = p[4 * g:4 * g + 4]
        hi1, lo1 = jnp.maximum(a, b), jnp.minimum(a, b)
        hi2, lo2 = jnp.maximum(c, d), jnp.minimum(c, d)
        top1 = jnp.maximum(hi1, hi2)
        top2 = jnp.maximum(jnp.minimum(hi1, hi2), jnp.maximum(lo1, lo2))
        gscore.append(top1 + top2)
    best = []
    for g in range(N_GROUPS):
        ok = None
        for o in range(N_GROUPS):
            if o == g:
                continue
            c = (gscore[g] > gscore[o]) if o < g else (gscore[g] >= gscore[o])
            ok = c if ok is None else jnp.logical_and(ok, c)
        best.append(ok)
    won = []
    for e in range(N_EXPERTS):
        g = e // EXPERTS_PER_GROUP
        rank = jnp.zeros_like(p[e])
        for o in range(4 * g, 4 * g + 4):
            if o == e:
                continue
            ahead = (p[o] > p[e]) if o > e else (p[o] >= p[e])
            rank = rank + jnp.where(ahead, 1.0, 0.0)
        won.append(jnp.where(jnp.logical_and(best[g], rank < 1.5), 1.0, 0.0))
    tot = won[0] * p[0]
    for e in range(1, N_EXPERTS):
        tot = tot + won[e] * p[e]
    bucket = jnp.zeros_like(tot)
    gate_a = jnp.zeros_like(tot)
    gate_b = jnp.zeros_like(tot)
    for g in range(N_GROUPS):
        for pid, (a, b) in enumerate(_EXPERT_PAIRS):
            ind = won[4 * g + a] * won[4 * g + b]
            bucket = bucket + ind * float(len(_EXPERT_PAIRS) * g + pid)
            gate_a = gate_a + ind * p[4 * g + a]
            gate_b = gate_b + ind * p[4 * g + b]
    o_ref[...] = jnp.zeros_like(o_ref)
    o_ref[0:1, :] = bucket
    o_ref[1:2, :] = gate_a / tot
    o_ref[2:3, :] = gate_b / tot


def _router(logits_t, router_b):
    e, n = logits_t.shape
    tr = math.gcd(2048, n)
    return pl.pallas_call(
        _router_body,
        out_shape=jax.ShapeDtypeStruct((SUBLANES, n), F32),
        grid=(n // tr,),
        in_specs=[pl.BlockSpec((e, tr), lambda i: (0, i)), pl.BlockSpec((e, 1), lambda i: (0, 0))],
        out_specs=pl.BlockSpec((SUBLANES, tr), lambda i: (0, i)),
        compiler_params=_cparams(("arbitrary",)),
        name="router_gates",
    )(logits_t, router_b.reshape(e, 1))


GATE_LANES = LANES


def _moe_plan(bucket, tg):
    n = bucket.shape[0]
    n_tiles = n // tg + N_BUCKETS
    ids = jnp.arange(N_BUCKETS, dtype=jnp.int32)
    onehot = (bucket[:, None] == ids[None, :]).astype(jnp.int32)
    csum = jnp.cumsum(onehot, axis=0)
    counts = csum[-1]
    tiles = (counts + tg - 1) // tg
    tile_end = jnp.cumsum(tiles)
    row_start = (tile_end - tiles) * tg
    pos = jnp.sum(onehot * (row_start[None, :] + csum - 1), axis=1).astype(jnp.int32)
    n_used = tile_end[-1]
    tile_ids = jnp.minimum(jnp.arange(n_tiles, dtype=jnp.int32), n_used - 1)
    tile_bucket = jnp.sum((tile_ids[:, None] >= tile_end[None, :]).astype(jnp.int32), axis=1)
    pair_a = jnp.asarray([a for a, _ in _EXPERT_PAIRS], jnp.int32)
    pair_b = jnp.asarray([b for _, b in _EXPERT_PAIRS], jnp.int32)
    grp, pid = tile_bucket // len(_EXPERT_PAIRS), tile_bucket % len(_EXPERT_PAIRS)
    ea = grp * EXPERTS_PER_GROUP + pair_a[pid]
    eb = grp * EXPERTS_PER_GROUP + pair_b[pid]
    return pos, ea.astype(jnp.int32), eb.astype(jnp.int32), n_used.reshape(1).astype(jnp.int32), n_tiles


def _row_copy(src_ref, src_row, dst_ref, dst_row, sem):
    return pltpu.make_async_copy(src_ref.at[pl.ds(src_row, 1), :], dst_ref.at[pl.ds(dst_row, 1), :], sem)


ROW_DMA_UNROLL = 8


def _row_copies(n_rows, row_copy, whole_tile_copy):
    def issue(blk, carry):
        for u in range(ROW_DMA_UNROLL):
            row_copy(blk * ROW_DMA_UNROLL + u).start(priority=u % 2)
        return carry

    lax.fori_loop(0, n_rows // ROW_DMA_UNROLL, issue, 0)
    whole_tile_copy.wait()


def _dispatch_body(tg, pos_ref, h_ref, g_ref, xs_in_ref, xs_ref, aug_ref, sem):
    del xs_in_ref
    base = pl.program_id(0) * tg
    aug_ref[:, :D_MODEL] = h_ref[...]
    aug_ref[:, D_MODEL:] = g_ref[...]

    _row_copies(tg, lambda r: _row_copy(aug_ref, r, xs_ref, pos_ref[base + r], sem),
                pltpu.make_async_copy(aug_ref, xs_ref.at[pl.ds(0, tg), :], sem))


def _moe_dispatch(pos, h2, gate_rows, n_tiles, tg):
    n, d = h2.shape
    wide = d + GATE_LANES
    xs0 = jnp.zeros((n_tiles * tg, wide), F32)
    grid_spec = pltpu.PrefetchScalarGridSpec(
        num_scalar_prefetch=1,
        grid=(n // tg,),
        in_specs=[pl.BlockSpec((tg, d), lambda i, pos_ref: (i, 0)),
                  pl.BlockSpec((tg, GATE_LANES), lambda i, pos_ref: (i, 0)),
                  pl.BlockSpec(memory_space=pl.ANY)],
        out_specs=pl.BlockSpec(memory_space=pl.ANY),
        scratch_shapes=[pltpu.VMEM((tg, wide), F32), pltpu.SemaphoreType.DMA(())],
    )
    return pl.pallas_call(
        functools.partial(_dispatch_body, tg),
        out_shape=jax.ShapeDtypeStruct(xs0.shape, F32),
        grid_spec=grid_spec,
        input_output_aliases={3: 0},
        compiler_params=_cparams(("arbitrary",)),
        name="moe_dispatch",
    )(pos, h2, gate_rows, xs0)


def _grouped_body(ea_ref, eb_ref, nu_ref, xs_ref, w1a_ref, w3a_ref, w2a_ref, w1b_ref, w3b_ref, w2b_ref, y_ref):
    del ea_ref, eb_ref

    @pl.when(pl.program_id(0) < nu_ref[0])
    def _():
        x = xs_ref[:, :D_MODEL].astype(BF16)

        def expert(w1_ref, w3_ref, w2_ref, gate):
            h1 = jnp.dot(x, w1_ref[0], preferred_element_type=F32)
            h3 = jnp.dot(x, w3_ref[0], preferred_element_type=F32)
            hid = (h1 * _sigmoid(h1) * h3 * gate).astype(BF16)
            return jnp.dot(hid, w2_ref[0], preferred_element_type=F32)

        y_ref[...] = (expert(w1a_ref, w3a_ref, w2a_ref, xs_ref[:, D_MODEL:D_MODEL + 1])
                      + expert(w1b_ref, w3b_ref, w2b_ref, xs_ref[:, D_MODEL + 1:D_MODEL + 2]))

    @pl.when(pl.program_id(0) >= nu_ref[0])
    def _():
        y_ref[...] = jnp.zeros_like(y_ref)


def _moe_grouped(xs, ea, eb, n_used, w1, w3, w2, tg):
    rows, wide = xs.shape
    d, ff = D_MODEL, EXPERT_FF
    n_tiles = rows // tg
    tile = lambda i, ea_r, eb_r, nu_r: (jnp.minimum(i, nu_r[0] - 1), 0)
    up_a = pl.BlockSpec((1, d, ff), lambda i, ea_r, eb_r, nu_r: (ea_r[i], 0, 0))
    up_b = pl.BlockSpec((1, d, ff), lambda i, ea_r, eb_r, nu_r: (eb_r[i], 0, 0))
    dn_a = pl.BlockSpec((1, ff, d), lambda i, ea_r, eb_r, nu_r: (ea_r[i], 0, 0))
    dn_b = pl.BlockSpec((1, ff, d), lambda i, ea_r, eb_r, nu_r: (eb_r[i], 0, 0))
    grid_spec = pltpu.PrefetchScalarGridSpec(
        num_scalar_prefetch=3,
        grid=(n_tiles,),
        in_specs=[pl.BlockSpec((tg, wide), tile), up_a, up_a, dn_a, up_b, up_b, dn_b],
        out_specs=pl.BlockSpec((tg, d), lambda i, ea_r, eb_r, nu_r: (i, 0)),
    )
    return pl.pallas_call(
        _grouped_body,
        out_shape=jax.ShapeDtypeStruct((rows, d), F32),
        grid_spec=grid_spec,
        compiler_params=_cparams(("arbitrary",)),
        name="moe_grouped",
    )(ea, eb, n_used, xs, w1, w3, w2, w1, w3, w2)


def _combine_body(tg, pos_ref, ys_ref, x_ref, mod_ref, lg_ref, lb_ref, o_ref, buf_ref, sem):
    base = pl.program_id(0) * tg

    _row_copies(tg, lambda r: _row_copy(ys_ref, pos_ref[base + r], buf_ref, r, sem),
                pltpu.make_async_copy(ys_ref.at[pl.ds(0, tg), :], buf_ref, sem))
    m = mod_ref[0]
    z = DEEPNORM_ALPHA * x_ref[...] + m[5:6] * buf_ref[...]
    o_ref[...] = _layer_norm_rows(z, lg_ref[...], lb_ref[...])


def _moe_combine(lay, pos, ys, x, mod, ln_g, ln_b, row_off, tg):
    n, d = x.shape
    grid_spec = pltpu.PrefetchScalarGridSpec(
        num_scalar_prefetch=1,
        grid=(n // tg,),
        in_specs=[pl.BlockSpec(memory_space=pl.ANY),
                  pl.BlockSpec((tg, d), lambda i, pos_ref: (i, 0)),
                  pl.BlockSpec((1, 6, d), lambda i, pos_ref: (lay.mod_row(i, tg, row_off), 0, 0)),
                  pl.BlockSpec((1, d), lambda i, pos_ref: (0, 0)),
                  pl.BlockSpec((1, d), lambda i, pos_ref: (0, 0))],
        out_specs=pl.BlockSpec((tg, d), lambda i, pos_ref: (i, 0)),
        scratch_shapes=[pltpu.VMEM((tg, d), F32), pltpu.SemaphoreType.DMA(())],
    )
    return pl.pallas_call(
        functools.partial(_combine_body, tg),
        out_shape=jax.ShapeDtypeStruct((n, d), F32),
        grid_spec=grid_spec,
        compiler_params=_cparams(("arbitrary",)),
        name="moe_combine_ln",
    )(pos, ys, x, mod, ln_g.reshape(1, d), ln_b.reshape(1, d))


def _moe(lay, h2, routed, w1, w3, w2, x, mod, ln_g, ln_b, row_off):
    tg = lay.tm
    bucket = routed[0].astype(jnp.int32)
    gate_rows = jnp.pad(routed[1:3].T, ((0, 0), (0, GATE_LANES - 2)))
    pos, ea, eb, n_used, n_tiles = _moe_plan(bucket, tg)
    xs = _moe_dispatch(pos, h2, gate_rows, n_tiles, tg)
    ys = _moe_grouped(xs, ea, eb, n_used, w1, w3, w2, tg)
    return _moe_combine(lay, pos, ys, x, mod, ln_g, ln_b, row_off, tg)


def kernel(x, c, ctx, c_ctx, router_w, router_b, ada_w, ada_b, ln1_g, ln1_b, ln2_g, ln2_b, moe_w1, moe_w3, moe_w2, ev_w_in, ev_w_out, ev_a_mu, ev_a_w0, ev_a_w2, ev_a_a0, ev_a_a2, ev_a_g2, ev_a_kk, ev_a_ka, ev_a_rk, ev_a_lnx_g, ev_a_lnx_b, ev_b_lam, ev_b_subln_g, od_w_in, od_w_out, od_qn_g, od_kn_g):
    bsz, seq, d = x.shape
    ctx_len = ctx.shape[1]
    assert d == D_MODEL and ada_w.shape[0] == DEPTH and seq % GRID_W == 0
    lay = _Layout(bsz, ctx_len, seq)

    cvec = jnp.zeros((lay.mod_rows, d), F32).at[:bsz].set(c).at[bsz].set(c_ctx)
    mods = _ada_mods(cvec, ada_w, ada_b).reshape(DEPTH, lay.mod_rows, 6, d)

    xs = jnp.concatenate([ctx.reshape(lay.rows_c, d), x.reshape(lay.rows_l, d)], axis=0)
    router_wt = router_w.T

    cos_b, sin_b = _rope_tables(lay, B_HEAD_DIM, B_QK)
    cos_c, sin_c = _rope_tables(lay, C_HEAD_DIM, C_HEAD_DIM)

    for i in range(DEPTH):
        last = i == DEPTH - 1
        j = i // 2
        mod = mods[i]
        if i % 2 == 0:
            lambda_init = 0.8 - 0.6 * math.exp(-0.3 * i)
            pa, q, k, v = _even_in_proj(lay, xs, mod, ev_w_in[j].astype(BF16), cos_b, sin_b)
            decay, scan_ops, g_, bonus = _rwkv_features(
                lay, pa, ev_a_mu[j], ev_a_w0[j], ev_a_w2[j], ev_a_a0[j], ev_a_a2[j], ev_a_g2[j],
                ev_a_kk[j], ev_a_ka[j], ev_a_rk[j].reshape(-1))
            y_f, y_b = _wkv_scan(lay, _to_chains(lay, decay), _to_chains(lay, scan_ops))
            nl = lay.B * A_HEADS
            yf = _from_chains(lay, y_f[:, :, :nl])
            yb = _from_chains(lay, y_b[:, :, nl:])
            ya = _rwkv_readout(lay, yf, yb, bonus, g_, ev_a_lnx_g[j], ev_a_lnx_b[j])
            yd = _diff_attention(lay, q, k, v, ev_b_lam[j], ev_b_subln_g[j], lambda_init)
            mixes, w_out = [ya, yd], ev_w_out[j]
            n_rows, tile_off = (lay.rows_l, lay.ntiles_c) if last else (lay.rows, 0)
            if last:
                mixes = [mx[lay.rows_c:] for mx in mixes]
        else:
            q, k, v = _odd_in_proj(lay, xs, mod, od_w_in[j].astype(BF16), cos_c, sin_c, od_qn_g[j], od_kn_g[j])
            assert last, "an odd layer that must also update the context stream is not supported"
            o = _gqa_attention(lay, q, k, v)
            mixes, w_out = [o], od_w_out[j]
            n_rows, tile_off = lay.rows_l, lay.ntiles_c
        row_off = tile_off * lay.tm
        x_new, h2, logits_t = _out_proj(lay, mixes, w_out.astype(BF16), xs, tile_off, mod,
                                        ln1_g[i], ln1_b[i], router_wt, n_rows)
        routed = _router(logits_t, router_b)
        xs = _moe(lay, h2, routed, moe_w1[i].astype(BF16), moe_w3[i].astype(BF16), moe_w2[i].astype(BF16),
                  x_new, mod, ln2_g[i], ln2_b[i], row_off)
    return xs.reshape(bsz, seq, d)
```

```python
import functools
import math

import jax
import jax.numpy as jnp
from jax import lax
from jax.experimental import pallas as pl
from jax.experimental.pallas import tpu as pltpu

F32 = jnp.float32
BF16 = jnp.bfloat16

D_MODEL = 1024
DEPTH = 2
GRID_W = 64
ROPE_THETA = 10000.0
LN_EPS = 1e-5
LOG2_E = 1.4426950408889634
DEEPNORM_ALPHA = (2 * DEPTH) ** 0.25

A_HEAD_DIM = 64
A_HEADS = 8
A_WIDTH = 512
A_LORA = 64
A_GATE_LORA = 128
A_GN_EPS = 64e-5
A_IN = 3 * A_WIDTH + 4 * A_LORA + A_GATE_LORA
_SCAN_W, _SCAN_R, _SCAN_K, _SCAN_V, _SCAN_A, _SCAN_B = range(6)
N_SCAN_BF16 = 5

B_HEAD_DIM = 64
B_V_DIM = 128
B_HEADS = 4
B_WIDTH = 512
B_QK = 512
B_QK_PAD = (B_QK // B_HEAD_DIM) * 128
B_SUBLN_EPS = 1e-5
EVEN_IN = A_IN + 2 * B_QK + B_WIDTH

C_HEAD_DIM = 128
C_HEADS = 8
C_KV_HEADS = 2
C_GROUP = 4
C_Q = 1024
C_KV = 256
ODD_IN = C_Q + 2 * C_KV
QK_NORM_EPS = 1e-6

N_EXPERTS = 16
N_GROUPS = 4
EXPERTS_PER_GROUP = 4
EXPERT_FF = 512

VMEM_LIMIT_BYTES = 56 * 1024 * 1024
LANES = 128
SUBLANES = 8


def _cparams(sem):
    return pltpu.CompilerParams(dimension_semantics=sem, vmem_limit_bytes=VMEM_LIMIT_BYTES)


def _dot(a, b):
    return jnp.dot(a.astype(BF16), b.astype(BF16), preferred_element_type=F32)


def _dot_nt(a, b):
    return lax.dot_general(a.astype(BF16), b.astype(BF16), (((1,), (1,)), ((), ())),
                           preferred_element_type=F32)


def _split(a):
    hi = a.astype(BF16)
    lo = (a - hi.astype(F32)).astype(BF16)
    return hi, lo


def _dot3(a, b):
    ah, al = _split(a)
    bh, bl = _split(b)
    return (jnp.dot(ah, bh, preferred_element_type=F32)
            + (jnp.dot(ah, bl, preferred_element_type=F32)
               + jnp.dot(al, bh, preferred_element_type=F32)))


def _dot2_exact_rhs(a, b_bf16):
    ah, al = _split(a)
    return jnp.dot(ah, b_bf16, preferred_element_type=F32) + jnp.dot(al, b_bf16, preferred_element_type=F32)


def _dot3_nt(a, b):
    ah, al = _split(a)
    bh, bl = _split(b)
    dn = (((1,), (1,)), ((), ()))
    return (lax.dot_general(ah, bh, dn, preferred_element_type=F32)
            + (lax.dot_general(ah, bl, dn, preferred_element_type=F32)
               + lax.dot_general(al, bh, dn, preferred_element_type=F32)))


def _sigmoid(x):
    return 1.0 / (1.0 + jnp.exp(-x))


def _layer_norm_rows(z, g, b):
    mu = jnp.mean(z, axis=-1, keepdims=True)
    zc = z - mu
    var = jnp.mean(zc * zc, axis=-1, keepdims=True)
    return zc * lax.rsqrt(var + LN_EPS) * g + b


class _Layout:
    def __init__(self, bsz, ctx_len, seq):
        self.B, self.CTX, self.S = bsz, ctx_len, seq
        self.T = ctx_len + seq
        self.tm = math.gcd(256, math.gcd(ctx_len, seq))
        self.nct = ctx_len // self.tm
        self.nlt = seq // self.tm
        self.rows_c = bsz * ctx_len
        self.rows_l = bsz * seq
        self.rows = self.rows_c + self.rows_l
        self.ntiles_c = bsz * self.nct
        self.ntiles = self.rows // self.tm
        assert self.rows_c % seq == 0, "latent K/V blocks are addressed in units of S rows"
        self.tmo = math.gcd(512, math.gcd(self.rows_c, seq))
        self.mod_rows = -(-(bsz + 1) // SUBLANES) * SUBLANES

    def seq_tile(self, b, j):
        return jnp.where(j < self.nct, b * self.nct + j, self.ntiles_c + b * self.nlt + (j - self.nct))

    def seq_block(self, i):
        il = i - self.ntiles_c
        return (jnp.where(i < self.ntiles_c, i // self.nct, il // self.nlt),
                jnp.where(i < self.ntiles_c, i % self.nct, self.nct + il % self.nlt))

    def mod_row(self, i, tile, row_offset=0):
        r = i * tile + row_offset
        return jnp.where(r < self.rows_c, self.B, (r - self.rows_c) // self.S)

    def pos_tile(self, i):
        il = i - self.ntiles_c
        return jnp.where(i < self.ntiles_c, i % self.nct, self.nct + il % self.nlt)


def _ada_body(cv_ref, w_ref, b_ref, o_ref):
    cv = cv_ref[...]
    s = cv * _sigmoid(cv)
    o_ref[0] = _dot3(s, w_ref[0]) + b_ref[0]


def _ada_mods(cvec, ada_w, ada_b):
    depth, d, n = ada_w.shape
    r = cvec.shape[0]
    tn = 512
    return pl.pallas_call(
        _ada_body,
        out_shape=jax.ShapeDtypeStruct((depth, r, n), F32),
        grid=(depth, n // tn),
        in_specs=[pl.BlockSpec((r, d), lambda l, j: (0, 0)),
                  pl.BlockSpec((1, d, tn), lambda l, j: (l, 0, j)),
                  pl.BlockSpec((1, 1, tn), lambda l, j: (l, 0, j))],
        out_specs=pl.BlockSpec((1, r, tn), lambda l, j: (l, 0, j)),
        compiler_params=_cparams(("arbitrary", "arbitrary")),
        name="ada_mods",
    )(cvec, ada_w, ada_b.reshape(depth, 1, n))


def _rope_tables(lay, head_dim, width):
    rows = lay.S // GRID_W
    rr, cc = jnp.meshgrid(jnp.arange(rows), jnp.arange(GRID_W), indexing="ij")
    row_pos = rr.reshape(-1).astype(F32)
    col_pos = cc.reshape(-1).astype(F32)
    axis_dim = head_dim // 2
    inv = ROPE_THETA ** (-jnp.arange(0, axis_dim, 2, dtype=F32) / axis_dim)
    ang = jnp.concatenate([row_pos[:, None] * inv, col_pos[:, None] * inv], -1)
    cos, sin = jnp.cos(ang), jnp.sin(ang)
    cos = jnp.concatenate([jnp.ones((lay.CTX, head_dim // 2), F32), cos], 0)
    sin = jnp.concatenate([jnp.zeros((lay.CTX, head_dim // 2), F32), sin], 0)
    cos_h = jnp.concatenate([cos, cos], -1)
    sin_h = jnp.concatenate([-sin, sin], -1)
    reps = width // head_dim
    return jnp.tile(cos_h, (1, reps)), jnp.tile(sin_h, (1, reps))


def _rope_lanes(x, cos, sin, head_dim):
    w = x.shape[-1]
    half = head_dim // 2
    if head_dim == LANES and w == LANES:
        rot = pltpu.roll(x, half, 1)
    else:
        fwd = pltpu.roll(x, w - half, 1)
        bwd = pltpu.roll(x, half, 1)
        lane = lax.broadcasted_iota(jnp.int32, x.shape, 1)
        rot = jnp.where((lane % head_dim) < half, fwd, bwd)
    return x * cos + rot * sin


def _even_in_body(x_ref, mod_ref, w_ref, cos_ref, sin_ref, pa_ref, q_ref, k_ref, v_ref):
    m = mod_ref[0]
    h = (x_ref[...] * (1.0 + m[1:2]) + m[0:1]).astype(BF16)
    pa_ref[...] = jnp.dot(h, w_ref[:, :A_IN], preferred_element_type=F32)
    cos, sin = cos_ref[...], sin_ref[...]
    o = A_IN

    def put_maps(dst_ref, val):
        pad = jnp.zeros((val.shape[0], LANES - B_HEAD_DIM), BF16)
        for mp in range(B_QK // B_HEAD_DIM):
            piece = val[:, mp * B_HEAD_DIM:(mp + 1) * B_HEAD_DIM].astype(BF16)
            dst_ref[:, mp * LANES:(mp + 1) * LANES] = jnp.concatenate([piece, pad], axis=1)

    q = jnp.dot(h, w_ref[:, o:o + B_QK], preferred_element_type=F32)
    put_maps(q_ref, _rope_lanes(q, cos, sin, B_HEAD_DIM) * (LOG2_E * B_HEAD_DIM ** -0.5))
    o += B_QK
    k = jnp.dot(h, w_ref[:, o:o + B_QK], preferred_element_type=F32)
    put_maps(k_ref, _rope_lanes(k, cos, sin, B_HEAD_DIM))
    o += B_QK
    v_ref[...] = jnp.dot(h, w_ref[:, o:o + B_WIDTH], preferred_element_type=F32).astype(BF16)


def _even_in_proj(lay, x, mod, w_bf16, cos, sin):
    tm, d = lay.tm, D_MODEL
    row = lambda i: (i, 0)
    return pl.pallas_call(
        _even_in_body,
        out_shape=(jax.ShapeDtypeStruct((lay.rows, A_IN), F32),
                   jax.ShapeDtypeStruct((lay.rows, B_QK_PAD), BF16),
                   jax.ShapeDtypeStruct((lay.rows, B_QK_PAD), BF16),
                   jax.ShapeDtypeStruct((lay.rows, B_WIDTH), BF16)),
        grid=(lay.ntiles,),
        in_specs=[pl.BlockSpec((tm, d), row),
                  pl.BlockSpec((1, 6, d), lambda i: (lay.mod_row(i, tm), 0, 0)),
                  pl.BlockSpec((d, EVEN_IN), lambda i: (0, 0)),
                  pl.BlockSpec((tm, B_QK), lambda i: (lay.pos_tile(i), 0)),
                  pl.BlockSpec((tm, B_QK), lambda i: (lay.pos_tile(i), 0))],
        out_specs=(pl.BlockSpec((tm, A_IN), row), pl.BlockSpec((tm, B_QK_PAD), row),
                   pl.BlockSpec((tm, B_QK_PAD), row), pl.BlockSpec((tm, B_WIDTH), row)),
        compiler_params=_cparams(("arbitrary",)),
        name="even_in_proj",
    )(x, mod, w_bf16, cos, sin)


def _rwkv_feat_body(lay, pa_ref, prev_ref, next_ref, mu_ref, w0_ref, w2_ref, a0_ref, a2_ref, g2_ref,
                    kk_ref, ka_ref, rk_ref, bd_ref,
                    w_ref, ops_ref, g_ref, bonus_ref):
    def put(idx, d, val):
        ops_ref[idx - 1, d, 0] = val.astype(BF16)

    i = pl.program_id(0)
    tm = lay.tm
    il = i - lay.ntiles_c
    in_ctx = i < lay.ntiles_c
    seg_first = jnp.where(in_ctx, i % lay.nct == 0, il % lay.nlt == 0)
    seg_last = jnp.where(in_ctx, i % lay.nct == lay.nct - 1, il % lay.nlt == lay.nlt - 1)

    pa = pa_ref[...]
    row = lax.broadcasted_iota(jnp.int32, pa.shape, 0)
    prev_edge = jnp.where(seg_first, 0.0, 1.0) * prev_ref[SUBLANES - 1:SUBLANES, :]
    next_edge = jnp.where(seg_last, 0.0, 1.0) * next_ref[0:1, :]
    prev = jnp.where(row == 0, prev_edge, pltpu.roll(pa, 1, 0))
    nxt = jnp.where(row == tm - 1, next_edge, pltpu.roll(pa, tm - 1, 0))
    u = pa + (0.5 * (prev + nxt) - pa) * mu_ref[...]

    o1, o2, o3 = A_WIDTH, 2 * A_WIDTH, 3 * A_WIDTH
    o4 = o3 + 2 * A_LORA
    o5 = o4 + 2 * A_LORA
    r, k, v = u[:, :o1], u[:, o1:o2], u[:, o2:o3]
    bd = bd_ref[...]

    kk = k * kk_ref[...]
    ss = _dot2_exact_rhs(kk * kk, bd)
    kkn = kk / jnp.maximum(jnp.sqrt(ss), 1e-12)
    g = _dot3(_sigmoid(u[:, o5:]), g2_ref[...])

    g_ref[0] = g

    kd_sum = jnp.zeros_like(k)
    for d in range(2):
        put(_SCAN_R, d, r)
        put(_SCAN_V, d, v)
        put(_SCAN_A, d, -kkn)
        wd = u[:, o3 + d * A_LORA:o3 + (d + 1) * A_LORA]
        ad = u[:, o4 + d * A_LORA:o4 + (d + 1) * A_LORA]
        z = -(w0_ref[d:d + 1, :] + _dot3(jnp.tanh(wd), w2_ref[d]))
        softplus = jnp.maximum(z, 0.0) + jnp.log(1.0 + jnp.exp(-jnp.abs(z)))
        w_log = -softplus - 0.5
        w_ref[0, d, 0] = jnp.exp(-jnp.exp(w_log))
        a = _sigmoid(a0_ref[d:d + 1, :] + _dot3(ad, a2_ref[d]))
        kd = k * (1.0 + (a - 1.0) * ka_ref[...])
        put(_SCAN_K, d, kd)
        put(_SCAN_B, d, kkn * a)
        kd_sum = kd_sum + kd
    bonus_ref[0] = _dot2_exact_rhs(r * kd_sum * rk_ref[...], bd) * v


def _head_block_diag(width, head_dim):
    h = jnp.arange(width) // head_dim
    return (h[:, None] == h[None, :]).astype(BF16)


def _rwkv_features(lay, pa, mu, w0, w2, a0, a2, g2, k_k, k_a, r_k):
    tm = lay.tm
    hb = tm // SUBLANES
    nb8 = lay.rows // SUBLANES
    row = lambda i: (i, 0)
    full2 = lambda i: (0, 0)
    full3 = lambda i: (0, 0, 0)
    w = A_WIDTH
    out = jax.ShapeDtypeStruct((lay.B, lay.T, w), F32)
    seq = lambda i: lay.seq_block(i) + (0,)
    return pl.pallas_call(
        functools.partial(_rwkv_feat_body, lay),
        out_shape=(jax.ShapeDtypeStruct((1, 2, lay.B, lay.T, w), F32),
                   jax.ShapeDtypeStruct((N_SCAN_BF16, 2, lay.B, lay.T, w), BF16), out, out),
        grid=(lay.ntiles,),
        in_specs=[pl.BlockSpec((tm, A_IN), row),
                  pl.BlockSpec((SUBLANES, A_IN), lambda i: (jnp.maximum(i * hb - 1, 0), 0)),
                  pl.BlockSpec((SUBLANES, A_IN), lambda i: (jnp.minimum((i + 1) * hb, nb8 - 1), 0)),
                  pl.BlockSpec((1, A_IN), full2),
                  pl.BlockSpec((2, w), full2),
                  pl.BlockSpec((2, A_LORA, w), full3),
                  pl.BlockSpec((2, w), full2),
                  pl.BlockSpec((2, A_LORA, w), full3),
                  pl.BlockSpec((A_GATE_LORA, w), full2),
                  pl.BlockSpec((1, w), full2),
                  pl.BlockSpec((1, w), full2),
                  pl.BlockSpec((1, w), full2),
                  pl.BlockSpec((w, w), full2)],
        out_specs=(pl.BlockSpec((1, 2, 1, tm, w), lambda i: (0, 0) + lay.seq_block(i) + (0,)),
                   pl.BlockSpec((N_SCAN_BF16, 2, 1, tm, w), lambda i: (0, 0) + lay.seq_block(i) + (0,)),
                   pl.BlockSpec((1, tm, w), seq), pl.BlockSpec((1, tm, w), seq)),
        compiler_params=_cparams(("arbitrary",)),
        name="rwkv_features",
    )(pa, pa, pa, mu.reshape(1, A_IN), w0, w2, a0, a2, g2, k_k.reshape(1, w), k_a.reshape(1, w),
      r_k.reshape(1, w), _head_block_diag(w, A_HEAD_DIM))


def _wkv_scan_body(tb, wf_ref, wb_ref, fwd_ref, bwd_ref, yf_ref, yb_ref, s_ref, m_ref):
    n = A_HEAD_DIM
    chains = s_ref.shape[-1]
    is_fwd = lax.broadcasted_iota(jnp.int32, (n, chains), 1) < chains // 2

    @pl.when(pl.program_id(0) == 0)
    def _():
        s_ref[...] = jnp.zeros_like(s_ref)

    def step(t, carry):
        tr = tb - 1 - t
        m_ref[_SCAN_W] = jnp.where(is_fwd, wf_ref[t, 0], wb_ref[tr, 0])
        for idx in range(N_SCAN_BF16):
            m_ref[idx + 1] = jnp.where(is_fwd, fwd_ref[t, idx], bwd_ref[tr, idx]).astype(F32)
        sa = jnp.zeros((n, chains), F32)
        for kk in range(n):
            sa = sa + s_ref[kk] * m_ref[_SCAN_A, kk:kk + 1, :]
        v_t = m_ref[_SCAN_V]
        y = jnp.zeros((n, chains), F32)
        for kk in range(n):
            s_new = (s_ref[kk] * m_ref[_SCAN_W, kk:kk + 1, :] + sa * m_ref[_SCAN_B, kk:kk + 1, :]
                     + v_t * m_ref[_SCAN_K, kk:kk + 1, :])
            s_ref[kk] = s_new
            y = y + s_new * m_ref[_SCAN_R, kk:kk + 1, :]
        yf_ref[t] = y
        yb_ref[tr] = y
        return carry

    lax.fori_loop(0, tb, step, 0)


def _wkv_scan(lay, decay, ops):
    t, nops, n, lanes = ops.shape
    tb = math.gcd(32, math.gcd(lay.CTX, lay.S))
    nctb, nt = lay.CTX // tb, t // tb
    bwd_tile = lambda g: jnp.where(g < nctb, nctb - 1 - g, nt - 1 - (g - nctb))
    out = jax.ShapeDtypeStruct((t, n, lanes), F32)
    fwd_map = lambda g: (g, 0, 0, 0)
    bwd_map = lambda g: (bwd_tile(g), 0, 0, 0)
    return pl.pallas_call(
        functools.partial(_wkv_scan_body, tb),
        out_shape=(out, out),
        grid=(nt,),
        in_specs=[pl.BlockSpec((tb, 1, n, lanes), fwd_map), pl.BlockSpec((tb, 1, n, lanes), bwd_map),
                  pl.BlockSpec((tb, nops, n, lanes), fwd_map), pl.BlockSpec((tb, nops, n, lanes), bwd_map)],
        out_specs=(pl.BlockSpec((tb, n, lanes), lambda g: (g, 0, 0)),
                   pl.BlockSpec((tb, n, lanes), lambda g: (bwd_tile(g), 0, 0))),
        scratch_shapes=[pltpu.VMEM((n, n, lanes), F32), pltpu.VMEM((nops + 1, n, lanes), F32)],
        compiler_params=_cparams(("arbitrary",)),
        name="wkv7_scan",
    )(decay, decay, ops, ops)


def _to_chains(lay, ops):
    h, n = A_HEADS, A_HEAD_DIM
    nops = ops.shape[0]
    x = ops.reshape(nops, 2, lay.B, lay.T, h, n).transpose(3, 0, 5, 1, 2, 4)
    return x.reshape(lay.T, nops, n, 2 * lay.B * h)


def _from_chains(lay, y):
    h, n = A_HEADS, A_HEAD_DIM
    return y.reshape(lay.T, n, lay.B, h).transpose(2, 0, 3, 1).reshape(lay.B, lay.T, h * n)


def _rwkv_readout_body(yf_ref, yb_ref, bonus_ref, g_ref, lg_ref, lb_ref, bd_ref, o_ref):
    y = yf_ref[0] + yb_ref[0]
    bd = bd_ref[...]
    inv_n = 1.0 / A_HEAD_DIM
    mu = _dot2_exact_rhs(y, bd) * inv_n
    yc = y - mu
    var = _dot2_exact_rhs(yc * yc, bd) * inv_n
    yn = yc * lax.rsqrt(var + A_GN_EPS) * lg_ref[...] + lb_ref[...]
    o_ref[...] = ((yn + bonus_ref[0]) * g_ref[0]).astype(BF16)


def _rwkv_readout(lay, yf, yb, bonus, g, lnx_g, lnx_b):
    tm, w = lay.tm, A_WIDTH
    row = lambda i: (i, 0)
    full = lambda i: (0, 0)
    seq = lambda i: lay.seq_block(i) + (0,)
    return pl.pallas_call(
        _rwkv_readout_body,
        out_shape=jax.ShapeDtypeStruct((lay.rows, w), BF16),
        grid=(lay.ntiles,),
        in_specs=[pl.BlockSpec((1, tm, w), seq)] * 4 + [pl.BlockSpec((1, w), full)] * 2 + [pl.BlockSpec((w, w), full)],
        out_specs=pl.BlockSpec((tm, w), row),
        compiler_params=_cparams(("arbitrary",)),
        name="rwkv_readout",
    )(yf, yb, bonus, g, lnx_g.reshape(1, w), lnx_b.reshape(1, w), _head_block_diag(w, A_HEAD_DIM))


def _exp2_scores(q, keys):
    scores = [_dot_nt(q, kk) for kk in keys]
    m = scores[0].max(axis=-1, keepdims=True)
    for s in scores[1:]:
        m = jnp.maximum(m, s.max(axis=-1, keepdims=True))
    return [jnp.exp2(s - m) for s in scores]


def _with_ones_column(v):
    lane = lax.broadcasted_iota(jnp.int32, (v.shape[0], LANES), 1)
    return jnp.concatenate([v, jnp.where(lane == 0, 1.0, 0.0).astype(BF16)], axis=1)


def _softmax_pv(q, keys, vals_aug):
    e_dim = vals_aug[0].shape[-1] - LANES
    o = None
    for p, vv in zip(_exp2_scores(q, keys), vals_aug):
        part = jnp.dot(p.astype(BF16), vv, preferred_element_type=F32)
        o = part if o is None else o + part
    return o[:, :e_dim] / o[:, e_dim:e_dim + 1]


def _diff_attn_body(lay, lambda_init, q_ref, kc_ref, kl_ref, vc_ref, vl_ref, lam_ref, g_ref, o_ref):
    j = pl.program_id(1)
    lv = lam_ref[...]
    lam = (jnp.exp(jnp.sum(lv[0:1] * lv[1:2], axis=1, keepdims=True))
           - jnp.exp(jnp.sum(lv[2:3] * lv[3:4], axis=1, keepdims=True)) + lambda_init)

    def run(with_latent):
        for h in range(B_HEADS):
            vs = slice(h * B_V_DIM, (h + 1) * B_V_DIM)
            vals = [_with_ones_column(vc_ref[:, vs])] + ([_with_ones_column(vl_ref[:, vs])] if with_latent else [])
            outs = []
            for mi in range(2):
                cs = slice((2 * h + mi) * LANES, (2 * h + mi + 1) * LANES)
                keys = [kc_ref[:, cs]] + ([kl_ref[:, cs]] if with_latent else [])
                outs.append(_softmax_pv(q_ref[:, cs], keys, vals))
            o = outs[0] - lam * outs[1]
            ms = jnp.mean(o * o, axis=-1, keepdims=True)
            o = o * lax.rsqrt(ms + B_SUBLN_EPS) * g_ref[...] * (1.0 - lambda_init)
            o_ref[:, vs] = o.astype(BF16)

    @pl.when(j < lay.nct)
    def _():
        run(False)

    @pl.when(j >= lay.nct)
    def _():
        run(True)


def _diff_attention(lay, q, k, v, lam_vecs, subln_g, lambda_init):
    tm = lay.tm
    w = B_WIDTH
    lat0 = lay.rows_c // lay.S
    return pl.pallas_call(
        functools.partial(_diff_attn_body, lay, lambda_init),
        out_shape=jax.ShapeDtypeStruct((lay.rows, w), BF16),
        grid=(lay.B, lay.nct + lay.nlt),
        in_specs=[pl.BlockSpec((tm, B_QK_PAD), lambda b, j: (lay.seq_tile(b, j), 0)),
                  pl.BlockSpec((lay.CTX, B_QK_PAD), lambda b, j: (b, 0)),
                  pl.BlockSpec((lay.S, B_QK_PAD), lambda b, j: (lat0 + b, 0)),
                  pl.BlockSpec((lay.CTX, w), lambda b, j: (b, 0)),
                  pl.BlockSpec((lay.S, w), lambda b, j: (lat0 + b, 0)),
                  pl.BlockSpec((4, B_HEAD_DIM), lambda b, j: (0, 0)),
                  pl.BlockSpec((1, B_V_DIM), lambda b, j: (0, 0))],
        out_specs=pl.BlockSpec((tm, w), lambda b, j: (lay.seq_tile(b, j), 0)),
        compiler_params=_cparams(("arbitrary", "arbitrary")),
        name="diff_attention",
    )(q, k, k, v, v, lam_vecs, subln_g.reshape(1, B_V_DIM))


def _odd_in_body(x_ref, mod_ref, w_ref, cos_ref, sin_ref, qn_ref, kn_ref, q_ref, k_ref, v_ref):
    m = mod_ref[0]
    h = (x_ref[...] * (1.0 + m[1:2]) + m[0:1]).astype(BF16)
    cos, sin = cos_ref[...], sin_ref[...]

    def norm_rope(p, g, scale):
        ms = jnp.mean(p * p, axis=-1, keepdims=True)
        y = p * lax.rsqrt(ms + QK_NORM_EPS) * g
        return (_rope_lanes(y, cos, sin, C_HEAD_DIM) * scale).astype(BF16)

    p = jnp.dot(h, w_ref[...], preferred_element_type=F32)
    for hd in range(C_HEADS):
        cs = slice(hd * C_HEAD_DIM, (hd + 1) * C_HEAD_DIM)
        q_ref[:, cs] = norm_rope(p[:, cs], qn_ref[...], LOG2_E * C_HEAD_DIM ** -0.5)
    for hd in range(C_KV_HEADS):
        cs = slice(hd * C_HEAD_DIM, (hd + 1) * C_HEAD_DIM)
        k_ref[:, cs] = norm_rope(p[:, C_Q + hd * C_HEAD_DIM:C_Q + (hd + 1) * C_HEAD_DIM], kn_ref[...], 1.0)
    v_ref[...] = p[:, C_Q + C_KV:].astype(BF16)


def _odd_in_proj(lay, x, mod, w_bf16, cos, sin, qn_g, kn_g):
    tm, d = lay.tm, D_MODEL
    row = lambda i: (i, 0)
    full = lambda i: (0, 0)
    return pl.pallas_call(
        _odd_in_body,
        out_shape=(jax.ShapeDtypeStruct((lay.rows, C_Q), BF16),
                   jax.ShapeDtypeStruct((lay.rows, C_KV), BF16),
                   jax.ShapeDtypeStruct((lay.rows, C_KV), BF16)),
        grid=(lay.ntiles,),
        in_specs=[pl.BlockSpec((tm, d), row),
                  pl.BlockSpec((1, 6, d), lambda i: (lay.mod_row(i, tm), 0, 0)),
                  pl.BlockSpec((d, ODD_IN), full),
                  pl.BlockSpec((tm, C_HEAD_DIM), lambda i: (lay.pos_tile(i), 0)),
                  pl.BlockSpec((tm, C_HEAD_DIM), lambda i: (lay.pos_tile(i), 0)),
                  pl.BlockSpec((1, C_HEAD_DIM), full),
                  pl.BlockSpec((1, C_HEAD_DIM), full)],
        out_specs=(pl.BlockSpec((tm, C_Q), row), pl.BlockSpec((tm, C_KV), row), pl.BlockSpec((tm, C_KV), row)),
        compiler_params=_cparams(("arbitrary",)),
        name="odd_in_proj",
    )(x, mod, w_bf16, cos, sin, qn_g.reshape(1, C_HEAD_DIM), kn_g.reshape(1, C_HEAD_DIM))


def _gqa_body(q_ref, kc_ref, kl_ref, vc_ref, vl_ref, o_ref):
    for kvh in range(C_KV_HEADS):
        ks = slice(kvh * C_HEAD_DIM, (kvh + 1) * C_HEAD_DIM)
        keys = [kc_ref[:, ks], kl_ref[:, ks]]
        vals = [_with_ones_column(vc_ref[:, ks]), _with_ones_column(vl_ref[:, ks])]
        for g in range(C_GROUP):
            hd = kvh * C_GROUP + g
            cs = slice(hd * C_HEAD_DIM, (hd + 1) * C_HEAD_DIM)
            o_ref[:, cs] = _softmax_pv(q_ref[:, cs], keys, vals).astype(BF16)


def _gqa_attention(lay, q, k, v):
    tm = lay.tm
    lat0 = lay.rows_c // lay.S
    return pl.pallas_call(
        _gqa_body,
        out_shape=jax.ShapeDtypeStruct((lay.rows_l, C_Q), BF16),
        grid=(lay.B, lay.nlt),
        in_specs=[pl.BlockSpec((tm, C_Q), lambda b, j: (lay.ntiles_c + b * lay.nlt + j, 0)),
                  pl.BlockSpec((lay.CTX, C_KV), lambda b, j: (b, 0)),
                  pl.BlockSpec((lay.S, C_KV), lambda b, j: (lat0 + b, 0)),
                  pl.BlockSpec((lay.CTX, C_KV), lambda b, j: (b, 0)),
                  pl.BlockSpec((lay.S, C_KV), lambda b, j: (lat0 + b, 0))],
        out_specs=pl.BlockSpec((tm, C_Q), lambda b, j: (b * lay.nlt + j, 0)),
        compiler_params=_cparams(("arbitrary", "arbitrary")),
        name="gqa_attention",
    )(q, k, k, v, v)


def _out_proj_body(n_mix, *refs):
    mix_refs = refs[:n_mix]
    w_ref, x_ref, mod_ref, lg_ref, lb_ref, rw_ref, xo_ref, h_ref, lt_ref = refs[n_mix:]
    m = mod_ref[0]
    off = 0
    mix = None
    for mr in mix_refs:
        kw = mr.shape[-1]
        part = jnp.dot(mr[...], w_ref[off:off + kw, :], preferred_element_type=F32)
        mix = part if mix is None else mix + part
        off += kw
    z = DEEPNORM_ALPHA * x_ref[...] + m[2:3] * mix
    xn = _layer_norm_rows(z, lg_ref[...], lb_ref[...])
    xo_ref[...] = xn
    h2 = xn * (1.0 + m[4:5]) + m[3:4]
    h_ref[...] = h2
    lt_ref[...] = _dot3_nt(rw_ref[...], h2)


def _out_proj(lay, mixes, w_bf16, x, row_off, mod, ln_g, ln_b, router_wt, n_rows):
    tm, d = lay.tmo, D_MODEL
    row = lambda i: (i, 0)
    full = lambda i: (0, 0)
    x_tile_off = row_off // tm
    in_specs = [pl.BlockSpec((tm, mx.shape[-1]), row) for mx in mixes]
    in_specs += [pl.BlockSpec((d, d), full),
                 pl.BlockSpec((tm, d), lambda i: (i + x_tile_off, 0)),
                 pl.BlockSpec((1, 6, d), lambda i: (lay.mod_row(i, tm, row_off), 0, 0)),
                 pl.BlockSpec((1, d), full), pl.BlockSpec((1, d), full),
                 pl.BlockSpec((N_EXPERTS, d), full)]
    return pl.pallas_call(
        functools.partial(_out_proj_body, len(mixes)),
        out_shape=(jax.ShapeDtypeStruct((n_rows, d), F32),
                   jax.ShapeDtypeStruct((n_rows, d), F32),
                   jax.ShapeDtypeStruct((N_EXPERTS, n_rows), F32)),
        grid=(n_rows // tm,),
        in_specs=in_specs,
        out_specs=(pl.BlockSpec((tm, d), row), pl.BlockSpec((tm, d), row),
                   pl.BlockSpec((N_EXPERTS, tm), lambda i: (0, i))),
        compiler_params=_cparams(("arbitrary",)),
        name="out_proj_ln",
    )(*mixes, w_bf16, x, mod, ln_g.reshape(1, d), ln_b.reshape(1, d), router_wt)


_EXPERT_PAIRS = ((0, 1), (0, 2), (0, 3), (1, 2), (1, 3), (2, 3))
N_BUCKETS = N_GROUPS * len(_EXPERT_PAIRS)


def _router_body(lt_ref, rb_ref, o_ref):
    logits = lt_ref[...] + rb_ref[...]
    rows = [logits[e:e + 1, :] for e in range(N_EXPERTS)]
    m = rows[0]
    for x in rows[1:]:
        m = jnp.maximum(m, x)
    ex = [jnp.exp(x - m) for x in rows]
    z = ex[0]
    for x in ex[1:]:
        z = z + x
    p = [x / z for x in ex]

    gscore = []
    for g in range(N_GROUPS):
        a, b, c, d = p[4 * g:4 * g + 4]
        hi1, lo1 = jnp.maximum(a, b), jnp.minimum(a, b)
        hi2, lo2 = jnp.maximum(c, d), jnp.minimum(c, d)
        top1 = jnp.maximum(hi1, hi2)
        top2 = jnp.maximum(jnp.minimum(hi1, hi2), jnp.maximum(lo1, lo2))
        gscore.append(top1 + top2)
    best = []
    for g in range(N_GROUPS):
        ok = None
        for o in range(N_GROUPS):
            if o == g:
                continue
            c = (gscore[g] > gscore[o]) if o < g else (gscore[g] >= gscore[o])
            ok = c if ok is None else jnp.logical_and(ok, c)
        best.append(ok)
    won = []
    for e in range(N_EXPERTS):
        g = e // EXPERTS_PER_GROUP
        rank = jnp.zeros_like(p[e])
        for o in range(4 * g, 4 * g + 4):
            if o == e:
                continue
            ahead = (p[o] > p[e]) if o > e else (p[o] >= p[e])
            rank = rank + jnp.where(ahead, 1.0, 0.0)
        won.append(jnp.where(jnp.logical_and(best[g], rank < 1.5), 1.0, 0.0))
    tot = won[0] * p[0]
    for e in range(1, N_EXPERTS):
        tot = tot + won[e] * p[e]
    bucket = jnp.zeros_like(tot)
    gate_a = jnp.zeros_like(tot)
    gate_b = jnp.zeros_like(tot)
    for g in range(N_GROUPS):
        for pid, (a, b) in enumerate(_EXPERT_PAIRS):
            ind = won[4 * g + a] * won[4 * g + b]
            bucket = bucket + ind * float(len(_EXPERT_PAIRS) * g + pid)
            gate_a = gate_a + ind * p[4 * g + a]
            gate_b = gate_b + ind * p[4 * g + b]
    o_ref[...] = jnp.zeros_like(o_ref)
    o_ref[0:1, :] = bucket
    o_ref[1:2, :] = gate_a / tot
    o_ref[2:3, :] = gate_b / tot


def _router(logits_t, router_b):
    e, n = logits_t.shape
    tr = math.gcd(2048, n)
    return pl.pallas_call(
        _router_body,
        out_shape=jax.ShapeDtypeStruct((SUBLANES, n), F32),
        grid=(n // tr,),
        in_specs=[pl.BlockSpec((e, tr), lambda i: (0, i)), pl.BlockSpec((e, 1), lambda i: (0, 0))],
        out_specs=pl.BlockSpec((SUBLANES, tr), lambda i: (0, i)),
        compiler_params=_cparams(("arbitrary",)),
        name="router_gates",
    )(logits_t, router_b.reshape(e, 1))


GATE_LANES = LANES


def _moe_plan(bucket, tg):
    n = bucket.shape[0]
    n_tiles = n // tg + N_BUCKETS
    ids = jnp.arange(N_BUCKETS, dtype=jnp.int32)
    onehot = (bucket[:, None] == ids[None, :]).astype(jnp.int32)
    csum = jnp.cumsum(onehot, axis=0)
    counts = csum[-1]
    tiles = (counts + tg - 1) // tg
    tile_end = jnp.cumsum(tiles)
    row_start = (tile_end - tiles) * tg
    pos = jnp.sum(onehot * (row_start[None, :] + csum - 1), axis=1).astype(jnp.int32)
    n_used = tile_end[-1]
    tile_ids = jnp.minimum(jnp.arange(n_tiles, dtype=jnp.int32), n_used - 1)
    tile_bucket = jnp.sum((tile_ids[:, None] >= tile_end[None, :]).astype(jnp.int32), axis=1)
    pair_a = jnp.asarray([a for a, _ in _EXPERT_PAIRS], jnp.int32)
    pair_b = jnp.asarray([b for _, b in _EXPERT_PAIRS], jnp.int32)
    grp, pid = tile_bucket // len(_EXPERT_PAIRS), tile_bucket % len(_EXPERT_PAIRS)
    ea = grp * EXPERTS_PER_GROUP + pair_a[pid]
    eb = grp * EXPERTS_PER_GROUP + pair_b[pid]
    return pos, ea.astype(jnp.int32), eb.astype(jnp.int32), n_used.reshape(1).astype(jnp.int32), n_tiles


def _row_copy(src_ref, src_row, dst_ref, dst_row, sem):
    return pltpu.make_async_copy(src_ref.at[pl.ds(src_row, 1), :], dst_ref.at[pl.ds(dst_row, 1), :], sem)


ROW_DMA_UNROLL = 8


def _row_copies(n_rows, row_copy, whole_tile_copy):
    def issue(blk, carry):
        for u in range(ROW_DMA_UNROLL):
            row_copy(blk * ROW_DMA_UNROLL + u).start(priority=u % 2)
        return carry

    lax.fori_loop(0, n_rows // ROW_DMA_UNROLL, issue, 0)
    whole_tile_copy.wait()


def _dispatch_body(tg, pos_ref, h_ref, g_ref, xs_in_ref, xs_ref, aug_ref, sem):
    del xs_in_ref
    base = pl.program_id(0) * tg
    aug_ref[:, :D_MODEL] = h_ref[...]
    aug_ref[:, D_MODEL:] = g_ref[...]

    _row_copies(tg, lambda r: _row_copy(aug_ref, r, xs_ref, pos_ref[base + r], sem),
                pltpu.make_async_copy(aug_ref, xs_ref.at[pl.ds(0, tg), :], sem))


def _moe_dispatch(pos, h2, gate_rows, n_tiles, tg):
    n, d = h2.shape
    wide = d + GATE_LANES
    xs0 = jnp.zeros((n_tiles * tg, wide), F32)
    grid_spec = pltpu.PrefetchScalarGridSpec(
        num_scalar_prefetch=1,
        grid=(n // tg,),
        in_specs=[pl.BlockSpec((tg, d), lambda i, pos_ref: (i, 0)),
                  pl.BlockSpec((tg, GATE_LANES), lambda i, pos_ref: (i, 0)),
                  pl.BlockSpec(memory_space=pl.ANY)],
        out_specs=pl.BlockSpec(memory_space=pl.ANY),
        scratch_shapes=[pltpu.VMEM((tg, wide), F32), pltpu.SemaphoreType.DMA(())],
    )
    return pl.pallas_call(
        functools.partial(_dispatch_body, tg),
        out_shape=jax.ShapeDtypeStruct(xs0.shape, F32),
        grid_spec=grid_spec,
        input_output_aliases={3: 0},
        compiler_params=_cparams(("arbitrary",)),
        name="moe_dispatch",
    )(pos, h2, gate_rows, xs0)


def _grouped_body(ea_ref, eb_ref, nu_ref, xs_ref, w1a_ref, w3a_ref, w2a_ref, w1b_ref, w3b_ref, w2b_ref, y_ref):
    del ea_ref, eb_ref

    @pl.when(pl.program_id(0) < nu_ref[0])
    def _():
        x = xs_ref[:, :D_MODEL].astype(BF16)

        def expert(w1_ref, w3_ref, w2_ref, gate):
            h1 = jnp.dot(x, w1_ref[0], preferred_element_type=F32)
            h3 = jnp.dot(x, w3_ref[0], preferred_element_type=F32)
            hid = (h1 * _sigmoid(h1) * h3 * gate).astype(BF16)
            return jnp.dot(hid, w2_ref[0], preferred_element_type=F32)

        y_ref[...] = (expert(w1a_ref, w3a_ref, w2a_ref, xs_ref[:, D_MODEL:D_MODEL + 1])
                      + expert(w1b_ref, w3b_ref, w2b_ref, xs_ref[:, D_MODEL + 1:D_MODEL + 2]))

    @pl.when(pl.program_id(0) >= nu_ref[0])
    def _():
        y_ref[...] = jnp.zeros_like(y_ref)


def _moe_grouped(xs, ea, eb, n_used, w1, w3, w2, tg):
    rows, wide = xs.shape
    d, ff = D_MODEL, EXPERT_FF
    n_tiles = rows // tg
    tile = lambda i, ea_r, eb_r, nu_r: (jnp.minimum(i, nu_r[0] - 1), 0)
    up_a = pl.BlockSpec((1, d, ff), lambda i, ea_r, eb_r, nu_r: (ea_r[i], 0, 0))
    up_b = pl.BlockSpec((1, d, ff), lambda i, ea_r, eb_r, nu_r: (eb_r[i], 0, 0))
    dn_a = pl.BlockSpec((1, ff, d), lambda i, ea_r, eb_r, nu_r: (ea_r[i], 0, 0))
    dn_b = pl.BlockSpec((1, ff, d), lambda i, ea_r, eb_r, nu_r: (eb_r[i], 0, 0))
    grid_spec = pltpu.PrefetchScalarGridSpec(
        num_scalar_prefetch=3,
        grid=(n_tiles,),
        in_specs=[pl.BlockSpec((tg, wide), tile), up_a, up_a, dn_a, up_b, up_b, dn_b],
        out_specs=pl.BlockSpec((tg, d), lambda i, ea_r, eb_r, nu_r: (i, 0)),
    )
    return pl.pallas_call(
        _grouped_body,
        out_shape=jax.ShapeDtypeStruct((rows, d), F32),
        grid_spec=grid_spec,
        compiler_params=_cparams(("arbitrary",)),
        name="moe_grouped",
    )(ea, eb, n_used, xs, w1, w3, w2, w1, w3, w2)


def _combine_body(tg, pos_ref, ys_ref, x_ref, mod_ref, lg_ref, lb_ref, o_ref, buf_ref, sem):
    base = pl.program_id(0) * tg

    _row_copies(tg, lambda r: _row_copy(ys_ref, pos_ref[base + r], buf_ref, r, sem),
                pltpu.make_async_copy(ys_ref.at[pl.ds(0, tg), :], buf_ref, sem))
    m = mod_ref[0]
    z = DEEPNORM_ALPHA * x_ref[...] + m[5:6] * buf_ref[...]
    o_ref[...] = _layer_norm_rows(z, lg_ref[...], lb_ref[...])


def _moe_combine(lay, pos, ys, x, mod, ln_g, ln_b, row_off, tg):
    n, d = x.shape
    grid_spec = pltpu.PrefetchScalarGridSpec(
        num_scalar_prefetch=1,
        grid=(n // tg,),
        in_specs=[pl.BlockSpec(memory_space=pl.ANY),
                  pl.BlockSpec((tg, d), lambda i, pos_ref: (i, 0)),
                  pl.BlockSpec((1, 6, d), lambda i, pos_ref: (lay.mod_row(i, tg, row_off), 0, 0)),
                  pl.BlockSpec((1, d), lambda i, pos_ref: (0, 0)),
                  pl.BlockSpec((1, d), lambda i, pos_ref: (0, 0))],
        out_specs=pl.BlockSpec((tg, d), lambda i, pos_ref: (i, 0)),
        scratch_shapes=[pltpu.VMEM((tg, d), F32), pltpu.SemaphoreType.DMA(())],
    )
    return pl.pallas_call(
        functools.partial(_combine_body, tg),
        out_shape=jax.ShapeDtypeStruct((n, d), F32),
        grid_spec=grid_spec,
        compiler_params=_cparams(("arbitrary",)),
        name="moe_combine_ln",
    )(pos, ys, x, mod, ln_g.reshape(1, d), ln_b.reshape(1, d))


def _moe(lay, h2, routed, w1, w3, w2, x, mod, ln_g, ln_b, row_off):
    tg = lay.tm
    bucket = routed[0].astype(jnp.int32)
    gate_rows = jnp.pad(routed[1:3].T, ((0, 0), (0, GATE_LANES - 2)))
    pos, ea, eb, n_used, n_tiles = _moe_plan(bucket, tg)
    xs = _moe_dispatch(pos, h2, gate_rows, n_tiles, tg)
    ys = _moe_grouped(xs, ea, eb, n_used, w1, w3, w2, tg)
    return _moe_combine(lay, pos, ys, x, mod, ln_g, ln_b, row_off, tg)


def kernel(x, c, ctx, c_ctx, router_w, router_b, ada_w, ada_b, ln1_g, ln1_b, ln2_g, ln2_b, moe_w1, moe_w3, moe_w2, ev_w_in, ev_w_out, ev_a_mu, ev_a_w0, ev_a_w2, ev_a_a0, ev_a_a2, ev_a_g2, ev_a_kk, ev_a_ka, ev_a_rk, ev_a_lnx_g, ev_a_lnx_b, ev_b_lam, ev_b_subln_g, od_w_in, od_w_out, od_qn_g, od_kn_g):
    bsz, seq, d = x.shape
    ctx_len = ctx.shape[1]
    assert d == D_MODEL and ada_w.shape[0] == DEPTH and seq % GRID_W == 0
    lay = _Layout(bsz, ctx_len, seq)

    cvec = jnp.zeros((lay.mod_rows, d), F32).at[:bsz].set(c).at[bsz].set(c_ctx)
    mods = _ada_mods(cvec, ada_w, ada_b).reshape(DEPTH, lay.mod_rows, 6, d)

    xs = jnp.concatenate([ctx.reshape(lay.rows_c, d), x.reshape(lay.rows_l, d)], axis=0)
    router_wt = router_w.T

    cos_b, sin_b = _rope_tables(lay, B_HEAD_DIM, B_QK)
    cos_c, sin_c = _rope_tables(lay, C_HEAD_DIM, C_HEAD_DIM)

    for i in range(DEPTH):
        last = i == DEPTH - 1
        j = i // 2
        mod = mods[i]
        if i % 2 == 0:
            lambda_init = 0.8 - 0.6 * math.exp(-0.3 * i)
            pa, q, k, v = _even_in_proj(lay, xs, mod, ev_w_in[j].astype(BF16), cos_b, sin_b)
            decay, scan_ops, g_, bonus = _rwkv_features(
                lay, pa, ev_a_mu[j], ev_a_w0[j], ev_a_w2[j], ev_a_a0[j], ev_a_a2[j], ev_a_g2[j],
                ev_a_kk[j], ev_a_ka[j], ev_a_rk[j].reshape(-1))
            y_f, y_b = _wkv_scan(lay, _to_chains(lay, decay), _to_chains(lay, scan_ops))
            nl = lay.B * A_HEADS
            yf = _from_chains(lay, y_f[:, :, :nl])
            yb = _from_chains(lay, y_b[:, :, nl:])
            ya = _rwkv_readout(lay, yf, yb, bonus, g_, ev_a_lnx_g[j], ev_a_lnx_b[j])
            yd = _diff_attention(lay, q, k, v, ev_b_lam[j], ev_b_subln_g[j], lambda_init)
            mixes, w_out = [ya, yd], ev_w_out[j]
            n_rows, tile_off = (lay.rows_l, lay.ntiles_c) if last else (lay.rows, 0)
            if last:
                mixes = [mx[lay.rows_c:] for mx in mixes]
        else:
            q, k, v = _odd_in_proj(lay, xs, mod, od_w_in[j].astype(BF16), cos_c, sin_c, od_qn_g[j], od_kn_g[j])
            assert last, "an odd layer that must also update the context stream is not supported"
            o = _gqa_attention(lay, q, k, v)
            mixes, w_out = [o], od_w_out[j]
            n_rows, tile_off = lay.rows_l, lay.ntiles_c
        row_off = tile_off * lay.tm
        x_new, h2, logits_t = _out_proj(lay, mixes, w_out.astype(BF16), xs, row_off, mod,
                                        ln1_g[i], ln1_b[i], router_wt, n_rows)
        routed = _router(logits_t, router_b)
        xs = _moe(lay, h2, routed, moe_w1[i].astype(BF16), moe_w3[i].astype(BF16), moe_w2[i].astype(BF16),
                  x_new, mod, ln2_g[i], ln2_b[i], row_off)
    return xs.reshape(bsz, seq, d)
```

```python
import functools
import math

import jax
import jax.numpy as jnp
from jax import lax
from jax.experimental import pallas as pl
from jax.experimental.pallas import tpu as pltpu

F32 = jnp.float32
BF16 = jnp.bfloat16

D_MODEL = 1024
DEPTH = 2
GRID_W = 64
ROPE_THETA = 10000.0
LN_EPS = 1e-5
LOG2_E = 1.4426950408889634
DEEPNORM_ALPHA = (2 * DEPTH) ** 0.25

A_HEAD_DIM = 64
A_HEADS = 8
A_WIDTH = 512
A_LORA = 64
A_GATE_LORA = 128
A_GN_EPS = 64e-5
A_IN = 3 * A_WIDTH + 4 * A_LORA + A_GATE_LORA
_SCAN_R, _SCAN_K, _SCAN_V, _SCAN_A, _SCAN_B = range(5)
N_SCAN_OPS = 5

B_HEAD_DIM = 64
B_V_DIM = 128
B_HEADS = 4
B_WIDTH = 512
B_QK = 512
B_QK_PAD = (B_QK // B_HEAD_DIM) * 128
B_SUBLN_EPS = 1e-5
EVEN_IN = A_IN + 2 * B_QK + B_WIDTH

C_HEAD_DIM = 128
C_HEADS = 8
C_KV_HEADS = 2
C_GROUP = 4
C_Q = 1024
C_KV = 256
ODD_IN = C_Q + 2 * C_KV
QK_NORM_EPS = 1e-6

N_EXPERTS = 16
N_GROUPS = 4
EXPERTS_PER_GROUP = 4
EXPERT_FF = 512

VMEM_LIMIT_BYTES = 56 * 1024 * 1024
LANES = 128
SUBLANES = 8


def _cparams(sem):
    return pltpu.CompilerParams(dimension_semantics=sem, vmem_limit_bytes=VMEM_LIMIT_BYTES)


def _dot(a, b):
    return jnp.dot(a.astype(BF16), b.astype(BF16), preferred_element_type=F32)


def _dot_nt(a, b):
    return lax.dot_general(a.astype(BF16), b.astype(BF16), (((1,), (1,)), ((), ())),
                           preferred_element_type=F32)


def _split(a):
    hi = a.astype(BF16)
    lo = (a - hi.astype(F32)).astype(BF16)
    return hi, lo


def _dot3(a, b):
    ah, al = _split(a)
    bh, bl = _split(b)
    return (jnp.dot(ah, bh, preferred_element_type=F32)
            + (jnp.dot(ah, bl, preferred_element_type=F32)
               + jnp.dot(al, bh, preferred_element_type=F32)))


def _dot2_exact_rhs(a, b_bf16):
    ah, al = _split(a)
    return jnp.dot(ah, b_bf16, preferred_element_type=F32) + jnp.dot(al, b_bf16, preferred_element_type=F32)


def _dot3_nt(a, b):
    ah, al = _split(a)
    bh, bl = _split(b)
    dn = (((1,), (1,)), ((), ()))
    return (lax.dot_general(ah, bh, dn, preferred_element_type=F32)
            + (lax.dot_general(ah, bl, dn, preferred_element_type=F32)
               + lax.dot_general(al, bh, dn, preferred_element_type=F32)))


def _sigmoid(x):
    return 1.0 / (1.0 + jnp.exp(-x))


def _layer_norm_rows(z, g, b):
    mu = jnp.mean(z, axis=-1, keepdims=True)
    zc = z - mu
    var = jnp.mean(zc * zc, axis=-1, keepdims=True)
    return zc * lax.rsqrt(var + LN_EPS) * g + b


class _Layout:
    def __init__(self, bsz, ctx_len, seq):
        self.B, self.CTX, self.S = bsz, ctx_len, seq
        self.T = ctx_len + seq
        self.tm = math.gcd(256, math.gcd(ctx_len, seq))
        self.nct = ctx_len // self.tm
        self.nlt = seq // self.tm
        self.rows_c = bsz * ctx_len
        self.rows_l = bsz * seq
        self.rows = self.rows_c + self.rows_l
        self.ntiles_c = bsz * self.nct
        self.ntiles = self.rows // self.tm
        assert self.rows_c % seq == 0, "latent K/V blocks are addressed in units of S rows"
        self.tb = math.gcd(32, math.gcd(ctx_len, seq))
        self.tmo = math.gcd(512, math.gcd(self.rows_c, seq))
        self.mod_rows = -(-(bsz + 1) // SUBLANES) * SUBLANES

    def seq_tile(self, b, j):
        return jnp.where(j < self.nct, b * self.nct + j, self.ntiles_c + b * self.nlt + (j - self.nct))

    def seq_block(self, i):
        il = i - self.ntiles_c
        return (jnp.where(i < self.ntiles_c, i // self.nct, il // self.nlt),
                jnp.where(i < self.ntiles_c, i % self.nct, self.nct + il % self.nlt))

    def mod_row(self, i, tile, row_offset=0):
        r = i * tile + row_offset
        return jnp.where(r < self.rows_c, self.B, (r - self.rows_c) // self.S)

    def pos_tile(self, i):
        il = i - self.ntiles_c
        return jnp.where(i < self.ntiles_c, i % self.nct, self.nct + il % self.nlt)


def _ada_body(cv_ref, w_ref, b_ref, o_ref):
    cv = cv_ref[...]
    s = cv * _sigmoid(cv)
    o_ref[0] = _dot3(s, w_ref[0]) + b_ref[0]


def _ada_mods(cvec, ada_w, ada_b):
    depth, d, n = ada_w.shape
    r = cvec.shape[0]
    tn = 512
    return pl.pallas_call(
        _ada_body,
        out_shape=jax.ShapeDtypeStruct((depth, r, n), F32),
        grid=(depth, n // tn),
        in_specs=[pl.BlockSpec((r, d), lambda l, j: (0, 0)),
                  pl.BlockSpec((1, d, tn), lambda l, j: (l, 0, j)),
                  pl.BlockSpec((1, 1, tn), lambda l, j: (l, 0, j))],
        out_specs=pl.BlockSpec((1, r, tn), lambda l, j: (l, 0, j)),
        compiler_params=_cparams(("arbitrary", "arbitrary")),
        name="ada_mods",
    )(cvec, ada_w, ada_b.reshape(depth, 1, n))


def _rope_tables(lay, head_dim, width):
    rows = lay.S // GRID_W
    rr, cc = jnp.meshgrid(jnp.arange(rows), jnp.arange(GRID_W), indexing="ij")
    row_pos = rr.reshape(-1).astype(F32)
    col_pos = cc.reshape(-1).astype(F32)
    axis_dim = head_dim // 2
    inv = ROPE_THETA ** (-jnp.arange(0, axis_dim, 2, dtype=F32) / axis_dim)
    ang = jnp.concatenate([row_pos[:, None] * inv, col_pos[:, None] * inv], -1)
    cos, sin = jnp.cos(ang), jnp.sin(ang)
    cos = jnp.concatenate([jnp.ones((lay.CTX, head_dim // 2), F32), cos], 0)
    sin = jnp.concatenate([jnp.zeros((lay.CTX, head_dim // 2), F32), sin], 0)
    cos_h = jnp.concatenate([cos, cos], -1)
    sin_h = jnp.concatenate([-sin, sin], -1)
    reps = width // head_dim
    return jnp.tile(cos_h, (1, reps)), jnp.tile(sin_h, (1, reps))


def _rope_lanes(x, cos, sin, head_dim):
    w = x.shape[-1]
    half = head_dim // 2
    if head_dim == LANES and w == LANES:
        rot = pltpu.roll(x, half, 1)
    else:
        fwd = pltpu.roll(x, w - half, 1)
        bwd = pltpu.roll(x, half, 1)
        lane = lax.broadcasted_iota(jnp.int32, x.shape, 1)
        rot = jnp.where((lane % head_dim) < half, fwd, bwd)
    return x * cos + rot * sin


def _even_in_body(x_ref, mod_ref, w_ref, cos_ref, sin_ref, pa_ref, q_ref, k_ref, v_ref):
    m = mod_ref[0]
    h = (x_ref[...] * (1.0 + m[1:2]) + m[0:1]).astype(BF16)
    pa_ref[...] = jnp.dot(h, w_ref[:, :A_IN], preferred_element_type=F32)
    cos, sin = cos_ref[...], sin_ref[...]
    o = A_IN

    def put_maps(dst_ref, val):
        pad = jnp.zeros((val.shape[0], LANES - B_HEAD_DIM), BF16)
        for mp in range(B_QK // B_HEAD_DIM):
            piece = val[:, mp * B_HEAD_DIM:(mp + 1) * B_HEAD_DIM].astype(BF16)
            dst_ref[:, mp * LANES:(mp + 1) * LANES] = jnp.concatenate([piece, pad], axis=1)

    q = jnp.dot(h, w_ref[:, o:o + B_QK], preferred_element_type=F32)
    put_maps(q_ref, _rope_lanes(q, cos, sin, B_HEAD_DIM) * (LOG2_E * B_HEAD_DIM ** -0.5))
    o += B_QK
    k = jnp.dot(h, w_ref[:, o:o + B_QK], preferred_element_type=F32)
    put_maps(k_ref, _rope_lanes(k, cos, sin, B_HEAD_DIM))
    o += B_QK
    v_ref[...] = jnp.dot(h, w_ref[:, o:o + B_WIDTH], preferred_element_type=F32).astype(BF16)


def _even_in_proj(lay, x, mod, w_bf16, cos, sin):
    tm, d = lay.tm, D_MODEL
    row = lambda i: (i, 0)
    return pl.pallas_call(
        _even_in_body,
        out_shape=(jax.ShapeDtypeStruct((lay.rows, A_IN), F32),
                   jax.ShapeDtypeStruct((lay.rows, B_QK_PAD), BF16),
                   jax.ShapeDtypeStruct((lay.rows, B_QK_PAD), BF16),
                   jax.ShapeDtypeStruct((lay.rows, B_WIDTH), BF16)),
        grid=(lay.ntiles,),
        in_specs=[pl.BlockSpec((tm, d), row),
                  pl.BlockSpec((1, 6, d), lambda i: (lay.mod_row(i, tm), 0, 0)),
                  pl.BlockSpec((d, EVEN_IN), lambda i: (0, 0)),
                  pl.BlockSpec((tm, B_QK), lambda i: (lay.pos_tile(i), 0)),
                  pl.BlockSpec((tm, B_QK), lambda i: (lay.pos_tile(i), 0))],
        out_specs=(pl.BlockSpec((tm, A_IN), row), pl.BlockSpec((tm, B_QK_PAD), row),
                   pl.BlockSpec((tm, B_QK_PAD), row), pl.BlockSpec((tm, B_WIDTH), row)),
        compiler_params=_cparams(("arbitrary",)),
        name="even_in_proj",
    )(x, mod, w_bf16, cos, sin)


def _rwkv_feat_body(lay, pa_ref, prev_ref, next_ref, mu_ref, w0_ref, w2_ref, a0_ref, a2_ref, g2_ref,
                    kk_ref, ka_ref, rk_ref, bd_ref, tri_ref, blk_ref,
                    pend_ref, ops_ref, g_ref, bonus_ref):
    def put(idx, d, val):
        ops_ref[idx, d, 0] = val.astype(BF16)

    i = pl.program_id(0)
    tm = lay.tm
    il = i - lay.ntiles_c
    in_ctx = i < lay.ntiles_c
    seg_first = jnp.where(in_ctx, i % lay.nct == 0, il % lay.nlt == 0)
    seg_last = jnp.where(in_ctx, i % lay.nct == lay.nct - 1, il % lay.nlt == lay.nlt - 1)

    pa = pa_ref[...]
    row = lax.broadcasted_iota(jnp.int32, pa.shape, 0)
    prev_edge = jnp.where(seg_first, 0.0, 1.0) * prev_ref[SUBLANES - 1:SUBLANES, :]
    next_edge = jnp.where(seg_last, 0.0, 1.0) * next_ref[0:1, :]
    prev = jnp.where(row == 0, prev_edge, pltpu.roll(pa, 1, 0))
    nxt = jnp.where(row == tm - 1, next_edge, pltpu.roll(pa, tm - 1, 0))
    u = pa + (0.5 * (prev + nxt) - pa) * mu_ref[...]

    o1, o2, o3 = A_WIDTH, 2 * A_WIDTH, 3 * A_WIDTH
    o4 = o3 + 2 * A_LORA
    o5 = o4 + 2 * A_LORA
    r, k, v = u[:, :o1], u[:, o1:o2], u[:, o2:o3]
    bd = bd_ref[...]

    kk = k * kk_ref[...]
    ss = _dot2_exact_rhs(kk * kk, bd)
    kkn = kk / jnp.maximum(jnp.sqrt(ss), 1e-12)
    g = _dot3(_sigmoid(u[:, o5:]), g2_ref[...])

    g_ref[0] = g

    kd_sum = jnp.zeros_like(k)
    for d in range(2):
        wd = u[:, o3 + d * A_LORA:o3 + (d + 1) * A_LORA]
        ad = u[:, o4 + d * A_LORA:o4 + (d + 1) * A_LORA]
        log_decay = -math.exp(-0.5) * _sigmoid(w0_ref[d:d + 1, :] + _dot3(jnp.tanh(wd), w2_ref[d]))
        ld_hi, ld_lo = _split(log_decay)
        cum = (jnp.dot(tri_ref[d], ld_hi, preferred_element_type=F32)
               + jnp.dot(tri_ref[d], ld_lo, preferred_element_type=F32))
        tot = (jnp.dot(blk_ref[...], ld_hi, preferred_element_type=F32)
               + jnp.dot(blk_ref[...], ld_lo, preferred_element_type=F32))
        pend_ref[d, 0] = jnp.exp(tot)
        p = jnp.exp(cum)
        p_inv = 1.0 / p
        a = _sigmoid(a0_ref[d:d + 1, :] + _dot3(ad, a2_ref[d]))
        kd = k * (1.0 + (a - 1.0) * ka_ref[...])
        put(_SCAN_R, d, r * p)
        put(_SCAN_V, d, v)
        put(_SCAN_A, d, -kkn * jnp.exp(cum - log_decay))
        put(_SCAN_K, d, kd * p_inv)
        put(_SCAN_B, d, kkn * a * p_inv)
        kd_sum = kd_sum + kd
    bonus_ref[0] = _dot2_exact_rhs(r * kd_sum * rk_ref[...], bd) * v


def _head_block_diag(width, head_dim):
    h = jnp.arange(width) // head_dim
    return (h[:, None] == h[None, :]).astype(BF16)


def _rwkv_features(lay, pa, mu, w0, w2, a0, a2, g2, k_k, k_a, r_k):
    tm = lay.tm
    hb = tm // SUBLANES
    nb8 = lay.rows // SUBLANES
    row = lambda i: (i, 0)
    full2 = lambda i: (0, 0)
    full3 = lambda i: (0, 0, 0)
    w = A_WIDTH
    out = jax.ShapeDtypeStruct((lay.B, lay.T, w), F32)
    seq = lambda i: lay.seq_block(i) + (0,)
    tb = lay.tb
    nblk = tm // tb
    pos = jnp.arange(tm)
    same_blk = (pos[:, None] // tb) == (pos[None, :] // tb)
    tri = jnp.stack([same_blk & (pos[None, :] <= pos[:, None]),
                     same_blk & (pos[None, :] >= pos[:, None])]).astype(BF16)
    blk = (jnp.arange(nblk)[:, None] == (pos[None, :] // tb)).astype(BF16)
    return pl.pallas_call(
        functools.partial(_rwkv_feat_body, lay),
        out_shape=(jax.ShapeDtypeStruct((2, lay.B, lay.T // tb, w), F32),
                   jax.ShapeDtypeStruct((N_SCAN_OPS, 2, lay.B, lay.T, w), BF16), out, out),
        grid=(lay.ntiles,),
        in_specs=[pl.BlockSpec((tm, A_IN), row),
                  pl.BlockSpec((SUBLANES, A_IN), lambda i: (jnp.maximum(i * hb - 1, 0), 0)),
                  pl.BlockSpec((SUBLANES, A_IN), lambda i: (jnp.minimum((i + 1) * hb, nb8 - 1), 0)),
                  pl.BlockSpec((1, A_IN), full2),
                  pl.BlockSpec((2, w), full2),
                  pl.BlockSpec((2, A_LORA, w), full3),
                  pl.BlockSpec((2, w), full2),
                  pl.BlockSpec((2, A_LORA, w), full3),
                  pl.BlockSpec((A_GATE_LORA, w), full2),
                  pl.BlockSpec((1, w), full2),
                  pl.BlockSpec((1, w), full2),
                  pl.BlockSpec((1, w), full2),
                  pl.BlockSpec((w, w), full2),
                  pl.BlockSpec((2, tm, tm), full3),
                  pl.BlockSpec((nblk, tm), full2)],
        out_specs=(pl.BlockSpec((2, 1, nblk, w), lambda i: (0,) + lay.seq_block(i) + (0,)),
                   pl.BlockSpec((N_SCAN_OPS, 2, 1, tm, w), lambda i: (0, 0) + lay.seq_block(i) + (0,)),
                   pl.BlockSpec((1, tm, w), seq), pl.BlockSpec((1, tm, w), seq)),
        compiler_params=_cparams(("arbitrary",)),
        name="rwkv_features",
    )(pa, pa, pa, mu.reshape(1, A_IN), w0, w2, a0, a2, g2, k_k.reshape(1, w), k_a.reshape(1, w),
      r_k.reshape(1, w), _head_block_diag(w, A_HEAD_DIM), tri, blk)


def _wkv_scan_body(tb, pf_ref, pb_ref, fwd_ref, bwd_ref, yf_ref, yb_ref, s_ref, m_ref):
    n = A_HEAD_DIM
    chains = s_ref.shape[-1]
    is_fwd = lax.broadcasted_iota(jnp.int32, (n, chains), 1) < chains // 2

    @pl.when(pl.program_id(0) == 0)
    def _():
        s_ref[...] = jnp.zeros_like(s_ref)

    def step(t, carry):
        tr = tb - 1 - t
        for idx in range(N_SCAN_OPS):
            m_ref[idx] = jnp.where(is_fwd, fwd_ref[t, idx], bwd_ref[tr, idx]).astype(F32)
        sa = jnp.zeros((n, chains), F32)
        for kk in range(n):
            sa = sa + s_ref[kk] * m_ref[_SCAN_A, kk:kk + 1, :]
        v_t = m_ref[_SCAN_V]
        y = jnp.zeros((n, chains), F32)
        for kk in range(n):
            s_new = s_ref[kk] + sa * m_ref[_SCAN_B, kk:kk + 1, :] + v_t * m_ref[_SCAN_K, kk:kk + 1, :]
            s_ref[kk] = s_new
            y = y + s_new * m_ref[_SCAN_R, kk:kk + 1, :]
        yf_ref[t] = y
        yb_ref[tr] = y
        return carry

    lax.fori_loop(0, tb, step, 0)
    m_ref[0] = jnp.where(is_fwd, pf_ref[0, 0], pb_ref[0, 0])
    for kk in range(n):
        s_ref[kk] = s_ref[kk] * m_ref[0, kk:kk + 1, :]


def _wkv_scan(lay, block_decay, ops):
    t, nops, n, lanes = ops.shape
    tb = lay.tb
    nctb, nt = lay.CTX // tb, t // tb
    bwd_tile = lambda g: jnp.where(g < nctb, nctb - 1 - g, nt - 1 - (g - nctb))
    out = jax.ShapeDtypeStruct((t, n, lanes), F32)
    fwd_map = lambda g: (g, 0, 0, 0)
    bwd_map = lambda g: (bwd_tile(g), 0, 0, 0)
    return pl.pallas_call(
        functools.partial(_wkv_scan_body, tb),
        out_shape=(out, out),
        grid=(nt,),
        in_specs=[pl.BlockSpec((1, 1, n, lanes), fwd_map), pl.BlockSpec((1, 1, n, lanes), bwd_map),
                  pl.BlockSpec((tb, nops, n, lanes), fwd_map), pl.BlockSpec((tb, nops, n, lanes), bwd_map)],
        out_specs=(pl.BlockSpec((tb, n, lanes), lambda g: (g, 0, 0)),
                   pl.BlockSpec((tb, n, lanes), lambda g: (bwd_tile(g), 0, 0))),
        scratch_shapes=[pltpu.VMEM((n, n, lanes), F32), pltpu.VMEM((nops, n, lanes), F32)],
        compiler_params=_cparams(("arbitrary",)),
        name="wkv7_scan",
    )(block_decay, block_decay, ops, ops)


def _to_chains(lay, ops):
    h, n = A_HEADS, A_HEAD_DIM
    nops, length = ops.shape[0], ops.shape[3]
    x = ops.reshape(nops, 2, lay.B, length, h, n).transpose(3, 0, 5, 1, 2, 4)
    return x.reshape(length, nops, n, 2 * lay.B * h)


def _from_chains(lay, y):
    h, n = A_HEADS, A_HEAD_DIM
    return y.reshape(lay.T, n, lay.B, h).transpose(2, 0, 3, 1).reshape(lay.B, lay.T, h * n)


def _rwkv_readout_body(yf_ref, yb_ref, bonus_ref, g_ref, lg_ref, lb_ref, bd_ref, o_ref):
    y = yf_ref[0] + yb_ref[0]
    bd = bd_ref[...]
    inv_n = 1.0 / A_HEAD_DIM
    mu = _dot2_exact_rhs(y, bd) * inv_n
    yc = y - mu
    var = _dot2_exact_rhs(yc * yc, bd) * inv_n
    yn = yc * lax.rsqrt(var + A_GN_EPS) * lg_ref[...] + lb_ref[...]
    o_ref[...] = ((yn + bonus_ref[0]) * g_ref[0]).astype(BF16)


def _rwkv_readout(lay, yf, yb, bonus, g, lnx_g, lnx_b):
    tm, w = lay.tm, A_WIDTH
    row = lambda i: (i, 0)
    full = lambda i: (0, 0)
    seq = lambda i: lay.seq_block(i) + (0,)
    return pl.pallas_call(
        _rwkv_readout_body,
        out_shape=jax.ShapeDtypeStruct((lay.rows, w), BF16),
        grid=(lay.ntiles,),
        in_specs=[pl.BlockSpec((1, tm, w), seq)] * 4 + [pl.BlockSpec((1, w), full)] * 2 + [pl.BlockSpec((w, w), full)],
        out_specs=pl.BlockSpec((tm, w), row),
        compiler_params=_cparams(("arbitrary",)),
        name="rwkv_readout",
    )(yf, yb, bonus, g, lnx_g.reshape(1, w), lnx_b.reshape(1, w), _head_block_diag(w, A_HEAD_DIM))


def _exp2_scores(q, keys):
    scores = [_dot_nt(q, kk) for kk in keys]
    m = scores[0].max(axis=-1, keepdims=True)
    for s in scores[1:]:
        m = jnp.maximum(m, s.max(axis=-1, keepdims=True))
    return [jnp.exp2(s - m) for s in scores]


def _with_ones_column(v):
    lane = lax.broadcasted_iota(jnp.int32, (v.shape[0], LANES), 1)
    return jnp.concatenate([v, jnp.where(lane == 0, 1.0, 0.0).astype(BF16)], axis=1)


def _softmax_pv(q, keys, vals_aug):
    e_dim = vals_aug[0].shape[-1] - LANES
    o = None
    for p, vv in zip(_exp2_scores(q, keys), vals_aug):
        part = jnp.dot(p.astype(BF16), vv, preferred_element_type=F32)
        o = part if o is None else o + part
    return o[:, :e_dim] / o[:, e_dim:e_dim + 1]


def _diff_attn_body(lay, lambda_init, q_ref, kc_ref, kl_ref, vc_ref, vl_ref, lam_ref, g_ref, o_ref):
    j = pl.program_id(1)
    lv = lam_ref[...]
    lam = (jnp.exp(jnp.sum(lv[0:1] * lv[1:2], axis=1, keepdims=True))
           - jnp.exp(jnp.sum(lv[2:3] * lv[3:4], axis=1, keepdims=True)) + lambda_init)

    def run(with_latent):
        for h in range(B_HEADS):
            vs = slice(h * B_V_DIM, (h + 1) * B_V_DIM)
            vals = [_with_ones_column(vc_ref[:, vs])] + ([_with_ones_column(vl_ref[:, vs])] if with_latent else [])
            outs = []
            for mi in range(2):
                cs = slice((2 * h + mi) * LANES, (2 * h + mi + 1) * LANES)
                keys = [kc_ref[:, cs]] + ([kl_ref[:, cs]] if with_latent else [])
                outs.append(_softmax_pv(q_ref[:, cs], keys, vals))
            o = outs[0] - lam * outs[1]
            ms = jnp.mean(o * o, axis=-1, keepdims=True)
            o = o * lax.rsqrt(ms + B_SUBLN_EPS) * g_ref[...] * (1.0 - lambda_init)
            o_ref[:, vs] = o.astype(BF16)

    @pl.when(j < lay.nct)
    def _():
        run(False)

    @pl.when(j >= lay.nct)
    def _():
        run(True)


def _diff_attention(lay, q, k, v, lam_vecs, subln_g, lambda_init):
    tm = lay.tm
    w = B_WIDTH
    lat0 = lay.rows_c // lay.S
    return pl.pallas_call(
        functools.partial(_diff_attn_body, lay, lambda_init),
        out_shape=jax.ShapeDtypeStruct((lay.rows, w), BF16),
        grid=(lay.B, lay.nct + lay.nlt),
        in_specs=[pl.BlockSpec((tm, B_QK_PAD), lambda b, j: (lay.seq_tile(b, j), 0)),
                  pl.BlockSpec((lay.CTX, B_QK_PAD), lambda b, j: (b, 0)),
                  pl.BlockSpec((lay.S, B_QK_PAD), lambda b, j: (lat0 + b, 0)),
                  pl.BlockSpec((lay.CTX, w), lambda b, j: (b, 0)),
                  pl.BlockSpec((lay.S, w), lambda b, j: (lat0 + b, 0)),
                  pl.BlockSpec((4, B_HEAD_DIM), lambda b, j: (0, 0)),
                  pl.BlockSpec((1, B_V_DIM), lambda b, j: (0, 0))],
        out_specs=pl.BlockSpec((tm, w), lambda b, j: (lay.seq_tile(b, j), 0)),
        compiler_params=_cparams(("arbitrary", "arbitrary")),
        name="diff_attention",
    )(q, k, k, v, v, lam_vecs, subln_g.reshape(1, B_V_DIM))


def _odd_in_body(x_ref, mod_ref, w_ref, cos_ref, sin_ref, qn_ref, kn_ref, q_ref, k_ref, v_ref):
    m = mod_ref[0]
    h = (x_ref[...] * (1.0 + m[1:2]) + m[0:1]).astype(BF16)
    cos, sin = cos_ref[...], sin_ref[...]

    def norm_rope(p, g, scale):
        ms = jnp.mean(p * p, axis=-1, keepdims=True)
        y = p * lax.rsqrt(ms + QK_NORM_EPS) * g
        return (_rope_lanes(y, cos, sin, C_HEAD_DIM) * scale).astype(BF16)

    p = jnp.dot(h, w_ref[...], preferred_element_type=F32)
    for hd in range(C_HEADS):
        cs = slice(hd * C_HEAD_DIM, (hd + 1) * C_HEAD_DIM)
        q_ref[:, cs] = norm_rope(p[:, cs], qn_ref[...], LOG2_E * C_HEAD_DIM ** -0.5)
    for hd in range(C_KV_HEADS):
        cs = slice(hd * C_HEAD_DIM, (hd + 1) * C_HEAD_DIM)
        k_ref[:, cs] = norm_rope(p[:, C_Q + hd * C_HEAD_DIM:C_Q + (hd + 1) * C_HEAD_DIM], kn_ref[...], 1.0)
    v_ref[...] = p[:, C_Q + C_KV:].astype(BF16)


def _odd_in_proj(lay, x, mod, w_bf16, cos, sin, qn_g, kn_g):
    tm, d = lay.tm, D_MODEL
    row = lambda i: (i, 0)
    full = lambda i: (0, 0)
    return pl.pallas_call(
        _odd_in_body,
        out_shape=(jax.ShapeDtypeStruct((lay.rows, C_Q), BF16),
                   jax.ShapeDtypeStruct((lay.rows, C_KV), BF16),
                   jax.ShapeDtypeStruct((lay.rows, C_KV), BF16)),
        grid=(lay.ntiles,),
        in_specs=[pl.BlockSpec((tm, d), row),
                  pl.BlockSpec((1, 6, d), lambda i: (lay.mod_row(i, tm), 0, 0)),
                  pl.BlockSpec((d, ODD_IN), full),
                  pl.BlockSpec((tm, C_HEAD_DIM), lambda i: (lay.pos_tile(i), 0)),
                  pl.BlockSpec((tm, C_HEAD_DIM), lambda i: (lay.pos_tile(i), 0)),
                  pl.BlockSpec((1, C_HEAD_DIM), full),
                  pl.BlockSpec((1, C_HEAD_DIM), full)],
        out_specs=(pl.BlockSpec((tm, C_Q), row), pl.BlockSpec((tm, C_KV), row), pl.BlockSpec((tm, C_KV), row)),
        compiler_params=_cparams(("arbitrary",)),
        name="odd_in_proj",
    )(x, mod, w_bf16, cos, sin, qn_g.reshape(1, C_HEAD_DIM), kn_g.reshape(1, C_HEAD_DIM))


def _gqa_body(q_ref, kc_ref, kl_ref, vc_ref, vl_ref, o_ref):
    for kvh in range(C_KV_HEADS):
        ks = slice(kvh * C_HEAD_DIM, (kvh + 1) * C_HEAD_DIM)
        keys = [kc_ref[:, ks], kl_ref[:, ks]]
        vals = [_with_ones_column(vc_ref[:, ks]), _with_ones_column(vl_ref[:, ks])]
        for g in range(C_GROUP):
            hd = kvh * C_GROUP + g
            cs = slice(hd * C_HEAD_DIM, (hd + 1) * C_HEAD_DIM)
            o_ref[:, cs] = _softmax_pv(q_ref[:, cs], keys, vals).astype(BF16)


def _gqa_attention(lay, q, k, v):
    tm = lay.tm
    lat0 = lay.rows_c // lay.S
    return pl.pallas_call(
        _gqa_body,
        out_shape=jax.ShapeDtypeStruct((lay.rows_l, C_Q), BF16),
        grid=(lay.B, lay.nlt),
        in_specs=[pl.BlockSpec((tm, C_Q), lambda b, j: (lay.ntiles_c + b * lay.nlt + j, 0)),
                  pl.BlockSpec((lay.CTX, C_KV), lambda b, j: (b, 0)),
                  pl.BlockSpec((lay.S, C_KV), lambda b, j: (lat0 + b, 0)),
                  pl.BlockSpec((lay.CTX, C_KV), lambda b, j: (b, 0)),
                  pl.BlockSpec((lay.S, C_KV), lambda b, j: (lat0 + b, 0))],
        out_specs=pl.BlockSpec((tm, C_Q), lambda b, j: (b * lay.nlt + j, 0)),
        compiler_params=_cparams(("arbitrary", "arbitrary")),
        name="gqa_attention",
    )(q, k, k, v, v)


def _out_proj_body(n_mix, *refs):
    mix_refs = refs[:n_mix]
    w_ref, x_ref, mod_ref, lg_ref, lb_ref, rw_ref, xo_ref, h_ref, lt_ref = refs[n_mix:]
    m = mod_ref[0]
    off = 0
    mix = None
    for mr in mix_refs:
        kw = mr.shape[-1]
        part = jnp.dot(mr[...], w_ref[off:off + kw, :], preferred_element_type=F32)
        mix = part if mix is None else mix + part
        off += kw
    z = DEEPNORM_ALPHA * x_ref[...] + m[2:3] * mix
    xn = _layer_norm_rows(z, lg_ref[...], lb_ref[...])
    xo_ref[...] = xn
    h2 = xn * (1.0 + m[4:5]) + m[3:4]
    h_ref[...] = h2
    lt_ref[...] = _dot3_nt(rw_ref[...], h2)


def _out_proj(lay, mixes, w_bf16, x, row_off, mod, ln_g, ln_b, router_wt, n_rows):
    tm, d = lay.tmo, D_MODEL
    row = lambda i: (i, 0)
    full = lambda i: (0, 0)
    x_tile_off = row_off // tm
    in_specs = [pl.BlockSpec((tm, mx.shape[-1]), row) for mx in mixes]
    in_specs += [pl.BlockSpec((d, d), full),
                 pl.BlockSpec((tm, d), lambda i: (i + x_tile_off, 0)),
                 pl.BlockSpec((1, 6, d), lambda i: (lay.mod_row(i, tm, row_off), 0, 0)),
                 pl.BlockSpec((1, d), full), pl.BlockSpec((1, d), full),
                 pl.BlockSpec((N_EXPERTS, d), full)]
    return pl.pallas_call(
        functools.partial(_out_proj_body, len(mixes)),
        out_shape=(jax.ShapeDtypeStruct((n_rows, d), F32),
                   jax.ShapeDtypeStruct((n_rows, d), F32),
                   jax.ShapeDtypeStruct((N_EXPERTS, n_rows), F32)),
        grid=(n_rows // tm,),
        in_specs=in_specs,
        out_specs=(pl.BlockSpec((tm, d), row), pl.BlockSpec((tm, d), row),
                   pl.BlockSpec((N_EXPERTS, tm), lambda i: (0, i))),
        compiler_params=_cparams(("arbitrary",)),
        name="out_proj_ln",
    )(*mixes, w_bf16, x, mod, ln_g.reshape(1, d), ln_b.reshape(1, d), router_wt)


_EXPERT_PAIRS = ((0, 1), (0, 2), (0, 3), (1, 2), (1, 3), (2, 3))
N_BUCKETS = N_GROUPS * len(_EXPERT_PAIRS)


def _router_body(lt_ref, rb_ref, o_ref):
    logits = lt_ref[...] + rb_ref[...]
    rows = [logits[e:e + 1, :] for e in range(N_EXPERTS)]
    m = rows[0]
    for x in rows[1:]:
        m = jnp.maximum(m, x)
    ex = [jnp.exp(x - m) for x in rows]
    z = ex[0]
    for x in ex[1:]:
        z = z + x
    p = [x / z for x in ex]

    gscore = []
    for g in range(N_GROUPS):
        a, b, c, d = p[4 * g:4 * g + 4]
        hi1, lo1 = jnp.maximum(a, b), jnp.minimum(a, b)
        hi2, lo2 = jnp.maximum(c, d), jnp.minimum(c, d)
        top1 = jnp.maximum(hi1, hi2)
        top2 = jnp.maximum(jnp.minimum(hi1, hi2), jnp.maximum(lo1, lo2))
        gscore.append(top1 + top2)
    best = []
    for g in range(N_GROUPS):
        ok = None
        for o in range(N_GROUPS):
            if o == g:
                continue
            c = (gscore[g] > gscore[o]) if o < g else (gscore[g] >= gscore[o])
            ok = c if ok is None else jnp.logical_and(ok, c)
        best.append(ok)
    won = []
    for e in range(N_EXPERTS):
        g = e // EXPERTS_PER_GROUP
        rank = jnp.zeros_like(p[e])
        for o in range(4 * g, 4 * g + 4):
            if o == e:
                continue
            ahead = (p[o] > p[e]) if o > e else (p[o] >= p[e])
            rank = rank + jnp.where(ahead, 1.0, 0.0)
        won.append(jnp.where(jnp.logical_and(best[g], rank < 1.5), 1.0, 0.0))
    tot = won[0] * p[0]
    for e in range(1, N_EXPERTS):
        tot = tot + won[e] * p[e]
    bucket = jnp.zeros_like(tot)
    gate_a = jnp.zeros_like(tot)
    gate_b = jnp.zeros_like(tot)
    for g in range(N_GROUPS):
        for pid, (a, b) in enumerate(_EXPERT_PAIRS):
            ind = won[4 * g + a] * won[4 * g + b]
            bucket = bucket + ind * float(len(_EXPERT_PAIRS) * g + pid)
            gate_a = gate_a + ind * p[4 * g + a]
            gate_b = gate_b + ind * p[4 * g + b]
    o_ref[...] = jnp.zeros_like(o_ref)
    o_ref[0:1, :] = bucket
    o_ref[1:2, :] = gate_a / tot
    o_ref[2:3, :] = gate_b / tot


def _router(logits_t, router_b):
    e, n = logits_t.shape
    tr = math.gcd(2048, n)
    return pl.pallas_call(
        _router_body,
        out_shape=jax.ShapeDtypeStruct((SUBLANES, n), F32),
        grid=(n // tr,),
        in_specs=[pl.BlockSpec((e, tr), lambda i: (0, i)), pl.BlockSpec((e, 1), lambda i: (0, 0))],
        out_specs=pl.BlockSpec((SUBLANES, tr), lambda i: (0, i)),
        compiler_params=_cparams(("arbitrary",)),
        name="router_gates",
    )(logits_t, router_b.reshape(e, 1))


GATE_LANES = LANES


def _moe_plan(bucket, tg):
    n = bucket.shape[0]
    n_tiles = n // tg + N_BUCKETS
    ids = jnp.arange(N_BUCKETS, dtype=jnp.int32)
    onehot = (bucket[:, None] == ids[None, :]).astype(jnp.int32)
    csum = jnp.cumsum(onehot, axis=0)
    counts = csum[-1]
    tiles = (counts + tg - 1) // tg
    tile_end = jnp.cumsum(tiles)
    row_start = (tile_end - tiles) * tg
    pos = jnp.sum(onehot * (row_start[None, :] + csum - 1), axis=1).astype(jnp.int32)
    n_used = tile_end[-1]
    tile_ids = jnp.minimum(jnp.arange(n_tiles, dtype=jnp.int32), n_used - 1)
    tile_bucket = jnp.sum((tile_ids[:, None] >= tile_end[None, :]).astype(jnp.int32), axis=1)
    pair_a = jnp.asarray([a for a, _ in _EXPERT_PAIRS], jnp.int32)
    pair_b = jnp.asarray([b for _, b in _EXPERT_PAIRS], jnp.int32)
    grp, pid = tile_bucket // len(_EXPERT_PAIRS), tile_bucket % len(_EXPERT_PAIRS)
    ea = grp * EXPERTS_PER_GROUP + pair_a[pid]
    eb = grp * EXPERTS_PER_GROUP + pair_b[pid]
    return pos, ea.astype(jnp.int32), eb.astype(jnp.int32), n_used.reshape(1).astype(jnp.int32), n_tiles


def _row_copy(src_ref, src_row, dst_ref, dst_row, sem):
    return pltpu.make_async_copy(src_ref.at[pl.ds(src_row, 1), :], dst_ref.at[pl.ds(dst_row, 1), :], sem)


ROW_DMA_UNROLL = 8


def _row_copies(n_rows, row_copy, whole_tile_copy):
    def issue(blk, carry):
        for u in range(ROW_DMA_UNROLL):
            row_copy(blk * ROW_DMA_UNROLL + u).start(priority=u % 2)
        return carry

    lax.fori_loop(0, n_rows // ROW_DMA_UNROLL, issue, 0)
    whole_tile_copy.wait()


def _dispatch_body(tg, pos_ref, h_ref, g_ref, xs_in_ref, xs_ref, aug_ref, sem):
    del xs_in_ref
    base = pl.program_id(0) * tg
    aug_ref[:, :D_MODEL] = h_ref[...]
    aug_ref[:, D_MODEL:] = g_ref[...]

    _row_copies(tg, lambda r: _row_copy(aug_ref, r, xs_ref, pos_ref[base + r], sem),
                pltpu.make_async_copy(aug_ref, xs_ref.at[pl.ds(0, tg), :], sem))


def _moe_dispatch(pos, h2, gate_rows, n_tiles, tg):
    n, d = h2.shape
    wide = d + GATE_LANES
    xs0 = jnp.zeros((n_tiles * tg, wide), F32)
    grid_spec = pltpu.PrefetchScalarGridSpec(
        num_scalar_prefetch=1,
        grid=(n // tg,),
        in_specs=[pl.BlockSpec((tg, d), lambda i, pos_ref: (i, 0)),
                  pl.BlockSpec((tg, GATE_LANES), lambda i, pos_ref: (i, 0)),
                  pl.BlockSpec(memory_space=pl.ANY)],
        out_specs=pl.BlockSpec(memory_space=pl.ANY),
        scratch_shapes=[pltpu.VMEM((tg, wide), F32), pltpu.SemaphoreType.DMA(())],
    )
    return pl.pallas_call(
        functools.partial(_dispatch_body, tg),
        out_shape=jax.ShapeDtypeStruct(xs0.shape, F32),
        grid_spec=grid_spec,
        input_output_aliases={3: 0},
        compiler_params=_cparams(("arbitrary",)),
        name="moe_dispatch",
    )(pos, h2, gate_rows, xs0)


def _grouped_body(ea_ref, eb_ref, nu_ref, xs_ref, w1a_ref, w3a_ref, w2a_ref, w1b_ref, w3b_ref, w2b_ref, y_ref):
    del ea_ref, eb_ref

    @pl.when(pl.program_id(0) < nu_ref[0])
    def _():
        x = xs_ref[:, :D_MODEL].astype(BF16)

        def expert(w1_ref, w3_ref, w2_ref, gate):
            h1 = jnp.dot(x, w1_ref[0], preferred_element_type=F32)
            h3 = jnp.dot(x, w3_ref[0], preferred_element_type=F32)
            hid = (h1 * _sigmoid(h1) * h3 * gate).astype(BF16)
            return jnp.dot(hid, w2_ref[0], preferred_element_type=F32)

        y_ref[...] = (expert(w1a_ref, w3a_ref, w2a_ref, xs_ref[:, D_MODEL:D_MODEL + 1])
                      + expert(w1b_ref, w3b_ref, w2b_ref, xs_ref[:, D_MODEL + 1:D_MODEL + 2]))

    @pl.when(pl.program_id(0) >= nu_ref[0])
    def _():
        y_ref[...] = jnp.zeros_like(y_ref)


def _moe_grouped(xs, ea, eb, n_used, w1, w3, w2, tg):
    rows, wide = xs.shape
    d, ff = D_MODEL, EXPERT_FF
    n_tiles = rows // tg
    tile = lambda i, ea_r, eb_r, nu_r: (jnp.minimum(i, nu_r[0] - 1), 0)
    up_a = pl.BlockSpec((1, d, ff), lambda i, ea_r, eb_r, nu_r: (ea_r[i], 0, 0))
    up_b = pl.BlockSpec((1, d, ff), lambda i, ea_r, eb_r, nu_r: (eb_r[i], 0, 0))
    dn_a = pl.BlockSpec((1, ff, d), lambda i, ea_r, eb_r, nu_r: (ea_r[i], 0, 0))
    dn_b = pl.BlockSpec((1, ff, d), lambda i, ea_r, eb_r, nu_r: (eb_r[i], 0, 0))
    grid_spec = pltpu.PrefetchScalarGridSpec(
        num_scalar_prefetch=3,
        grid=(n_tiles,),
        in_specs=[pl.BlockSpec((tg, wide), tile), up_a, up_a, dn_a, up_b, up_b, dn_b],
        out_specs=pl.BlockSpec((tg, d), lambda i, ea_r, eb_r, nu_r: (i, 0)),
    )
    return pl.pallas_call(
        _grouped_body,
        out_shape=jax.ShapeDtypeStruct((rows, d), F32),
        grid_spec=grid_spec,
        compiler_params=_cparams(("arbitrary",)),
        name="moe_grouped",
    )(ea, eb, n_used, xs, w1, w3, w2, w1, w3, w2)


def _combine_body(tg, pos_ref, ys_ref, x_ref, mod_ref, lg_ref, lb_ref, o_ref, buf_ref, sem):
    base = pl.program_id(0) * tg

    _row_copies(tg, lambda r: _row_copy(ys_ref, pos_ref[base + r], buf_ref, r, sem),
                pltpu.make_async_copy(ys_ref.at[pl.ds(0, tg), :], buf_ref, sem))
    m = mod_ref[0]
    z = DEEPNORM_ALPHA * x_ref[...] + m[5:6] * buf_ref[...]
    o_ref[...] = _layer_norm_rows(z, lg_ref[...], lb_ref[...])


def _moe_combine(lay, pos, ys, x, mod, ln_g, ln_b, row_off, tg):
    n, d = x.shape
    grid_spec = pltpu.PrefetchScalarGridSpec(
        num_scalar_prefetch=1,
        grid=(n // tg,),
        in_specs=[pl.BlockSpec(memory_space=pl.ANY),
                  pl.BlockSpec((tg, d), lambda i, pos_ref: (i, 0)),
                  pl.BlockSpec((1, 6, d), lambda i, pos_ref: (lay.mod_row(i, tg, row_off), 0, 0)),
                  pl.BlockSpec((1, d), lambda i, pos_ref: (0, 0)),
                  pl.BlockSpec((1, d), lambda i, pos_ref: (0, 0))],
        out_specs=pl.BlockSpec((tg, d), lambda i, pos_ref: (i, 0)),
        scratch_shapes=[pltpu.VMEM((tg, d), F32), pltpu.SemaphoreType.DMA(())],
    )
    return pl.pallas_call(
        functools.partial(_combine_body, tg),
        out_shape=jax.ShapeDtypeStruct((n, d), F32),
        grid_spec=grid_spec,
        compiler_params=_cparams(("arbitrary",)),
        name="moe_combine_ln",
    )(pos, ys, x, mod, ln_g.reshape(1, d), ln_b.reshape(1, d))


def _moe(lay, h2, routed, w1, w3, w2, x, mod, ln_g, ln_b, row_off):
    tg = lay.tm
    bucket = routed[0].astype(jnp.int32)
    gate_rows = jnp.pad(routed[1:3].T, ((0, 0), (0, GATE_LANES - 2)))
    pos, ea, eb, n_used, n_tiles = _moe_plan(bucket, tg)
    xs = _moe_dispatch(pos, h2, gate_rows, n_tiles, tg)
    ys = _moe_grouped(xs, ea, eb, n_used, w1, w3, w2, tg)
    return _moe_combine(lay, pos, ys, x, mod, ln_g, ln_b, row_off, tg)


def kernel(x, c, ctx, c_ctx, router_w, router_b, ada_w, ada_b, ln1_g, ln1_b, ln2_g, ln2_b, moe_w1, moe_w3, moe_w2, ev_w_in, ev_w_out, ev_a_mu, ev_a_w0, ev_a_w2, ev_a_a0, ev_a_a2, ev_a_g2, ev_a_kk, ev_a_ka, ev_a_rk, ev_a_lnx_g, ev_a_lnx_b, ev_b_lam, ev_b_subln_g, od_w_in, od_w_out, od_qn_g, od_kn_g):
    bsz, seq, d = x.shape
    ctx_len = ctx.shape[1]
    assert d == D_MODEL and ada_w.shape[0] == DEPTH and seq % GRID_W == 0
    lay = _Layout(bsz, ctx_len, seq)

    cvec = jnp.zeros((lay.mod_rows, d), F32).at[:bsz].set(c).at[bsz].set(c_ctx)
    mods = _ada_mods(cvec, ada_w, ada_b).reshape(DEPTH, lay.mod_rows, 6, d)

    xs = jnp.concatenate([ctx.reshape(lay.rows_c, d), x.reshape(lay.rows_l, d)], axis=0)
    router_wt = router_w.T

    cos_b, sin_b = _rope_tables(lay, B_HEAD_DIM, B_QK)
    cos_c, sin_c = _rope_tables(lay, C_HEAD_DIM, C_HEAD_DIM)

    for i in range(DEPTH):
        last = i == DEPTH - 1
        j = i // 2
        mod = mods[i]
        if i % 2 == 0:
            lambda_init = 0.8 - 0.6 * math.exp(-0.3 * i)
            pa, q, k, v = _even_in_proj(lay, xs, mod, ev_w_in[j].astype(BF16), cos_b, sin_b)
            decay, scan_ops, g_, bonus = _rwkv_features(
                lay, pa, ev_a_mu[j], ev_a_w0[j], ev_a_w2[j], ev_a_a0[j], ev_a_a2[j], ev_a_g2[j],
                ev_a_kk[j], ev_a_ka[j], ev_a_rk[j].reshape(-1))
            y_f, y_b = _wkv_scan(lay, _to_chains(lay, decay[None]), _to_chains(lay, scan_ops))
            nl = lay.B * A_HEADS
            yf = _from_chains(lay, y_f[:, :, :nl])
            yb = _from_chains(lay, y_b[:, :, nl:])
            ya = _rwkv_readout(lay, yf, yb, bonus, g_, ev_a_lnx_g[j], ev_a_lnx_b[j])
            yd = _diff_attention(lay, q, k, v, ev_b_lam[j], ev_b_subln_g[j], lambda_init)
            mixes, w_out = [ya, yd], ev_w_out[j]
            n_rows, tile_off = (lay.rows_l, lay.ntiles_c) if last else (lay.rows, 0)
            if last:
                mixes = [mx[lay.rows_c:] for mx in mixes]
        else:
            q, k, v = _odd_in_proj(lay, xs, mod, od_w_in[j].astype(BF16), cos_c, sin_c, od_qn_g[j], od_kn_g[j])
            assert last, "an odd layer that must also update the context stream is not supported"
            o = _gqa_attention(lay, q, k, v)
            mixes, w_out = [o], od_w_out[j]
            n_rows, tile_off = lay.rows_l, lay.ntiles_c
        row_off = tile_off * lay.tm
        x_new, h2, logits_t = _out_proj(lay, mixes, w_out.astype(BF16), xs, row_off, mod,
                                        ln1_g[i], ln1_b[i], router_wt, n_rows)
        routed = _router(logits_t, router_b)
        xs = _moe(lay, h2, routed, moe_w1[i].astype(BF16), moe_w3[i].astype(BF16), moe_w2[i].astype(BF16),
                  x_new, mod, ln2_g[i], ln2_b[i], row_off)
    return xs.reshape(bsz, seq, d)
```

```python
import functools
import math

import jax
import jax.numpy as jnp
from jax import lax
from jax.experimental import pallas as pl
from jax.experimental.pallas import tpu as pltpu

F32 = jnp.float32
BF16 = jnp.bfloat16

D_MODEL = 1024
DEPTH = 2
GRID_W = 64
ROPE_THETA = 10000.0
LN_EPS = 1e-5
LOG2_E = 1.4426950408889634
DEEPNORM_ALPHA = (2 * DEPTH) ** 0.25

A_HEAD_DIM = 64
A_HEADS = 8
A_WIDTH = 512
A_LORA = 64
A_GATE_LORA = 128
A_GN_EPS = 64e-5
A_IN = 3 * A_WIDTH + 4 * A_LORA + A_GATE_LORA
_SCAN_R, _SCAN_K, _SCAN_V, _SCAN_A, _SCAN_B = range(5)
N_SCAN_OPS = 5

B_HEAD_DIM = 64
B_V_DIM = 128
B_HEADS = 4
B_WIDTH = 512
B_QK = 512
B_QK_PAD = (B_QK // B_HEAD_DIM) * 128
B_SUBLN_EPS = 1e-5
EVEN_IN = A_IN + 2 * B_QK + B_WIDTH

C_HEAD_DIM = 128
C_HEADS = 8
C_KV_HEADS = 2
C_GROUP = 4
C_Q = 1024
C_KV = 256
ODD_IN = C_Q + 2 * C_KV
QK_NORM_EPS = 1e-6

N_EXPERTS = 16
N_GROUPS = 4
EXPERTS_PER_GROUP = 4
EXPERT_FF = 512

VMEM_LIMIT_BYTES = 56 * 1024 * 1024
LANES = 128
SUBLANES = 8


def _cparams(sem):
    return pltpu.CompilerParams(dimension_semantics=sem, vmem_limit_bytes=VMEM_LIMIT_BYTES)


def _dot(a, b):
    return jnp.dot(a.astype(BF16), b.astype(BF16), preferred_element_type=F32)


def _dot_nt(a, b):
    return lax.dot_general(a.astype(BF16), b.astype(BF16), (((1,), (1,)), ((), ())),
                           preferred_element_type=F32)


def _split(a):
    hi = a.astype(BF16)
    lo = (a - hi.astype(F32)).astype(BF16)
    return hi, lo


def _dot3(a, b):
    ah, al = _split(a)
    bh, bl = _split(b)
    return (jnp.dot(ah, bh, preferred_element_type=F32)
            + (jnp.dot(ah, bl, preferred_element_type=F32)
               + jnp.dot(al, bh, preferred_element_type=F32)))


def _dot2_exact_rhs(a, b_bf16):
    ah, al = _split(a)
    return jnp.dot(ah, b_bf16, preferred_element_type=F32) + jnp.dot(al, b_bf16, preferred_element_type=F32)


def _dot3_nt(a, b):
    ah, al = _split(a)
    bh, bl = _split(b)
    dn = (((1,), (1,)), ((), ()))
    return (lax.dot_general(ah, bh, dn, preferred_element_type=F32)
            + (lax.dot_general(ah, bl, dn, preferred_element_type=F32)
               + lax.dot_general(al, bh, dn, preferred_element_type=F32)))


def _sigmoid(x):
    return 1.0 / (1.0 + jnp.exp(-x))


def _layer_norm_rows(z, g, b):
    mu = jnp.mean(z, axis=-1, keepdims=True)
    zc = z - mu
    var = jnp.mean(zc * zc, axis=-1, keepdims=True)
    return zc * lax.rsqrt(var + LN_EPS) * g + b


class _Layout:
    def __init__(self, bsz, ctx_len, seq):
        self.B, self.CTX, self.S = bsz, ctx_len, seq
        self.T = ctx_len + seq
        self.tm = math.gcd(256, math.gcd(ctx_len, seq))
        self.nct = ctx_len // self.tm
        self.nlt = seq // self.tm
        self.rows_c = bsz * ctx_len
        self.rows_l = bsz * seq
        self.rows = self.rows_c + self.rows_l
        self.ntiles_c = bsz * self.nct
        self.ntiles = self.rows // self.tm
        assert self.rows_c % seq == 0, "latent K/V blocks are addressed in units of S rows"
        self.tb = math.gcd(32, math.gcd(ctx_len, seq))
        self.tmo = math.gcd(512, math.gcd(self.rows_c, seq))
        self.mod_rows = -(-(bsz + 1) // SUBLANES) * SUBLANES

    def seq_tile(self, b, j):
        return jnp.where(j < self.nct, b * self.nct + j, self.ntiles_c + b * self.nlt + (j - self.nct))

    def seq_block(self, i):
        il = i - self.ntiles_c
        return (jnp.where(i < self.ntiles_c, i // self.nct, il // self.nlt),
                jnp.where(i < self.ntiles_c, i % self.nct, self.nct + il % self.nlt))

    def mod_row(self, i, tile, row_offset=0):
        r = i * tile + row_offset
        return jnp.where(r < self.rows_c, self.B, (r - self.rows_c) // self.S)

    def pos_tile(self, i):
        il = i - self.ntiles_c
        return jnp.where(i < self.ntiles_c, i % self.nct, self.nct + il % self.nlt)


def _ada_body(cv_ref, w_ref, b_ref, o_ref):
    cv = cv_ref[...]
    s = cv * _sigmoid(cv)
    o_ref[0] = _dot3(s, w_ref[0]) + b_ref[0]


def _ada_mods(cvec, ada_w, ada_b):
    depth, d, n = ada_w.shape
    r = cvec.shape[0]
    tn = 512
    return pl.pallas_call(
        _ada_body,
        out_shape=jax.ShapeDtypeStruct((depth, r, n), F32),
        grid=(depth, n // tn),
        in_specs=[pl.BlockSpec((r, d), lambda l, j: (0, 0)),
                  pl.BlockSpec((1, d, tn), lambda l, j: (l, 0, j)),
                  pl.BlockSpec((1, 1, tn), lambda l, j: (l, 0, j))],
        out_specs=pl.BlockSpec((1, r, tn), lambda l, j: (l, 0, j)),
        compiler_params=_cparams(("arbitrary", "arbitrary")),
        name="ada_mods",
    )(cvec, ada_w, ada_b.reshape(depth, 1, n))


def _rope_tables(lay, head_dim, width):
    rows = lay.S // GRID_W
    rr, cc = jnp.meshgrid(jnp.arange(rows), jnp.arange(GRID_W), indexing="ij")
    row_pos = rr.reshape(-1).astype(F32)
    col_pos = cc.reshape(-1).astype(F32)
    axis_dim = head_dim // 2
    inv = ROPE_THETA ** (-jnp.arange(0, axis_dim, 2, dtype=F32) / axis_dim)
    ang = jnp.concatenate([row_pos[:, None] * inv, col_pos[:, None] * inv], -1)
    cos, sin = jnp.cos(ang), jnp.sin(ang)
    cos = jnp.concatenate([jnp.ones((lay.CTX, head_dim // 2), F32), cos], 0)
    sin = jnp.concatenate([jnp.zeros((lay.CTX, head_dim // 2), F32), sin], 0)
    cos_h = jnp.concatenate([cos, cos], -1)
    sin_h = jnp.concatenate([-sin, sin], -1)
    reps = width // head_dim
    return jnp.tile(cos_h, (1, reps)), jnp.tile(sin_h, (1, reps))


def _rope_lanes(x, cos, sin, head_dim):
    w = x.shape[-1]
    half = head_dim // 2
    if head_dim == LANES and w == LANES:
        rot = pltpu.roll(x, half, 1)
    else:
        fwd = pltpu.roll(x, w - half, 1)
        bwd = pltpu.roll(x, half, 1)
        lane = lax.broadcasted_iota(jnp.int32, x.shape, 1)
        rot = jnp.where((lane % head_dim) < half, fwd, bwd)
    return x * cos + rot * sin


def _even_in_body(x_ref, mod_ref, w_ref, cos_ref, sin_ref, pa_ref, q_ref, k_ref, v_ref):
    m = mod_ref[0]
    h = (x_ref[...] * (1.0 + m[1:2]) + m[0:1]).astype(BF16)
    pa_ref[...] = jnp.dot(h, w_ref[:, :A_IN], preferred_element_type=F32)
    cos, sin = cos_ref[...], sin_ref[...]
    o = A_IN

    def put_maps(dst_ref, val):
        pad = jnp.zeros((val.shape[0], LANES - B_HEAD_DIM), BF16)
        for mp in range(B_QK // B_HEAD_DIM):
            piece = val[:, mp * B_HEAD_DIM:(mp + 1) * B_HEAD_DIM].astype(BF16)
            dst_ref[:, mp * LANES:(mp + 1) * LANES] = jnp.concatenate([piece, pad], axis=1)

    q = jnp.dot(h, w_ref[:, o:o + B_QK], preferred_element_type=F32)
    put_maps(q_ref, _rope_lanes(q, cos, sin, B_HEAD_DIM) * (LOG2_E * B_HEAD_DIM ** -0.5))
    o += B_QK
    k = jnp.dot(h, w_ref[:, o:o + B_QK], preferred_element_type=F32)
    put_maps(k_ref, _rope_lanes(k, cos, sin, B_HEAD_DIM))
    o += B_QK
    v_ref[...] = jnp.dot(h, w_ref[:, o:o + B_WIDTH], preferred_element_type=F32).astype(BF16)


def _even_in_proj(lay, x, mod, w_bf16, cos, sin):
    tm, d = lay.tm, D_MODEL
    row = lambda i: (i, 0)
    return pl.pallas_call(
        _even_in_body,
        out_shape=(jax.ShapeDtypeStruct((lay.rows, A_IN), F32),
                   jax.ShapeDtypeStruct((lay.rows, B_QK_PAD), BF16),
                   jax.ShapeDtypeStruct((lay.rows, B_QK_PAD), BF16),
                   jax.ShapeDtypeStruct((lay.rows, B_WIDTH), BF16)),
        grid=(lay.ntiles,),
        in_specs=[pl.BlockSpec((tm, d), row),
                  pl.BlockSpec((1, 6, d), lambda i: (lay.mod_row(i, tm), 0, 0)),
                  pl.BlockSpec((d, EVEN_IN), lambda i: (0, 0)),
                  pl.BlockSpec((tm, B_QK), lambda i: (lay.pos_tile(i), 0)),
                  pl.BlockSpec((tm, B_QK), lambda i: (lay.pos_tile(i), 0))],
        out_specs=(pl.BlockSpec((tm, A_IN), row), pl.BlockSpec((tm, B_QK_PAD), row),
                   pl.BlockSpec((tm, B_QK_PAD), row), pl.BlockSpec((tm, B_WIDTH), row)),
        compiler_params=_cparams(("arbitrary",)),
        name="even_in_proj",
    )(x, mod, w_bf16, cos, sin)


def _rwkv_feat_body(lay, pa_ref, prev_ref, next_ref, mu_ref, w0_ref, w2_ref, a0_ref, a2_ref, g2_ref,
                    kk_ref, ka_ref, rk_ref, bd_ref, tri_ref, blk_ref,
                    pend_ref, ops_ref, g_ref, bonus_ref):
    def put(idx, d, val):
        ops_ref[idx, d, 0] = val.astype(BF16)

    i = pl.program_id(0)
    tm = lay.tm
    il = i - lay.ntiles_c
    in_ctx = i < lay.ntiles_c
    seg_first = jnp.where(in_ctx, i % lay.nct == 0, il % lay.nlt == 0)
    seg_last = jnp.where(in_ctx, i % lay.nct == lay.nct - 1, il % lay.nlt == lay.nlt - 1)

    pa = pa_ref[...]
    row = lax.broadcasted_iota(jnp.int32, pa.shape, 0)
    prev_edge = jnp.where(seg_first, 0.0, 1.0) * prev_ref[SUBLANES - 1:SUBLANES, :]
    next_edge = jnp.where(seg_last, 0.0, 1.0) * next_ref[0:1, :]
    prev = jnp.where(row == 0, prev_edge, pltpu.roll(pa, 1, 0))
    nxt = jnp.where(row == tm - 1, next_edge, pltpu.roll(pa, tm - 1, 0))
    u = pa + (0.5 * (prev + nxt) - pa) * mu_ref[...]

    o1, o2, o3 = A_WIDTH, 2 * A_WIDTH, 3 * A_WIDTH
    o4 = o3 + 2 * A_LORA
    o5 = o4 + 2 * A_LORA
    r, k, v = u[:, :o1], u[:, o1:o2], u[:, o2:o3]
    bd = bd_ref[...]

    kk = k * kk_ref[...]
    ss = _dot2_exact_rhs(kk * kk, bd)
    kkn = kk / jnp.maximum(jnp.sqrt(ss), 1e-12)
    g = _dot3(_sigmoid(u[:, o5:]), g2_ref[...])

    g_ref[0] = g

    kd_sum = jnp.zeros_like(k)
    for d in range(2):
        wd = u[:, o3 + d * A_LORA:o3 + (d + 1) * A_LORA]
        ad = u[:, o4 + d * A_LORA:o4 + (d + 1) * A_LORA]
        log_decay = -math.exp(-0.5) * _sigmoid(w0_ref[d:d + 1, :] + _dot3(jnp.tanh(wd), w2_ref[d]))
        ld_hi, ld_lo = _split(log_decay)
        cum = (jnp.dot(tri_ref[d], ld_hi, preferred_element_type=F32)
               + jnp.dot(tri_ref[d], ld_lo, preferred_element_type=F32))
        tot = (jnp.dot(blk_ref[...], ld_hi, preferred_element_type=F32)
               + jnp.dot(blk_ref[...], ld_lo, preferred_element_type=F32))
        pend_ref[d, 0] = jnp.exp(tot)
        p = jnp.exp(cum)
        p_inv = 1.0 / p
        a = _sigmoid(a0_ref[d:d + 1, :] + _dot3(ad, a2_ref[d]))
        kd = k * (1.0 + (a - 1.0) * ka_ref[...])
        put(_SCAN_R, d, r * p)
        put(_SCAN_V, d, v)
        put(_SCAN_A, d, -kkn * jnp.exp(cum - log_decay))
        put(_SCAN_K, d, kd * p_inv)
        put(_SCAN_B, d, kkn * a * p_inv)
        kd_sum = kd_sum + kd
    bonus_ref[0] = _dot2_exact_rhs(r * kd_sum * rk_ref[...], bd) * v


def _head_block_diag(width, head_dim):
    h = jnp.arange(width) // head_dim
    return (h[:, None] == h[None, :]).astype(BF16)


def _rwkv_features(lay, pa, mu, w0, w2, a0, a2, g2, k_k, k_a, r_k):
    tm = lay.tm
    hb = tm // SUBLANES
    nb8 = lay.rows // SUBLANES
    row = lambda i: (i, 0)
    full2 = lambda i: (0, 0)
    full3 = lambda i: (0, 0, 0)
    w = A_WIDTH
    out = jax.ShapeDtypeStruct((lay.B, lay.T, w), F32)
    seq = lambda i: lay.seq_block(i) + (0,)
    tb = lay.tb
    nblk = tm // tb
    pos = jnp.arange(tm)
    same_blk = (pos[:, None] // tb) == (pos[None, :] // tb)
    tri = jnp.stack([same_blk & (pos[None, :] <= pos[:, None]),
                     same_blk & (pos[None, :] >= pos[:, None])]).astype(BF16)
    blk = (jnp.arange(nblk)[:, None] == (pos[None, :] // tb)).astype(BF16)
    return pl.pallas_call(
        functools.partial(_rwkv_feat_body, lay),
        out_shape=(jax.ShapeDtypeStruct((2, lay.B, lay.T // tb, w), F32),
                   jax.ShapeDtypeStruct((N_SCAN_OPS, 2, lay.B, lay.T, w), BF16), out, out),
        grid=(lay.ntiles,),
        in_specs=[pl.BlockSpec((tm, A_IN), row),
                  pl.BlockSpec((SUBLANES, A_IN), lambda i: (jnp.maximum(i * hb - 1, 0), 0)),
                  pl.BlockSpec((SUBLANES, A_IN), lambda i: (jnp.minimum((i + 1) * hb, nb8 - 1), 0)),
                  pl.BlockSpec((1, A_IN), full2),
                  pl.BlockSpec((2, w), full2),
                  pl.BlockSpec((2, A_LORA, w), full3),
                  pl.BlockSpec((2, w), full2),
                  pl.BlockSpec((2, A_LORA, w), full3),
                  pl.BlockSpec((A_GATE_LORA, w), full2),
                  pl.BlockSpec((1, w), full2),
                  pl.BlockSpec((1, w), full2),
                  pl.BlockSpec((1, w), full2),
                  pl.BlockSpec((w, w), full2),
                  pl.BlockSpec((2, tm, tm), full3),
                  pl.BlockSpec((nblk, tm), full2)],
        out_specs=(pl.BlockSpec((2, 1, nblk, w), lambda i: (0,) + lay.seq_block(i) + (0,)),
                   pl.BlockSpec((N_SCAN_OPS, 2, 1, tm, w), lambda i: (0, 0) + lay.seq_block(i) + (0,)),
                   pl.BlockSpec((1, tm, w), seq), pl.BlockSpec((1, tm, w), seq)),
        compiler_params=_cparams(("arbitrary",)),
        name="rwkv_features",
    )(pa, pa, pa, mu.reshape(1, A_IN), w0, w2, a0, a2, g2, k_k.reshape(1, w), k_a.reshape(1, w),
      r_k.reshape(1, w), _head_block_diag(w, A_HEAD_DIM), tri, blk)


def _wkv_scan_body(tb, pf_ref, pb_ref, fwd_ref, bwd_ref, yf_ref, yb_ref, s_ref, m_ref):
    n = A_HEAD_DIM
    chains = s_ref.shape[-1]
    is_fwd = lax.broadcasted_iota(jnp.int32, (n, chains), 1) < chains // 2

    @pl.when(pl.program_id(0) == 0)
    def _():
        s_ref[...] = jnp.zeros_like(s_ref)

    def step(t, carry):
        tr = tb - 1 - t
        for idx in range(N_SCAN_OPS):
            m_ref[idx] = jnp.where(is_fwd, fwd_ref[t, idx], bwd_ref[tr, idx]).astype(F32)
        sa = jnp.zeros((n, chains), F32)
        for kk in range(n):
            sa = sa + s_ref[kk] * m_ref[_SCAN_A, kk:kk + 1, :]
        v_t = m_ref[_SCAN_V]
        y = jnp.zeros((n, chains), F32)
        for kk in range(n):
            s_new = s_ref[kk] + sa * m_ref[_SCAN_B, kk:kk + 1, :] + v_t * m_ref[_SCAN_K, kk:kk + 1, :]
            s_ref[kk] = s_new
            y = y + s_new * m_ref[_SCAN_R, kk:kk + 1, :]
        yf_ref[t] = y
        yb_ref[tr] = y
        return carry

    lax.fori_loop(0, tb, step, 0)
    m_ref[0] = jnp.where(is_fwd, pf_ref[0, 0], pb_ref[0, 0])
    for kk in range(n):
        s_ref[kk] = s_ref[kk] * m_ref[0, kk:kk + 1, :]


def _wkv_scan(lay, block_decay, ops):
    t, nops, n, lanes = ops.shape
    tb = lay.tb
    nctb, nt = lay.CTX // tb, t // tb
    bwd_tile = lambda g: jnp.where(g < nctb, nctb - 1 - g, nt - 1 - (g - nctb))
    out = jax.ShapeDtypeStruct((t, n, lanes), F32)
    fwd_map = lambda g: (g, 0, 0, 0)
    bwd_map = lambda g: (bwd_tile(g), 0, 0, 0)
    return pl.pallas_call(
        functools.partial(_wkv_scan_body, tb),
        out_shape=(out, out),
        grid=(nt,),
        in_specs=[pl.BlockSpec((1, 1, n, lanes), fwd_map), pl.BlockSpec((1, 1, n, lanes), bwd_map),
                  pl.BlockSpec((tb, nops, n, lanes), fwd_map), pl.BlockSpec((tb, nops, n, lanes), bwd_map)],
        out_specs=(pl.BlockSpec((tb, n, lanes), lambda g: (g, 0, 0)),
                   pl.BlockSpec((tb, n, lanes), lambda g: (bwd_tile(g), 0, 0))),
        scratch_shapes=[pltpu.VMEM((n, n, lanes), F32), pltpu.VMEM((nops, n, lanes), F32)],
        compiler_params=_cparams(("arbitrary",)),
        name="wkv7_scan",
    )(block_decay, block_decay, ops, ops)


def _to_chains(lay, ops):
    h, n = A_HEADS, A_HEAD_DIM
    nops, length = ops.shape[0], ops.shape[3]
    x = ops.reshape(nops, 2, lay.B, length, h, n).transpose(3, 0, 5, 1, 2, 4)
    return x.reshape(length, nops, n, 2 * lay.B * h)


def _from_chains(lay, y):
    h, n = A_HEADS, A_HEAD_DIM
    return y.reshape(lay.T, n, lay.B, h).transpose(2, 0, 3, 1).reshape(lay.B, lay.T, h * n)


def _rwkv_readout_body(yf_ref, yb_ref, bonus_ref, g_ref, lg_ref, lb_ref, bd_ref, o_ref):
    y = yf_ref[0] + yb_ref[0]
    bd = bd_ref[...]
    inv_n = 1.0 / A_HEAD_DIM
    mu = _dot2_exact_rhs(y, bd) * inv_n
    yc = y - mu
    var = _dot2_exact_rhs(yc * yc, bd) * inv_n
    yn = yc * lax.rsqrt(var + A_GN_EPS) * lg_ref[...] + lb_ref[...]
    o_ref[...] = ((yn + bonus_ref[0]) * g_ref[0]).astype(BF16)


def _rwkv_readout(lay, yf, yb, bonus, g, lnx_g, lnx_b):
    tm, w = lay.tm, A_WIDTH
    row = lambda i: (i, 0)
    full = lambda i: (0, 0)
    seq = lambda i: lay.seq_block(i) + (0,)
    return pl.pallas_call(
        _rwkv_readout_body,
        out_shape=jax.ShapeDtypeStruct((lay.rows, w), BF16),
        grid=(lay.ntiles,),
        in_specs=[pl.BlockSpec((1, tm, w), seq)] * 4 + [pl.BlockSpec((1, w), full)] * 2 + [pl.BlockSpec((w, w), full)],
        out_specs=pl.BlockSpec((tm, w), row),
        compiler_params=_cparams(("arbitrary",)),
        name="rwkv_readout",
    )(yf, yb, bonus, g, lnx_g.reshape(1, w), lnx_b.reshape(1, w), _head_block_diag(w, A_HEAD_DIM))


def _exp2_scores(q, keys):
    scores = [_dot_nt(q, kk) for kk in keys]
    m = scores[0].max(axis=-1, keepdims=True)
    for s in scores[1:]:
        m = jnp.maximum(m, s.max(axis=-1, keepdims=True))
    return [jnp.exp2(s - m) for s in scores]


def _with_ones_column(v):
    lane = lax.broadcasted_iota(jnp.int32, (v.shape[0], LANES), 1)
    return jnp.concatenate([v, jnp.where(lane == 0, 1.0, 0.0).astype(BF16)], axis=1)


def _softmax_pv(q, keys, vals_aug):
    e_dim = vals_aug[0].shape[-1] - LANES
    o = None
    for p, vv in zip(_exp2_scores(q, keys), vals_aug):
        part = jnp.dot(p.astype(BF16), vv, preferred_element_type=F32)
        o = part if o is None else o + part
    return o[:, :e_dim] / o[:, e_dim:e_dim + 1]


def _diff_attn_body(lay, lambda_init, q_ref, kc_ref, kl_ref, vc_ref, vl_ref, lam_ref, g_ref, o_ref):
    j = pl.program_id(1)
    lv = lam_ref[...]
    lam = (jnp.exp(jnp.sum(lv[0:1] * lv[1:2], axis=1, keepdims=True))
           - jnp.exp(jnp.sum(lv[2:3] * lv[3:4], axis=1, keepdims=True)) + lambda_init)

    def run(with_latent):
        for h in range(B_HEADS):
            vs = slice(h * B_V_DIM, (h + 1) * B_V_DIM)
            vals = [_with_ones_column(vc_ref[:, vs])] + ([_with_ones_column(vl_ref[:, vs])] if with_latent else [])
            outs = []
            for mi in range(2):
                cs = slice((2 * h + mi) * LANES, (2 * h + mi + 1) * LANES)
                keys = [kc_ref[:, cs]] + ([kl_ref[:, cs]] if with_latent else [])
                outs.append(_softmax_pv(q_ref[:, cs], keys, vals))
            o = outs[0] - lam * outs[1]
            ms = jnp.mean(o * o, axis=-1, keepdims=True)
            o = o * lax.rsqrt(ms + B_SUBLN_EPS) * g_ref[...] * (1.0 - lambda_init)
            o_ref[:, vs] = o.astype(BF16)

    @pl.when(j < lay.nct)
    def _():
        run(False)

    @pl.when(j >= lay.nct)
    def _():
        run(True)


def _diff_attention(lay, q, k, v, lam_vecs, subln_g, lambda_init):
    tm = lay.tm
    w = B_WIDTH
    lat0 = lay.rows_c // lay.S
    return pl.pallas_call(
        functools.partial(_diff_attn_body, lay, lambda_init),
        out_shape=jax.ShapeDtypeStruct((lay.rows, w), BF16),
        grid=(lay.B, lay.nct + lay.nlt),
        in_specs=[pl.BlockSpec((tm, B_QK_PAD), lambda b, j: (lay.seq_tile(b, j), 0)),
                  pl.BlockSpec((lay.CTX, B_QK_PAD), lambda b, j: (b, 0)),
                  pl.BlockSpec((lay.S, B_QK_PAD), lambda b, j: (lat0 + b, 0)),
                  pl.BlockSpec((lay.CTX, w), lambda b, j: (b, 0)),
                  pl.BlockSpec((lay.S, w), lambda b, j: (lat0 + b, 0)),
                  pl.BlockSpec((4, B_HEAD_DIM), lambda b, j: (0, 0)),
                  pl.BlockSpec((1, B_V_DIM), lambda b, j: (0, 0))],
        out_specs=pl.BlockSpec((tm, w), lambda b, j: (lay.seq_tile(b, j), 0)),
        compiler_params=_cparams(("arbitrary", "arbitrary")),
        name="diff_attention",
    )(q, k, k, v, v, lam_vecs, subln_g.reshape(1, B_V_DIM))


def _odd_in_body(x_ref, mod_ref, w_ref, cos_ref, sin_ref, qn_ref, kn_ref, q_ref, k_ref, v_ref):
    m = mod_ref[0]
    h = (x_ref[...] * (1.0 + m[1:2]) + m[0:1]).astype(BF16)
    cos, sin = cos_ref[...], sin_ref[...]

    def norm_rope(p, g, scale):
        ms = jnp.mean(p * p, axis=-1, keepdims=True)
        y = p * lax.rsqrt(ms + QK_NORM_EPS) * g
        return (_rope_lanes(y, cos, sin, C_HEAD_DIM) * scale).astype(BF16)

    p = jnp.dot(h, w_ref[...], preferred_element_type=F32)
    for hd in range(C_HEADS):
        cs = slice(hd * C_HEAD_DIM, (hd + 1) * C_HEAD_DIM)
        q_ref[:, cs] = norm_rope(p[:, cs], qn_ref[...], LOG2_E * C_HEAD_DIM ** -0.5)
    for hd in range(C_KV_HEADS):
        cs = slice(hd * C_HEAD_DIM, (hd + 1) * C_HEAD_DIM)
        k_ref[:, cs] = norm_rope(p[:, C_Q + hd * C_HEAD_DIM:C_Q + (hd + 1) * C_HEAD_DIM], kn_ref[...], 1.0)
    v_ref[...] = p[:, C_Q + C_KV:].astype(BF16)


def _odd_in_proj(lay, x, mod, w_bf16, cos, sin, qn_g, kn_g):
    tm, d = lay.tm, D_MODEL
    row = lambda i: (i, 0)
    full = lambda i: (0, 0)
    return pl.pallas_call(
        _odd_in_body,
        out_shape=(jax.ShapeDtypeStruct((lay.rows, C_Q), BF16),
                   jax.ShapeDtypeStruct((lay.rows, C_KV), BF16),
                   jax.ShapeDtypeStruct((lay.rows, C_KV), BF16)),
        grid=(lay.ntiles,),
        in_specs=[pl.BlockSpec((tm, d), row),
                  pl.BlockSpec((1, 6, d), lambda i: (lay.mod_row(i, tm), 0, 0)),
                  pl.BlockSpec((d, ODD_IN), full),
                  pl.BlockSpec((tm, C_HEAD_DIM), lambda i: (lay.pos_tile(i), 0)),
                  pl.BlockSpec((tm, C_HEAD_DIM), lambda i: (lay.pos_tile(i), 0)),
                  pl.BlockSpec((1, C_HEAD_DIM), full),
                  pl.BlockSpec((1, C_HEAD_DIM), full)],
        out_specs=(pl.BlockSpec((tm, C_Q), row), pl.BlockSpec((tm, C_KV), row), pl.BlockSpec((tm, C_KV), row)),
        compiler_params=_cparams(("arbitrary",)),
        name="odd_in_proj",
    )(x, mod, w_bf16, cos, sin, qn_g.reshape(1, C_HEAD_DIM), kn_g.reshape(1, C_HEAD_DIM))


def _gqa_body(q_ref, kc_ref, kl_ref, vc_ref, vl_ref, o_ref):
    for kvh in range(C_KV_HEADS):
        ks = slice(kvh * C_HEAD_DIM, (kvh + 1) * C_HEAD_DIM)
        keys = [kc_ref[:, ks], kl_ref[:, ks]]
        vals = [_with_ones_column(vc_ref[:, ks]), _with_ones_column(vl_ref[:, ks])]
        for g in range(C_GROUP):
            hd = kvh * C_GROUP + g
            cs = slice(hd * C_HEAD_DIM, (hd + 1) * C_HEAD_DIM)
            o_ref[:, cs] = _softmax_pv(q_ref[:, cs], keys, vals).astype(BF16)


def _gqa_attention(lay, q, k, v):
    tm = lay.tm
    lat0 = lay.rows_c // lay.S
    return pl.pallas_call(
        _gqa_body,
        out_shape=jax.ShapeDtypeStruct((lay.rows_l, C_Q), BF16),
        grid=(lay.B, lay.nlt),
        in_specs=[pl.BlockSpec((tm, C_Q), lambda b, j: (lay.ntiles_c + b * lay.nlt + j, 0)),
                  pl.BlockSpec((lay.CTX, C_KV), lambda b, j: (b, 0)),
                  pl.BlockSpec((lay.S, C_KV), lambda b, j: (lat0 + b, 0)),
                  pl.BlockSpec((lay.CTX, C_KV), lambda b, j: (b, 0)),
                  pl.BlockSpec((lay.S, C_KV), lambda b, j: (lat0 + b, 0))],
        out_specs=pl.BlockSpec((tm, C_Q), lambda b, j: (b * lay.nlt + j, 0)),
        compiler_params=_cparams(("arbitrary", "arbitrary")),
        name="gqa_attention",
    )(q, k, k, v, v)


def _out_proj_body(n_mix, *refs):
    mix_refs = refs[:n_mix]
    w_ref, x_ref, mod_ref, lg_ref, lb_ref, rw_ref, xo_ref, h_ref, lt_ref = refs[n_mix:]
    m = mod_ref[0]
    off = 0
    mix = None
    for mr in mix_refs:
        kw = mr.shape[-1]
        part = jnp.dot(mr[...], w_ref[off:off + kw, :], preferred_element_type=F32)
        mix = part if mix is None else mix + part
        off += kw
    z = DEEPNORM_ALPHA * x_ref[...] + m[2:3] * mix
    xn = _layer_norm_rows(z, lg_ref[...], lb_ref[...])
    xo_ref[...] = xn
    h2 = xn * (1.0 + m[4:5]) + m[3:4]
    h_ref[...] = h2
    lt_ref[...] = _dot3_nt(rw_ref[...], h2)


def _out_proj(lay, mixes, w_bf16, x, row_off, mod, ln_g, ln_b, router_wt, n_rows):
    tm, d = lay.tmo, D_MODEL
    row = lambda i: (i, 0)
    full = lambda i: (0, 0)
    x_tile_off = row_off // tm
    in_specs = [pl.BlockSpec((tm, mx.shape[-1]), row) for mx in mixes]
    in_specs += [pl.BlockSpec((d, d), full),
                 pl.BlockSpec((tm, d), lambda i: (i + x_tile_off, 0)),
                 pl.BlockSpec((1, 6, d), lambda i: (lay.mod_row(i, tm, row_off), 0, 0)),
                 pl.BlockSpec((1, d), full), pl.BlockSpec((1, d), full),
                 pl.BlockSpec((N_EXPERTS, d), full)]
    return pl.pallas_call(
        functools.partial(_out_proj_body, len(mixes)),
        out_shape=(jax.ShapeDtypeStruct((n_rows, d), F32),
                   jax.ShapeDtypeStruct((n_rows, d), F32),
                   jax.ShapeDtypeStruct((N_EXPERTS, n_rows), F32)),
        grid=(n_rows // tm,),
        in_specs=in_specs,
        out_specs=(pl.BlockSpec((tm, d), row), pl.BlockSpec((tm, d), row),
                   pl.BlockSpec((N_EXPERTS, tm), lambda i: (0, i))),
        compiler_params=_cparams(("arbitrary",)),
        name="out_proj_ln",
    )(*mixes, w_bf16, x, mod, ln_g.reshape(1, d), ln_b.reshape(1, d), router_wt)


_EXPERT_PAIRS = ((0, 1), (0, 2), (0, 3), (1, 2), (1, 3), (2, 3))
N_BUCKETS = N_GROUPS * len(_EXPERT_PAIRS)


def _router_body(lt_ref, rb_ref, o_ref):
    logits = lt_ref[...] + rb_ref[...]
    rows = [logits[e:e + 1, :] for e in range(N_EXPERTS)]
    m = rows[0]
    for x in rows[1:]:
        m = jnp.maximum(m, x)
    ex = [jnp.exp(x - m) for x in rows]
    z = ex[0]
    for x in ex[1:]:
        z = z + x
    p = [x / z for x in ex]

    gscore = []
    for g in range(N_GROUPS):
        a, b, c, d = p[4 * g:4 * g + 4]
        hi1, lo1 = jnp.maximum(a, b), jnp.minimum(a, b)
        hi2, lo2 = jnp.maximum(c, d), jnp.minimum(c, d)
        top1 = jnp.maximum(hi1, hi2)
        top2 = jnp.maximum(jnp.minimum(hi1, hi2), jnp.maximum(lo1, lo2))
        gscore.append(top1 + top2)
    best = []
    for g in range(N_GROUPS):
        ok = None
        for o in range(N_GROUPS):
            if o == g:
                continue
            c = (gscore[g] > gscore[o]) if o < g else (gscore[g] >= gscore[o])
            ok = c if ok is None else jnp.logical_and(ok, c)
        best.append(ok)
    won = []
    for e in range(N_EXPERTS):
        g = e // EXPERTS_PER_GROUP
        rank = jnp.zeros_like(p[e])
        for o in range(4 * g, 4 * g + 4):
            if o == e:
                continue
            ahead = (p[o] > p[e]) if o > e else (p[o] >= p[e])
            rank = rank + jnp.where(ahead, 1.0, 0.0)
        won.append(jnp.where(jnp.logical_and(best[g], rank < 1.5), 1.0, 0.0))
    tot = won[0] * p[0]
    for e in range(1, N_EXPERTS):
        tot = tot + won[e] * p[e]
    bucket = jnp.zeros_like(tot)
    gate_a = jnp.zeros_like(tot)
    gate_b = jnp.zeros_like(tot)
    for g in range(N_GROUPS):
        for pid, (a, b) in enumerate(_EXPERT_PAIRS):
            ind = won[4 * g + a] * won[4 * g + b]
            bucket = bucket + ind * float(len(_EXPERT_PAIRS) * g + pid)
            gate_a = gate_a + ind * p[4 * g + a]
            gate_b = gate_b + ind * p[4 * g + b]
    o_ref[...] = jnp.zeros_like(o_ref)
    o_ref[0:1, :] = bucket
    o_ref[1:2, :] = gate_a / tot
    o_ref[2:3, :] = gate_b / tot


def _router(logits_t, router_b):
    e, n = logits_t.shape
    tr = math.gcd(2048, n)
    return pl.pallas_call(
        _router_body,
        out_shape=jax.ShapeDtypeStruct((SUBLANES, n), F32),
        grid=(n // tr,),
        in_specs=[pl.BlockSpec((e, tr), lambda i: (0, i)), pl.BlockSpec((e, 1), lambda i: (0, 0))],
        out_specs=pl.BlockSpec((SUBLANES, tr), lambda i: (0, i)),
        compiler_params=_cparams(("arbitrary",)),
        name="router_gates",
    )(logits_t, router_b.reshape(e, 1))


def _moe_plan(bucket, tg):
    n = bucket.shape[0]
    n_tiles = n // tg + N_BUCKETS
    ids = jnp.arange(N_BUCKETS, dtype=jnp.int32)
    onehot = (bucket[:, None] == ids[None, :]).astype(jnp.int32)
    csum = jnp.cumsum(onehot, axis=0)
    counts = csum[-1]
    tiles = (counts + tg - 1) // tg
    tile_end = jnp.cumsum(tiles)
    row_start = (tile_end - tiles) * tg
    pos = jnp.sum(onehot * (row_start[None, :] + csum - 1), axis=1).astype(jnp.int32)
    n_used = tile_end[-1]
    tile_ids = jnp.minimum(jnp.arange(n_tiles, dtype=jnp.int32), n_used - 1)
    tile_bucket = jnp.sum((tile_ids[:, None] >= tile_end[None, :]).astype(jnp.int32), axis=1)
    pair_a = jnp.asarray([a for a, _ in _EXPERT_PAIRS], jnp.int32)
    pair_b = jnp.asarray([b for _, b in _EXPERT_PAIRS], jnp.int32)
    grp, pid = tile_bucket // len(_EXPERT_PAIRS), tile_bucket % len(_EXPERT_PAIRS)
    ea = grp * EXPERTS_PER_GROUP + pair_a[pid]
    eb = grp * EXPERTS_PER_GROUP + pair_b[pid]
    return pos, ea.astype(jnp.int32), eb.astype(jnp.int32), n_used.reshape(1).astype(jnp.int32), n_tiles


def _row_copy(src_ref, src_row, dst_ref, dst_row, sem):
    return pltpu.make_async_copy(src_ref.at[pl.ds(src_row, 1), :], dst_ref.at[pl.ds(dst_row, 1), :], sem)


ROW_DMA_UNROLL = 8


def _row_copies(n_rows, row_copy, whole_tile_copy):
    def issue(blk, carry):
        for u in range(ROW_DMA_UNROLL):
            row_copy(blk * ROW_DMA_UNROLL + u).start(priority=u % 2)
        return carry

    lax.fori_loop(0, n_rows // ROW_DMA_UNROLL, issue, 0)
    whole_tile_copy.wait()


def _grouped_body(tg, src_ref, ea_ref, eb_ref, nu_ref, h_ref, g_ref,
                  w1a_ref, w3a_ref, w2a_ref, w1b_ref, w3b_ref, w2b_ref, y_ref, buf_ref, sems):
    del ea_ref, eb_ref
    i = pl.program_id(0)
    slot = i % 2

    def gather(tile, slot_):
        base = tile * tg

        def issue(blk, carry):
            for u in range(ROW_DMA_UNROLL):
                r = blk * ROW_DMA_UNROLL + u
                _row_copy(h_ref, src_ref[base + r], buf_ref.at[slot_], r, sems.at[slot_]).start(priority=u % 2)
            return carry

        lax.fori_loop(0, tg // ROW_DMA_UNROLL, issue, 0)

    @pl.when(i == 0)
    def _():
        gather(0, 0)

    @pl.when(i < nu_ref[0])
    def _():
        pltpu.make_async_copy(h_ref.at[pl.ds(0, tg), :], buf_ref.at[slot], sems.at[slot]).wait()

        @pl.when(i + 1 < nu_ref[0])
        def _():
            gather(i + 1, 1 - slot)

        x = buf_ref[slot].astype(BF16)

        def expert(w1_ref, w3_ref, w2_ref, gate):
            h1 = jnp.dot(x, w1_ref[0], preferred_element_type=F32)
            h3 = jnp.dot(x, w3_ref[0], preferred_element_type=F32)
            hid = (h1 * _sigmoid(h1) * h3 * gate).astype(BF16)
            return jnp.dot(hid, w2_ref[0], preferred_element_type=F32)

        y_ref[...] = (expert(w1a_ref, w3a_ref, w2a_ref, g_ref[:, 0:1])
                      + expert(w1b_ref, w3b_ref, w2b_ref, g_ref[:, 1:2]))

    @pl.when(i >= nu_ref[0])
    def _():
        y_ref[...] = jnp.zeros_like(y_ref)


def _moe_grouped(src, gates, h2, ea, eb, n_used, w1, w3, w2, tg):
    rows = src.shape[0]
    d, ff = D_MODEL, EXPERT_FF
    n_tiles = rows // tg
    tile = lambda i, s_r, ea_r, eb_r, nu_r: (jnp.minimum(i, nu_r[0] - 1), 0)
    up_a = pl.BlockSpec((1, d, ff), lambda i, s_r, ea_r, eb_r, nu_r: (ea_r[i], 0, 0))
    up_b = pl.BlockSpec((1, d, ff), lambda i, s_r, ea_r, eb_r, nu_r: (eb_r[i], 0, 0))
    dn_a = pl.BlockSpec((1, ff, d), lambda i, s_r, ea_r, eb_r, nu_r: (ea_r[i], 0, 0))
    dn_b = pl.BlockSpec((1, ff, d), lambda i, s_r, ea_r, eb_r, nu_r: (eb_r[i], 0, 0))
    grid_spec = pltpu.PrefetchScalarGridSpec(
        num_scalar_prefetch=4,
        grid=(n_tiles,),
        in_specs=[pl.BlockSpec(memory_space=pl.ANY), pl.BlockSpec((tg, gates.shape[1]), tile),
                  up_a, up_a, dn_a, up_b, up_b, dn_b],
        out_specs=pl.BlockSpec((tg, d), lambda i, s_r, ea_r, eb_r, nu_r: (i, 0)),
        scratch_shapes=[pltpu.VMEM((2, tg, d), F32), pltpu.SemaphoreType.DMA((2,))],
    )
    return pl.pallas_call(
        functools.partial(_grouped_body, tg),
        out_shape=jax.ShapeDtypeStruct((rows, d), F32),
        grid_spec=grid_spec,
        compiler_params=_cparams(("arbitrary",)),
        name="moe_grouped",
    )(src, ea, eb, n_used, h2, gates, w1, w3, w2, w1, w3, w2)


def _combine_body(tg, pos_ref, ys_ref, x_ref, mod_ref, lg_ref, lb_ref, o_ref, buf_ref, sem):
    base = pl.program_id(0) * tg

    _row_copies(tg, lambda r: _row_copy(ys_ref, pos_ref[base + r], buf_ref, r, sem),
                pltpu.make_async_copy(ys_ref.at[pl.ds(0, tg), :], buf_ref, sem))
    m = mod_ref[0]
    z = DEEPNORM_ALPHA * x_ref[...] + m[5:6] * buf_ref[...]
    o_ref[...] = _layer_norm_rows(z, lg_ref[...], lb_ref[...])


def _moe_combine(lay, pos, ys, x, mod, ln_g, ln_b, row_off, tg):
    n, d = x.shape
    grid_spec = pltpu.PrefetchScalarGridSpec(
        num_scalar_prefetch=1,
        grid=(n // tg,),
        in_specs=[pl.BlockSpec(memory_space=pl.ANY),
                  pl.BlockSpec((tg, d), lambda i, pos_ref: (i, 0)),
                  pl.BlockSpec((1, 6, d), lambda i, pos_ref: (lay.mod_row(i, tg, row_off), 0, 0)),
                  pl.BlockSpec((1, d), lambda i, pos_ref: (0, 0)),
                  pl.BlockSpec((1, d), lambda i, pos_ref: (0, 0))],
        out_specs=pl.BlockSpec((tg, d), lambda i, pos_ref: (i, 0)),
        scratch_shapes=[pltpu.VMEM((tg, d), F32), pltpu.SemaphoreType.DMA(())],
    )
    return pl.pallas_call(
        functools.partial(_combine_body, tg),
        out_shape=jax.ShapeDtypeStruct((n, d), F32),
        grid_spec=grid_spec,
        compiler_params=_cparams(("arbitrary",)),
        name="moe_combine_ln",
    )(pos, ys, x, mod, ln_g.reshape(1, d), ln_b.reshape(1, d))


def _moe(lay, h2, routed, w1, w3, w2, x, mod, ln_g, ln_b, row_off):
    tg = lay.tm
    bucket = routed[0].astype(jnp.int32)
    pos, ea, eb, n_used, n_tiles = _moe_plan(bucket, tg)
    n = bucket.shape[0]
    src = jnp.zeros((n_tiles * tg,), jnp.int32).at[pos].set(jnp.arange(n, dtype=jnp.int32), unique_indices=True)
    gates = jnp.zeros((n_tiles * tg, SUBLANES), F32).at[pos, :2].set(routed[1:3].T, unique_indices=True)
    ys = _moe_grouped(src, gates, h2, ea, eb, n_used, w1, w3, w2, tg)
    return _moe_combine(lay, pos, ys, x, mod, ln_g, ln_b, row_off, tg)


def kernel(x, c, ctx, c_ctx, router_w, router_b, ada_w, ada_b, ln1_g, ln1_b, ln2_g, ln2_b, moe_w1, moe_w3, moe_w2, ev_w_in, ev_w_out, ev_a_mu, ev_a_w0, ev_a_w2, ev_a_a0, ev_a_a2, ev_a_g2, ev_a_kk, ev_a_ka, ev_a_rk, ev_a_lnx_g, ev_a_lnx_b, ev_b_lam, ev_b_subln_g, od_w_in, od_w_out, od_qn_g, od_kn_g):
    bsz, seq, d = x.shape
    ctx_len = ctx.shape[1]
    assert d == D_MODEL and ada_w.shape[0] == DEPTH and seq % GRID_W == 0
    lay = _Layout(bsz, ctx_len, seq)

    cvec = jnp.zeros((lay.mod_rows, d), F32).at[:bsz].set(c).at[bsz].set(c_ctx)
    mods = _ada_mods(cvec, ada_w, ada_b).reshape(DEPTH, lay.mod_rows, 6, d)

    xs = jnp.concatenate([ctx.reshape(lay.rows_c, d), x.reshape(lay.rows_l, d)], axis=0)
    router_wt = router_w.T

    cos_b, sin_b = _rope_tables(lay, B_HEAD_DIM, B_QK)
    cos_c, sin_c = _rope_tables(lay, C_HEAD_DIM, C_HEAD_DIM)

    for i in range(DEPTH):
        last = i == DEPTH - 1
        j = i // 2
        mod = mods[i]
        if i % 2 == 0:
            lambda_init = 0.8 - 0.6 * math.exp(-0.3 * i)
            pa, q, k, v = _even_in_proj(lay, xs, mod, ev_w_in[j].astype(BF16), cos_b, sin_b)
            decay, scan_ops, g_, bonus = _rwkv_features(
                lay, pa, ev_a_mu[j], ev_a_w0[j], ev_a_w2[j], ev_a_a0[j], ev_a_a2[j], ev_a_g2[j],
                ev_a_kk[j], ev_a_ka[j], ev_a_rk[j].reshape(-1))
            y_f, y_b = _wkv_scan(lay, _to_chains(lay, decay[None]), _to_chains(lay, scan_ops))
            nl = lay.B * A_HEADS
            yf = _from_chains(lay, y_f[:, :, :nl])
            yb = _from_chains(lay, y_b[:, :, nl:])
            ya = _rwkv_readout(lay, yf, yb, bonus, g_, ev_a_lnx_g[j], ev_a_lnx_b[j])
            yd = _diff_attention(lay, q, k, v, ev_b_lam[j], ev_b_subln_g[j], lambda_init)
            mixes, w_out = [ya, yd], ev_w_out[j]
            n_rows, tile_off = (lay.rows_l, lay.ntiles_c) if last else (lay.rows, 0)
            if last:
                mixes = [mx[lay.rows_c:] for mx in mixes]
        else:
            q, k, v = _odd_in_proj(lay, xs, mod, od_w_in[j].astype(BF16), cos_c, sin_c, od_qn_g[j], od_kn_g[j])
            assert last, "an odd layer that must also update the context stream is not supported"
            o = _gqa_attention(lay, q, k, v)
            mixes, w_out = [o], od_w_out[j]
            n_rows, tile_off = lay.rows_l, lay.ntiles_c
        row_off = tile_off * lay.tm
        x_new, h2, logits_t = _out_proj(lay, mixes, w_out.astype(BF16), xs, row_off, mod,
                                        ln1_g[i], ln1_b[i], router_wt, n_rows)
        routed = _router(logits_t, router_b)
        xs = _moe(lay, h2, routed, moe_w1[i].astype(BF16), moe_w3[i].astype(BF16), moe_w2[i].astype(BF16),
                  x_new, mod, ln2_g[i], ln2_b[i], row_off)
    return xs.reshape(bsz, seq, d)
```

```python
import functools
import math

import jax
import jax.numpy as jnp
from jax import lax
from jax.experimental import pallas as pl
from jax.experimental.pallas import tpu as pltpu

F32 = jnp.float32
BF16 = jnp.bfloat16

D_MODEL = 1024
DEPTH = 2
GRID_W = 64
ROPE_THETA = 10000.0
LN_EPS = 1e-5
LOG2_E = 1.4426950408889634
DEEPNORM_ALPHA = (2 * DEPTH) ** 0.25

A_HEAD_DIM = 64
A_HEADS = 8
A_WIDTH = 512
A_LORA = 64
A_GATE_LORA = 128
A_GN_EPS = 64e-5
A_IN = 3 * A_WIDTH + 4 * A_LORA + A_GATE_LORA
_SCAN_R, _SCAN_K, _SCAN_V, _SCAN_A, _SCAN_B = range(5)
N_SCAN_OPS = 5

B_HEAD_DIM = 64
B_V_DIM = 128
B_HEADS = 4
B_WIDTH = 512
B_QK = 512
B_QK_PAD = (B_QK // B_HEAD_DIM) * 128
B_SUBLN_EPS = 1e-5
EVEN_IN = A_IN + 2 * B_QK + B_WIDTH

C_HEAD_DIM = 128
C_HEADS = 8
C_KV_HEADS = 2
C_GROUP = 4
C_Q = 1024
C_KV = 256
ODD_IN = C_Q + 2 * C_KV
QK_NORM_EPS = 1e-6

N_EXPERTS = 16
N_GROUPS = 4
EXPERTS_PER_GROUP = 4
EXPERT_FF = 512

VMEM_LIMIT_BYTES = 56 * 1024 * 1024
LANES = 128
SUBLANES = 8


def _cparams(sem):
    return pltpu.CompilerParams(dimension_semantics=sem, vmem_limit_bytes=VMEM_LIMIT_BYTES)


def _dot(a, b):
    return jnp.dot(a.astype(BF16), b.astype(BF16), preferred_element_type=F32)


def _dot_nt(a, b):
    return lax.dot_general(a.astype(BF16), b.astype(BF16), (((1,), (1,)), ((), ())),
                           preferred_element_type=F32)


def _split(a):
    hi = a.astype(BF16)
    lo = (a - hi.astype(F32)).astype(BF16)
    return hi, lo


def _dot3(a, b):
    ah, al = _split(a)
    bh, bl = _split(b)
    return (jnp.dot(ah, bh, preferred_element_type=F32)
            + (jnp.dot(ah, bl, preferred_element_type=F32)
               + jnp.dot(al, bh, preferred_element_type=F32)))


def _dot2_exact_rhs(a, b_bf16):
    ah, al = _split(a)
    return jnp.dot(ah, b_bf16, preferred_element_type=F32) + jnp.dot(al, b_bf16, preferred_element_type=F32)


def _dot3_nt(a, b):
    ah, al = _split(a)
    bh, bl = _split(b)
    dn = (((1,), (1,)), ((), ()))
    return (lax.dot_general(ah, bh, dn, preferred_element_type=F32)
            + (lax.dot_general(ah, bl, dn, preferred_element_type=F32)
               + lax.dot_general(al, bh, dn, preferred_element_type=F32)))


def _sigmoid(x):
    return 1.0 / (1.0 + jnp.exp(-x))


def _layer_norm_rows(z, g, b):
    mu = jnp.mean(z, axis=-1, keepdims=True)
    zc = z - mu
    var = jnp.mean(zc * zc, axis=-1, keepdims=True)
    return zc * lax.rsqrt(var + LN_EPS) * g + b


class _Layout:
    def __init__(self, bsz, ctx_len, seq):
        self.B, self.CTX, self.S = bsz, ctx_len, seq
        self.T = ctx_len + seq
        self.tm = math.gcd(256, math.gcd(ctx_len, seq))
        self.nct = ctx_len // self.tm
        self.nlt = seq // self.tm
        self.rows_c = bsz * ctx_len
        self.rows_l = bsz * seq
        self.rows = self.rows_c + self.rows_l
        self.ntiles_c = bsz * self.nct
        self.ntiles = self.rows // self.tm
        assert self.rows_c % seq == 0, "latent K/V blocks are addressed in units of S rows"
        self.tb = math.gcd(32, math.gcd(ctx_len, seq))
        self.tmo = math.gcd(512, math.gcd(self.rows_c, seq))
        self.mod_rows = -(-(bsz + 1) // SUBLANES) * SUBLANES

    def seq_tile(self, b, j):
        return jnp.where(j < self.nct, b * self.nct + j, self.ntiles_c + b * self.nlt + (j - self.nct))

    def seq_block(self, i):
        il = i - self.ntiles_c
        return (jnp.where(i < self.ntiles_c, i // self.nct, il // self.nlt),
                jnp.where(i < self.ntiles_c, i % self.nct, self.nct + il % self.nlt))

    def mod_row(self, i, tile, row_offset=0):
        r = i * tile + row_offset
        return jnp.where(r < self.rows_c, self.B, (r - self.rows_c) // self.S)

    def pos_tile(self, i):
        il = i - self.ntiles_c
        return jnp.where(i < self.ntiles_c, i % self.nct, self.nct + il % self.nlt)


def _ada_body(cv_ref, w_ref, b_ref, o_ref):
    cv = cv_ref[...]
    s = cv * _sigmoid(cv)
    o_ref[0] = _dot3(s, w_ref[0]) + b_ref[0]


def _ada_mods(cvec, ada_w, ada_b):
    depth, d, n = ada_w.shape
    r = cvec.shape[0]
    tn = 512
    return pl.pallas_call(
        _ada_body,
        out_shape=jax.ShapeDtypeStruct((depth, r, n), F32),
        grid=(depth, n // tn),
        in_specs=[pl.BlockSpec((r, d), lambda l, j: (0, 0)),
                  pl.BlockSpec((1, d, tn), lambda l, j: (l, 0, j)),
                  pl.BlockSpec((1, 1, tn), lambda l, j: (l, 0, j))],
        out_specs=pl.BlockSpec((1, r, tn), lambda l, j: (l, 0, j)),
        compiler_params=_cparams(("arbitrary", "arbitrary")),
        name="ada_mods",
    )(cvec, ada_w, ada_b.reshape(depth, 1, n))


def _rope_tables(lay, head_dim, width):
    rows = lay.S // GRID_W
    rr, cc = jnp.meshgrid(jnp.arange(rows), jnp.arange(GRID_W), indexing="ij")
    row_pos = rr.reshape(-1).astype(F32)
    col_pos = cc.reshape(-1).astype(F32)
    axis_dim = head_dim // 2
    inv = ROPE_THETA ** (-jnp.arange(0, axis_dim, 2, dtype=F32) / axis_dim)
    ang = jnp.concatenate([row_pos[:, None] * inv, col_pos[:, None] * inv], -1)
    cos, sin = jnp.cos(ang), jnp.sin(ang)
    cos = jnp.concatenate([jnp.ones((lay.CTX, head_dim // 2), F32), cos], 0)
    sin = jnp.concatenate([jnp.zeros((lay.CTX, head_dim // 2), F32), sin], 0)
    cos_h = jnp.concatenate([cos, cos], -1)
    sin_h = jnp.concatenate([-sin, sin], -1)
    reps = width // head_dim
    return jnp.tile(cos_h, (1, reps)), jnp.tile(sin_h, (1, reps))


def _rope_lanes(x, cos, sin, head_dim):
    w = x.shape[-1]
    half = head_dim // 2
    if head_dim == LANES and w == LANES:
        rot = pltpu.roll(x, half, 1)
    else:
        fwd = pltpu.roll(x, w - half, 1)
        bwd = pltpu.roll(x, half, 1)
        lane = lax.broadcasted_iota(jnp.int32, x.shape, 1)
        rot = jnp.where((lane % head_dim) < half, fwd, bwd)
    return x * cos + rot * sin


def _row_tile(n_ctx_tiles, xc_ref, xl_ref):
    return jnp.where(pl.program_id(0) < n_ctx_tiles, xc_ref[...], xl_ref[...])


def _row_tile_specs(tile, n_ctx_tiles, d):
    return [pl.BlockSpec((tile, d), lambda i: (jnp.minimum(i, n_ctx_tiles - 1), 0)),
            pl.BlockSpec((tile, d), lambda i: (jnp.maximum(i - n_ctx_tiles, 0), 0))]


def _even_in_body(n_ctx_tiles, xc_ref, xl_ref, mod_ref, w_ref, cos_ref, sin_ref, pa_ref, q_ref, k_ref, v_ref):
    m = mod_ref[0]
    h = (_row_tile(n_ctx_tiles, xc_ref, xl_ref) * (1.0 + m[1:2]) + m[0:1]).astype(BF16)
    pa_ref[...] = jnp.dot(h, w_ref[:, :A_IN], preferred_element_type=F32)
    cos, sin = cos_ref[...], sin_ref[...]
    o = A_IN

    def put_maps(dst_ref, val):
        pad = jnp.zeros((val.shape[0], LANES - B_HEAD_DIM), BF16)
        for mp in range(B_QK // B_HEAD_DIM):
            piece = val[:, mp * B_HEAD_DIM:(mp + 1) * B_HEAD_DIM].astype(BF16)
            dst_ref[:, mp * LANES:(mp + 1) * LANES] = jnp.concatenate([piece, pad], axis=1)

    q = jnp.dot(h, w_ref[:, o:o + B_QK], preferred_element_type=F32)
    put_maps(q_ref, _rope_lanes(q, cos, sin, B_HEAD_DIM) * (LOG2_E * B_HEAD_DIM ** -0.5))
    o += B_QK
    k = jnp.dot(h, w_ref[:, o:o + B_QK], preferred_element_type=F32)
    put_maps(k_ref, _rope_lanes(k, cos, sin, B_HEAD_DIM))
    o += B_QK
    v_ref[...] = jnp.dot(h, w_ref[:, o:o + B_WIDTH], preferred_element_type=F32).astype(BF16)


def _even_in_proj(lay, x_ctx, x_lat, mod, w_bf16, cos, sin):
    tm, d = lay.tm, D_MODEL
    row = lambda i: (i, 0)
    return pl.pallas_call(
        functools.partial(_even_in_body, lay.ntiles_c),
        out_shape=(jax.ShapeDtypeStruct((lay.rows, A_IN), F32),
                   jax.ShapeDtypeStruct((lay.rows, B_QK_PAD), BF16),
                   jax.ShapeDtypeStruct((lay.rows, B_QK_PAD), BF16),
                   jax.ShapeDtypeStruct((lay.rows, B_WIDTH), BF16)),
        grid=(lay.ntiles,),
        in_specs=_row_tile_specs(tm, lay.ntiles_c, d) + [
                  pl.BlockSpec((1, 6, d), lambda i: (lay.mod_row(i, tm), 0, 0)),
                  pl.BlockSpec((d, EVEN_IN), lambda i: (0, 0)),
                  pl.BlockSpec((tm, B_QK), lambda i: (lay.pos_tile(i), 0)),
                  pl.BlockSpec((tm, B_QK), lambda i: (lay.pos_tile(i), 0))],
        out_specs=(pl.BlockSpec((tm, A_IN), row), pl.BlockSpec((tm, B_QK_PAD), row),
                   pl.BlockSpec((tm, B_QK_PAD), row), pl.BlockSpec((tm, B_WIDTH), row)),
        compiler_params=_cparams(("arbitrary",)),
        name="even_in_proj",
    )(x_ctx, x_lat, mod, w_bf16, cos, sin)


def _rwkv_feat_body(lay, pa_ref, prev_ref, next_ref, mu_ref, w0_ref, w2_ref, a0_ref, a2_ref, g2_ref,
                    kk_ref, ka_ref, rk_ref, bd_ref, tri_ref, blk_ref,
                    pend_ref, ops_ref, g_ref, bonus_ref):
    def put(idx, d, val):
        ops_ref[idx, d, 0] = val.astype(BF16)

    i = pl.program_id(0)
    tm = lay.tm
    il = i - lay.ntiles_c
    in_ctx = i < lay.ntiles_c
    seg_first = jnp.where(in_ctx, i % lay.nct == 0, il % lay.nlt == 0)
    seg_last = jnp.where(in_ctx, i % lay.nct == lay.nct - 1, il % lay.nlt == lay.nlt - 1)

    pa = pa_ref[...]
    row = lax.broadcasted_iota(jnp.int32, pa.shape, 0)
    prev_edge = jnp.where(seg_first, 0.0, 1.0) * prev_ref[SUBLANES - 1:SUBLANES, :]
    next_edge = jnp.where(seg_last, 0.0, 1.0) * next_ref[0:1, :]
    prev = jnp.where(row == 0, prev_edge, pltpu.roll(pa, 1, 0))
    nxt = jnp.where(row == tm - 1, next_edge, pltpu.roll(pa, tm - 1, 0))
    u = pa + (0.5 * (prev + nxt) - pa) * mu_ref[...]

    o1, o2, o3 = A_WIDTH, 2 * A_WIDTH, 3 * A_WIDTH
    o4 = o3 + 2 * A_LORA
    o5 = o4 + 2 * A_LORA
    r, k, v = u[:, :o1], u[:, o1:o2], u[:, o2:o3]
    bd = bd_ref[...]

    kk = k * kk_ref[...]
    ss = _dot2_exact_rhs(kk * kk, bd)
    kkn = kk / jnp.maximum(jnp.sqrt(ss), 1e-12)
    g = _dot3(_sigmoid(u[:, o5:]), g2_ref[...])

    g_ref[0] = g

    kd_sum = jnp.zeros_like(k)
    for d in range(2):
        wd = u[:, o3 + d * A_LORA:o3 + (d + 1) * A_LORA]
        ad = u[:, o4 + d * A_LORA:o4 + (d + 1) * A_LORA]
        log_decay = -math.exp(-0.5) * _sigmoid(w0_ref[d:d + 1, :] + _dot3(jnp.tanh(wd), w2_ref[d]))
        ld_hi, ld_lo = _split(log_decay)
        cum = (jnp.dot(tri_ref[d], ld_hi, preferred_element_type=F32)
               + jnp.dot(tri_ref[d], ld_lo, preferred_element_type=F32))
        tot = (jnp.dot(blk_ref[...], ld_hi, preferred_element_type=F32)
               + jnp.dot(blk_ref[...], ld_lo, preferred_element_type=F32))
        pend_ref[d, 0] = jnp.exp(tot)
        p = jnp.exp(cum)
        p_inv = 1.0 / p
        a = _sigmoid(a0_ref[d:d + 1, :] + _dot3(ad, a2_ref[d]))
        kd = k * (1.0 + (a - 1.0) * ka_ref[...])
        put(_SCAN_R, d, r * p)
        put(_SCAN_V, d, v)
        put(_SCAN_A, d, -kkn * jnp.exp(cum - log_decay))
        put(_SCAN_K, d, kd * p_inv)
        put(_SCAN_B, d, kkn * a * p_inv)
        kd_sum = kd_sum + kd
    bonus_ref[0] = _dot2_exact_rhs(r * kd_sum * rk_ref[...], bd) * v


def _head_block_diag(width, head_dim):
    h = jnp.arange(width) // head_dim
    return (h[:, None] == h[None, :]).astype(BF16)


def _rwkv_features(lay, pa, mu, w0, w2, a0, a2, g2, k_k, k_a, r_k):
    tm = lay.tm
    hb = tm // SUBLANES
    nb8 = lay.rows // SUBLANES
    row = lambda i: (i, 0)
    full2 = lambda i: (0, 0)
    full3 = lambda i: (0, 0, 0)
    w = A_WIDTH
    out = jax.ShapeDtypeStruct((lay.B, lay.T, w), F32)
    seq = lambda i: lay.seq_block(i) + (0,)
    tb = lay.tb
    nblk = tm // tb
    pos = jnp.arange(tm)
    same_blk = (pos[:, None] // tb) == (pos[None, :] // tb)
    tri = jnp.stack([same_blk & (pos[None, :] <= pos[:, None]),
                     same_blk & (pos[None, :] >= pos[:, None])]).astype(BF16)
    blk = (jnp.arange(nblk)[:, None] == (pos[None, :] // tb)).astype(BF16)
    return pl.pallas_call(
        functools.partial(_rwkv_feat_body, lay),
        out_shape=(jax.ShapeDtypeStruct((2, lay.B, lay.T // tb, w), F32),
                   jax.ShapeDtypeStruct((N_SCAN_OPS, 2, lay.B, lay.T, w), BF16), out, out),
        grid=(lay.ntiles,),
        in_specs=[pl.BlockSpec((tm, A_IN), row),
                  pl.BlockSpec((SUBLANES, A_IN), lambda i: (jnp.maximum(i * hb - 1, 0), 0)),
                  pl.BlockSpec((SUBLANES, A_IN), lambda i: (jnp.minimum((i + 1) * hb, nb8 - 1), 0)),
                  pl.BlockSpec((1, A_IN), full2),
                  pl.BlockSpec((2, w), full2),
                  pl.BlockSpec((2, A_LORA, w), full3),
                  pl.BlockSpec((2, w), full2),
                  pl.BlockSpec((2, A_LORA, w), full3),
                  pl.BlockSpec((A_GATE_LORA, w), full2),
                  pl.BlockSpec((1, w), full2),
                  pl.BlockSpec((1, w), full2),
                  pl.BlockSpec((1, w), full2),
                  pl.BlockSpec((w, w), full2),
                  pl.BlockSpec((2, tm, tm), full3),
                  pl.BlockSpec((nblk, tm), full2)],
        out_specs=(pl.BlockSpec((2, 1, nblk, w), lambda i: (0,) + lay.seq_block(i) + (0,)),
                   pl.BlockSpec((N_SCAN_OPS, 2, 1, tm, w), lambda i: (0, 0) + lay.seq_block(i) + (0,)),
                   pl.BlockSpec((1, tm, w), seq), pl.BlockSpec((1, tm, w), seq)),
        compiler_params=_cparams(("arbitrary",)),
        name="rwkv_features",
    )(pa, pa, pa, mu.reshape(1, A_IN), w0, w2, a0, a2, g2, k_k.reshape(1, w), k_a.reshape(1, w),
      r_k.reshape(1, w), _head_block_diag(w, A_HEAD_DIM), tri, blk)


def _wkv_scan_body(tb, pf_ref, pb_ref, fwd_ref, bwd_ref, yf_ref, yb_ref, s_ref, m_ref):
    n = A_HEAD_DIM
    chains = s_ref.shape[-1]
    is_fwd = lax.broadcasted_iota(jnp.int32, (n, chains), 1) < chains // 2

    @pl.when(pl.program_id(0) == 0)
    def _():
        s_ref[...] = jnp.zeros_like(s_ref)

    def step(t, carry):
        tr = tb - 1 - t
        for idx in range(N_SCAN_OPS):
            m_ref[idx] = jnp.where(is_fwd, fwd_ref[t, idx], bwd_ref[tr, idx]).astype(F32)
        sa = jnp.zeros((n, chains), F32)
        for kk in range(n):
            sa = sa + s_ref[kk] * m_ref[_SCAN_A, kk:kk + 1, :]
        v_t = m_ref[_SCAN_V]
        y = jnp.zeros((n, chains), F32)
        for kk in range(n):
            s_new = s_ref[kk] + sa * m_ref[_SCAN_B, kk:kk + 1, :] + v_t * m_ref[_SCAN_K, kk:kk + 1, :]
            s_ref[kk] = s_new
            y = y + s_new * m_ref[_SCAN_R, kk:kk + 1, :]
        yf_ref[t] = y.astype(BF16)
        yb_ref[tr] = y.astype(BF16)
        return carry

    lax.fori_loop(0, tb, step, 0)
    m_ref[0] = jnp.where(is_fwd, pf_ref[0, 0], pb_ref[0, 0])
    for kk in range(n):
        s_ref[kk] = s_ref[kk] * m_ref[0, kk:kk + 1, :]


def _wkv_scan(lay, block_decay, ops):
    t, nops, n, lanes = ops.shape
    tb = lay.tb
    nctb, nt = lay.CTX // tb, t // tb
    bwd_tile = lambda g: jnp.where(g < nctb, nctb - 1 - g, nt - 1 - (g - nctb))
    out = jax.ShapeDtypeStruct((t, n, lanes), BF16)
    fwd_map = lambda g: (g, 0, 0, 0)
    bwd_map = lambda g: (bwd_tile(g), 0, 0, 0)
    return pl.pallas_call(
        functools.partial(_wkv_scan_body, tb),
        out_shape=(out, out),
        grid=(nt,),
        in_specs=[pl.BlockSpec((1, 1, n, lanes), fwd_map), pl.BlockSpec((1, 1, n, lanes), bwd_map),
                  pl.BlockSpec((tb, nops, n, lanes), fwd_map), pl.BlockSpec((tb, nops, n, lanes), bwd_map)],
        out_specs=(pl.BlockSpec((tb, n, lanes), lambda g: (g, 0, 0)),
                   pl.BlockSpec((tb, n, lanes), lambda g: (bwd_tile(g), 0, 0))),
        scratch_shapes=[pltpu.VMEM((n, n, lanes), F32), pltpu.VMEM((nops, n, lanes), F32)],
        compiler_params=_cparams(("arbitrary",)),
        name="wkv7_scan",
    )(block_decay, block_decay, ops, ops)


def _to_chains(lay, ops):
    h, n = A_HEADS, A_HEAD_DIM
    nops, length = ops.shape[0], ops.shape[3]
    x = ops.reshape(nops, 2, lay.B, length, h, n).transpose(3, 0, 5, 1, 2, 4)
    return x.reshape(length, nops, n, 2 * lay.B * h)


def _from_chains(lay, y_f, y_b):
    h, n = A_HEADS, A_HEAD_DIM
    low = jnp.arange(2 * lay.B * h) < lay.B * h
    y = jnp.where(low, y_f, y_b)
    return y.reshape(lay.T, n, 2, lay.B, h).transpose(2, 3, 0, 4, 1).reshape(2, lay.B, lay.T, h * n)


def _rwkv_readout_body(yf_ref, yb_ref, bonus_ref, g_ref, lg_ref, lb_ref, bd_ref, o_ref):
    y = yf_ref[0, 0].astype(F32) + yb_ref[0, 0].astype(F32)
    bd = bd_ref[...]
    inv_n = 1.0 / A_HEAD_DIM
    mu = _dot2_exact_rhs(y, bd) * inv_n
    yc = y - mu
    var = _dot2_exact_rhs(yc * yc, bd) * inv_n
    yn = yc * lax.rsqrt(var + A_GN_EPS) * lg_ref[...] + lb_ref[...]
    o_ref[...] = ((yn + bonus_ref[0]) * g_ref[0]).astype(BF16)


def _rwkv_readout(lay, y2, bonus, g, lnx_g, lnx_b):
    tm, w = lay.tm, A_WIDTH
    row = lambda i: (i, 0)
    full = lambda i: (0, 0)
    seq = lambda i: lay.seq_block(i) + (0,)
    return pl.pallas_call(
        _rwkv_readout_body,
        out_shape=jax.ShapeDtypeStruct((lay.rows, w), BF16),
        grid=(lay.ntiles,),
        in_specs=[pl.BlockSpec((1, 1, tm, w), lambda i: (0,) + seq(i)),
                  pl.BlockSpec((1, 1, tm, w), lambda i: (1,) + seq(i)),
                  pl.BlockSpec((1, tm, w), seq), pl.BlockSpec((1, tm, w), seq),
                  pl.BlockSpec((1, w), full), pl.BlockSpec((1, w), full), pl.BlockSpec((w, w), full)],
        out_specs=pl.BlockSpec((tm, w), row),
        compiler_params=_cparams(("arbitrary",)),
        name="rwkv_readout",
    )(y2, y2, bonus, g, lnx_g.reshape(1, w), lnx_b.reshape(1, w), _head_block_diag(w, A_HEAD_DIM))


def _exp2_scores(q, keys):
    scores = [_dot_nt(q, kk) for kk in keys]
    m = scores[0].max(axis=-1, keepdims=True)
    for s in scores[1:]:
        m = jnp.maximum(m, s.max(axis=-1, keepdims=True))
    return [jnp.exp2(s - m) for s in scores]


def _with_ones_column(v):
    lane = lax.broadcasted_iota(jnp.int32, (v.shape[0], LANES), 1)
    return jnp.concatenate([v, jnp.where(lane == 0, 1.0, 0.0).astype(BF16)], axis=1)


def _softmax_pv(q, keys, vals_aug):
    e_dim = vals_aug[0].shape[-1] - LANES
    o = None
    for p, vv in zip(_exp2_scores(q, keys), vals_aug):
        part = jnp.dot(p.astype(BF16), vv, preferred_element_type=F32)
        o = part if o is None else o + part
    return o[:, :e_dim] / o[:, e_dim:e_dim + 1]


def _diff_attn_body(lay, lambda_init, q_ref, kc_ref, kl_ref, vc_ref, vl_ref, lam_ref, g_ref, o_ref):
    j = pl.program_id(1)
    lv = lam_ref[...]
    lam = (jnp.exp(jnp.sum(lv[0:1] * lv[1:2], axis=1, keepdims=True))
           - jnp.exp(jnp.sum(lv[2:3] * lv[3:4], axis=1, keepdims=True)) + lambda_init)

    def run(with_latent):
        for h in range(B_HEADS):
            vs = slice(h * B_V_DIM, (h + 1) * B_V_DIM)
            vals = [_with_ones_column(vc_ref[:, vs])] + ([_with_ones_column(vl_ref[:, vs])] if with_latent else [])
            outs = []
            for mi in range(2):
                cs = slice((2 * h + mi) * LANES, (2 * h + mi + 1) * LANES)
                keys = [kc_ref[:, cs]] + ([kl_ref[:, cs]] if with_latent else [])
                outs.append(_softmax_pv(q_ref[:, cs], keys, vals))
            o = outs[0] - lam * outs[1]
            ms = jnp.mean(o * o, axis=-1, keepdims=True)
            o = o * lax.rsqrt(ms + B_SUBLN_EPS) * g_ref[...] * (1.0 - lambda_init)
            o_ref[:, vs] = o.astype(BF16)

    @pl.when(j < lay.nct)
    def _():
        run(False)

    @pl.when(j >= lay.nct)
    def _():
        run(True)


def _diff_attention(lay, q, k, v, lam_vecs, subln_g, lambda_init):
    tm = lay.tm
    w = B_WIDTH
    lat0 = lay.rows_c // lay.S
    return pl.pallas_call(
        functools.partial(_diff_attn_body, lay, lambda_init),
        out_shape=jax.ShapeDtypeStruct((lay.rows, w), BF16),
        grid=(lay.B, lay.nct + lay.nlt),
        in_specs=[pl.BlockSpec((tm, B_QK_PAD), lambda b, j: (lay.seq_tile(b, j), 0)),
                  pl.BlockSpec((lay.CTX, B_QK_PAD), lambda b, j: (b, 0)),
                  pl.BlockSpec((lay.S, B_QK_PAD), lambda b, j: (lat0 + b, 0)),
                  pl.BlockSpec((lay.CTX, w), lambda b, j: (b, 0)),
                  pl.BlockSpec((lay.S, w), lambda b, j: (lat0 + b, 0)),
                  pl.BlockSpec((4, B_HEAD_DIM), lambda b, j: (0, 0)),
                  pl.BlockSpec((1, B_V_DIM), lambda b, j: (0, 0))],
        out_specs=pl.BlockSpec((tm, w), lambda b, j: (lay.seq_tile(b, j), 0)),
        compiler_params=_cparams(("arbitrary", "arbitrary")),
        name="diff_attention",
    )(q, k, k, v, v, lam_vecs, subln_g.reshape(1, B_V_DIM))


def _odd_in_body(x_ref, mod_ref, w_ref, cos_ref, sin_ref, qn_ref, kn_ref, q_ref, k_ref, v_ref):
    m = mod_ref[0]
    h = (x_ref[...] * (1.0 + m[1:2]) + m[0:1]).astype(BF16)
    cos, sin = cos_ref[...], sin_ref[...]

    def norm_rope(p, g, scale):
        ms = jnp.mean(p * p, axis=-1, keepdims=True)
        y = p * lax.rsqrt(ms + QK_NORM_EPS) * g
        return (_rope_lanes(y, cos, sin, C_HEAD_DIM) * scale).astype(BF16)

    p = jnp.dot(h, w_ref[...], preferred_element_type=F32)
    for hd in range(C_HEADS):
        cs = slice(hd * C_HEAD_DIM, (hd + 1) * C_HEAD_DIM)
        q_ref[:, cs] = norm_rope(p[:, cs], qn_ref[...], LOG2_E * C_HEAD_DIM ** -0.5)
    for hd in range(C_KV_HEADS):
        cs = slice(hd * C_HEAD_DIM, (hd + 1) * C_HEAD_DIM)
        k_ref[:, cs] = norm_rope(p[:, C_Q + hd * C_HEAD_DIM:C_Q + (hd + 1) * C_HEAD_DIM], kn_ref[...], 1.0)
    v_ref[...] = p[:, C_Q + C_KV:].astype(BF16)


def _odd_in_proj(lay, x, mod, w_bf16, cos, sin, qn_g, kn_g):
    tm, d = lay.tm, D_MODEL
    row = lambda i: (i, 0)
    full = lambda i: (0, 0)
    return pl.pallas_call(
        _odd_in_body,
        out_shape=(jax.ShapeDtypeStruct((lay.rows, C_Q), BF16),
                   jax.ShapeDtypeStruct((lay.rows, C_KV), BF16),
                   jax.ShapeDtypeStruct((lay.rows, C_KV), BF16)),
        grid=(lay.ntiles,),
        in_specs=[pl.BlockSpec((tm, d), row),
                  pl.BlockSpec((1, 6, d), lambda i: (lay.mod_row(i, tm), 0, 0)),
                  pl.BlockSpec((d, ODD_IN), full),
                  pl.BlockSpec((tm, C_HEAD_DIM), lambda i: (lay.pos_tile(i), 0)),
                  pl.BlockSpec((tm, C_HEAD_DIM), lambda i: (lay.pos_tile(i), 0)),
                  pl.BlockSpec((1, C_HEAD_DIM), full),
                  pl.BlockSpec((1, C_HEAD_DIM), full)],
        out_specs=(pl.BlockSpec((tm, C_Q), row), pl.BlockSpec((tm, C_KV), row), pl.BlockSpec((tm, C_KV), row)),
        compiler_params=_cparams(("arbitrary",)),
        name="odd_in_proj",
    )(x, mod, w_bf16, cos, sin, qn_g.reshape(1, C_HEAD_DIM), kn_g.reshape(1, C_HEAD_DIM))


def _gqa_body(q_ref, kc_ref, kl_ref, vc_ref, vl_ref, o_ref):
    for kvh in range(C_KV_HEADS):
        ks = slice(kvh * C_HEAD_DIM, (kvh + 1) * C_HEAD_DIM)
        keys = [kc_ref[:, ks], kl_ref[:, ks]]
        vals = [_with_ones_column(vc_ref[:, ks]), _with_ones_column(vl_ref[:, ks])]
        for g in range(C_GROUP):
            hd = kvh * C_GROUP + g
            cs = slice(hd * C_HEAD_DIM, (hd + 1) * C_HEAD_DIM)
            o_ref[:, cs] = _softmax_pv(q_ref[:, cs], keys, vals).astype(BF16)


def _gqa_attention(lay, q, k, v):
    tm = lay.tm
    lat0 = lay.rows_c // lay.S
    return pl.pallas_call(
        _gqa_body,
        out_shape=jax.ShapeDtypeStruct((lay.rows_l, C_Q), BF16),
        grid=(lay.B, lay.nlt),
        in_specs=[pl.BlockSpec((tm, C_Q), lambda b, j: (lay.ntiles_c + b * lay.nlt + j, 0)),
                  pl.BlockSpec((lay.CTX, C_KV), lambda b, j: (b, 0)),
                  pl.BlockSpec((lay.S, C_KV), lambda b, j: (lat0 + b, 0)),
                  pl.BlockSpec((lay.CTX, C_KV), lambda b, j: (b, 0)),
                  pl.BlockSpec((lay.S, C_KV), lambda b, j: (lat0 + b, 0))],
        out_specs=pl.BlockSpec((tm, C_Q), lambda b, j: (b * lay.nlt + j, 0)),
        compiler_params=_cparams(("arbitrary", "arbitrary")),
        name="gqa_attention",
    )(q, k, k, v, v)


def _out_proj_body(n_mix, n_ctx_tiles, *refs):
    mix_refs = refs[:n_mix]
    n_x = 1 if n_ctx_tiles is None else 2
    x_refs = refs[n_mix + 1:n_mix + 1 + n_x]
    w_ref = refs[n_mix]
    mod_ref, lg_ref, lb_ref, rw_ref, xo_ref, h_ref, lt_ref = refs[n_mix + 1 + n_x:]
    x = x_refs[0][...] if n_ctx_tiles is None else _row_tile(n_ctx_tiles, *x_refs)
    m = mod_ref[0]
    off = 0
    mix = None
    for mr in mix_refs:
        kw = mr.shape[-1]
        part = jnp.dot(mr[...], w_ref[off:off + kw, :], preferred_element_type=F32)
        mix = part if mix is None else mix + part
        off += kw
    z = DEEPNORM_ALPHA * x + m[2:3] * mix
    xn = _layer_norm_rows(z, lg_ref[...], lb_ref[...])
    xo_ref[...] = xn
    h2 = xn * (1.0 + m[4:5]) + m[3:4]
    h_ref[...] = h2
    lt_ref[...] = _dot3_nt(rw_ref[...], h2)


def _out_proj(lay, mixes, w_bf16, x, row_off, mod, ln_g, ln_b, router_wt, n_rows):
    tm, d = lay.tmo, D_MODEL
    row = lambda i: (i, 0)
    full = lambda i: (0, 0)
    x_tile_off = row_off // tm
    if isinstance(x, tuple):
        assert row_off == 0
        n_ctx_tiles = lay.rows_c // tm
        x_specs = _row_tile_specs(tm, n_ctx_tiles, d)
    else:
        n_ctx_tiles, x = None, (x,)
        x_specs = [pl.BlockSpec((tm, d), lambda i: (i + x_tile_off, 0))]
    in_specs = [pl.BlockSpec((tm, mx.shape[-1]), row) for mx in mixes]
    in_specs += [pl.BlockSpec((d, d), full)] + x_specs + [
                 pl.BlockSpec((1, 6, d), lambda i: (lay.mod_row(i, tm, row_off), 0, 0)),
                 pl.BlockSpec((1, d), full), pl.BlockSpec((1, d), full),
                 pl.BlockSpec((N_EXPERTS, d), full)]
    return pl.pallas_call(
        functools.partial(_out_proj_body, len(mixes), n_ctx_tiles),
        out_shape=(jax.ShapeDtypeStruct((n_rows, d), F32),
                   jax.ShapeDtypeStruct((n_rows, d), F32),
                   jax.ShapeDtypeStruct((N_EXPERTS, n_rows), F32)),
        grid=(n_rows // tm,),
        in_specs=in_specs,
        out_specs=(pl.BlockSpec((tm, d), row), pl.BlockSpec((tm, d), row),
                   pl.BlockSpec((N_EXPERTS, tm), lambda i: (0, i))),
        compiler_params=_cparams(("arbitrary",)),
        name="out_proj_ln",
    )(*mixes, w_bf16, *x, mod, ln_g.reshape(1, d), ln_b.reshape(1, d), router_wt)


_EXPERT_PAIRS = ((0, 1), (0, 2), (0, 3), (1, 2), (1, 3), (2, 3))
N_BUCKETS = N_GROUPS * len(_EXPERT_PAIRS)


def _router_body(lt_ref, rb_ref, o_ref):
    logits = lt_ref[...] + rb_ref[...]
    rows = [logits[e:e + 1, :] for e in range(N_EXPERTS)]
    m = rows[0]
    for x in rows[1:]:
        m = jnp.maximum(m, x)
    ex = [jnp.exp(x - m) for x in rows]
    z = ex[0]
    for x in ex[1:]:
        z = z + x
    p = [x / z for x in ex]

    gscore = []
    for g in range(N_GROUPS):
        a, b, c, d = p[4 * g:4 * g + 4]
        hi1, lo1 = jnp.maximum(a, b), jnp.minimum(a, b)
        hi2, lo2 = jnp.maximum(c, d), jnp.minimum(c, d)
        top1 = jnp.maximum(hi1, hi2)
        top2 = jnp.maximum(jnp.minimum(hi1, hi2), jnp.maximum(lo1, lo2))
        gscore.append(top1 + top2)
    best = []
    for g in range(N_GROUPS):
        ok = None
        for o in range(N_GROUPS):
            if o == g:
                continue
            c = (gscore[g] > gscore[o]) if o < g else (gscore[g] >= gscore[o])
            ok = c if ok is None else jnp.logical_and(ok, c)
        best.append(ok)
    won = []
    for e in range(N_EXPERTS):
        g = e // EXPERTS_PER_GROUP
        rank = jnp.zeros_like(p[e])
        for o in range(4 * g, 4 * g + 4):
            if o == e:
                continue
            ahead = (p[o] > p[e]) if o > e else (p[o] >= p[e])
            rank = rank + jnp.where(ahead, 1.0, 0.0)
        won.append(jnp.where(jnp.logical_and(best[g], rank < 1.5), 1.0, 0.0))
    tot = won[0] * p[0]
    for e in range(1, N_EXPERTS):
        tot = tot + won[e] * p[e]
    bucket = jnp.zeros_like(tot)
    gate_a = jnp.zeros_like(tot)
    gate_b = jnp.zeros_like(tot)
    for g in range(N_GROUPS):
        for pid, (a, b) in enumerate(_EXPERT_PAIRS):
            ind = won[4 * g + a] * won[4 * g + b]
            bucket = bucket + ind * float(len(_EXPERT_PAIRS) * g + pid)
            gate_a = gate_a + ind * p[4 * g + a]
            gate_b = gate_b + ind * p[4 * g + b]
    o_ref[...] = jnp.zeros_like(o_ref)
    o_ref[0:1, :] = bucket
    o_ref[1:2, :] = gate_a / tot
    o_ref[2:3, :] = gate_b / tot


def _router(logits_t, router_b):
    e, n = logits_t.shape
    tr = math.gcd(2048, n)
    return pl.pallas_call(
        _router_body,
        out_shape=jax.ShapeDtypeStruct((SUBLANES, n), F32),
        grid=(n // tr,),
        in_specs=[pl.BlockSpec((e, tr), lambda i: (0, i)), pl.BlockSpec((e, 1), lambda i: (0, 0))],
        out_specs=pl.BlockSpec((SUBLANES, tr), lambda i: (0, i)),
        compiler_params=_cparams(("arbitrary",)),
        name="router_gates",
    )(logits_t, router_b.reshape(e, 1))


GATE_LANES = LANES


def _moe_plan(bucket, tg):
    n = bucket.shape[0]
    n_tiles = n // tg + N_BUCKETS
    ids = jnp.arange(N_BUCKETS, dtype=jnp.int32)
    onehot = (bucket[:, None] == ids[None, :]).astype(jnp.int32)
    csum = jnp.cumsum(onehot, axis=0)
    counts = csum[-1]
    tiles = (counts + tg - 1) // tg
    tile_end = jnp.cumsum(tiles)
    row_start = (tile_end - tiles) * tg
    pos = jnp.sum(onehot * (row_start[None, :] + csum - 1), axis=1).astype(jnp.int32)
    n_used = tile_end[-1]
    tile_ids = jnp.minimum(jnp.arange(n_tiles, dtype=jnp.int32), n_used - 1)
    tile_bucket = jnp.sum((tile_ids[:, None] >= tile_end[None, :]).astype(jnp.int32), axis=1)
    pair_a = jnp.asarray([a for a, _ in _EXPERT_PAIRS], jnp.int32)
    pair_b = jnp.asarray([b for _, b in _EXPERT_PAIRS], jnp.int32)
    grp, pid = tile_bucket // len(_EXPERT_PAIRS), tile_bucket % len(_EXPERT_PAIRS)
    ea = grp * EXPERTS_PER_GROUP + pair_a[pid]
    eb = grp * EXPERTS_PER_GROUP + pair_b[pid]
    return pos, ea.astype(jnp.int32), eb.astype(jnp.int32), n_used.reshape(1).astype(jnp.int32), n_tiles


def _row_copy(src_ref, src_row, dst_ref, dst_row, sem):
    return pltpu.make_async_copy(src_ref.at[pl.ds(src_row, 1), :], dst_ref.at[pl.ds(dst_row, 1), :], sem)


ROW_DMA_UNROLL = 8


def _row_copies(n_rows, row_copy, whole_tile_copy):
    def issue(blk, carry):
        for u in range(ROW_DMA_UNROLL):
            row_copy(blk * ROW_DMA_UNROLL + u).start(priority=u % 2)
        return carry

    lax.fori_loop(0, n_rows // ROW_DMA_UNROLL, issue, 0)
    whole_tile_copy.wait()


def _dispatch_body(tg, pos_ref, h_ref, g_ref, xs_in_ref, xs_ref, aug_ref, sem):
    del xs_in_ref
    base = pl.program_id(0) * tg
    aug_ref[:, :D_MODEL] = h_ref[...]
    aug_ref[:, D_MODEL:] = g_ref[...]

    _row_copies(tg, lambda r: _row_copy(aug_ref, r, xs_ref, pos_ref[base + r], sem),
                pltpu.make_async_copy(aug_ref, xs_ref.at[pl.ds(0, tg), :], sem))


def _moe_dispatch(pos, h2, gate_rows, n_tiles, tg):
    n, d = h2.shape
    wide = d + GATE_LANES
    xs0 = jnp.zeros((n_tiles * tg, wide), F32)
    grid_spec = pltpu.PrefetchScalarGridSpec(
        num_scalar_prefetch=1,
        grid=(n // tg,),
        in_specs=[pl.BlockSpec((tg, d), lambda i, pos_ref: (i, 0)),
                  pl.BlockSpec((tg, GATE_LANES), lambda i, pos_ref: (i, 0)),
                  pl.BlockSpec(memory_space=pl.ANY)],
        out_specs=pl.BlockSpec(memory_space=pl.ANY),
        scratch_shapes=[pltpu.VMEM((tg, wide), F32), pltpu.SemaphoreType.DMA(())],
    )
    return pl.pallas_call(
        functools.partial(_dispatch_body, tg),
        out_shape=jax.ShapeDtypeStruct(xs0.shape, F32),
        grid_spec=grid_spec,
        input_output_aliases={3: 0},
        compiler_params=_cparams(("arbitrary",)),
        name="moe_dispatch",
    )(pos, h2, gate_rows, xs0)


def _grouped_body(ea_ref, eb_ref, nu_ref, xs_ref, w1a_ref, w3a_ref, w2a_ref, w1b_ref, w3b_ref, w2b_ref, y_ref):
    del ea_ref, eb_ref

    @pl.when(pl.program_id(0) < nu_ref[0])
    def _():
        x = xs_ref[:, :D_MODEL].astype(BF16)

        def expert(w1_ref, w3_ref, w2_ref, gate):
            h1 = jnp.dot(x, w1_ref[0], preferred_element_type=F32)
            h3 = jnp.dot(x, w3_ref[0], preferred_element_type=F32)
            hid = (h1 * _sigmoid(h1) * h3 * gate).astype(BF16)
            return jnp.dot(hid, w2_ref[0], preferred_element_type=F32)

        y_ref[...] = (expert(w1a_ref, w3a_ref, w2a_ref, xs_ref[:, D_MODEL:D_MODEL + 1])
                      + expert(w1b_ref, w3b_ref, w2b_ref, xs_ref[:, D_MODEL + 1:D_MODEL + 2]))

    @pl.when(pl.program_id(0) >= nu_ref[0])
    def _():
        y_ref[...] = jnp.zeros_like(y_ref)


def _moe_grouped(xs, ea, eb, n_used, w1, w3, w2, tg):
    rows, wide = xs.shape
    d, ff = D_MODEL, EXPERT_FF
    n_tiles = rows // tg
    tile = lambda i, ea_r, eb_r, nu_r: (jnp.minimum(i, nu_r[0] - 1), 0)
    up_a = pl.BlockSpec((1, d, ff), lambda i, ea_r, eb_r, nu_r: (ea_r[i], 0, 0))
    up_b = pl.BlockSpec((1, d, ff), lambda i, ea_r, eb_r, nu_r: (eb_r[i], 0, 0))
    dn_a = pl.BlockSpec((1, ff, d), lambda i, ea_r, eb_r, nu_r: (ea_r[i], 0, 0))
    dn_b = pl.BlockSpec((1, ff, d), lambda i, ea_r, eb_r, nu_r: (eb_r[i], 0, 0))
    grid_spec = pltpu.PrefetchScalarGridSpec(
        num_scalar_prefetch=3,
        grid=(n_tiles,),
        in_specs=[pl.BlockSpec((tg, wide), tile), up_a, up_a, dn_a, up_b, up_b, dn_b],
        out_specs=pl.BlockSpec((tg, d), lambda i, ea_r, eb_r, nu_r: (i, 0)),
    )
    return pl.pallas_call(
        _grouped_body,
        out_shape=jax.ShapeDtypeStruct((rows, d), F32),
        grid_spec=grid_spec,
        compiler_params=_cparams(("arbitrary",)),
        name="moe_grouped",
    )(ea, eb, n_used, xs, w1, w3, w2, w1, w3, w2)


def _combine_body(tg, pos_ref, ys_ref, x_ref, mod_ref, lg_ref, lb_ref, o_ref, buf_ref, sem):
    base = pl.program_id(0) * tg

    _row_copies(tg, lambda r: _row_copy(ys_ref, pos_ref[base + r], buf_ref, r, sem),
                pltpu.make_async_copy(ys_ref.at[pl.ds(0, tg), :], buf_ref, sem))
    m = mod_ref[0]
    z = DEEPNORM_ALPHA * x_ref[...] + m[5:6] * buf_ref[...]
    o_ref[...] = _layer_norm_rows(z, lg_ref[...], lb_ref[...])


def _moe_combine(lay, pos, ys, x, mod, ln_g, ln_b, row_off, tg):
    n, d = x.shape
    grid_spec = pltpu.PrefetchScalarGridSpec(
        num_scalar_prefetch=1,
        grid=(n // tg,),
        in_specs=[pl.BlockSpec(memory_space=pl.ANY),
                  pl.BlockSpec((tg, d), lambda i, pos_ref: (i, 0)),
                  pl.BlockSpec((1, 6, d), lambda i, pos_ref: (lay.mod_row(i, tg, row_off), 0, 0)),
                  pl.BlockSpec((1, d), lambda i, pos_ref: (0, 0)),
                  pl.BlockSpec((1, d), lambda i, pos_ref: (0, 0))],
        out_specs=pl.BlockSpec((tg, d), lambda i, pos_ref: (i, 0)),
        scratch_shapes=[pltpu.VMEM((tg, d), F32), pltpu.SemaphoreType.DMA(())],
    )
    return pl.pallas_call(
        functools.partial(_combine_body, tg),
        out_shape=jax.ShapeDtypeStruct((n, d), F32),
        grid_spec=grid_spec,
        compiler_params=_cparams(("arbitrary",)),
        name="moe_combine_ln",
    )(pos, ys, x, mod, ln_g.reshape(1, d), ln_b.reshape(1, d))


def _moe(lay, h2, routed, w1, w3, w2, x, mod, ln_g, ln_b, row_off):
    tg = lay.tm
    bucket = routed[0].astype(jnp.int32)
    gate_rows = jnp.pad(routed[1:3].T, ((0, 0), (0, GATE_LANES - 2)))
    pos, ea, eb, n_used, n_tiles = _moe_plan(bucket, tg)
    xs = _moe_dispatch(pos, h2, gate_rows, n_tiles, tg)
    ys = _moe_grouped(xs, ea, eb, n_used, w1, w3, w2, tg)
    return _moe_combine(lay, pos, ys, x, mod, ln_g, ln_b, row_off, tg)


def kernel(x, c, ctx, c_ctx, router_w, router_b, ada_w, ada_b, ln1_g, ln1_b, ln2_g, ln2_b, moe_w1, moe_w3, moe_w2, ev_w_in, ev_w_out, ev_a_mu, ev_a_w0, ev_a_w2, ev_a_a0, ev_a_a2, ev_a_g2, ev_a_kk, ev_a_ka, ev_a_rk, ev_a_lnx_g, ev_a_lnx_b, ev_b_lam, ev_b_subln_g, od_w_in, od_w_out, od_qn_g, od_kn_g):
    bsz, seq, d = x.shape
    ctx_len = ctx.shape[1]
    assert d == D_MODEL and ada_w.shape[0] == DEPTH and seq % GRID_W == 0
    lay = _Layout(bsz, ctx_len, seq)

    cvec = jnp.zeros((lay.mod_rows, d), F32).at[:bsz].set(c).at[bsz].set(c_ctx)
    mods = _ada_mods(cvec, ada_w, ada_b).reshape(DEPTH, lay.mod_rows, 6, d)

    xs = (ctx.reshape(lay.rows_c, d), x.reshape(lay.rows_l, d))
    router_wt = router_w.T

    cos_b, sin_b = _rope_tables(lay, B_HEAD_DIM, B_QK)
    cos_c, sin_c = _rope_tables(lay, C_HEAD_DIM, C_HEAD_DIM)

    for i in range(DEPTH):
        last = i == DEPTH - 1
        j = i // 2
        mod = mods[i]
        if i % 2 == 0:
            lambda_init = 0.8 - 0.6 * math.exp(-0.3 * i)
            x_ctx, x_lat = xs if isinstance(xs, tuple) else (xs[:lay.rows_c], xs[lay.rows_c:])
            pa, q, k, v = _even_in_proj(lay, x_ctx, x_lat, mod, ev_w_in[j].astype(BF16), cos_b, sin_b)
            decay, scan_ops, g_, bonus = _rwkv_features(
                lay, pa, ev_a_mu[j], ev_a_w0[j], ev_a_w2[j], ev_a_a0[j], ev_a_a2[j], ev_a_g2[j],
                ev_a_kk[j], ev_a_ka[j], ev_a_rk[j].reshape(-1))
            y_f, y_b = _wkv_scan(lay, _to_chains(lay, decay[None]), _to_chains(lay, scan_ops))
            ya = _rwkv_readout(lay, _from_chains(lay, y_f, y_b), bonus, g_, ev_a_lnx_g[j], ev_a_lnx_b[j])
            yd = _diff_attention(lay, q, k, v, ev_b_lam[j], ev_b_subln_g[j], lambda_init)
            mixes, w_out = [ya, yd], ev_w_out[j]
            n_rows, tile_off = (lay.rows_l, lay.ntiles_c) if last else (lay.rows, 0)
            if last:
                mixes = [mx[lay.rows_c:] for mx in mixes]
        else:
            xs = jnp.concatenate(xs, axis=0) if isinstance(xs, tuple) else xs
            q, k, v = _odd_in_proj(lay, xs, mod, od_w_in[j].astype(BF16), cos_c, sin_c, od_qn_g[j], od_kn_g[j])
            assert last, "an odd layer that must also update the context stream is not supported"
            o = _gqa_attention(lay, q, k, v)
            mixes, w_out = [o], od_w_out[j]
            n_rows, tile_off = lay.rows_l, lay.ntiles_c
        row_off = tile_off * lay.tm
        if isinstance(xs, tuple) and row_off:
            xs = jnp.concatenate(xs, axis=0)
        x_new, h2, logits_t = _out_proj(lay, mixes, w_out.astype(BF16), xs, row_off, mod,
                                        ln1_g[i], ln1_b[i], router_wt, n_rows)
        routed = _router(logits_t, router_b)
        xs = _moe(lay, h2, routed, moe_w1[i].astype(BF16), moe_w3[i].astype(BF16), moe_w2[i].astype(BF16),
                  x_new, mod, ln2_g[i], ln2_b[i], row_off)
    return xs.reshape(bsz, seq, d)
```

```python
import functools
import math

import jax
import jax.numpy as jnp
from jax import lax
from jax.experimental import pallas as pl
from jax.experimental.pallas import tpu as pltpu

F32 = jnp.float32
BF16 = jnp.bfloat16

D_MODEL = 1024
DEPTH = 2
GRID_W = 64
ROPE_THETA = 10000.0
LN_EPS = 1e-5
LOG2_E = 1.4426950408889634
DEEPNORM_ALPHA = (2 * DEPTH) ** 0.25

A_HEAD_DIM = 64
A_HEADS = 8
A_WIDTH = 512
A_LORA = 64
A_GATE_LORA = 128
A_GN_EPS = 64e-5
A_IN = 3 * A_WIDTH + 4 * A_LORA + A_GATE_LORA
_SCAN_R, _SCAN_K, _SCAN_V, _SCAN_A, _SCAN_B = range(5)
N_SCAN_OPS = 5

B_HEAD_DIM = 64
B_V_DIM = 128
B_HEADS = 4
B_WIDTH = 512
B_QK = 512
B_QK_PAD = (B_QK // B_HEAD_DIM) * 128
B_SUBLN_EPS = 1e-5
EVEN_IN = A_IN + 2 * B_QK + B_WIDTH

C_HEAD_DIM = 128
C_HEADS = 8
C_KV_HEADS = 2
C_GROUP = 4
C_Q = 1024
C_KV = 256
ODD_IN = C_Q + 2 * C_KV
QK_NORM_EPS = 1e-6

N_EXPERTS = 16
N_GROUPS = 4
EXPERTS_PER_GROUP = 4
EXPERT_FF = 512

VMEM_LIMIT_BYTES = 56 * 1024 * 1024
LANES = 128
SUBLANES = 8


def _cparams(sem):
    return pltpu.CompilerParams(dimension_semantics=sem, vmem_limit_bytes=VMEM_LIMIT_BYTES)


def _dot(a, b):
    return jnp.dot(a.astype(BF16), b.astype(BF16), preferred_element_type=F32)


def _dot_nt(a, b):
    return lax.dot_general(a.astype(BF16), b.astype(BF16), (((1,), (1,)), ((), ())),
                           preferred_element_type=F32)


def _split(a):
    hi = a.astype(BF16)
    lo = (a - hi.astype(F32)).astype(BF16)
    return hi, lo


def _dot3(a, b):
    ah, al = _split(a)
    bh, bl = _split(b)
    return (jnp.dot(ah, bh, preferred_element_type=F32)
            + (jnp.dot(ah, bl, preferred_element_type=F32)
               + jnp.dot(al, bh, preferred_element_type=F32)))


def _dot2_exact_rhs(a, b_bf16):
    ah, al = _split(a)
    return jnp.dot(ah, b_bf16, preferred_element_type=F32) + jnp.dot(al, b_bf16, preferred_element_type=F32)


def _dot3_nt(a, b):
    ah, al = _split(a)
    bh, bl = _split(b)
    dn = (((1,), (1,)), ((), ()))
    return (lax.dot_general(ah, bh, dn, preferred_element_type=F32)
            + (lax.dot_general(ah, bl, dn, preferred_element_type=F32)
               + lax.dot_general(al, bh, dn, preferred_element_type=F32)))


def _sigmoid(x):
    return 1.0 / (1.0 + jnp.exp(-x))


def _layer_norm_rows(z, g, b):
    mu = jnp.mean(z, axis=-1, keepdims=True)
    zc = z - mu
    var = jnp.mean(zc * zc, axis=-1, keepdims=True)
    return zc * lax.rsqrt(var + LN_EPS) * g + b


class _Layout:
    def __init__(self, bsz, ctx_len, seq):
        self.B, self.CTX, self.S = bsz, ctx_len, seq
        self.T = ctx_len + seq
        self.tm = math.gcd(256, math.gcd(ctx_len, seq))
        self.nct = ctx_len // self.tm
        self.nlt = seq // self.tm
        self.rows_c = bsz * ctx_len
        self.rows_l = bsz * seq
        self.rows = self.rows_c + self.rows_l
        self.ntiles_c = bsz * self.nct
        self.ntiles = self.rows // self.tm
        assert self.rows_c % seq == 0, "latent K/V blocks are addressed in units of S rows"
        self.tb = math.gcd(32, math.gcd(ctx_len, seq))
        self.tmo = math.gcd(512, math.gcd(self.rows_c, seq))
        self.mod_rows = -(-(bsz + 1) // SUBLANES) * SUBLANES

    def seq_tile(self, b, j):
        return jnp.where(j < self.nct, b * self.nct + j, self.ntiles_c + b * self.nlt + (j - self.nct))

    def seq_block(self, i):
        il = i - self.ntiles_c
        return (jnp.where(i < self.ntiles_c, i // self.nct, il // self.nlt),
                jnp.where(i < self.ntiles_c, i % self.nct, self.nct + il % self.nlt))

    def mod_row(self, i, tile, row_offset=0):
        r = i * tile + row_offset
        return jnp.where(r < self.rows_c, self.B, (r - self.rows_c) // self.S)

    def pos_tile(self, i):
        il = i - self.ntiles_c
        return jnp.where(i < self.ntiles_c, i % self.nct, self.nct + il % self.nlt)


def _ada_body(cv_ref, w_ref, b_ref, o_ref):
    cv = cv_ref[...]
    s = cv * _sigmoid(cv)
    o_ref[0] = _dot3(s, w_ref[0]) + b_ref[0]


def _ada_mods(cvec, ada_w, ada_b):
    depth, d, n = ada_w.shape
    r = cvec.shape[0]
    tn = 512
    return pl.pallas_call(
        _ada_body,
        out_shape=jax.ShapeDtypeStruct((depth, r, n), F32),
        grid=(depth, n // tn),
        in_specs=[pl.BlockSpec((r, d), lambda l, j: (0, 0)),
                  pl.BlockSpec((1, d, tn), lambda l, j: (l, 0, j)),
                  pl.BlockSpec((1, 1, tn), lambda l, j: (l, 0, j))],
        out_specs=pl.BlockSpec((1, r, tn), lambda l, j: (l, 0, j)),
        compiler_params=_cparams(("arbitrary", "arbitrary")),
        name="ada_mods",
    )(cvec, ada_w, ada_b.reshape(depth, 1, n))


def _rope_tables(lay, head_dim, width):
    rows = lay.S // GRID_W
    rr, cc = jnp.meshgrid(jnp.arange(rows), jnp.arange(GRID_W), indexing="ij")
    row_pos = rr.reshape(-1).astype(F32)
    col_pos = cc.reshape(-1).astype(F32)
    axis_dim = head_dim // 2
    inv = ROPE_THETA ** (-jnp.arange(0, axis_dim, 2, dtype=F32) / axis_dim)
    ang = jnp.concatenate([row_pos[:, None] * inv, col_pos[:, None] * inv], -1)
    cos, sin = jnp.cos(ang), jnp.sin(ang)
    cos = jnp.concatenate([jnp.ones((lay.CTX, head_dim // 2), F32), cos], 0)
    sin = jnp.concatenate([jnp.zeros((lay.CTX, head_dim // 2), F32), sin], 0)
    cos_h = jnp.concatenate([cos, cos], -1)
    sin_h = jnp.concatenate([-sin, sin], -1)
    reps = width // head_dim
    return jnp.tile(cos_h, (1, reps)), jnp.tile(sin_h, (1, reps))


def _rope_lanes(x, cos, sin, head_dim):
    w = x.shape[-1]
    half = head_dim // 2
    if head_dim == LANES and w == LANES:
        rot = pltpu.roll(x, half, 1)
    else:
        fwd = pltpu.roll(x, w - half, 1)
        bwd = pltpu.roll(x, half, 1)
        lane = lax.broadcasted_iota(jnp.int32, x.shape, 1)
        rot = jnp.where((lane % head_dim) < half, fwd, bwd)
    return x * cos + rot * sin


def _row_tile(n_ctx_tiles, xc_ref, xl_ref):
    return jnp.where(pl.program_id(0) < n_ctx_tiles, xc_ref[...], xl_ref[...])


def _row_tile_specs(tile, n_ctx_tiles, d):
    return [pl.BlockSpec((tile, d), lambda i: (jnp.minimum(i, n_ctx_tiles - 1), 0)),
            pl.BlockSpec((tile, d), lambda i: (jnp.maximum(i - n_ctx_tiles, 0), 0))]


def _even_in_body(n_ctx_tiles, xc_ref, xl_ref, mod_ref, w_ref, cos_ref, sin_ref, pa_ref, q_ref, k_ref, v_ref):
    m = mod_ref[0]
    h = (_row_tile(n_ctx_tiles, xc_ref, xl_ref) * (1.0 + m[1:2]) + m[0:1]).astype(BF16)
    pa_ref[...] = jnp.dot(h, w_ref[:, :A_IN], preferred_element_type=F32)
    cos, sin = cos_ref[...], sin_ref[...]
    o = A_IN

    def put_maps(dst_ref, val):
        pad = jnp.zeros((val.shape[0], LANES - B_HEAD_DIM), BF16)
        for mp in range(B_QK // B_HEAD_DIM):
            piece = val[:, mp * B_HEAD_DIM:(mp + 1) * B_HEAD_DIM].astype(BF16)
            dst_ref[:, mp * LANES:(mp + 1) * LANES] = jnp.concatenate([piece, pad], axis=1)

    q = jnp.dot(h, w_ref[:, o:o + B_QK], preferred_element_type=F32)
    put_maps(q_ref, _rope_lanes(q, cos, sin, B_HEAD_DIM) * (LOG2_E * B_HEAD_DIM ** -0.5))
    o += B_QK
    k = jnp.dot(h, w_ref[:, o:o + B_QK], preferred_element_type=F32)
    put_maps(k_ref, _rope_lanes(k, cos, sin, B_HEAD_DIM))
    o += B_QK
    v_ref[...] = jnp.dot(h, w_ref[:, o:o + B_WIDTH], preferred_element_type=F32).astype(BF16)


def _even_in_proj(lay, x_ctx, x_lat, mod, w_bf16, cos, sin):
    tm, d = lay.tm, D_MODEL
    row = lambda i: (i, 0)
    return pl.pallas_call(
        functools.partial(_even_in_body, lay.ntiles_c),
        out_shape=(jax.ShapeDtypeStruct((lay.rows, A_IN), F32),
                   jax.ShapeDtypeStruct((lay.rows, B_QK_PAD), BF16),
                   jax.ShapeDtypeStruct((lay.rows, B_QK_PAD), BF16),
                   jax.ShapeDtypeStruct((lay.rows, B_WIDTH), BF16)),
        grid=(lay.ntiles,),
        in_specs=_row_tile_specs(tm, lay.ntiles_c, d) + [
                  pl.BlockSpec((1, 6, d), lambda i: (lay.mod_row(i, tm), 0, 0)),
                  pl.BlockSpec((d, EVEN_IN), lambda i: (0, 0)),
                  pl.BlockSpec((tm, B_QK), lambda i: (lay.pos_tile(i), 0)),
                  pl.BlockSpec((tm, B_QK), lambda i: (lay.pos_tile(i), 0))],
        out_specs=(pl.BlockSpec((tm, A_IN), row), pl.BlockSpec((tm, B_QK_PAD), row),
                   pl.BlockSpec((tm, B_QK_PAD), row), pl.BlockSpec((tm, B_WIDTH), row)),
        compiler_params=_cparams(("arbitrary",)),
        name="even_in_proj",
    )(x_ctx, x_lat, mod, w_bf16, cos, sin)


def _rwkv_feat_body(lay, pa_ref, prev_ref, next_ref, mu_ref, w0_ref, w2_ref, a0_ref, a2_ref, g2_ref,
                    kk_ref, ka_ref, rk_ref, bd_ref, tri_ref, blk_ref,
                    pend_ref, ops_ref, g_ref, bonus_ref):
    def put(idx, d, val):
        ops_ref[idx, d, 0] = val.astype(BF16)

    i = pl.program_id(0)
    tm = lay.tm
    il = i - lay.ntiles_c
    in_ctx = i < lay.ntiles_c
    seg_first = jnp.where(in_ctx, i % lay.nct == 0, il % lay.nlt == 0)
    seg_last = jnp.where(in_ctx, i % lay.nct == lay.nct - 1, il % lay.nlt == lay.nlt - 1)

    pa = pa_ref[...]
    row = lax.broadcasted_iota(jnp.int32, pa.shape, 0)
    prev_edge = jnp.where(seg_first, 0.0, 1.0) * prev_ref[SUBLANES - 1:SUBLANES, :]
    next_edge = jnp.where(seg_last, 0.0, 1.0) * next_ref[0:1, :]
    prev = jnp.where(row == 0, prev_edge, pltpu.roll(pa, 1, 0))
    nxt = jnp.where(row == tm - 1, next_edge, pltpu.roll(pa, tm - 1, 0))
    u = pa + (0.5 * (prev + nxt) - pa) * mu_ref[...]

    o1, o2, o3 = A_WIDTH, 2 * A_WIDTH, 3 * A_WIDTH
    o4 = o3 + 2 * A_LORA
    o5 = o4 + 2 * A_LORA
    r, k, v = u[:, :o1], u[:, o1:o2], u[:, o2:o3]
    bd = bd_ref[...]

    kk = k * kk_ref[...]
    ss = _dot2_exact_rhs(kk * kk, bd)
    kkn = kk / jnp.maximum(jnp.sqrt(ss), 1e-12)
    g = _dot3(_sigmoid(u[:, o5:]), g2_ref[...])

    g_ref[0] = g

    kd_sum = jnp.zeros_like(k)
    for d in range(2):
        wd = u[:, o3 + d * A_LORA:o3 + (d + 1) * A_LORA]
        ad = u[:, o4 + d * A_LORA:o4 + (d + 1) * A_LORA]
        log_decay = -math.exp(-0.5) * _sigmoid(w0_ref[d:d + 1, :] + _dot3(jnp.tanh(wd), w2_ref[d]))
        ld_hi, ld_lo = _split(log_decay)
        cum = (jnp.dot(tri_ref[d], ld_hi, preferred_element_type=F32)
               + jnp.dot(tri_ref[d], ld_lo, preferred_element_type=F32))
        tot = (jnp.dot(blk_ref[...], ld_hi, preferred_element_type=F32)
               + jnp.dot(blk_ref[...], ld_lo, preferred_element_type=F32))
        pend_ref[d, 0] = jnp.exp(tot)
        p = jnp.exp(cum)
        p_inv = 1.0 / p
        a = _sigmoid(a0_ref[d:d + 1, :] + _dot3(ad, a2_ref[d]))
        kd = k * (1.0 + (a - 1.0) * ka_ref[...])
        put(_SCAN_R, d, r * p)
        put(_SCAN_V, d, v)
        put(_SCAN_A, d, -kkn * jnp.exp(cum - log_decay))
        put(_SCAN_K, d, kd * p_inv)
        put(_SCAN_B, d, kkn * a * p_inv)
        kd_sum = kd_sum + kd
    bonus_ref[0] = _dot2_exact_rhs(r * kd_sum * rk_ref[...], bd) * v


def _head_block_diag(width, head_dim):
    h = jnp.arange(width) // head_dim
    return (h[:, None] == h[None, :]).astype(BF16)


def _rwkv_features(lay, pa, mu, w0, w2, a0, a2, g2, k_k, k_a, r_k):
    tm = lay.tm
    hb = tm // SUBLANES
    nb8 = lay.rows // SUBLANES
    row = lambda i: (i, 0)
    full2 = lambda i: (0, 0)
    full3 = lambda i: (0, 0, 0)
    w = A_WIDTH
    out = jax.ShapeDtypeStruct((lay.B, lay.T, w), F32)
    seq = lambda i: lay.seq_block(i) + (0,)
    tb = lay.tb
    nblk = tm // tb
    pos = jnp.arange(tm)
    same_blk = (pos[:, None] // tb) == (pos[None, :] // tb)
    tri = jnp.stack([same_blk & (pos[None, :] <= pos[:, None]),
                     same_blk & (pos[None, :] >= pos[:, None])]).astype(BF16)
    blk = (jnp.arange(nblk)[:, None] == (pos[None, :] // tb)).astype(BF16)
    return pl.pallas_call(
        functools.partial(_rwkv_feat_body, lay),
        out_shape=(jax.ShapeDtypeStruct((2, lay.B, lay.T // tb, w), F32),
                   jax.ShapeDtypeStruct((N_SCAN_OPS, 2, lay.B, lay.T, w), BF16), out, out),
        grid=(lay.ntiles,),
        in_specs=[pl.BlockSpec((tm, A_IN), row),
                  pl.BlockSpec((SUBLANES, A_IN), lambda i: (jnp.maximum(i * hb - 1, 0), 0)),
                  pl.BlockSpec((SUBLANES, A_IN), lambda i: (jnp.minimum((i + 1) * hb, nb8 - 1), 0)),
                  pl.BlockSpec((1, A_IN), full2),
                  pl.BlockSpec((2, w), full2),
                  pl.BlockSpec((2, A_LORA, w), full3),
                  pl.BlockSpec((2, w), full2),
                  pl.BlockSpec((2, A_LORA, w), full3),
                  pl.BlockSpec((A_GATE_LORA, w), full2),
                  pl.BlockSpec((1, w), full2),
                  pl.BlockSpec((1, w), full2),
                  pl.BlockSpec((1, w), full2),
                  pl.BlockSpec((w, w), full2),
                  pl.BlockSpec((2, tm, tm), full3),
                  pl.BlockSpec((nblk, tm), full2)],
        out_specs=(pl.BlockSpec((2, 1, nblk, w), lambda i: (0,) + lay.seq_block(i) + (0,)),
                   pl.BlockSpec((N_SCAN_OPS, 2, 1, tm, w), lambda i: (0, 0) + lay.seq_block(i) + (0,)),
                   pl.BlockSpec((1, tm, w), seq), pl.BlockSpec((1, tm, w), seq)),
        compiler_params=_cparams(("arbitrary",)),
        name="rwkv_features",
    )(pa, pa, pa, mu.reshape(1, A_IN), w0, w2, a0, a2, g2, k_k.reshape(1, w), k_a.reshape(1, w),
      r_k.reshape(1, w), _head_block_diag(w, A_HEAD_DIM), tri, blk)


def _wkv_scan_body(tb, pf_ref, pb_ref, fwd_ref, bwd_ref, yf_ref, yb_ref, s_ref, m_ref):
    n = A_HEAD_DIM
    chains = s_ref.shape[-1]
    is_fwd = lax.broadcasted_iota(jnp.int32, (n, chains), 1) < chains // 2

    @pl.when(pl.program_id(0) == 0)
    def _():
        s_ref[...] = jnp.zeros_like(s_ref)

    def step(t, carry):
        tr = tb - 1 - t
        for idx in range(N_SCAN_OPS):
            m_ref[idx] = jnp.where(is_fwd, fwd_ref[t, idx], bwd_ref[tr, idx]).astype(F32)
        sa = jnp.zeros((n, chains), F32)
        for kk in range(n):
            sa = sa + s_ref[kk] * m_ref[_SCAN_A, kk:kk + 1, :]
        v_t = m_ref[_SCAN_V]
        y = jnp.zeros((n, chains), F32)
        for kk in range(n):
            s_new = s_ref[kk] + sa * m_ref[_SCAN_B, kk:kk + 1, :] + v_t * m_ref[_SCAN_K, kk:kk + 1, :]
            s_ref[kk] = s_new
            y = y + s_new * m_ref[_SCAN_R, kk:kk + 1, :]
        yf_ref[t] = y.astype(BF16)
        yb_ref[tr] = y.astype(BF16)
        return carry

    lax.fori_loop(0, tb, step, 0)
    m_ref[0] = jnp.where(is_fwd, pf_ref[0, 0], pb_ref[0, 0])
    for kk in range(n):
        s_ref[kk] = s_ref[kk] * m_ref[0, kk:kk + 1, :]


def _wkv_scan(lay, block_decay, ops):
    t, nops, n, lanes = ops.shape
    tb = lay.tb
    nctb, nt = lay.CTX // tb, t // tb
    bwd_tile = lambda g: jnp.where(g < nctb, nctb - 1 - g, nt - 1 - (g - nctb))
    out = jax.ShapeDtypeStruct((t, n, lanes), BF16)
    fwd_map = lambda g: (g, 0, 0, 0)
    bwd_map = lambda g: (bwd_tile(g), 0, 0, 0)
    return pl.pallas_call(
        functools.partial(_wkv_scan_body, tb),
        out_shape=(out, out),
        grid=(nt,),
        in_specs=[pl.BlockSpec((1, 1, n, lanes), fwd_map), pl.BlockSpec((1, 1, n, lanes), bwd_map),
                  pl.BlockSpec((tb, nops, n, lanes), fwd_map), pl.BlockSpec((tb, nops, n, lanes), bwd_map)],
        out_specs=(pl.BlockSpec((tb, n, lanes), lambda g: (g, 0, 0)),
                   pl.BlockSpec((tb, n, lanes), lambda g: (bwd_tile(g), 0, 0))),
        scratch_shapes=[pltpu.VMEM((n, n, lanes), F32), pltpu.VMEM((nops, n, lanes), F32)],
        compiler_params=_cparams(("arbitrary",)),
        name="wkv7_scan",
    )(block_decay, block_decay, ops, ops)


def _to_chains(lay, ops):
    h, n = A_HEADS, A_HEAD_DIM
    nops, length = ops.shape[0], ops.shape[3]
    x = ops.reshape(nops, 2, lay.B, length, h, n).transpose(3, 0, 5, 1, 2, 4)
    return x.reshape(length, nops, n, 2 * lay.B * h)


def _from_chains(lay, y_f, y_b):
    h, n = A_HEADS, A_HEAD_DIM
    low = jnp.arange(2 * lay.B * h) < lay.B * h
    y = jnp.where(low, y_f, y_b)
    return y.reshape(lay.T, n, 2, lay.B, h).transpose(2, 3, 0, 4, 1).reshape(2, lay.B, lay.T, h * n)


def _rwkv_readout_body(yf_ref, yb_ref, bonus_ref, g_ref, lg_ref, lb_ref, bd_ref, o_ref):
    y = yf_ref[0, 0].astype(F32) + yb_ref[0, 0].astype(F32)
    bd = bd_ref[...]
    inv_n = 1.0 / A_HEAD_DIM
    mu = _dot2_exact_rhs(y, bd) * inv_n
    yc = y - mu
    var = _dot2_exact_rhs(yc * yc, bd) * inv_n
    yn = yc * lax.rsqrt(var + A_GN_EPS) * lg_ref[...] + lb_ref[...]
    o_ref[...] = ((yn + bonus_ref[0]) * g_ref[0]).astype(BF16)


def _rwkv_readout(lay, y2, bonus, g, lnx_g, lnx_b):
    tm, w = lay.tm, A_WIDTH
    row = lambda i: (i, 0)
    full = lambda i: (0, 0)
    seq = lambda i: lay.seq_block(i) + (0,)
    return pl.pallas_call(
        _rwkv_readout_body,
        out_shape=jax.ShapeDtypeStruct((lay.rows, w), BF16),
        grid=(lay.ntiles,),
        in_specs=[pl.BlockSpec((1, 1, tm, w), lambda i: (0,) + seq(i)),
                  pl.BlockSpec((1, 1, tm, w), lambda i: (1,) + seq(i)),
                  pl.BlockSpec((1, tm, w), seq), pl.BlockSpec((1, tm, w), seq),
                  pl.BlockSpec((1, w), full), pl.BlockSpec((1, w), full), pl.BlockSpec((w, w), full)],
        out_specs=pl.BlockSpec((tm, w), row),
        compiler_params=_cparams(("arbitrary",)),
        name="rwkv_readout",
    )(y2, y2, bonus, g, lnx_g.reshape(1, w), lnx_b.reshape(1, w), _head_block_diag(w, A_HEAD_DIM))


def _exp2_scores(q, keys):
    scores = [_dot_nt(q, kk) for kk in keys]
    m = scores[0].max(axis=-1, keepdims=True)
    for s in scores[1:]:
        m = jnp.maximum(m, s.max(axis=-1, keepdims=True))
    return [jnp.exp2(s - m) for s in scores]


def _with_ones_column(v):
    lane = lax.broadcasted_iota(jnp.int32, (v.shape[0], LANES), 1)
    return jnp.concatenate([v, jnp.where(lane == 0, 1.0, 0.0).astype(BF16)], axis=1)


def _softmax_pv(q, keys, vals_aug):
    e_dim = vals_aug[0].shape[-1] - LANES
    o = None
    for p, vv in zip(_exp2_scores(q, keys), vals_aug):
        part = jnp.dot(p.astype(BF16), vv, preferred_element_type=F32)
        o = part if o is None else o + part
    return o[:, :e_dim] / o[:, e_dim:e_dim + 1]


def _diff_attn_body(lay, lambda_init, q_ref, kc_ref, kl_ref, vc_ref, vl_ref, lam_ref, g_ref, o_ref):
    j = pl.program_id(1)
    lv = lam_ref[...]
    lam = (jnp.exp(jnp.sum(lv[0:1] * lv[1:2], axis=1, keepdims=True))
           - jnp.exp(jnp.sum(lv[2:3] * lv[3:4], axis=1, keepdims=True)) + lambda_init)

    def run(with_latent):
        for h in range(B_HEADS):
            vs = slice(h * B_V_DIM, (h + 1) * B_V_DIM)
            vals = [_with_ones_column(vc_ref[:, vs])] + ([_with_ones_column(vl_ref[:, vs])] if with_latent else [])
            outs = []
            for mi in range(2):
                cs = slice((2 * h + mi) * LANES, (2 * h + mi + 1) * LANES)
                keys = [kc_ref[:, cs]] + ([kl_ref[:, cs]] if with_latent else [])
                outs.append(_softmax_pv(q_ref[:, cs], keys, vals))
            o = outs[0] - lam * outs[1]
            ms = jnp.mean(o * o, axis=-1, keepdims=True)
            o = o * lax.rsqrt(ms + B_SUBLN_EPS) * g_ref[...] * (1.0 - lambda_init)
            o_ref[:, vs] = o.astype(BF16)

    @pl.when(j < lay.nct)
    def _():
        run(False)

    @pl.when(j >= lay.nct)
    def _():
        run(True)


def _diff_attention(lay, q, k, v, lam_vecs, subln_g, lambda_init):
    tm = lay.tm
    w = B_WIDTH
    lat0 = lay.rows_c // lay.S
    return pl.pallas_call(
        functools.partial(_diff_attn_body, lay, lambda_init),
        out_shape=jax.ShapeDtypeStruct((lay.rows, w), BF16),
        grid=(lay.B, lay.nct + lay.nlt),
        in_specs=[pl.BlockSpec((tm, B_QK_PAD), lambda b, j: (lay.seq_tile(b, j), 0)),
                  pl.BlockSpec((lay.CTX, B_QK_PAD), lambda b, j: (b, 0)),
                  pl.BlockSpec((lay.S, B_QK_PAD), lambda b, j: (lat0 + b, 0)),
                  pl.BlockSpec((lay.CTX, w), lambda b, j: (b, 0)),
                  pl.BlockSpec((lay.S, w), lambda b, j: (lat0 + b, 0)),
                  pl.BlockSpec((4, B_HEAD_DIM), lambda b, j: (0, 0)),
                  pl.BlockSpec((1, B_V_DIM), lambda b, j: (0, 0))],
        out_specs=pl.BlockSpec((tm, w), lambda b, j: (lay.seq_tile(b, j), 0)),
        compiler_params=_cparams(("arbitrary", "arbitrary")),
        name="diff_attention",
    )(q, k, k, v, v, lam_vecs, subln_g.reshape(1, B_V_DIM))


def _odd_in_body(x_ref, mod_ref, w_ref, cos_ref, sin_ref, qn_ref, kn_ref, q_ref, k_ref, v_ref):
    m = mod_ref[0]
    h = (x_ref[...] * (1.0 + m[1:2]) + m[0:1]).astype(BF16)
    cos, sin = cos_ref[...], sin_ref[...]

    def norm_rope(p, g, scale):
        ms = jnp.mean(p * p, axis=-1, keepdims=True)
        y = p * lax.rsqrt(ms + QK_NORM_EPS) * g
        return (_rope_lanes(y, cos, sin, C_HEAD_DIM) * scale).astype(BF16)

    p = jnp.dot(h, w_ref[...], preferred_element_type=F32)
    for hd in range(C_HEADS):
        cs = slice(hd * C_HEAD_DIM, (hd + 1) * C_HEAD_DIM)
        q_ref[:, cs] = norm_rope(p[:, cs], qn_ref[...], LOG2_E * C_HEAD_DIM ** -0.5)
    for hd in range(C_KV_HEADS):
        cs = slice(hd * C_HEAD_DIM, (hd + 1) * C_HEAD_DIM)
        k_ref[:, cs] = norm_rope(p[:, C_Q + hd * C_HEAD_DIM:C_Q + (hd + 1) * C_HEAD_DIM], kn_ref[...], 1.0)
    v_ref[...] = p[:, C_Q + C_KV:].astype(BF16)


def _odd_in_proj(lay, x, mod, w_bf16, cos, sin, qn_g, kn_g):
    tm, d = lay.tm, D_MODEL
    row = lambda i: (i, 0)
    full = lambda i: (0, 0)
    return pl.pallas_call(
        _odd_in_body,
        out_shape=(jax.ShapeDtypeStruct((lay.rows, C_Q), BF16),
                   jax.ShapeDtypeStruct((lay.rows, C_KV), BF16),
                   jax.ShapeDtypeStruct((lay.rows, C_KV), BF16)),
        grid=(lay.ntiles,),
        in_specs=[pl.BlockSpec((tm, d), row),
                  pl.BlockSpec((1, 6, d), lambda i: (lay.mod_row(i, tm), 0, 0)),
                  pl.BlockSpec((d, ODD_IN), full),
                  pl.BlockSpec((tm, C_HEAD_DIM), lambda i: (lay.pos_tile(i), 0)),
                  pl.BlockSpec((tm, C_HEAD_DIM), lambda i: (lay.pos_tile(i), 0)),
                  pl.BlockSpec((1, C_HEAD_DIM), full),
                  pl.BlockSpec((1, C_HEAD_DIM), full)],
        out_specs=(pl.BlockSpec((tm, C_Q), row), pl.BlockSpec((tm, C_KV), row), pl.BlockSpec((tm, C_KV), row)),
        compiler_params=_cparams(("arbitrary",)),
        name="odd_in_proj",
    )(x, mod, w_bf16, cos, sin, qn_g.reshape(1, C_HEAD_DIM), kn_g.reshape(1, C_HEAD_DIM))


def _gqa_body(q_ref, kc_ref, kl_ref, vc_ref, vl_ref, o_ref):
    for kvh in range(C_KV_HEADS):
        ks = slice(kvh * C_HEAD_DIM, (kvh + 1) * C_HEAD_DIM)
        keys = [kc_ref[:, ks], kl_ref[:, ks]]
        vals = [_with_ones_column(vc_ref[:, ks]), _with_ones_column(vl_ref[:, ks])]
        for g in range(C_GROUP):
            hd = kvh * C_GROUP + g
            cs = slice(hd * C_HEAD_DIM, (hd + 1) * C_HEAD_DIM)
            o_ref[:, cs] = _softmax_pv(q_ref[:, cs], keys, vals).astype(BF16)


def _gqa_attention(lay, q, k, v):
    tm = lay.tm
    lat0 = lay.rows_c // lay.S
    return pl.pallas_call(
        _gqa_body,
        out_shape=jax.ShapeDtypeStruct((lay.rows_l, C_Q), BF16),
        grid=(lay.B, lay.nlt),
        in_specs=[pl.BlockSpec((tm, C_Q), lambda b, j: (lay.ntiles_c + b * lay.nlt + j, 0)),
                  pl.BlockSpec((lay.CTX, C_KV), lambda b, j: (b, 0)),
                  pl.BlockSpec((lay.S, C_KV), lambda b, j: (lat0 + b, 0)),
                  pl.BlockSpec((lay.CTX, C_KV), lambda b, j: (b, 0)),
                  pl.BlockSpec((lay.S, C_KV), lambda b, j: (lat0 + b, 0))],
        out_specs=pl.BlockSpec((tm, C_Q), lambda b, j: (b * lay.nlt + j, 0)),
        compiler_params=_cparams(("arbitrary", "arbitrary")),
        name="gqa_attention",
    )(q, k, k, v, v)


def _out_proj_body(n_mix, n_ctx_tiles, *refs):
    mix_refs = refs[:n_mix]
    n_x = 1 if n_ctx_tiles is None else 2
    x_refs = refs[n_mix + 1:n_mix + 1 + n_x]
    w_ref = refs[n_mix]
    mod_ref, lg_ref, lb_ref, rw_ref, xo_ref, h_ref, lt_ref = refs[n_mix + 1 + n_x:]
    x = x_refs[0][...] if n_ctx_tiles is None else _row_tile(n_ctx_tiles, *x_refs)
    m = mod_ref[0]
    off = 0
    mix = None
    for mr in mix_refs:
        kw = mr.shape[-1]
        part = jnp.dot(mr[...], w_ref[off:off + kw, :], preferred_element_type=F32)
        mix = part if mix is None else mix + part
        off += kw
    z = DEEPNORM_ALPHA * x + m[2:3] * mix
    xn = _layer_norm_rows(z, lg_ref[...], lb_ref[...])
    xo_ref[...] = xn
    h2 = xn * (1.0 + m[4:5]) + m[3:4]
    h_ref[...] = h2
    lt_ref[...] = _dot3_nt(rw_ref[...], h2)


def _out_proj(lay, mixes, w_bf16, x, row_off, mod, ln_g, ln_b, router_wt, n_rows):
    tm, d = lay.tmo, D_MODEL
    row = lambda i: (i, 0)
    full = lambda i: (0, 0)
    x_tile_off = row_off // tm
    if isinstance(x, tuple):
        assert row_off == 0
        n_ctx_tiles = lay.rows_c // tm
        x_specs = _row_tile_specs(tm, n_ctx_tiles, d)
    else:
        n_ctx_tiles, x = None, (x,)
        x_specs = [pl.BlockSpec((tm, d), lambda i: (i + x_tile_off, 0))]
    in_specs = [pl.BlockSpec((tm, mx.shape[-1]), row) for mx in mixes]
    in_specs += [pl.BlockSpec((d, d), full)] + x_specs + [
                 pl.BlockSpec((1, 6, d), lambda i: (lay.mod_row(i, tm, row_off), 0, 0)),
                 pl.BlockSpec((1, d), full), pl.BlockSpec((1, d), full),
                 pl.BlockSpec((N_EXPERTS, d), full)]
    return pl.pallas_call(
        functools.partial(_out_proj_body, len(mixes), n_ctx_tiles),
        out_shape=(jax.ShapeDtypeStruct((n_rows, d), F32),
                   jax.ShapeDtypeStruct((n_rows, d), F32),
                   jax.ShapeDtypeStruct((N_EXPERTS, n_rows), F32)),
        grid=(n_rows // tm,),
        in_specs=in_specs,
        out_specs=(pl.BlockSpec((tm, d), row), pl.BlockSpec((tm, d), row),
                   pl.BlockSpec((N_EXPERTS, tm), lambda i: (0, i))),
        compiler_params=_cparams(("arbitrary",)),
        name="out_proj_ln",
    )(*mixes, w_bf16, *x, mod, ln_g.reshape(1, d), ln_b.reshape(1, d), router_wt)


_EXPERT_PAIRS = ((0, 1), (0, 2), (0, 3), (1, 2), (1, 3), (2, 3))
N_BUCKETS = N_GROUPS * len(_EXPERT_PAIRS)


def _router_body(lt_ref, rb_ref, o_ref):
    logits = lt_ref[...] + rb_ref[...]
    rows = [logits[e:e + 1, :] for e in range(N_EXPERTS)]
    m = rows[0]
    for x in rows[1:]:
        m = jnp.maximum(m, x)
    ex = [jnp.exp(x - m) for x in rows]
    z = ex[0]
    for x in ex[1:]:
        z = z + x
    p = [x / z for x in ex]

    gscore = []
    for g in range(N_GROUPS):
        a, b, c, d = p[4 * g:4 * g + 4]
        hi1, lo1 = jnp.maximum(a, b), jnp.minimum(a, b)
        hi2, lo2 = jnp.maximum(c, d), jnp.minimum(c, d)
        top1 = jnp.maximum(hi1, hi2)
        top2 = jnp.maximum(jnp.minimum(hi1, hi2), jnp.maximum(lo1, lo2))
        gscore.append(top1 + top2)
    best = []
    for g in range(N_GROUPS):
        ok = None
        for o in range(N_GROUPS):
            if o == g:
                continue
            c = (gscore[g] > gscore[o]) if o < g else (gscore[g] >= gscore[o])
            ok = c if ok is None else jnp.logical_and(ok, c)
        best.append(ok)
    won = []
    for e in range(N_EXPERTS):
        g = e // EXPERTS_PER_GROUP
        rank = jnp.zeros_like(p[e])
        for o in range(4 * g, 4 * g + 4):
            if o == e:
                continue
            ahead = (p[o] > p[e]) if o > e else (p[o] >= p[e])
            rank = rank + jnp.where(ahead, 1.0, 0.0)
        won.append(jnp.where(jnp.logical_and(best[g], rank < 1.5), 1.0, 0.0))
    tot = won[0] * p[0]
    for e in range(1, N_EXPERTS):
        tot = tot + won[e] * p[e]
    bucket = jnp.zeros_like(tot)
    gate_a = jnp.zeros_like(tot)
    gate_b = jnp.zeros_like(tot)
    for g in range(N_GROUPS):
        for pid, (a, b) in enumerate(_EXPERT_PAIRS):
            ind = won[4 * g + a] * won[4 * g + b]
            bucket = bucket + ind * float(len(_EXPERT_PAIRS) * g + pid)
            gate_a = gate_a + ind * p[4 * g + a]
            gate_b = gate_b + ind * p[4 * g + b]
    o_ref[...] = jnp.zeros_like(o_ref)
    o_ref[0:1, :] = bucket
    o_ref[1:2, :] = gate_a / tot
    o_ref[2:3, :] = gate_b / tot


def _router(logits_t, router_b):
    e, n = logits_t.shape
    tr = math.gcd(2048, n)
    return pl.pallas_call(
        _router_body,
        out_shape=jax.ShapeDtypeStruct((SUBLANES, n), F32),
        grid=(n // tr,),
        in_specs=[pl.BlockSpec((e, tr), lambda i: (0, i)), pl.BlockSpec((e, 1), lambda i: (0, 0))],
        out_specs=pl.BlockSpec((SUBLANES, tr), lambda i: (0, i)),
        compiler_params=_cparams(("arbitrary",)),
        name="router_gates",
    )(logits_t, router_b.reshape(e, 1))


GATE_LANES = LANES


def _moe_plan(bucket, tg):
    n = bucket.shape[0]
    n_tiles = n // tg + N_BUCKETS
    ids = jnp.arange(N_BUCKETS, dtype=jnp.int32)
    onehot = (bucket[:, None] == ids[None, :]).astype(jnp.int32)
    csum = jnp.cumsum(onehot, axis=0)
    counts = csum[-1]
    tiles = (counts + tg - 1) // tg
    tile_end = jnp.cumsum(tiles)
    row_start = (tile_end - tiles) * tg
    pos = jnp.sum(onehot * (row_start[None, :] + csum - 1), axis=1).astype(jnp.int32)
    n_used = tile_end[-1]
    tile_ids = jnp.minimum(jnp.arange(n_tiles, dtype=jnp.int32), n_used - 1)
    tile_bucket = jnp.sum((tile_ids[:, None] >= tile_end[None, :]).astype(jnp.int32), axis=1)
    pair_a = jnp.asarray([a for a, _ in _EXPERT_PAIRS], jnp.int32)
    pair_b = jnp.asarray([b for _, b in _EXPERT_PAIRS], jnp.int32)
    grp, pid = tile_bucket // len(_EXPERT_PAIRS), tile_bucket % len(_EXPERT_PAIRS)
    ea = grp * EXPERTS_PER_GROUP + pair_a[pid]
    eb = grp * EXPERTS_PER_GROUP + pair_b[pid]
    spare = n_used + ids
    zero_tiles = jnp.concatenate([jnp.where(tiles > 0, tile_end - 1, -1),
                                  jnp.where(spare < n_tiles, spare, -1)]).astype(jnp.int32)
    return (pos, ea.astype(jnp.int32), eb.astype(jnp.int32), n_used.reshape(1).astype(jnp.int32),
            zero_tiles, n_tiles)


def _row_copy(src_ref, src_row, dst_ref, dst_row, sem):
    return pltpu.make_async_copy(src_ref.at[pl.ds(src_row, 1), :], dst_ref.at[pl.ds(dst_row, 1), :], sem)


ROW_DMA_UNROLL = 8


def _row_copies(n_rows, row_copy, whole_tile_copy):
    def issue(blk, carry):
        for u in range(ROW_DMA_UNROLL):
            row_copy(blk * ROW_DMA_UNROLL + u).start(priority=u % 2)
        return carry

    lax.fori_loop(0, n_rows // ROW_DMA_UNROLL, issue, 0)
    whole_tile_copy.wait()


def _dispatch_body(tg, pos_ref, zero_tiles_ref, h_ref, g_ref, xs_ref, aug_ref, sem):
    base = pl.program_id(0) * tg

    @pl.when(pl.program_id(0) == 0)
    def _():
        aug_ref[...] = jnp.zeros_like(aug_ref)

        def fill(k):
            row0 = pl.multiple_of(zero_tiles_ref[k] * tg, tg)
            return pltpu.make_async_copy(aug_ref, xs_ref.at[pl.ds(row0, tg), :], sem)

        for k in range(2 * N_BUCKETS):
            pl.when(zero_tiles_ref[k] >= 0)(lambda k=k: fill(k).start())
        for k in range(2 * N_BUCKETS):
            pl.when(zero_tiles_ref[k] >= 0)(lambda k=k: fill(k).wait())

    aug_ref[:, :D_MODEL] = h_ref[...]
    aug_ref[:, D_MODEL:] = g_ref[...]

    _row_copies(tg, lambda r: _row_copy(aug_ref, r, xs_ref, pos_ref[base + r], sem),
                pltpu.make_async_copy(aug_ref, xs_ref.at[pl.ds(0, tg), :], sem))


def _moe_dispatch(pos, zero_tiles, h2, gate_rows, n_tiles, tg):
    n, d = h2.shape
    wide = d + GATE_LANES
    grid_spec = pltpu.PrefetchScalarGridSpec(
        num_scalar_prefetch=2,
        grid=(n // tg,),
        in_specs=[pl.BlockSpec((tg, d), lambda i, pos_ref, zt_ref: (i, 0)),
                  pl.BlockSpec((tg, GATE_LANES), lambda i, pos_ref, zt_ref: (i, 0))],
        out_specs=pl.BlockSpec(memory_space=pl.ANY),
        scratch_shapes=[pltpu.VMEM((tg, wide), F32), pltpu.SemaphoreType.DMA(())],
    )
    return pl.pallas_call(
        functools.partial(_dispatch_body, tg),
        out_shape=jax.ShapeDtypeStruct((n_tiles * tg, wide), F32),
        grid_spec=grid_spec,
        compiler_params=_cparams(("arbitrary",)),
        name="moe_dispatch",
    )(pos, zero_tiles, h2, gate_rows)


def _grouped_body(ea_ref, eb_ref, nu_ref, xs_ref, w1a_ref, w3a_ref, w2a_ref, w1b_ref, w3b_ref, w2b_ref, y_ref):
    del ea_ref, eb_ref

    @pl.when(pl.program_id(0) < nu_ref[0])
    def _():
        x = xs_ref[:, :D_MODEL].astype(BF16)

        def expert(w1_ref, w3_ref, w2_ref, gate):
            h1 = jnp.dot(x, w1_ref[0], preferred_element_type=F32)
            h3 = jnp.dot(x, w3_ref[0], preferred_element_type=F32)
            hid = (h1 * _sigmoid(h1) * h3 * gate).astype(BF16)
            return jnp.dot(hid, w2_ref[0], preferred_element_type=F32)

        y_ref[...] = (expert(w1a_ref, w3a_ref, w2a_ref, xs_ref[:, D_MODEL:D_MODEL + 1])
                      + expert(w1b_ref, w3b_ref, w2b_ref, xs_ref[:, D_MODEL + 1:D_MODEL + 2]))

    @pl.when(pl.program_id(0) >= nu_ref[0])
    def _():
        y_ref[...] = jnp.zeros_like(y_ref)


def _moe_grouped(xs, ea, eb, n_used, w1, w3, w2, tg):
    rows, wide = xs.shape
    d, ff = D_MODEL, EXPERT_FF
    n_tiles = rows // tg
    tile = lambda i, ea_r, eb_r, nu_r: (jnp.minimum(i, nu_r[0] - 1), 0)
    up_a = pl.BlockSpec((1, d, ff), lambda i, ea_r, eb_r, nu_r: (ea_r[i], 0, 0))
    up_b = pl.BlockSpec((1, d, ff), lambda i, ea_r, eb_r, nu_r: (eb_r[i], 0, 0))
    dn_a = pl.BlockSpec((1, ff, d), lambda i, ea_r, eb_r, nu_r: (ea_r[i], 0, 0))
    dn_b = pl.BlockSpec((1, ff, d), lambda i, ea_r, eb_r, nu_r: (eb_r[i], 0, 0))
    grid_spec = pltpu.PrefetchScalarGridSpec(
        num_scalar_prefetch=3,
        grid=(n_tiles,),
        in_specs=[pl.BlockSpec((tg, wide), tile), up_a, up_a, dn_a, up_b, up_b, dn_b],
        out_specs=pl.BlockSpec((tg, d), lambda i, ea_r, eb_r, nu_r: (i, 0)),
    )
    return pl.pallas_call(
        _grouped_body,
        out_shape=jax.ShapeDtypeStruct((rows, d), F32),
        grid_spec=grid_spec,
        compiler_params=_cparams(("arbitrary",)),
        name="moe_grouped",
    )(ea, eb, n_used, xs, w1, w3, w2, w1, w3, w2)


def _combine_body(tg, pos_ref, ys_ref, x_ref, mod_ref, lg_ref, lb_ref, o_ref, buf_ref, sem):
    base = pl.program_id(0) * tg

    _row_copies(tg, lambda r: _row_copy(ys_ref, pos_ref[base + r], buf_ref, r, sem),
                pltpu.make_async_copy(ys_ref.at[pl.ds(0, tg), :], buf_ref, sem))
    m = mod_ref[0]
    z = DEEPNORM_ALPHA * x_ref[...] + m[5:6] * buf_ref[...]
    o_ref[...] = _layer_norm_rows(z, lg_ref[...], lb_ref[...])


def _moe_combine(lay, pos, ys, x, mod, ln_g, ln_b, row_off, tg):
    n, d = x.shape
    grid_spec = pltpu.PrefetchScalarGridSpec(
        num_scalar_prefetch=1,
        grid=(n // tg,),
        in_specs=[pl.BlockSpec(memory_space=pl.ANY),
                  pl.BlockSpec((tg, d), lambda i, pos_ref: (i, 0)),
                  pl.BlockSpec((1, 6, d), lambda i, pos_ref: (lay.mod_row(i, tg, row_off), 0, 0)),
                  pl.BlockSpec((1, d), lambda i, pos_ref: (0, 0)),
                  pl.BlockSpec((1, d), lambda i, pos_ref: (0, 0))],
        out_specs=pl.BlockSpec((tg, d), lambda i, pos_ref: (i, 0)),
        scratch_shapes=[pltpu.VMEM((tg, d), F32), pltpu.SemaphoreType.DMA(())],
    )
    return pl.pallas_call(
        functools.partial(_combine_body, tg),
        out_shape=jax.ShapeDtypeStruct((n, d), F32),
        grid_spec=grid_spec,
        compiler_params=_cparams(("arbitrary",)),
        name="moe_combine_ln",
    )(pos, ys, x, mod, ln_g.reshape(1, d), ln_b.reshape(1, d))


def _moe(lay, h2, routed, w1, w3, w2, x, mod, ln_g, ln_b, row_off):
    tg = lay.tm
    bucket = routed[0].astype(jnp.int32)
    gate_rows = jnp.pad(routed[1:3].T, ((0, 0), (0, GATE_LANES - 2)))
    pos, ea, eb, n_used, zero_tiles, n_tiles = _moe_plan(bucket, tg)
    xs = _moe_dispatch(pos, zero_tiles, h2, gate_rows, n_tiles, tg)
    ys = _moe_grouped(xs, ea, eb, n_used, w1, w3, w2, tg)
    return _moe_combine(lay, pos, ys, x, mod, ln_g, ln_b, row_off, tg)


def kernel(x, c, ctx, c_ctx, router_w, router_b, ada_w, ada_b, ln1_g, ln1_b, ln2_g, ln2_b, moe_w1, moe_w3, moe_w2, ev_w_in, ev_w_out, ev_a_mu, ev_a_w0, ev_a_w2, ev_a_a0, ev_a_a2, ev_a_g2, ev_a_kk, ev_a_ka, ev_a_rk, ev_a_lnx_g, ev_a_lnx_b, ev_b_lam, ev_b_subln_g, od_w_in, od_w_out, od_qn_g, od_kn_g):
    bsz, seq, d = x.shape
    ctx_len = ctx.shape[1]
    assert d == D_MODEL and ada_w.shape[0] == DEPTH and seq % GRID_W == 0
    lay = _Layout(bsz, ctx_len, seq)

    cvec = jnp.zeros((lay.mod_rows, d), F32).at[:bsz].set(c).at[bsz].set(c_ctx)
    mods = _ada_mods(cvec, ada_w, ada_b).reshape(DEPTH, lay.mod_rows, 6, d)

    xs = (ctx.reshape(lay.rows_c, d), x.reshape(lay.rows_l, d))
    router_wt = router_w.T

    cos_b, sin_b = _rope_tables(lay, B_HEAD_DIM, B_QK)
    cos_c, sin_c = _rope_tables(lay, C_HEAD_DIM, C_HEAD_DIM)

    for i in range(DEPTH):
        last = i == DEPTH - 1
        j = i // 2
        mod = mods[i]
        if i % 2 == 0:
            lambda_init = 0.8 - 0.6 * math.exp(-0.3 * i)
            x_ctx, x_lat = xs if isinstance(xs, tuple) else (xs[:lay.rows_c], xs[lay.rows_c:])
            pa, q, k, v = _even_in_proj(lay, x_ctx, x_lat, mod, ev_w_in[j].astype(BF16), cos_b, sin_b)
            decay, scan_ops, g_, bonus = _rwkv_features(
                lay, pa, ev_a_mu[j], ev_a_w0[j], ev_a_w2[j], ev_a_a0[j], ev_a_a2[j], ev_a_g2[j],
                ev_a_kk[j], ev_a_ka[j], ev_a_rk[j].reshape(-1))
            chain_decay, chain_ops = _to_chains(lay, decay[None]), _to_chains(lay, scan_ops)
            yd = _diff_attention(lay, q, k, v, ev_b_lam[j], ev_b_subln_g[j], lambda_init)
            y_f, y_b = _wkv_scan(lay, chain_decay, chain_ops)
            ya = _rwkv_readout(lay, _from_chains(lay, y_f, y_b), bonus, g_, ev_a_lnx_g[j], ev_a_lnx_b[j])
            mixes, w_out = [ya, yd], ev_w_out[j]
            n_rows, tile_off = (lay.rows_l, lay.ntiles_c) if last else (lay.rows, 0)
            if last:
                mixes = [mx[lay.rows_c:] for mx in mixes]
        else:
            xs = jnp.concatenate(xs, axis=0) if isinstance(xs, tuple) else xs
            q, k, v = _odd_in_proj(lay, xs, mod, od_w_in[j].astype(BF16), cos_c, sin_c, od_qn_g[j], od_kn_g[j])
            assert last, "an odd layer that must also update the context stream is not supported"
            o = _gqa_attention(lay, q, k, v)
            mixes, w_out = [o], od_w_out[j]
            n_rows, tile_off = lay.rows_l, lay.ntiles_c
        row_off = tile_off * lay.tm
        if isinstance(xs, tuple) and row_off:
            xs = jnp.concatenate(xs, axis=0)
        x_new, h2, logits_t = _out_proj(lay, mixes, w_out.astype(BF16), xs, row_off, mod,
                                        ln1_g[i], ln1_b[i], router_wt, n_rows)
        routed = _router(logits_t, router_b)
        xs = _moe(lay, h2, routed, moe_w1[i].astype(BF16), moe_w3[i].astype(BF16), moe_w2[i].astype(BF16),
                  x_new, mod, ln2_g[i], ln2_b[i], row_off)
    return xs.reshape(bsz, seq, d)
```

```python
import functools
import math

import jax
import jax.numpy as jnp
from jax import lax
from jax.experimental import pallas as pl
from jax.experimental.pallas import tpu as pltpu

F32 = jnp.float32
BF16 = jnp.bfloat16

D_MODEL = 1024
DEPTH = 2
GRID_W = 64
ROPE_THETA = 10000.0
LN_EPS = 1e-5
LOG2_E = 1.4426950408889634
DEEPNORM_ALPHA = (2 * DEPTH) ** 0.25

A_HEAD_DIM = 64
A_HEADS = 8
A_WIDTH = 512
A_LORA = 64
A_GATE_LORA = 128
A_GN_EPS = 64e-5
A_IN = 3 * A_WIDTH + 4 * A_LORA + A_GATE_LORA
_SCAN_R, _SCAN_K, _SCAN_V, _SCAN_A, _SCAN_B = range(5)
N_SCAN_OPS = 5
SCAN_ROW_GROUPS = 2

B_HEAD_DIM = 64
B_V_DIM = 128
B_HEADS = 4
B_WIDTH = 512
B_QK = 512
B_QK_PAD = (B_QK // B_HEAD_DIM) * 128
B_SUBLN_EPS = 1e-5
EVEN_IN = A_IN + 2 * B_QK + B_WIDTH

C_HEAD_DIM = 128
C_HEADS = 8
C_KV_HEADS = 2
C_GROUP = 4
C_Q = 1024
C_KV = 256
ODD_IN = C_Q + 2 * C_KV
QK_NORM_EPS = 1e-6

N_EXPERTS = 16
N_GROUPS = 4
EXPERTS_PER_GROUP = 4
EXPERT_FF = 512

VMEM_LIMIT_BYTES = 56 * 1024 * 1024
LANES = 128
SUBLANES = 8


def _cparams(sem):
    return pltpu.CompilerParams(dimension_semantics=sem, vmem_limit_bytes=VMEM_LIMIT_BYTES)


def _dot(a, b):
    return jnp.dot(a.astype(BF16), b.astype(BF16), preferred_element_type=F32)


def _dot_nt(a, b):
    return lax.dot_general(a.astype(BF16), b.astype(BF16), (((1,), (1,)), ((), ())),
                           preferred_element_type=F32)


def _split(a):
    hi = a.astype(BF16)
    lo = (a - hi.astype(F32)).astype(BF16)
    return hi, lo


def _dot3(a, b):
    ah, al = _split(a)
    bh, bl = _split(b)
    return (jnp.dot(ah, bh, preferred_element_type=F32)
            + (jnp.dot(ah, bl, preferred_element_type=F32)
               + jnp.dot(al, bh, preferred_element_type=F32)))


def _dot2_exact_rhs(a, b_bf16):
    ah, al = _split(a)
    return jnp.dot(ah, b_bf16, preferred_element_type=F32) + jnp.dot(al, b_bf16, preferred_element_type=F32)


def _dot3_nt(a, b):
    ah, al = _split(a)
    bh, bl = _split(b)
    dn = (((1,), (1,)), ((), ()))
    return (lax.dot_general(ah, bh, dn, preferred_element_type=F32)
            + (lax.dot_general(ah, bl, dn, preferred_element_type=F32)
               + lax.dot_general(al, bh, dn, preferred_element_type=F32)))


def _sigmoid(x):
    return 1.0 / (1.0 + jnp.exp(-x))


def _layer_norm_rows(z, g, b):
    mu = jnp.mean(z, axis=-1, keepdims=True)
    zc = z - mu
    var = jnp.mean(zc * zc, axis=-1, keepdims=True)
    return zc * lax.rsqrt(var + LN_EPS) * g + b


class _Layout:
    def __init__(self, bsz, ctx_len, seq):
        self.B, self.CTX, self.S = bsz, ctx_len, seq
        self.T = ctx_len + seq
        self.tm = math.gcd(256, math.gcd(ctx_len, seq))
        self.nct = ctx_len // self.tm
        self.nlt = seq // self.tm
        self.rows_c = bsz * ctx_len
        self.rows_l = bsz * seq
        self.rows = self.rows_c + self.rows_l
        self.ntiles_c = bsz * self.nct
        self.ntiles = self.rows // self.tm
        assert self.rows_c % seq == 0, "latent K/V blocks are addressed in units of S rows"
        self.tb = math.gcd(32, math.gcd(ctx_len, seq))
        self.tmo = math.gcd(512, math.gcd(self.rows_c, seq))
        self.mod_rows = -(-(bsz + 1) // SUBLANES) * SUBLANES

    def seq_tile(self, b, j):
        return jnp.where(j < self.nct, b * self.nct + j, self.ntiles_c + b * self.nlt + (j - self.nct))

    def seq_block(self, i):
        il = i - self.ntiles_c
        return (jnp.where(i < self.ntiles_c, i // self.nct, il // self.nlt),
                jnp.where(i < self.ntiles_c, i % self.nct, self.nct + il % self.nlt))

    def mod_row(self, i, tile, row_offset=0):
        r = i * tile + row_offset
        return jnp.where(r < self.rows_c, self.B, (r - self.rows_c) // self.S)

    def pos_tile(self, i):
        il = i - self.ntiles_c
        return jnp.where(i < self.ntiles_c, i % self.nct, self.nct + il % self.nlt)


def _ada_body(cv_ref, w_ref, b_ref, o_ref):
    cv = cv_ref[...]
    s = cv * _sigmoid(cv)
    o_ref[0] = _dot3(s, w_ref[0]) + b_ref[0]


def _ada_mods(cvec, ada_w, ada_b):
    depth, d, n = ada_w.shape
    r = cvec.shape[0]
    tn = 512
    return pl.pallas_call(
        _ada_body,
        out_shape=jax.ShapeDtypeStruct((depth, r, n), F32),
        grid=(depth, n // tn),
        in_specs=[pl.BlockSpec((r, d), lambda l, j: (0, 0)),
                  pl.BlockSpec((1, d, tn), lambda l, j: (l, 0, j)),
                  pl.BlockSpec((1, 1, tn), lambda l, j: (l, 0, j))],
        out_specs=pl.BlockSpec((1, r, tn), lambda l, j: (l, 0, j)),
        compiler_params=_cparams(("arbitrary", "arbitrary")),
        name="ada_mods",
    )(cvec, ada_w, ada_b.reshape(depth, 1, n))


def _rope_tables(lay, head_dim, width):
    rows = lay.S // GRID_W
    rr, cc = jnp.meshgrid(jnp.arange(rows), jnp.arange(GRID_W), indexing="ij")
    row_pos = rr.reshape(-1).astype(F32)
    col_pos = cc.reshape(-1).astype(F32)
    axis_dim = head_dim // 2
    inv = ROPE_THETA ** (-jnp.arange(0, axis_dim, 2, dtype=F32) / axis_dim)
    ang = jnp.concatenate([row_pos[:, None] * inv, col_pos[:, None] * inv], -1)
    cos, sin = jnp.cos(ang), jnp.sin(ang)
    cos = jnp.concatenate([jnp.ones((lay.CTX, head_dim // 2), F32), cos], 0)
    sin = jnp.concatenate([jnp.zeros((lay.CTX, head_dim // 2), F32), sin], 0)
    cos_h = jnp.concatenate([cos, cos], -1)
    sin_h = jnp.concatenate([-sin, sin], -1)
    reps = width // head_dim
    return jnp.tile(cos_h, (1, reps)), jnp.tile(sin_h, (1, reps))


def _rope_lanes(x, cos, sin, head_dim):
    w = x.shape[-1]
    half = head_dim // 2
    if head_dim == LANES and w == LANES:
        rot = pltpu.roll(x, half, 1)
    else:
        fwd = pltpu.roll(x, w - half, 1)
        bwd = pltpu.roll(x, half, 1)
        lane = lax.broadcasted_iota(jnp.int32, x.shape, 1)
        rot = jnp.where((lane % head_dim) < half, fwd, bwd)
    return x * cos + rot * sin


def _row_tile(n_ctx_tiles, xc_ref, xl_ref):
    return jnp.where(pl.program_id(0) < n_ctx_tiles, xc_ref[...], xl_ref[...])


def _row_tile_specs(tile, n_ctx_tiles, d):
    return [pl.BlockSpec((tile, d), lambda i: (jnp.minimum(i, n_ctx_tiles - 1), 0)),
            pl.BlockSpec((tile, d), lambda i: (jnp.maximum(i - n_ctx_tiles, 0), 0))]


def _even_in_body(n_ctx_tiles, xc_ref, xl_ref, mod_ref, w_ref, cos_ref, sin_ref, pa_ref, q_ref, k_ref, v_ref):
    m = mod_ref[0]
    h = (_row_tile(n_ctx_tiles, xc_ref, xl_ref) * (1.0 + m[1:2]) + m[0:1]).astype(BF16)
    pa_ref[...] = jnp.dot(h, w_ref[:, :A_IN], preferred_element_type=F32)
    cos, sin = cos_ref[...], sin_ref[...]
    o = A_IN

    def put_maps(dst_ref, val):
        pad = jnp.zeros((val.shape[0], LANES - B_HEAD_DIM), BF16)
        for mp in range(B_QK // B_HEAD_DIM):
            piece = val[:, mp * B_HEAD_DIM:(mp + 1) * B_HEAD_DIM].astype(BF16)
            dst_ref[:, mp * LANES:(mp + 1) * LANES] = jnp.concatenate([piece, pad], axis=1)

    q = jnp.dot(h, w_ref[:, o:o + B_QK], preferred_element_type=F32)
    put_maps(q_ref, _rope_lanes(q, cos, sin, B_HEAD_DIM) * (LOG2_E * B_HEAD_DIM ** -0.5))
    o += B_QK
    k = jnp.dot(h, w_ref[:, o:o + B_QK], preferred_element_type=F32)
    put_maps(k_ref, _rope_lanes(k, cos, sin, B_HEAD_DIM))
    o += B_QK
    v_ref[...] = jnp.dot(h, w_ref[:, o:o + B_WIDTH], preferred_element_type=F32).astype(BF16)


def _even_in_proj(lay, x_ctx, x_lat, mod, w_bf16, cos, sin):
    tm, d = lay.tm, D_MODEL
    row = lambda i: (i, 0)
    return pl.pallas_call(
        functools.partial(_even_in_body, lay.ntiles_c),
        out_shape=(jax.ShapeDtypeStruct((lay.rows, A_IN), F32),
                   jax.ShapeDtypeStruct((lay.rows, B_QK_PAD), BF16),
                   jax.ShapeDtypeStruct((lay.rows, B_QK_PAD), BF16),
                   jax.ShapeDtypeStruct((lay.rows, B_WIDTH), BF16)),
        grid=(lay.ntiles,),
        in_specs=_row_tile_specs(tm, lay.ntiles_c, d) + [
                  pl.BlockSpec((1, 6, d), lambda i: (lay.mod_row(i, tm), 0, 0)),
                  pl.BlockSpec((d, EVEN_IN), lambda i: (0, 0)),
                  pl.BlockSpec((tm, B_QK), lambda i: (lay.pos_tile(i), 0)),
                  pl.BlockSpec((tm, B_QK), lambda i: (lay.pos_tile(i), 0))],
        out_specs=(pl.BlockSpec((tm, A_IN), row), pl.BlockSpec((tm, B_QK_PAD), row),
                   pl.BlockSpec((tm, B_QK_PAD), row), pl.BlockSpec((tm, B_WIDTH), row)),
        compiler_params=_cparams(("arbitrary",)),
        name="even_in_proj",
    )(x_ctx, x_lat, mod, w_bf16, cos, sin)


def _rwkv_feat_body(lay, pa_ref, prev_ref, next_ref, mu_ref, w0_ref, w2_ref, a0_ref, a2_ref, g2_ref,
                    kk_ref, ka_ref, rk_ref, bd_ref, tri_ref, blk_ref,
                    pend_ref, ops_ref, g_ref, bonus_ref):
    def put(idx, d, val):
        ops_ref[idx, d, 0] = val.astype(BF16)

    i = pl.program_id(0)
    tm = lay.tm
    il = i - lay.ntiles_c
    in_ctx = i < lay.ntiles_c
    seg_first = jnp.where(in_ctx, i % lay.nct == 0, il % lay.nlt == 0)
    seg_last = jnp.where(in_ctx, i % lay.nct == lay.nct - 1, il % lay.nlt == lay.nlt - 1)

    pa = pa_ref[...]
    row = lax.broadcasted_iota(jnp.int32, pa.shape, 0)
    prev_edge = jnp.where(seg_first, 0.0, 1.0) * prev_ref[SUBLANES - 1:SUBLANES, :]
    next_edge = jnp.where(seg_last, 0.0, 1.0) * next_ref[0:1, :]
    prev = jnp.where(row == 0, prev_edge, pltpu.roll(pa, 1, 0))
    nxt = jnp.where(row == tm - 1, next_edge, pltpu.roll(pa, tm - 1, 0))
    u = pa + (0.5 * (prev + nxt) - pa) * mu_ref[...]

    o1, o2, o3 = A_WIDTH, 2 * A_WIDTH, 3 * A_WIDTH
    o4 = o3 + 2 * A_LORA
    o5 = o4 + 2 * A_LORA
    r, k, v = u[:, :o1], u[:, o1:o2], u[:, o2:o3]
    bd = bd_ref[...]

    kk = k * kk_ref[...]
    ss = _dot2_exact_rhs(kk * kk, bd)
    kkn = kk / jnp.maximum(jnp.sqrt(ss), 1e-12)
    g = _dot3(_sigmoid(u[:, o5:]), g2_ref[...])

    g_ref[0] = g

    kd_sum = jnp.zeros_like(k)
    for d in range(2):
        wd = u[:, o3 + d * A_LORA:o3 + (d + 1) * A_LORA]
        ad = u[:, o4 + d * A_LORA:o4 + (d + 1) * A_LORA]
        log_decay = -math.exp(-0.5) * _sigmoid(w0_ref[d:d + 1, :] + _dot3(jnp.tanh(wd), w2_ref[d]))
        ld_hi, ld_lo = _split(log_decay)
        cum = (jnp.dot(tri_ref[d], ld_hi, preferred_element_type=F32)
               + jnp.dot(tri_ref[d], ld_lo, preferred_element_type=F32))
        tot = (jnp.dot(blk_ref[...], ld_hi, preferred_element_type=F32)
               + jnp.dot(blk_ref[...], ld_lo, preferred_element_type=F32))
        pend_ref[d, 0] = jnp.exp(tot)
        p = jnp.exp(cum)
        p_inv = 1.0 / p
        a = _sigmoid(a0_ref[d:d + 1, :] + _dot3(ad, a2_ref[d]))
        kd = k * (1.0 + (a - 1.0) * ka_ref[...])
        put(_SCAN_R, d, r * p)
        put(_SCAN_V, d, v)
        put(_SCAN_A, d, -kkn * jnp.exp(cum - log_decay))
        put(_SCAN_K, d, kd * p_inv)
        put(_SCAN_B, d, kkn * a * p_inv)
        kd_sum = kd_sum + kd
    bonus_ref[0] = _dot2_exact_rhs(r * kd_sum * rk_ref[...], bd) * v


def _head_block_diag(width, head_dim):
    h = jnp.arange(width) // head_dim
    return (h[:, None] == h[None, :]).astype(BF16)


def _rwkv_features(lay, pa, mu, w0, w2, a0, a2, g2, k_k, k_a, r_k):
    tm = lay.tm
    hb = tm // SUBLANES
    nb8 = lay.rows // SUBLANES
    row = lambda i: (i, 0)
    full2 = lambda i: (0, 0)
    full3 = lambda i: (0, 0, 0)
    w = A_WIDTH
    out = jax.ShapeDtypeStruct((lay.B, lay.T, w), F32)
    seq = lambda i: lay.seq_block(i) + (0,)
    tb = lay.tb
    nblk = tm // tb
    pos = jnp.arange(tm)
    same_blk = (pos[:, None] // tb) == (pos[None, :] // tb)
    tri = jnp.stack([same_blk & (pos[None, :] <= pos[:, None]),
                     same_blk & (pos[None, :] >= pos[:, None])]).astype(BF16)
    blk = (jnp.arange(nblk)[:, None] == (pos[None, :] // tb)).astype(BF16)
    return pl.pallas_call(
        functools.partial(_rwkv_feat_body, lay),
        out_shape=(jax.ShapeDtypeStruct((2, lay.B, lay.T // tb, w), F32),
                   jax.ShapeDtypeStruct((N_SCAN_OPS, 2, lay.B, lay.T, w), BF16), out, out),
        grid=(lay.ntiles,),
        in_specs=[pl.BlockSpec((tm, A_IN), row),
                  pl.BlockSpec((SUBLANES, A_IN), lambda i: (jnp.maximum(i * hb - 1, 0), 0)),
                  pl.BlockSpec((SUBLANES, A_IN), lambda i: (jnp.minimum((i + 1) * hb, nb8 - 1), 0)),
                  pl.BlockSpec((1, A_IN), full2),
                  pl.BlockSpec((2, w), full2),
                  pl.BlockSpec((2, A_LORA, w), full3),
                  pl.BlockSpec((2, w), full2),
                  pl.BlockSpec((2, A_LORA, w), full3),
                  pl.BlockSpec((A_GATE_LORA, w), full2),
                  pl.BlockSpec((1, w), full2),
                  pl.BlockSpec((1, w), full2),
                  pl.BlockSpec((1, w), full2),
                  pl.BlockSpec((w, w), full2),
                  pl.BlockSpec((2, tm, tm), full3),
                  pl.BlockSpec((nblk, tm), full2)],
        out_specs=(pl.BlockSpec((2, 1, nblk, w), lambda i: (0,) + lay.seq_block(i) + (0,)),
                   pl.BlockSpec((N_SCAN_OPS, 2, 1, tm, w), lambda i: (0, 0) + lay.seq_block(i) + (0,)),
                   pl.BlockSpec((1, tm, w), seq), pl.BlockSpec((1, tm, w), seq)),
        compiler_params=_cparams(("arbitrary",)),
        name="rwkv_features",
    )(pa, pa, pa, mu.reshape(1, A_IN), w0, w2, a0, a2, g2, k_k.reshape(1, w), k_a.reshape(1, w),
      r_k.reshape(1, w), _head_block_diag(w, A_HEAD_DIM), tri, blk)


def _wkv_scan_body(tb, pf_ref, pb_ref, fwd_ref, bwd_ref, yf_ref, yb_ref, s_ref, m_ref):
    n = A_HEAD_DIM
    chains = s_ref.shape[-1]
    is_fwd = lax.broadcasted_iota(jnp.int32, (n, chains), 1) < chains // 2

    @pl.when(pl.program_id(0) == 0)
    def _():
        s_ref[...] = jnp.zeros_like(s_ref)

    def step(t, carry):
        tr = tb - 1 - t
        for idx in range(N_SCAN_OPS):
            m_ref[idx] = jnp.where(is_fwd, fwd_ref[t, idx], bwd_ref[tr, idx]).astype(F32)
        ys = []
        for g in range(SCAN_ROW_GROUPS):
            rows = slice(g * (n // SCAN_ROW_GROUPS), (g + 1) * (n // SCAN_ROW_GROUPS))
            sa = jnp.zeros((n // SCAN_ROW_GROUPS, chains), F32)
            for kk in range(n):
                sa = sa + s_ref[kk, rows, :] * m_ref[_SCAN_A, kk:kk + 1, :]
            v_t = m_ref[_SCAN_V, rows, :]
            y = jnp.zeros((n // SCAN_ROW_GROUPS, chains), F32)
            for kk in range(n):
                s_new = (s_ref[kk, rows, :] + sa * m_ref[_SCAN_B, kk:kk + 1, :]
                         + v_t * m_ref[_SCAN_K, kk:kk + 1, :])
                s_ref[kk, rows, :] = s_new
                y = y + s_new * m_ref[_SCAN_R, kk:kk + 1, :]
            ys.append(y)
        y = jnp.concatenate(ys, axis=0).astype(BF16)
        yf_ref[t] = y
        yb_ref[tr] = y
        return carry

    lax.fori_loop(0, tb, step, 0)
    m_ref[0] = jnp.where(is_fwd, pf_ref[0, 0], pb_ref[0, 0])
    for kk in range(n):
        s_ref[kk] = s_ref[kk] * m_ref[0, kk:kk + 1, :]


def _wkv_scan(lay, block_decay, ops):
    t, nops, n, lanes = ops.shape
    tb = lay.tb
    nctb, nt = lay.CTX // tb, t // tb
    bwd_tile = lambda g: jnp.where(g < nctb, nctb - 1 - g, nt - 1 - (g - nctb))
    out = jax.ShapeDtypeStruct((t, n, lanes), BF16)
    fwd_map = lambda g: (g, 0, 0, 0)
    bwd_map = lambda g: (bwd_tile(g), 0, 0, 0)
    return pl.pallas_call(
        functools.partial(_wkv_scan_body, tb),
        out_shape=(out, out),
        grid=(nt,),
        in_specs=[pl.BlockSpec((1, 1, n, lanes), fwd_map), pl.BlockSpec((1, 1, n, lanes), bwd_map),
                  pl.BlockSpec((tb, nops, n, lanes), fwd_map), pl.BlockSpec((tb, nops, n, lanes), bwd_map)],
        out_specs=(pl.BlockSpec((tb, n, lanes), lambda g: (g, 0, 0)),
                   pl.BlockSpec((tb, n, lanes), lambda g: (bwd_tile(g), 0, 0))),
        scratch_shapes=[pltpu.VMEM((n, n, lanes), F32), pltpu.VMEM((nops, n, lanes), F32)],
        compiler_params=_cparams(("arbitrary",)),
        name="wkv7_scan",
    )(block_decay, block_decay, ops, ops)


def _to_chains(lay, ops):
    h, n = A_HEADS, A_HEAD_DIM
    nops, length = ops.shape[0], ops.shape[3]
    x = ops.reshape(nops, 2, lay.B, length, h, n).transpose(3, 0, 5, 1, 2, 4)
    return x.reshape(length, nops, n, 2 * lay.B * h)


def _from_chains(lay, y_f, y_b):
    h, n = A_HEADS, A_HEAD_DIM
    low = jnp.arange(2 * lay.B * h) < lay.B * h
    y = jnp.where(low, y_f, y_b)
    return y.reshape(lay.T, n, 2, lay.B, h).transpose(2, 3, 0, 4, 1).reshape(2, lay.B, lay.T, h * n)


def _rwkv_readout_body(yf_ref, yb_ref, bonus_ref, g_ref, lg_ref, lb_ref, bd_ref, o_ref):
    y = yf_ref[0, 0].astype(F32) + yb_ref[0, 0].astype(F32)
    bd = bd_ref[...]
    inv_n = 1.0 / A_HEAD_DIM
    mu = _dot2_exact_rhs(y, bd) * inv_n
    yc = y - mu
    var = _dot2_exact_rhs(yc * yc, bd) * inv_n
    yn = yc * lax.rsqrt(var + A_GN_EPS) * lg_ref[...] + lb_ref[...]
    o_ref[...] = ((yn + bonus_ref[0]) * g_ref[0]).astype(BF16)


def _rwkv_readout(lay, y2, bonus, g, lnx_g, lnx_b):
    tm, w = lay.tm, A_WIDTH
    row = lambda i: (i, 0)
    full = lambda i: (0, 0)
    seq = lambda i: lay.seq_block(i) + (0,)
    return pl.pallas_call(
        _rwkv_readout_body,
        out_shape=jax.ShapeDtypeStruct((lay.rows, w), BF16),
        grid=(lay.ntiles,),
        in_specs=[pl.BlockSpec((1, 1, tm, w), lambda i: (0,) + seq(i)),
                  pl.BlockSpec((1, 1, tm, w), lambda i: (1,) + seq(i)),
                  pl.BlockSpec((1, tm, w), seq), pl.BlockSpec((1, tm, w), seq),
                  pl.BlockSpec((1, w), full), pl.BlockSpec((1, w), full), pl.BlockSpec((w, w), full)],
        out_specs=pl.BlockSpec((tm, w), row),
        compiler_params=_cparams(("arbitrary",)),
        name="rwkv_readout",
    )(y2, y2, bonus, g, lnx_g.reshape(1, w), lnx_b.reshape(1, w), _head_block_diag(w, A_HEAD_DIM))


def _exp2_scores(q, keys):
    scores = [_dot_nt(q, kk) for kk in keys]
    m = scores[0].max(axis=-1, keepdims=True)
    for s in scores[1:]:
        m = jnp.maximum(m, s.max(axis=-1, keepdims=True))
    return [jnp.exp2(s - m) for s in scores]


def _with_ones_column(v):
    lane = lax.broadcasted_iota(jnp.int32, (v.shape[0], LANES), 1)
    return jnp.concatenate([v, jnp.where(lane == 0, 1.0, 0.0).astype(BF16)], axis=1)


def _softmax_pv(q, keys, vals_aug):
    e_dim = vals_aug[0].shape[-1] - LANES
    o = None
    for p, vv in zip(_exp2_scores(q, keys), vals_aug):
        part = jnp.dot(p.astype(BF16), vv, preferred_element_type=F32)
        o = part if o is None else o + part
    return o[:, :e_dim] / o[:, e_dim:e_dim + 1]


def _diff_attn_body(lay, lambda_init, q_ref, kc_ref, kl_ref, vc_ref, vl_ref, lam_ref, g_ref, o_ref):
    j = pl.program_id(1)
    lv = lam_ref[...]
    lam = (jnp.exp(jnp.sum(lv[0:1] * lv[1:2], axis=1, keepdims=True))
           - jnp.exp(jnp.sum(lv[2:3] * lv[3:4], axis=1, keepdims=True)) + lambda_init)

    def run(with_latent):
        for h in range(B_HEADS):
            vs = slice(h * B_V_DIM, (h + 1) * B_V_DIM)
            vals = [_with_ones_column(vc_ref[:, vs])] + ([_with_ones_column(vl_ref[:, vs])] if with_latent else [])
            outs = []
            for mi in range(2):
                cs = slice((2 * h + mi) * LANES, (2 * h + mi + 1) * LANES)
                keys = [kc_ref[:, cs]] + ([kl_ref[:, cs]] if with_latent else [])
                outs.append(_softmax_pv(q_ref[:, cs], keys, vals))
            o = outs[0] - lam * outs[1]
            ms = jnp.mean(o * o, axis=-1, keepdims=True)
            o = o * lax.rsqrt(ms + B_SUBLN_EPS) * g_ref[...] * (1.0 - lambda_init)
            o_ref[:, vs] = o.astype(BF16)

    @pl.when(j < lay.nct)
    def _():
        run(False)

    @pl.when(j >= lay.nct)
    def _():
        run(True)


def _diff_attention(lay, q, k, v, lam_vecs, subln_g, lambda_init):
    tm = lay.tm
    w = B_WIDTH
    lat0 = lay.rows_c // lay.S
    return pl.pallas_call(
        functools.partial(_diff_attn_body, lay, lambda_init),
        out_shape=jax.ShapeDtypeStruct((lay.rows, w), BF16),
        grid=(lay.B, lay.nct + lay.nlt),
        in_specs=[pl.BlockSpec((tm, B_QK_PAD), lambda b, j: (lay.seq_tile(b, j), 0)),
                  pl.BlockSpec((lay.CTX, B_QK_PAD), lambda b, j: (b, 0)),
                  pl.BlockSpec((lay.S, B_QK_PAD), lambda b, j: (lat0 + b, 0)),
                  pl.BlockSpec((lay.CTX, w), lambda b, j: (b, 0)),
                  pl.BlockSpec((lay.S, w), lambda b, j: (lat0 + b, 0)),
                  pl.BlockSpec((4, B_HEAD_DIM), lambda b, j: (0, 0)),
                  pl.BlockSpec((1, B_V_DIM), lambda b, j: (0, 0))],
        out_specs=pl.BlockSpec((tm, w), lambda b, j: (lay.seq_tile(b, j), 0)),
        compiler_params=_cparams(("arbitrary", "arbitrary")),
        name="diff_attention",
    )(q, k, k, v, v, lam_vecs, subln_g.reshape(1, B_V_DIM))


def _odd_in_body(x_ref, mod_ref, w_ref, cos_ref, sin_ref, qn_ref, kn_ref, q_ref, k_ref, v_ref):
    m = mod_ref[0]
    h = (x_ref[...] * (1.0 + m[1:2]) + m[0:1]).astype(BF16)
    cos, sin = cos_ref[...], sin_ref[...]

    def norm_rope(p, g, scale):
        ms = jnp.mean(p * p, axis=-1, keepdims=True)
        y = p * lax.rsqrt(ms + QK_NORM_EPS) * g
        return (_rope_lanes(y, cos, sin, C_HEAD_DIM) * scale).astype(BF16)

    p = jnp.dot(h, w_ref[...], preferred_element_type=F32)
    for hd in range(C_HEADS):
        cs = slice(hd * C_HEAD_DIM, (hd + 1) * C_HEAD_DIM)
        q_ref[:, cs] = norm_rope(p[:, cs], qn_ref[...], LOG2_E * C_HEAD_DIM ** -0.5)
    for hd in range(C_KV_HEADS):
        cs = slice(hd * C_HEAD_DIM, (hd + 1) * C_HEAD_DIM)
        k_ref[:, cs] = norm_rope(p[:, C_Q + hd * C_HEAD_DIM:C_Q + (hd + 1) * C_HEAD_DIM], kn_ref[...], 1.0)
    v_ref[...] = p[:, C_Q + C_KV:].astype(BF16)


def _odd_in_proj(lay, x, mod, w_bf16, cos, sin, qn_g, kn_g):
    tm, d = lay.tm, D_MODEL
    row = lambda i: (i, 0)
    full = lambda i: (0, 0)
    return pl.pallas_call(
        _odd_in_body,
        out_shape=(jax.ShapeDtypeStruct((lay.rows, C_Q), BF16),
                   jax.ShapeDtypeStruct((lay.rows, C_KV), BF16),
                   jax.ShapeDtypeStruct((lay.rows, C_KV), BF16)),
        grid=(lay.ntiles,),
        in_specs=[pl.BlockSpec((tm, d), row),
                  pl.BlockSpec((1, 6, d), lambda i: (lay.mod_row(i, tm), 0, 0)),
                  pl.BlockSpec((d, ODD_IN), full),
                  pl.BlockSpec((tm, C_HEAD_DIM), lambda i: (lay.pos_tile(i), 0)),
                  pl.BlockSpec((tm, C_HEAD_DIM), lambda i: (lay.pos_tile(i), 0)),
                  pl.BlockSpec((1, C_HEAD_DIM), full),
                  pl.BlockSpec((1, C_HEAD_DIM), full)],
        out_specs=(pl.BlockSpec((tm, C_Q), row), pl.BlockSpec((tm, C_KV), row), pl.BlockSpec((tm, C_KV), row)),
        compiler_params=_cparams(("arbitrary",)),
        name="odd_in_proj",
    )(x, mod, w_bf16, cos, sin, qn_g.reshape(1, C_HEAD_DIM), kn_g.reshape(1, C_HEAD_DIM))


def _gqa_body(q_ref, kc_ref, kl_ref, vc_ref, vl_ref, o_ref):
    for kvh in range(C_KV_HEADS):
        ks = slice(kvh * C_HEAD_DIM, (kvh + 1) * C_HEAD_DIM)
        keys = [kc_ref[:, ks], kl_ref[:, ks]]
        vals = [_with_ones_column(vc_ref[:, ks]), _with_ones_column(vl_ref[:, ks])]
        for g in range(C_GROUP):
            hd = kvh * C_GROUP + g
            cs = slice(hd * C_HEAD_DIM, (hd + 1) * C_HEAD_DIM)
            o_ref[:, cs] = _softmax_pv(q_ref[:, cs], keys, vals).astype(BF16)


def _gqa_attention(lay, q, k, v):
    tm = lay.tm
    lat0 = lay.rows_c // lay.S
    return pl.pallas_call(
        _gqa_body,
        out_shape=jax.ShapeDtypeStruct((lay.rows_l, C_Q), BF16),
        grid=(lay.B, lay.nlt),
        in_specs=[pl.BlockSpec((tm, C_Q), lambda b, j: (lay.ntiles_c + b * lay.nlt + j, 0)),
                  pl.BlockSpec((lay.CTX, C_KV), lambda b, j: (b, 0)),
                  pl.BlockSpec((lay.S, C_KV), lambda b, j: (lat0 + b, 0)),
                  pl.BlockSpec((lay.CTX, C_KV), lambda b, j: (b, 0)),
                  pl.BlockSpec((lay.S, C_KV), lambda b, j: (lat0 + b, 0))],
        out_specs=pl.BlockSpec((tm, C_Q), lambda b, j: (b * lay.nlt + j, 0)),
        compiler_params=_cparams(("arbitrary", "arbitrary")),
        name="gqa_attention",
    )(q, k, k, v, v)


def _out_proj_body(n_mix, n_ctx_tiles, *refs):
    mix_refs = refs[:n_mix]
    n_x = 1 if n_ctx_tiles is None else 2
    x_refs = refs[n_mix + 1:n_mix + 1 + n_x]
    w_ref = refs[n_mix]
    mod_ref, lg_ref, lb_ref, rw_ref, xo_ref, h_ref, lt_ref = refs[n_mix + 1 + n_x:]
    x = x_refs[0][...] if n_ctx_tiles is None else _row_tile(n_ctx_tiles, *x_refs)
    m = mod_ref[0]
    off = 0
    mix = None
    for mr in mix_refs:
        kw = mr.shape[-1]
        part = jnp.dot(mr[...], w_ref[off:off + kw, :], preferred_element_type=F32)
        mix = part if mix is None else mix + part
        off += kw
    z = DEEPNORM_ALPHA * x + m[2:3] * mix
    xn = _layer_norm_rows(z, lg_ref[...], lb_ref[...])
    xo_ref[...] = xn
    h2 = xn * (1.0 + m[4:5]) + m[3:4]
    h_ref[...] = h2
    lt_ref[...] = _dot3_nt(rw_ref[...], h2)


def _out_proj(lay, mixes, w_bf16, x, row_off, mod, ln_g, ln_b, router_wt, n_rows):
    tm, d = lay.tmo, D_MODEL
    row = lambda i: (i, 0)
    full = lambda i: (0, 0)
    x_tile_off = row_off // tm
    if isinstance(x, tuple):
        assert row_off == 0
        n_ctx_tiles = lay.rows_c // tm
        x_specs = _row_tile_specs(tm, n_ctx_tiles, d)
    else:
        n_ctx_tiles, x = None, (x,)
        x_specs = [pl.BlockSpec((tm, d), lambda i: (i + x_tile_off, 0))]
    in_specs = [pl.BlockSpec((tm, mx.shape[-1]), row) for mx in mixes]
    in_specs += [pl.BlockSpec((d, d), full)] + x_specs + [
                 pl.BlockSpec((1, 6, d), lambda i: (lay.mod_row(i, tm, row_off), 0, 0)),
                 pl.BlockSpec((1, d), full), pl.BlockSpec((1, d), full),
                 pl.BlockSpec((N_EXPERTS, d), full)]
    return pl.pallas_call(
        functools.partial(_out_proj_body, len(mixes), n_ctx_tiles),
        out_shape=(jax.ShapeDtypeStruct((n_rows, d), F32),
                   jax.ShapeDtypeStruct((n_rows, d), F32),
                   jax.ShapeDtypeStruct((N_EXPERTS, n_rows), F32)),
        grid=(n_rows // tm,),
        in_specs=in_specs,
        out_specs=(pl.BlockSpec((tm, d), row), pl.BlockSpec((tm, d), row),
                   pl.BlockSpec((N_EXPERTS, tm), lambda i: (0, i))),
        compiler_params=_cparams(("arbitrary",)),
        name="out_proj_ln",
    )(*mixes, w_bf16, *x, mod, ln_g.reshape(1, d), ln_b.reshape(1, d), router_wt)


_EXPERT_PAIRS = ((0, 1), (0, 2), (0, 3), (1, 2), (1, 3), (2, 3))
N_BUCKETS = N_GROUPS * len(_EXPERT_PAIRS)


def _router_body(lt_ref, rb_ref, o_ref):
    logits = lt_ref[...] + rb_ref[...]
    rows = [logits[e:e + 1, :] for e in range(N_EXPERTS)]
    m = rows[0]
    for x in rows[1:]:
        m = jnp.maximum(m, x)
    ex = [jnp.exp(x - m) for x in rows]
    z = ex[0]
    for x in ex[1:]:
        z = z + x
    p = [x / z for x in ex]

    gscore = []
    for g in range(N_GROUPS):
        a, b, c, d = p[4 * g:4 * g + 4]
        hi1, lo1 = jnp.maximum(a, b), jnp.minimum(a, b)
        hi2, lo2 = jnp.maximum(c, d), jnp.minimum(c, d)
        top1 = jnp.maximum(hi1, hi2)
        top2 = jnp.maximum(jnp.minimum(hi1, hi2), jnp.maximum(lo1, lo2))
        gscore.append(top1 + top2)
    best = []
    for g in range(N_GROUPS):
        ok = None
        for o in range(N_GROUPS):
            if o == g:
                continue
            c = (gscore[g] > gscore[o]) if o < g else (gscore[g] >= gscore[o])
            ok = c if ok is None else jnp.logical_and(ok, c)
        best.append(ok)
    won = []
    for e in range(N_EXPERTS):
        g = e // EXPERTS_PER_GROUP
        rank = jnp.zeros_like(p[e])
        for o in range(4 * g, 4 * g + 4):
            if o == e:
                continue
            ahead = (p[o] > p[e]) if o > e else (p[o] >= p[e])
            rank = rank + jnp.where(ahead, 1.0, 0.0)
        won.append(jnp.where(jnp.logical_and(best[g], rank < 1.5), 1.0, 0.0))
    tot = won[0] * p[0]
    for e in range(1, N_EXPERTS):
        tot = tot + won[e] * p[e]
    bucket = jnp.zeros_like(tot)
    gate_a = jnp.zeros_like(tot)
    gate_b = jnp.zeros_like(tot)
    for g in range(N_GROUPS):
        for pid, (a, b) in enumerate(_EXPERT_PAIRS):
            ind = won[4 * g + a] * won[4 * g + b]
            bucket = bucket + ind * float(len(_EXPERT_PAIRS) * g + pid)
            gate_a = gate_a + ind * p[4 * g + a]
            gate_b = gate_b + ind * p[4 * g + b]
    o_ref[...] = jnp.zeros_like(o_ref)
    o_ref[0:1, :] = bucket
    o_ref[1:2, :] = gate_a / tot
    o_ref[2:3, :] = gate_b / tot


def _router(logits_t, router_b):
    e, n = logits_t.shape
    tr = math.gcd(2048, n)
    return pl.pallas_call(
        _router_body,
        out_shape=jax.ShapeDtypeStruct((SUBLANES, n), F32),
        grid=(n // tr,),
        in_specs=[pl.BlockSpec((e, tr), lambda i: (0, i)), pl.BlockSpec((e, 1), lambda i: (0, 0))],
        out_specs=pl.BlockSpec((SUBLANES, tr), lambda i: (0, i)),
        compiler_params=_cparams(("arbitrary",)),
        name="router_gates",
    )(logits_t, router_b.reshape(e, 1))


GATE_LANES = LANES


def _moe_plan(bucket, tg):
    n = bucket.shape[0]
    n_tiles = n // tg + N_BUCKETS
    ids = jnp.arange(N_BUCKETS, dtype=jnp.int32)
    onehot = (bucket[:, None] == ids[None, :]).astype(jnp.int32)
    csum = jnp.cumsum(onehot, axis=0)
    counts = csum[-1]
    tiles = (counts + tg - 1) // tg
    tile_end = jnp.cumsum(tiles)
    row_start = (tile_end - tiles) * tg
    pos = jnp.sum(onehot * (row_start[None, :] + csum - 1), axis=1).astype(jnp.int32)
    n_used = tile_end[-1]
    tile_ids = jnp.minimum(jnp.arange(n_tiles, dtype=jnp.int32), n_used - 1)
    tile_bucket = jnp.sum((tile_ids[:, None] >= tile_end[None, :]).astype(jnp.int32), axis=1)
    pair_a = jnp.asarray([a for a, _ in _EXPERT_PAIRS], jnp.int32)
    pair_b = jnp.asarray([b for _, b in _EXPERT_PAIRS], jnp.int32)
    grp, pid = tile_bucket // len(_EXPERT_PAIRS), tile_bucket % len(_EXPERT_PAIRS)
    ea = grp * EXPERTS_PER_GROUP + pair_a[pid]
    eb = grp * EXPERTS_PER_GROUP + pair_b[pid]
    spare = n_used + ids
    zero_tiles = jnp.concatenate([jnp.where(tiles > 0, tile_end - 1, -1),
                                  jnp.where(spare < n_tiles, spare, -1)]).astype(jnp.int32)
    return (pos, ea.astype(jnp.int32), eb.astype(jnp.int32), n_used.reshape(1).astype(jnp.int32),
            zero_tiles, n_tiles)


def _row_copy(src_ref, src_row, dst_ref, dst_row, sem):
    return pltpu.make_async_copy(src_ref.at[pl.ds(src_row, 1), :], dst_ref.at[pl.ds(dst_row, 1), :], sem)


ROW_DMA_UNROLL = 8


def _row_copies(n_rows, row_copy, whole_tile_copy):
    def issue(blk, carry):
        for u in range(ROW_DMA_UNROLL):
            row_copy(blk * ROW_DMA_UNROLL + u).start(priority=u % 2)
        return carry

    lax.fori_loop(0, n_rows // ROW_DMA_UNROLL, issue, 0)
    whole_tile_copy.wait()


def _dispatch_body(tg, pos_ref, zero_tiles_ref, h_ref, g_ref, xs_ref, aug_ref, sem):
    base = pl.program_id(0) * tg

    @pl.when(pl.program_id(0) == 0)
    def _():
        aug_ref[...] = jnp.zeros_like(aug_ref)

        def fill(k):
            row0 = pl.multiple_of(zero_tiles_ref[k] * tg, tg)
            return pltpu.make_async_copy(aug_ref, xs_ref.at[pl.ds(row0, tg), :], sem)

        for k in range(2 * N_BUCKETS):
            pl.when(zero_tiles_ref[k] >= 0)(lambda k=k: fill(k).start())
        for k in range(2 * N_BUCKETS):
            pl.when(zero_tiles_ref[k] >= 0)(lambda k=k: fill(k).wait())

    aug_ref[:, :D_MODEL] = h_ref[...]
    aug_ref[:, D_MODEL:] = g_ref[...]

    _row_copies(tg, lambda r: _row_copy(aug_ref, r, xs_ref, pos_ref[base + r], sem),
                pltpu.make_async_copy(aug_ref, xs_ref.at[pl.ds(0, tg), :], sem))


def _moe_dispatch(pos, zero_tiles, h2, gate_rows, n_tiles, tg):
    n, d = h2.shape
    wide = d + GATE_LANES
    grid_spec = pltpu.PrefetchScalarGridSpec(
        num_scalar_prefetch=2,
        grid=(n // tg,),
        in_specs=[pl.BlockSpec((tg, d), lambda i, pos_ref, zt_ref: (i, 0)),
                  pl.BlockSpec((tg, GATE_LANES), lambda i, pos_ref, zt_ref: (i, 0))],
        out_specs=pl.BlockSpec(memory_space=pl.ANY),
        scratch_shapes=[pltpu.VMEM((tg, wide), F32), pltpu.SemaphoreType.DMA(())],
    )
    return pl.pallas_call(
        functools.partial(_dispatch_body, tg),
        out_shape=jax.ShapeDtypeStruct((n_tiles * tg, wide), F32),
        grid_spec=grid_spec,
        compiler_params=_cparams(("arbitrary",)),
        name="moe_dispatch",
    )(pos, zero_tiles, h2, gate_rows)


def _grouped_body(ea_ref, eb_ref, nu_ref, xs_ref, w1a_ref, w3a_ref, w2a_ref, w1b_ref, w3b_ref, w2b_ref, y_ref):
    del ea_ref, eb_ref

    @pl.when(pl.program_id(0) < nu_ref[0])
    def _():
        x = xs_ref[:, :D_MODEL].astype(BF16)

        def expert(w1_ref, w3_ref, w2_ref, gate):
            h1 = jnp.dot(x, w1_ref[0], preferred_element_type=F32)
            h3 = jnp.dot(x, w3_ref[0], preferred_element_type=F32)
            hid = (h1 * _sigmoid(h1) * h3 * gate).astype(BF16)
            return jnp.dot(hid, w2_ref[0], preferred_element_type=F32)

        y_ref[...] = (expert(w1a_ref, w3a_ref, w2a_ref, xs_ref[:, D_MODEL:D_MODEL + 1])
                      + expert(w1b_ref, w3b_ref, w2b_ref, xs_ref[:, D_MODEL + 1:D_MODEL + 2]))

    @pl.when(pl.program_id(0) >= nu_ref[0])
    def _():
        y_ref[...] = jnp.zeros_like(y_ref)


def _moe_grouped(xs, ea, eb, n_used, w1, w3, w2, tg):
    rows, wide = xs.shape
    d, ff = D_MODEL, EXPERT_FF
    n_tiles = rows // tg
    tile = lambda i, ea_r, eb_r, nu_r: (jnp.minimum(i, nu_r[0] - 1), 0)
    up_a = pl.BlockSpec((1, d, ff), lambda i, ea_r, eb_r, nu_r: (ea_r[i], 0, 0))
    up_b = pl.BlockSpec((1, d, ff), lambda i, ea_r, eb_r, nu_r: (eb_r[i], 0, 0))
    dn_a = pl.BlockSpec((1, ff, d), lambda i, ea_r, eb_r, nu_r: (ea_r[i], 0, 0))
    dn_b = pl.BlockSpec((1, ff, d), lambda i, ea_r, eb_r, nu_r: (eb_r[i], 0, 0))
    grid_spec = pltpu.PrefetchScalarGridSpec(
        num_scalar_prefetch=3,
        grid=(n_tiles,),
        in_specs=[pl.BlockSpec((tg, wide), tile), up_a, up_a, dn_a, up_b, up_b, dn_b],
        out_specs=pl.BlockSpec((tg, d), lambda i, ea_r, eb_r, nu_r: (i, 0)),
    )
    return pl.pallas_call(
        _grouped_body,
        out_shape=jax.ShapeDtypeStruct((rows, d), F32),
        grid_spec=grid_spec,
        compiler_params=_cparams(("arbitrary",)),
        name="moe_grouped",
    )(ea, eb, n_used, xs, w1, w3, w2, w1, w3, w2)


def _combine_body(tg, pos_ref, ys_ref, x_ref, mod_ref, lg_ref, lb_ref, o_ref, buf_ref, sem):
    base = pl.program_id(0) * tg

    _row_copies(tg, lambda r: _row_copy(ys_ref, pos_ref[base + r], buf_ref, r, sem),
                pltpu.make_async_copy(ys_ref.at[pl.ds(0, tg), :], buf_ref, sem))
    m = mod_ref[0]
    z = DEEPNORM_ALPHA * x_ref[...] + m[5:6] * buf_ref[...]
    o_ref[...] = _layer_norm_rows(z, lg_ref[...], lb_ref[...])


def _moe_combine(lay, pos, ys, x, mod, ln_g, ln_b, row_off, tg):
    n, d = x.shape
    grid_spec = pltpu.PrefetchScalarGridSpec(
        num_scalar_prefetch=1,
        grid=(n // tg,),
        in_specs=[pl.BlockSpec(memory_space=pl.ANY),
                  pl.BlockSpec((tg, d), lambda i, pos_ref: (i, 0)),
                  pl.BlockSpec((1, 6, d), lambda i, pos_ref: (lay.mod_row(i, tg, row_off), 0, 0)),
                  pl.BlockSpec((1, d), lambda i, pos_ref: (0, 0)),
                  pl.BlockSpec((1, d), lambda i, pos_ref: (0, 0))],
        out_specs=pl.BlockSpec((tg, d), lambda i, pos_ref: (i, 0)),
        scratch_shapes=[pltpu.VMEM((tg, d), F32), pltpu.SemaphoreType.DMA(())],
    )
    return pl.pallas_call(
        functools.partial(_combine_body, tg),
        out_shape=jax.ShapeDtypeStruct((n, d), F32),
        grid_spec=grid_spec,
        compiler_params=_cparams(("arbitrary",)),
        name="moe_combine_ln",
    )(pos, ys, x, mod, ln_g.reshape(1, d), ln_b.reshape(1, d))


def _moe(lay, h2, routed, w1, w3, w2, x, mod, ln_g, ln_b, row_off):
    tg = lay.tm
    bucket = routed[0].astype(jnp.int32)
    gate_rows = jnp.pad(routed[1:3].T, ((0, 0), (0, GATE_LANES - 2)))
    pos, ea, eb, n_used, zero_tiles, n_tiles = _moe_plan(bucket, tg)
    xs = _moe_dispatch(pos, zero_tiles, h2, gate_rows, n_tiles, tg)
    ys = _moe_grouped(xs, ea, eb, n_used, w1, w3, w2, tg)
    return _moe_combine(lay, pos, ys, x, mod, ln_g, ln_b, row_off, tg)


def kernel(x, c, ctx, c_ctx, router_w, router_b, ada_w, ada_b, ln1_g, ln1_b, ln2_g, ln2_b, moe_w1, moe_w3, moe_w2, ev_w_in, ev_w_out, ev_a_mu, ev_a_w0, ev_a_w2, ev_a_a0, ev_a_a2, ev_a_g2, ev_a_kk, ev_a_ka, ev_a_rk, ev_a_lnx_g, ev_a_lnx_b, ev_b_lam, ev_b_subln_g, od_w_in, od_w_out, od_qn_g, od_kn_g):
    bsz, seq, d = x.shape
    ctx_len = ctx.shape[1]
    assert d == D_MODEL and ada_w.shape[0] == DEPTH and seq % GRID_W == 0
    lay = _Layout(bsz, ctx_len, seq)

    cvec = jnp.zeros((lay.mod_rows, d), F32).at[:bsz].set(c).at[bsz].set(c_ctx)
    mods = _ada_mods(cvec, ada_w, ada_b).reshape(DEPTH, lay.mod_rows, 6, d)

    xs = (ctx.reshape(lay.rows_c, d), x.reshape(lay.rows_l, d))
    router_wt = router_w.T

    cos_b, sin_b = _rope_tables(lay, B_HEAD_DIM, B_QK)
    cos_c, sin_c = _rope_tables(lay, C_HEAD_DIM, C_HEAD_DIM)

    for i in range(DEPTH):
        last = i == DEPTH - 1
        j = i // 2
        mod = mods[i]
        if i % 2 == 0:
            lambda_init = 0.8 - 0.6 * math.exp(-0.3 * i)
            x_ctx, x_lat = xs if isinstance(xs, tuple) else (xs[:lay.rows_c], xs[lay.rows_c:])
            pa, q, k, v = _even_in_proj(lay, x_ctx, x_lat, mod, ev_w_in[j].astype(BF16), cos_b, sin_b)
            decay, scan_ops, g_, bonus = _rwkv_features(
                lay, pa, ev_a_mu[j], ev_a_w0[j], ev_a_w2[j], ev_a_a0[j], ev_a_a2[j], ev_a_g2[j],
                ev_a_kk[j], ev_a_ka[j], ev_a_rk[j].reshape(-1))
            chain_decay, chain_ops = _to_chains(lay, decay[None]), _to_chains(lay, scan_ops)
            yd = _diff_attention(lay, q, k, v, ev_b_lam[j], ev_b_subln_g[j], lambda_init)
            y_f, y_b = _wkv_scan(lay, chain_decay, chain_ops)
            ya = _rwkv_readout(lay, _from_chains(lay, y_f, y_b), bonus, g_, ev_a_lnx_g[j], ev_a_lnx_b[j])
            mixes, w_out = [ya, yd], ev_w_out[j]
            n_rows, tile_off = (lay.rows_l, lay.ntiles_c) if last else (lay.rows, 0)
            if last:
                mixes = [mx[lay.rows_c:] for mx in mixes]
        else:
            xs = jnp.concatenate(xs, axis=0) if isinstance(xs, tuple) else xs
            q, k, v = _odd_in_proj(lay, xs, mod, od_w_in[j].astype(BF16), cos_c, sin_c, od_qn_g[j], od_kn_g[j])
            assert last, "an odd layer that must also update the context stream is not supported"
            o = _gqa_attention(lay, q, k, v)
            mixes, w_out = [o], od_w_out[j]
            n_rows, tile_off = lay.rows_l, lay.ntiles_c
        row_off = tile_off * lay.tm
        if isinstance(xs, tuple) and row_off:
            xs = jnp.concatenate(xs, axis=0)
        x_new, h2, logits_t = _out_proj(lay, mixes, w_out.astype(BF16), xs, row_off, mod,
                                        ln1_g[i], ln1_b[i], router_wt, n_rows)
        routed = _router(logits_t, router_b)
        xs = _moe(lay, h2, routed, moe_w1[i].astype(BF16), moe_w3[i].astype(BF16), moe_w2[i].astype(BF16),
                  x_new, mod, ln2_g[i], ln2_b[i], row_off)
    return xs.reshape(bsz, seq, d)
```

```python
import functools
import math

import jax
import jax.numpy as jnp
from jax import lax
from jax.experimental import pallas as pl
from jax.experimental.pallas import tpu as pltpu

F32 = jnp.float32
BF16 = jnp.bfloat16

D_MODEL = 1024
DEPTH = 2
GRID_W = 64
ROPE_THETA = 10000.0
LN_EPS = 1e-5
LOG2_E = 1.4426950408889634
DEEPNORM_ALPHA = (2 * DEPTH) ** 0.25

A_HEAD_DIM = 64
A_HEADS = 8
A_WIDTH = 512
A_LORA = 64
A_GATE_LORA = 128
A_GN_EPS = 64e-5
A_IN = 3 * A_WIDTH + 4 * A_LORA + A_GATE_LORA
_SCAN_R, _SCAN_K, _SCAN_V, _SCAN_A, _SCAN_B = range(5)
N_SCAN_OPS = 5
SCAN_ROW_GROUPS = 2

B_HEAD_DIM = 64
B_V_DIM = 128
B_HEADS = 4
B_WIDTH = 512
B_QK = 512
B_QK_PAD = (B_QK // B_HEAD_DIM) * 128
B_SUBLN_EPS = 1e-5
EVEN_IN = A_IN + 2 * B_QK + B_WIDTH

C_HEAD_DIM = 128
C_HEADS = 8
C_KV_HEADS = 2
C_GROUP = 4
C_Q = 1024
C_KV = 256
ODD_IN = C_Q + 2 * C_KV
QK_NORM_EPS = 1e-6

N_EXPERTS = 16
N_GROUPS = 4
EXPERTS_PER_GROUP = 4
EXPERT_FF = 512

VMEM_LIMIT_BYTES = 56 * 1024 * 1024
LANES = 128
SUBLANES = 8


def _cparams(sem):
    return pltpu.CompilerParams(dimension_semantics=sem, vmem_limit_bytes=VMEM_LIMIT_BYTES)


def _dot(a, b):
    return jnp.dot(a.astype(BF16), b.astype(BF16), preferred_element_type=F32)


def _dot_nt(a, b):
    return lax.dot_general(a.astype(BF16), b.astype(BF16), (((1,), (1,)), ((), ())),
                           preferred_element_type=F32)


def _split(a):
    hi = a.astype(BF16)
    lo = (a - hi.astype(F32)).astype(BF16)
    return hi, lo


def _dot3(a, b):
    ah, al = _split(a)
    bh, bl = _split(b)
    return (jnp.dot(ah, bh, preferred_element_type=F32)
            + (jnp.dot(ah, bl, preferred_element_type=F32)
               + jnp.dot(al, bh, preferred_element_type=F32)))


def _dot2_exact_rhs(a, b_bf16):
    ah, al = _split(a)
    return jnp.dot(ah, b_bf16, preferred_element_type=F32) + jnp.dot(al, b_bf16, preferred_element_type=F32)


def _dot3_nt(a, b):
    ah, al = _split(a)
    bh, bl = _split(b)
    dn = (((1,), (1,)), ((), ()))
    return (lax.dot_general(ah, bh, dn, preferred_element_type=F32)
            + (lax.dot_general(ah, bl, dn, preferred_element_type=F32)
               + lax.dot_general(al, bh, dn, preferred_element_type=F32)))


def _sigmoid(x):
    return 1.0 / (1.0 + jnp.exp(-x))


def _layer_norm_rows(z, g, b):
    mu = jnp.mean(z, axis=-1, keepdims=True)
    zc = z - mu
    var = jnp.mean(zc * zc, axis=-1, keepdims=True)
    return zc * lax.rsqrt(var + LN_EPS) * g + b


class _Layout:
    def __init__(self, bsz, ctx_len, seq):
        self.B, self.CTX, self.S = bsz, ctx_len, seq
        self.T = ctx_len + seq
        self.tm = math.gcd(256, math.gcd(ctx_len, seq))
        self.nct = ctx_len // self.tm
        self.nlt = seq // self.tm
        self.rows_c = bsz * ctx_len
        self.rows_l = bsz * seq
        self.rows = self.rows_c + self.rows_l
        self.ntiles_c = bsz * self.nct
        self.ntiles = self.rows // self.tm
        assert self.rows_c % seq == 0, "latent K/V blocks are addressed in units of S rows"
        self.tb = math.gcd(32, math.gcd(ctx_len, seq))
        self.tmo = math.gcd(512, math.gcd(self.rows_c, seq))
        self.mod_rows = -(-(bsz + 1) // SUBLANES) * SUBLANES

    def seq_tile(self, b, j):
        return jnp.where(j < self.nct, b * self.nct + j, self.ntiles_c + b * self.nlt + (j - self.nct))

    def seq_block(self, i):
        il = i - self.ntiles_c
        return (jnp.where(i < self.ntiles_c, i // self.nct, il // self.nlt),
                jnp.where(i < self.ntiles_c, i % self.nct, self.nct + il % self.nlt))

    def mod_row(self, i, tile, row_offset=0):
        r = i * tile + row_offset
        return jnp.where(r < self.rows_c, self.B, (r - self.rows_c) // self.S)

    def pos_tile(self, i):
        il = i - self.ntiles_c
        return jnp.where(i < self.ntiles_c, i % self.nct, self.nct + il % self.nlt)


def _ada_body(cv_ref, w_ref, b_ref, o_ref):
    cv = cv_ref[...]
    s = cv * _sigmoid(cv)
    o_ref[0] = _dot3(s, w_ref[0]) + b_ref[0]


def _ada_mods(cvec, ada_w, ada_b):
    depth, d, n = ada_w.shape
    r = cvec.shape[0]
    tn = 512
    return pl.pallas_call(
        _ada_body,
        out_shape=jax.ShapeDtypeStruct((depth, r, n), F32),
        grid=(depth, n // tn),
        in_specs=[pl.BlockSpec((r, d), lambda l, j: (0, 0)),
                  pl.BlockSpec((1, d, tn), lambda l, j: (l, 0, j)),
                  pl.BlockSpec((1, 1, tn), lambda l, j: (l, 0, j))],
        out_specs=pl.BlockSpec((1, r, tn), lambda l, j: (l, 0, j)),
        compiler_params=_cparams(("arbitrary", "arbitrary")),
        name="ada_mods",
    )(cvec, ada_w, ada_b.reshape(depth, 1, n))


def _rope_tables(lay, head_dim, width):
    rows = lay.S // GRID_W
    rr, cc = jnp.meshgrid(jnp.arange(rows), jnp.arange(GRID_W), indexing="ij")
    row_pos = rr.reshape(-1).astype(F32)
    col_pos = cc.reshape(-1).astype(F32)
    axis_dim = head_dim // 2
    inv = ROPE_THETA ** (-jnp.arange(0, axis_dim, 2, dtype=F32) / axis_dim)
    ang = jnp.concatenate([row_pos[:, None] * inv, col_pos[:, None] * inv], -1)
    cos, sin = jnp.cos(ang), jnp.sin(ang)
    cos = jnp.concatenate([jnp.ones((lay.CTX, head_dim // 2), F32), cos], 0)
    sin = jnp.concatenate([jnp.zeros((lay.CTX, head_dim // 2), F32), sin], 0)
    cos_h = jnp.concatenate([cos, cos], -1)
    sin_h = jnp.concatenate([-sin, sin], -1)
    reps = width // head_dim
    return jnp.tile(cos_h, (1, reps)), jnp.tile(sin_h, (1, reps))


def _rope_lanes(x, cos, sin, head_dim):
    w = x.shape[-1]
    half = head_dim // 2
    if head_dim == LANES and w == LANES:
        rot = pltpu.roll(x, half, 1)
    else:
        fwd = pltpu.roll(x, w - half, 1)
        bwd = pltpu.roll(x, half, 1)
        lane = lax.broadcasted_iota(jnp.int32, x.shape, 1)
        rot = jnp.where((lane % head_dim) < half, fwd, bwd)
    return x * cos + rot * sin


def _row_tile(n_ctx_tiles, xc_ref, xl_ref):
    return jnp.where(pl.program_id(0) < n_ctx_tiles, xc_ref[...], xl_ref[...])


def _row_tile_specs(tile, n_ctx_tiles, d):
    return [pl.BlockSpec((tile, d), lambda i: (jnp.minimum(i, n_ctx_tiles - 1), 0)),
            pl.BlockSpec((tile, d), lambda i: (jnp.maximum(i - n_ctx_tiles, 0), 0))]


def _even_in_body(n_ctx_tiles, xc_ref, xl_ref, mod_ref, w_ref, cos_ref, sin_ref, pa_ref, q_ref, k_ref, v_ref):
    m = mod_ref[0]
    h = (_row_tile(n_ctx_tiles, xc_ref, xl_ref) * (1.0 + m[1:2]) + m[0:1]).astype(BF16)
    pa_ref[...] = jnp.dot(h, w_ref[:, :A_IN], preferred_element_type=F32)
    cos, sin = cos_ref[...], sin_ref[...]
    o = A_IN

    def put_maps(dst_ref, val):
        pad = jnp.zeros((val.shape[0], LANES - B_HEAD_DIM), BF16)
        for mp in range(B_QK // B_HEAD_DIM):
            piece = val[:, mp * B_HEAD_DIM:(mp + 1) * B_HEAD_DIM].astype(BF16)
            dst_ref[:, mp * LANES:(mp + 1) * LANES] = jnp.concatenate([piece, pad], axis=1)

    q = jnp.dot(h, w_ref[:, o:o + B_QK], preferred_element_type=F32)
    put_maps(q_ref, _rope_lanes(q, cos, sin, B_HEAD_DIM) * (LOG2_E * B_HEAD_DIM ** -0.5))
    o += B_QK
    k = jnp.dot(h, w_ref[:, o:o + B_QK], preferred_element_type=F32)
    put_maps(k_ref, _rope_lanes(k, cos, sin, B_HEAD_DIM))
    o += B_QK
    v_ref[...] = jnp.dot(h, w_ref[:, o:o + B_WIDTH], preferred_element_type=F32).astype(BF16)


def _even_in_proj(lay, x_ctx, x_lat, mod, w_bf16, cos, sin):
    tm, d = lay.tm, D_MODEL
    row = lambda i: (i, 0)
    return pl.pallas_call(
        functools.partial(_even_in_body, lay.ntiles_c),
        out_shape=(jax.ShapeDtypeStruct((lay.rows, A_IN), F32),
                   jax.ShapeDtypeStruct((lay.rows, B_QK_PAD), BF16),
                   jax.ShapeDtypeStruct((lay.rows, B_QK_PAD), BF16),
                   jax.ShapeDtypeStruct((lay.rows, B_WIDTH), BF16)),
        grid=(lay.ntiles,),
        in_specs=_row_tile_specs(tm, lay.ntiles_c, d) + [
                  pl.BlockSpec((1, 6, d), lambda i: (lay.mod_row(i, tm), 0, 0)),
                  pl.BlockSpec((d, EVEN_IN), lambda i: (0, 0)),
                  pl.BlockSpec((tm, B_QK), lambda i: (lay.pos_tile(i), 0)),
                  pl.BlockSpec((tm, B_QK), lambda i: (lay.pos_tile(i), 0))],
        out_specs=(pl.BlockSpec((tm, A_IN), row), pl.BlockSpec((tm, B_QK_PAD), row),
                   pl.BlockSpec((tm, B_QK_PAD), row), pl.BlockSpec((tm, B_WIDTH), row)),
        compiler_params=_cparams(("arbitrary",)),
        name="even_in_proj",
    )(x_ctx, x_lat, mod, w_bf16, cos, sin)


def _rwkv_feat_body(lay, pa_ref, prev_ref, next_ref, mu_ref, w0_ref, w2_ref, a0_ref, a2_ref, g2_ref,
                    kk_ref, ka_ref, rk_ref, bd_ref, tri_ref, blk_ref,
                    pend_ref, ops_ref, g_ref, bonus_ref):
    def put(idx, d, val):
        ops_ref[idx, d, 0] = val.astype(BF16)

    i = pl.program_id(0)
    tm = lay.tm
    il = i - lay.ntiles_c
    in_ctx = i < lay.ntiles_c
    seg_first = jnp.where(in_ctx, i % lay.nct == 0, il % lay.nlt == 0)
    seg_last = jnp.where(in_ctx, i % lay.nct == lay.nct - 1, il % lay.nlt == lay.nlt - 1)

    pa = pa_ref[...]
    row = lax.broadcasted_iota(jnp.int32, pa.shape, 0)
    prev_edge = jnp.where(seg_first, 0.0, 1.0) * prev_ref[SUBLANES - 1:SUBLANES, :]
    next_edge = jnp.where(seg_last, 0.0, 1.0) * next_ref[0:1, :]
    prev = jnp.where(row == 0, prev_edge, pltpu.roll(pa, 1, 0))
    nxt = jnp.where(row == tm - 1, next_edge, pltpu.roll(pa, tm - 1, 0))
    u = pa + (0.5 * (prev + nxt) - pa) * mu_ref[...]

    o1, o2, o3 = A_WIDTH, 2 * A_WIDTH, 3 * A_WIDTH
    o4 = o3 + 2 * A_LORA
    o5 = o4 + 2 * A_LORA
    r, k, v = u[:, :o1], u[:, o1:o2], u[:, o2:o3]
    bd = bd_ref[...]

    kk = k * kk_ref[...]
    ss = _dot2_exact_rhs(kk * kk, bd)
    kkn = kk / jnp.maximum(jnp.sqrt(ss), 1e-12)
    g = _dot3(_sigmoid(u[:, o5:]), g2_ref[...])

    g_ref[0] = g

    kd_sum = jnp.zeros_like(k)
    for d in range(2):
        wd = u[:, o3 + d * A_LORA:o3 + (d + 1) * A_LORA]
        ad = u[:, o4 + d * A_LORA:o4 + (d + 1) * A_LORA]
        log_decay = -math.exp(-0.5) * _sigmoid(w0_ref[d:d + 1, :] + _dot3(jnp.tanh(wd), w2_ref[d]))
        ld_hi, ld_lo = _split(log_decay)
        cum = (jnp.dot(tri_ref[d], ld_hi, preferred_element_type=F32)
               + jnp.dot(tri_ref[d], ld_lo, preferred_element_type=F32))
        tot = (jnp.dot(blk_ref[...], ld_hi, preferred_element_type=F32)
               + jnp.dot(blk_ref[...], ld_lo, preferred_element_type=F32))
        pend_ref[d, 0] = jnp.exp(tot)
        p = jnp.exp(cum)
        p_inv = 1.0 / p
        a = _sigmoid(a0_ref[d:d + 1, :] + _dot3(ad, a2_ref[d]))
        kd = k * (1.0 + (a - 1.0) * ka_ref[...])
        put(_SCAN_R, d, r * p)
        put(_SCAN_V, d, v)
        put(_SCAN_A, d, -kkn * jnp.exp(cum - log_decay))
        put(_SCAN_K, d, kd * p_inv)
        put(_SCAN_B, d, kkn * a * p_inv)
        kd_sum = kd_sum + kd
    bonus_ref[0] = _dot2_exact_rhs(r * kd_sum * rk_ref[...], bd) * v


def _head_block_diag(width, head_dim):
    h = jnp.arange(width) // head_dim
    return (h[:, None] == h[None, :]).astype(BF16)


def _rwkv_features(lay, pa, mu, w0, w2, a0, a2, g2, k_k, k_a, r_k):
    tm = lay.tm
    hb = tm // SUBLANES
    nb8 = lay.rows // SUBLANES
    row = lambda i: (i, 0)
    full2 = lambda i: (0, 0)
    full3 = lambda i: (0, 0, 0)
    w = A_WIDTH
    out = jax.ShapeDtypeStruct((lay.B, lay.T, w), F32)
    seq = lambda i: lay.seq_block(i) + (0,)
    tb = lay.tb
    nblk = tm // tb
    pos = jnp.arange(tm)
    same_blk = (pos[:, None] // tb) == (pos[None, :] // tb)
    tri = jnp.stack([same_blk & (pos[None, :] <= pos[:, None]),
                     same_blk & (pos[None, :] >= pos[:, None])]).astype(BF16)
    blk = (jnp.arange(nblk)[:, None] == (pos[None, :] // tb)).astype(BF16)
    return pl.pallas_call(
        functools.partial(_rwkv_feat_body, lay),
        out_shape=(jax.ShapeDtypeStruct((2, lay.B, lay.T // tb, w), F32),
                   jax.ShapeDtypeStruct((N_SCAN_OPS, 2, lay.B, lay.T, w), BF16), out, out),
        grid=(lay.ntiles,),
        in_specs=[pl.BlockSpec((tm, A_IN), row),
                  pl.BlockSpec((SUBLANES, A_IN), lambda i: (jnp.maximum(i * hb - 1, 0), 0)),
                  pl.BlockSpec((SUBLANES, A_IN), lambda i: (jnp.minimum((i + 1) * hb, nb8 - 1), 0)),
                  pl.BlockSpec((1, A_IN), full2),
                  pl.BlockSpec((2, w), full2),
                  pl.BlockSpec((2, A_LORA, w), full3),
                  pl.BlockSpec((2, w), full2),
                  pl.BlockSpec((2, A_LORA, w), full3),
                  pl.BlockSpec((A_GATE_LORA, w), full2),
                  pl.BlockSpec((1, w), full2),
                  pl.BlockSpec((1, w), full2),
                  pl.BlockSpec((1, w), full2),
                  pl.BlockSpec((w, w), full2),
                  pl.BlockSpec((2, tm, tm), full3),
                  pl.BlockSpec((nblk, tm), full2)],
        out_specs=(pl.BlockSpec((2, 1, nblk, w), lambda i: (0,) + lay.seq_block(i) + (0,)),
                   pl.BlockSpec((N_SCAN_OPS, 2, 1, tm, w), lambda i: (0, 0) + lay.seq_block(i) + (0,)),
                   pl.BlockSpec((1, tm, w), seq), pl.BlockSpec((1, tm, w), seq)),
        compiler_params=_cparams(("arbitrary",)),
        name="rwkv_features",
    )(pa, pa, pa, mu.reshape(1, A_IN), w0, w2, a0, a2, g2, k_k.reshape(1, w), k_a.reshape(1, w),
      r_k.reshape(1, w), _head_block_diag(w, A_HEAD_DIM), tri, blk)


def _wkv_scan_body(tb, pf_ref, pb_ref, fwd_ref, bwd_ref, yf_ref, yb_ref, s_ref, m_ref):
    n = A_HEAD_DIM
    chains = s_ref.shape[-1]
    is_fwd = lax.broadcasted_iota(jnp.int32, (n, chains), 1) < chains // 2

    @pl.when(pl.program_id(0) == 0)
    def _():
        s_ref[...] = jnp.zeros_like(s_ref)

    def step(t, carry):
        tr = tb - 1 - t
        for idx in range(N_SCAN_OPS):
            m_ref[idx] = jnp.where(is_fwd, fwd_ref[t, idx], bwd_ref[tr, idx]).astype(F32)
        ys = []
        for g in range(SCAN_ROW_GROUPS):
            rows = slice(g * (n // SCAN_ROW_GROUPS), (g + 1) * (n // SCAN_ROW_GROUPS))
            sa = jnp.zeros((n // SCAN_ROW_GROUPS, chains), F32)
            for kk in range(n):
                sa = sa + s_ref[kk, rows, :] * m_ref[_SCAN_A, kk:kk + 1, :]
            v_t = m_ref[_SCAN_V, rows, :]
            y = jnp.zeros((n // SCAN_ROW_GROUPS, chains), F32)
            for kk in range(n):
                s_new = (s_ref[kk, rows, :] + sa * m_ref[_SCAN_B, kk:kk + 1, :]
                         + v_t * m_ref[_SCAN_K, kk:kk + 1, :])
                s_ref[kk, rows, :] = s_new
                y = y + s_new * m_ref[_SCAN_R, kk:kk + 1, :]
            ys.append(y)
        y = jnp.concatenate(ys, axis=0).astype(BF16)
        yf_ref[t] = y
        yb_ref[tr] = y
        return carry

    lax.fori_loop(0, tb, step, 0)
    m_ref[0] = jnp.where(is_fwd, pf_ref[0, 0], pb_ref[0, 0])
    for kk in range(n):
        s_ref[kk] = s_ref[kk] * m_ref[0, kk:kk + 1, :]


def _wkv_scan(lay, block_decay, ops):
    t, nops, n, lanes = ops.shape
    tb = lay.tb
    nctb, nt = lay.CTX // tb, t // tb
    bwd_tile = lambda g: jnp.where(g < nctb, nctb - 1 - g, nt - 1 - (g - nctb))
    out = jax.ShapeDtypeStruct((t, n, lanes), BF16)
    fwd_map = lambda g: (g, 0, 0, 0)
    bwd_map = lambda g: (bwd_tile(g), 0, 0, 0)
    return pl.pallas_call(
        functools.partial(_wkv_scan_body, tb),
        out_shape=(out, out),
        grid=(nt,),
        in_specs=[pl.BlockSpec((1, 1, n, lanes), fwd_map), pl.BlockSpec((1, 1, n, lanes), bwd_map),
                  pl.BlockSpec((tb, nops, n, lanes), fwd_map), pl.BlockSpec((tb, nops, n, lanes), bwd_map)],
        out_specs=(pl.BlockSpec((tb, n, lanes), lambda g: (g, 0, 0)),
                   pl.BlockSpec((tb, n, lanes), lambda g: (bwd_tile(g), 0, 0))),
        scratch_shapes=[pltpu.VMEM((n, n, lanes), F32), pltpu.VMEM((nops, n, lanes), F32)],
        compiler_params=_cparams(("arbitrary",)),
        name="wkv7_scan",
    )(block_decay, block_decay, ops, ops)


def _to_chains(lay, ops):
    h, n = A_HEADS, A_HEAD_DIM
    nops, length = ops.shape[0], ops.shape[3]
    x = ops.reshape(nops, 2, lay.B, length, h, n).transpose(3, 0, 5, 1, 2, 4)
    return x.reshape(length, nops, n, 2 * lay.B * h)


def _from_chains(lay, y_f, y_b):
    h, n = A_HEADS, A_HEAD_DIM
    low = jnp.arange(2 * lay.B * h) < lay.B * h
    y = jnp.where(low, y_f, y_b)
    return y.reshape(lay.T, n, 2, lay.B, h).transpose(2, 3, 0, 4, 1).reshape(2, lay.B, lay.T, h * n)


def _rwkv_readout_body(yf_ref, yb_ref, bonus_ref, g_ref, lg_ref, lb_ref, bd_ref, o_ref):
    y = yf_ref[0, 0].astype(F32) + yb_ref[0, 0].astype(F32)
    bd = bd_ref[...]
    inv_n = 1.0 / A_HEAD_DIM
    mu = _dot2_exact_rhs(y, bd) * inv_n
    yc = y - mu
    var = _dot2_exact_rhs(yc * yc, bd) * inv_n
    yn = yc * lax.rsqrt(var + A_GN_EPS) * lg_ref[...] + lb_ref[...]
    o_ref[...] = ((yn + bonus_ref[0]) * g_ref[0]).astype(BF16)


def _rwkv_readout(lay, y2, bonus, g, lnx_g, lnx_b):
    tm, w = lay.tm, A_WIDTH
    row = lambda i: (i, 0)
    full = lambda i: (0, 0)
    seq = lambda i: lay.seq_block(i) + (0,)
    return pl.pallas_call(
        _rwkv_readout_body,
        out_shape=jax.ShapeDtypeStruct((lay.rows, w), BF16),
        grid=(lay.ntiles,),
        in_specs=[pl.BlockSpec((1, 1, tm, w), lambda i: (0,) + seq(i)),
                  pl.BlockSpec((1, 1, tm, w), lambda i: (1,) + seq(i)),
                  pl.BlockSpec((1, tm, w), seq), pl.BlockSpec((1, tm, w), seq),
                  pl.BlockSpec((1, w), full), pl.BlockSpec((1, w), full), pl.BlockSpec((w, w), full)],
        out_specs=pl.BlockSpec((tm, w), row),
        compiler_params=_cparams(("arbitrary",)),
        name="rwkv_readout",
    )(y2, y2, bonus, g, lnx_g.reshape(1, w), lnx_b.reshape(1, w), _head_block_diag(w, A_HEAD_DIM))


def _exp2_scores(q, keys):
    scores = [_dot_nt(q, kk) for kk in keys]
    m = scores[0].max(axis=-1, keepdims=True)
    for s in scores[1:]:
        m = jnp.maximum(m, s.max(axis=-1, keepdims=True))
    return [jnp.exp2(s - m) for s in scores]


def _with_ones_column(v):
    lane = lax.broadcasted_iota(jnp.int32, (v.shape[0], LANES), 1)
    return jnp.concatenate([v, jnp.where(lane == 0, 1.0, 0.0).astype(BF16)], axis=1)


def _softmax_pv(q, keys, vals_aug):
    e_dim = vals_aug[0].shape[-1] - LANES
    o = None
    for p, vv in zip(_exp2_scores(q, keys), vals_aug):
        part = jnp.dot(p.astype(BF16), vv, preferred_element_type=F32)
        o = part if o is None else o + part
    return o[:, :e_dim] / o[:, e_dim:e_dim + 1]


def _diff_attn_body(lay, lambda_init, q_ref, kc_ref, kl_ref, vc_ref, vl_ref, lam_ref, g_ref, o_ref):
    j = pl.program_id(1)
    lv = lam_ref[...]
    lam = (jnp.exp(jnp.sum(lv[0:1] * lv[1:2], axis=1, keepdims=True))
           - jnp.exp(jnp.sum(lv[2:3] * lv[3:4], axis=1, keepdims=True)) + lambda_init)

    def run(with_latent):
        for h in range(B_HEADS):
            vs = slice(h * B_V_DIM, (h + 1) * B_V_DIM)
            vals = [_with_ones_column(vc_ref[:, vs])] + ([_with_ones_column(vl_ref[:, vs])] if with_latent else [])
            outs = []
            for mi in range(2):
                cs = slice((2 * h + mi) * LANES, (2 * h + mi + 1) * LANES)
                keys = [kc_ref[:, cs]] + ([kl_ref[:, cs]] if with_latent else [])
                outs.append(_softmax_pv(q_ref[:, cs], keys, vals))
            o = outs[0] - lam * outs[1]
            ms = jnp.mean(o * o, axis=-1, keepdims=True)
            o = o * lax.rsqrt(ms + B_SUBLN_EPS) * g_ref[...] * (1.0 - lambda_init)
            o_ref[:, vs] = o.astype(BF16)

    @pl.when(j < lay.nct)
    def _():
        run(False)

    @pl.when(j >= lay.nct)
    def _():
        run(True)


def _diff_attention(lay, q, k, v, lam_vecs, subln_g, lambda_init):
    tm = lay.tm
    w = B_WIDTH
    lat0 = lay.rows_c // lay.S
    return pl.pallas_call(
        functools.partial(_diff_attn_body, lay, lambda_init),
        out_shape=jax.ShapeDtypeStruct((lay.rows, w), BF16),
        grid=(lay.B, lay.nct + lay.nlt),
        in_specs=[pl.BlockSpec((tm, B_QK_PAD), lambda b, j: (lay.seq_tile(b, j), 0)),
                  pl.BlockSpec((lay.CTX, B_QK_PAD), lambda b, j: (b, 0)),
                  pl.BlockSpec((lay.S, B_QK_PAD), lambda b, j: (lat0 + b, 0)),
                  pl.BlockSpec((lay.CTX, w), lambda b, j: (b, 0)),
                  pl.BlockSpec((lay.S, w), lambda b, j: (lat0 + b, 0)),
                  pl.BlockSpec((4, B_HEAD_DIM), lambda b, j: (0, 0)),
                  pl.BlockSpec((1, B_V_DIM), lambda b, j: (0, 0))],
        out_specs=pl.BlockSpec((tm, w), lambda b, j: (lay.seq_tile(b, j), 0)),
        compiler_params=_cparams(("arbitrary", "arbitrary")),
        name="diff_attention",
    )(q, k, k, v, v, lam_vecs, subln_g.reshape(1, B_V_DIM))


def _odd_in_body(x_ref, mod_ref, w_ref, cos_ref, sin_ref, qn_ref, kn_ref, q_ref, k_ref, v_ref):
    m = mod_ref[0]
    h = (x_ref[...] * (1.0 + m[1:2]) + m[0:1]).astype(BF16)
    cos, sin = cos_ref[...], sin_ref[...]

    def norm_rope(p, g, scale):
        ms = jnp.mean(p * p, axis=-1, keepdims=True)
        y = p * lax.rsqrt(ms + QK_NORM_EPS) * g
        return (_rope_lanes(y, cos, sin, C_HEAD_DIM) * scale).astype(BF16)

    p = jnp.dot(h, w_ref[...], preferred_element_type=F32)
    for hd in range(C_HEADS):
        cs = slice(hd * C_HEAD_DIM, (hd + 1) * C_HEAD_DIM)
        q_ref[:, cs] = norm_rope(p[:, cs], qn_ref[...], LOG2_E * C_HEAD_DIM ** -0.5)
    for hd in range(C_KV_HEADS):
        cs = slice(hd * C_HEAD_DIM, (hd + 1) * C_HEAD_DIM)
        k_ref[:, cs] = norm_rope(p[:, C_Q + hd * C_HEAD_DIM:C_Q + (hd + 1) * C_HEAD_DIM], kn_ref[...], 1.0)
    v_ref[...] = p[:, C_Q + C_KV:].astype(BF16)


def _odd_in_proj(lay, x, mod, w_bf16, cos, sin, qn_g, kn_g):
    tm, d = lay.tm, D_MODEL
    row = lambda i: (i, 0)
    full = lambda i: (0, 0)
    return pl.pallas_call(
        _odd_in_body,
        out_shape=(jax.ShapeDtypeStruct((lay.rows, C_Q), BF16),
                   jax.ShapeDtypeStruct((lay.rows, C_KV), BF16),
                   jax.ShapeDtypeStruct((lay.rows, C_KV), BF16)),
        grid=(lay.ntiles,),
        in_specs=[pl.BlockSpec((tm, d), row),
                  pl.BlockSpec((1, 6, d), lambda i: (lay.mod_row(i, tm), 0, 0)),
                  pl.BlockSpec((d, ODD_IN), full),
                  pl.BlockSpec((tm, C_HEAD_DIM), lambda i: (lay.pos_tile(i), 0)),
                  pl.BlockSpec((tm, C_HEAD_DIM), lambda i: (lay.pos_tile(i), 0)),
                  pl.BlockSpec((1, C_HEAD_DIM), full),
                  pl.BlockSpec((1, C_HEAD_DIM), full)],
        out_specs=(pl.BlockSpec((tm, C_Q), row), pl.BlockSpec((tm, C_KV), row), pl.BlockSpec((tm, C_KV), row)),
        compiler_params=_cparams(("arbitrary",)),
        name="odd_in_proj",
    )(x, mod, w_bf16, cos, sin, qn_g.reshape(1, C_HEAD_DIM), kn_g.reshape(1, C_HEAD_DIM))


def _gqa_body(q_ref, kc_ref, kl_ref, vc_ref, vl_ref, o_ref):
    for kvh in range(C_KV_HEADS):
        ks = slice(kvh * C_HEAD_DIM, (kvh + 1) * C_HEAD_DIM)
        keys = [kc_ref[:, ks], kl_ref[:, ks]]
        vals = [_with_ones_column(vc_ref[:, ks]), _with_ones_column(vl_ref[:, ks])]
        for g in range(C_GROUP):
            hd = kvh * C_GROUP + g
            cs = slice(hd * C_HEAD_DIM, (hd + 1) * C_HEAD_DIM)
            o_ref[:, cs] = _softmax_pv(q_ref[:, cs], keys, vals).astype(BF16)


def _gqa_attention(lay, q, k, v):
    tm = lay.tm
    lat0 = lay.rows_c // lay.S
    return pl.pallas_call(
        _gqa_body,
        out_shape=jax.ShapeDtypeStruct((lay.rows_l, C_Q), BF16),
        grid=(lay.B, lay.nlt),
        in_specs=[pl.BlockSpec((tm, C_Q), lambda b, j: (lay.ntiles_c + b * lay.nlt + j, 0)),
                  pl.BlockSpec((lay.CTX, C_KV), lambda b, j: (b, 0)),
                  pl.BlockSpec((lay.S, C_KV), lambda b, j: (lat0 + b, 0)),
                  pl.BlockSpec((lay.CTX, C_KV), lambda b, j: (b, 0)),
                  pl.BlockSpec((lay.S, C_KV), lambda b, j: (lat0 + b, 0))],
        out_specs=pl.BlockSpec((tm, C_Q), lambda b, j: (b * lay.nlt + j, 0)),
        compiler_params=_cparams(("arbitrary", "arbitrary")),
        name="gqa_attention",
    )(q, k, k, v, v)


def _out_proj_body(n_mix, n_ctx_tiles, *refs):
    mix_refs = refs[:n_mix]
    n_x = 1 if n_ctx_tiles is None else 2
    x_refs = refs[n_mix + 1:n_mix + 1 + n_x]
    w_ref = refs[n_mix]
    mod_ref, lg_ref, lb_ref, rw_ref, xo_ref, h_ref, lt_ref = refs[n_mix + 1 + n_x:]
    x = x_refs[0][...] if n_ctx_tiles is None else _row_tile(n_ctx_tiles, *x_refs)
    m = mod_ref[0]
    off = 0
    mix = None
    for mr in mix_refs:
        kw = mr.shape[-1]
        part = jnp.dot(mr[...], w_ref[off:off + kw, :], preferred_element_type=F32)
        mix = part if mix is None else mix + part
        off += kw
    z = DEEPNORM_ALPHA * x + m[2:3] * mix
    xn = _layer_norm_rows(z, lg_ref[...], lb_ref[...])
    xo_ref[...] = xn
    h2 = xn * (1.0 + m[4:5]) + m[3:4]
    h_ref[...] = h2
    lt_ref[...] = _dot3_nt(rw_ref[...], h2)


def _out_proj(lay, mixes, w_bf16, x, row_off, mod, ln_g, ln_b, router_wt, n_rows):
    tm, d = lay.tmo, D_MODEL
    row = lambda i: (i, 0)
    full = lambda i: (0, 0)
    x_tile_off = row_off // tm
    if isinstance(x, tuple):
        assert row_off == 0
        n_ctx_tiles = lay.rows_c // tm
        x_specs = _row_tile_specs(tm, n_ctx_tiles, d)
    else:
        n_ctx_tiles, x = None, (x,)
        x_specs = [pl.BlockSpec((tm, d), lambda i: (i + x_tile_off, 0))]
    in_specs = [pl.BlockSpec((tm, mx.shape[-1]), row) for mx in mixes]
    in_specs += [pl.BlockSpec((d, d), full)] + x_specs + [
                 pl.BlockSpec((1, 6, d), lambda i: (lay.mod_row(i, tm, row_off), 0, 0)),
                 pl.BlockSpec((1, d), full), pl.BlockSpec((1, d), full),
                 pl.BlockSpec((N_EXPERTS, d), full)]
    return pl.pallas_call(
        functools.partial(_out_proj_body, len(mixes), n_ctx_tiles),
        out_shape=(jax.ShapeDtypeStruct((n_rows, d), F32),
                   jax.ShapeDtypeStruct((n_rows, d), F32),
                   jax.ShapeDtypeStruct((N_EXPERTS, n_rows), F32)),
        grid=(n_rows // tm,),
        in_specs=in_specs,
        out_specs=(pl.BlockSpec((tm, d), row), pl.BlockSpec((tm, d), row),
                   pl.BlockSpec((N_EXPERTS, tm), lambda i: (0, i))),
        compiler_params=_cparams(("arbitrary",)),
        name="out_proj_ln",
    )(*mixes, w_bf16, *x, mod, ln_g.reshape(1, d), ln_b.reshape(1, d), router_wt)


_EXPERT_PAIRS = ((0, 1), (0, 2), (0, 3), (1, 2), (1, 3), (2, 3))
N_BUCKETS = N_GROUPS * len(_EXPERT_PAIRS)


def _router_body(lt_ref, rb_ref, o_ref):
    logits = lt_ref[...] + rb_ref[...]
    rows = [logits[e] for e in range(N_EXPERTS)]
    m = rows[0]
    for x in rows[1:]:
        m = jnp.maximum(m, x)
    ex = [jnp.exp(x - m) for x in rows]
    z = ex[0]
    for x in ex[1:]:
        z = z + x
    p = [x / z for x in ex]

    gscore = []
    for g in range(N_GROUPS):
        a, b, c, d = p[4 * g:4 * g + 4]
        hi1, lo1 = jnp.maximum(a, b), jnp.minimum(a, b)
        hi2, lo2 = jnp.maximum(c, d), jnp.minimum(c, d)
        top1 = jnp.maximum(hi1, hi2)
        top2 = jnp.maximum(jnp.minimum(hi1, hi2), jnp.maximum(lo1, lo2))
        gscore.append(top1 + top2)
    best = []
    for g in range(N_GROUPS):
        ok = None
        for o in range(N_GROUPS):
            if o == g:
                continue
            c = (gscore[g] > gscore[o]) if o < g else (gscore[g] >= gscore[o])
            ok = c if ok is None else jnp.logical_and(ok, c)
        best.append(ok)
    won = []
    for e in range(N_EXPERTS):
        g = e // EXPERTS_PER_GROUP
        rank = jnp.zeros_like(p[e])
        for o in range(4 * g, 4 * g + 4):
            if o == e:
                continue
            ahead = (p[o] > p[e]) if o > e else (p[o] >= p[e])
            rank = rank + jnp.where(ahead, 1.0, 0.0)
        won.append(jnp.where(jnp.logical_and(best[g], rank < 1.5), 1.0, 0.0))
    tot = won[0] * p[0]
    for e in range(1, N_EXPERTS):
        tot = tot + won[e] * p[e]
    bucket = jnp.zeros_like(tot)
    gate_a = jnp.zeros_like(tot)
    gate_b = jnp.zeros_like(tot)
    for g in range(N_GROUPS):
        for pid, (a, b) in enumerate(_EXPERT_PAIRS):
            ind = won[4 * g + a] * won[4 * g + b]
            bucket = bucket + ind * float(len(_EXPERT_PAIRS) * g + pid)
            gate_a = gate_a + ind * p[4 * g + a]
            gate_b = gate_b + ind * p[4 * g + b]
    o_ref[0] = bucket
    o_ref[1] = gate_a / tot
    o_ref[2] = gate_b / tot


def _router(logits_t, router_b):
    e, n = logits_t.shape
    tr = math.gcd(2048, n)
    blk = (tr // LANES, LANES)
    routed = pl.pallas_call(
        _router_body,
        out_shape=jax.ShapeDtypeStruct((3, n // LANES, LANES), F32),
        grid=(n // tr,),
        in_specs=[pl.BlockSpec((e,) + blk, lambda i: (0, i, 0)), pl.BlockSpec((e, 1, 1), lambda i: (0, 0, 0))],
        out_specs=pl.BlockSpec((3,) + blk, lambda i: (0, i, 0)),
        compiler_params=_cparams(("arbitrary",)),
        name="router_gates",
    )(logits_t.reshape(e, n // LANES, LANES), router_b.reshape(e, 1, 1))
    return routed.reshape(3, n)


GATE_LANES = LANES


def _moe_plan(bucket, tg):
    n = bucket.shape[0]
    n_tiles = n // tg + N_BUCKETS
    ids = jnp.arange(N_BUCKETS, dtype=jnp.int32)
    onehot = (bucket[:, None] == ids[None, :]).astype(jnp.int32)
    csum = jnp.cumsum(onehot, axis=0)
    counts = csum[-1]
    tiles = (counts + tg - 1) // tg
    tile_end = jnp.cumsum(tiles)
    row_start = (tile_end - tiles) * tg
    pos = jnp.sum(onehot * (row_start[None, :] + csum - 1), axis=1).astype(jnp.int32)
    n_used = tile_end[-1]
    tile_ids = jnp.minimum(jnp.arange(n_tiles, dtype=jnp.int32), n_used - 1)
    tile_bucket = jnp.sum((tile_ids[:, None] >= tile_end[None, :]).astype(jnp.int32), axis=1)
    pair_a = jnp.asarray([a for a, _ in _EXPERT_PAIRS], jnp.int32)
    pair_b = jnp.asarray([b for _, b in _EXPERT_PAIRS], jnp.int32)
    grp, pid = tile_bucket // len(_EXPERT_PAIRS), tile_bucket % len(_EXPERT_PAIRS)
    ea = grp * EXPERTS_PER_GROUP + pair_a[pid]
    eb = grp * EXPERTS_PER_GROUP + pair_b[pid]
    spare = n_used + ids
    zero_tiles = jnp.concatenate([jnp.where(tiles > 0, tile_end - 1, -1),
                                  jnp.where(spare < n_tiles, spare, -1)]).astype(jnp.int32)
    return (pos, ea.astype(jnp.int32), eb.astype(jnp.int32), n_used.reshape(1).astype(jnp.int32),
            zero_tiles, n_tiles)


def _row_copy(src_ref, src_row, dst_ref, dst_row, sem):
    return pltpu.make_async_copy(src_ref.at[pl.ds(src_row, 1), :], dst_ref.at[pl.ds(dst_row, 1), :], sem)


ROW_DMA_UNROLL = 8


def _row_copies(n_rows, row_copy, whole_tile_copy):
    def issue(blk, carry):
        for u in range(ROW_DMA_UNROLL):
            row_copy(blk * ROW_DMA_UNROLL + u).start(priority=u % 2)
        return carry

    lax.fori_loop(0, n_rows // ROW_DMA_UNROLL, issue, 0)
    whole_tile_copy.wait()


def _dispatch_body(tg, pos_ref, zero_tiles_ref, h_ref, g_ref, xs_ref, aug_ref, sem):
    base = pl.program_id(0) * tg

    @pl.when(pl.program_id(0) == 0)
    def _():
        aug_ref[...] = jnp.zeros_like(aug_ref)

        def fill(k):
            row0 = pl.multiple_of(zero_tiles_ref[k] * tg, tg)
            return pltpu.make_async_copy(aug_ref, xs_ref.at[pl.ds(row0, tg), :], sem)

        for k in range(2 * N_BUCKETS):
            pl.when(zero_tiles_ref[k] >= 0)(lambda k=k: fill(k).start())
        for k in range(2 * N_BUCKETS):
            pl.when(zero_tiles_ref[k] >= 0)(lambda k=k: fill(k).wait())

    aug_ref[:, :D_MODEL] = h_ref[...]
    aug_ref[:, D_MODEL:] = g_ref[...]

    _row_copies(tg, lambda r: _row_copy(aug_ref, r, xs_ref, pos_ref[base + r], sem),
                pltpu.make_async_copy(aug_ref, xs_ref.at[pl.ds(0, tg), :], sem))


def _moe_dispatch(pos, zero_tiles, h2, gate_rows, n_tiles, tg):
    n, d = h2.shape
    wide = d + GATE_LANES
    grid_spec = pltpu.PrefetchScalarGridSpec(
        num_scalar_prefetch=2,
        grid=(n // tg,),
        in_specs=[pl.BlockSpec((tg, d), lambda i, pos_ref, zt_ref: (i, 0)),
                  pl.BlockSpec((tg, GATE_LANES), lambda i, pos_ref, zt_ref: (i, 0))],
        out_specs=pl.BlockSpec(memory_space=pl.ANY),
        scratch_shapes=[pltpu.VMEM((tg, wide), F32), pltpu.SemaphoreType.DMA(())],
    )
    return pl.pallas_call(
        functools.partial(_dispatch_body, tg),
        out_shape=jax.ShapeDtypeStruct((n_tiles * tg, wide), F32),
        grid_spec=grid_spec,
        compiler_params=_cparams(("arbitrary",)),
        name="moe_dispatch",
    )(pos, zero_tiles, h2, gate_rows)


def _grouped_body(ea_ref, eb_ref, nu_ref, xs_ref, w1a_ref, w3a_ref, w2a_ref, w1b_ref, w3b_ref, w2b_ref, y_ref):
    del ea_ref, eb_ref

    @pl.when(pl.program_id(0) < nu_ref[0])
    def _():
        x = xs_ref[:, :D_MODEL].astype(BF16)

        def expert(w1_ref, w3_ref, w2_ref, gate):
            h1 = jnp.dot(x, w1_ref[0], preferred_element_type=F32)
            h3 = jnp.dot(x, w3_ref[0], preferred_element_type=F32)
            hid = (h1 * _sigmoid(h1) * h3 * gate).astype(BF16)
            return jnp.dot(hid, w2_ref[0], preferred_element_type=F32)

        y_ref[...] = (expert(w1a_ref, w3a_ref, w2a_ref, xs_ref[:, D_MODEL:D_MODEL + 1])
                      + expert(w1b_ref, w3b_ref, w2b_ref, xs_ref[:, D_MODEL + 1:D_MODEL + 2]))

    @pl.when(pl.program_id(0) >= nu_ref[0])
    def _():
        y_ref[...] = jnp.zeros_like(y_ref)


def _moe_grouped(xs, ea, eb, n_used, w1, w3, w2, tg):
    rows, wide = xs.shape
    d, ff = D_MODEL, EXPERT_FF
    n_tiles = rows // tg
    tile = lambda i, ea_r, eb_r, nu_r: (jnp.minimum(i, nu_r[0] - 1), 0)
    up_a = pl.BlockSpec((1, d, ff), lambda i, ea_r, eb_r, nu_r: (ea_r[i], 0, 0))
    up_b = pl.BlockSpec((1, d, ff), lambda i, ea_r, eb_r, nu_r: (eb_r[i], 0, 0))
    dn_a = pl.BlockSpec((1, ff, d), lambda i, ea_r, eb_r, nu_r: (ea_r[i], 0, 0))
    dn_b = pl.BlockSpec((1, ff, d), lambda i, ea_r, eb_r, nu_r: (eb_r[i], 0, 0))
    grid_spec = pltpu.PrefetchScalarGridSpec(
        num_scalar_prefetch=3,
        grid=(n_tiles,),
        in_specs=[pl.BlockSpec((tg, wide), tile), up_a, up_a, dn_a, up_b, up_b, dn_b],
        out_specs=pl.BlockSpec((tg, d), lambda i, ea_r, eb_r, nu_r: (i, 0)),
    )
    return pl.pallas_call(
        _grouped_body,
        out_shape=jax.ShapeDtypeStruct((rows, d), F32),
        grid_spec=grid_spec,
        compiler_params=_cparams(("arbitrary",)),
        name="moe_grouped",
    )(ea, eb, n_used, xs, w1, w3, w2, w1, w3, w2)


def _combine_body(tg, pos_ref, ys_ref, x_ref, mod_ref, lg_ref, lb_ref, o_ref, buf_ref, sem):
    base = pl.program_id(0) * tg

    _row_copies(tg, lambda r: _row_copy(ys_ref, pos_ref[base + r], buf_ref, r, sem),
                pltpu.make_async_copy(ys_ref.at[pl.ds(0, tg), :], buf_ref, sem))
    m = mod_ref[0]
    z = DEEPNORM_ALPHA * x_ref[...] + m[5:6] * buf_ref[...]
    o_ref[...] = _layer_norm_rows(z, lg_ref[...], lb_ref[...])


def _moe_combine(lay, pos, ys, x, mod, ln_g, ln_b, row_off, tg):
    n, d = x.shape
    grid_spec = pltpu.PrefetchScalarGridSpec(
        num_scalar_prefetch=1,
        grid=(n // tg,),
        in_specs=[pl.BlockSpec(memory_space=pl.ANY),
                  pl.BlockSpec((tg, d), lambda i, pos_ref: (i, 0)),
                  pl.BlockSpec((1, 6, d), lambda i, pos_ref: (lay.mod_row(i, tg, row_off), 0, 0)),
                  pl.BlockSpec((1, d), lambda i, pos_ref: (0, 0)),
                  pl.BlockSpec((1, d), lambda i, pos_ref: (0, 0))],
        out_specs=pl.BlockSpec((tg, d), lambda i, pos_ref: (i, 0)),
        scratch_shapes=[pltpu.VMEM((tg, d), F32), pltpu.SemaphoreType.DMA(())],
    )
    return pl.pallas_call(
        functools.partial(_combine_body, tg),
        out_shape=jax.ShapeDtypeStruct((n, d), F32),
        grid_spec=grid_spec,
        compiler_params=_cparams(("arbitrary",)),
        name="moe_combine_ln",
    )(pos, ys, x, mod, ln_g.reshape(1, d), ln_b.reshape(1, d))


def _moe(lay, h2, routed, w1, w3, w2, x, mod, ln_g, ln_b, row_off):
    tg = lay.tm
    bucket = routed[0].astype(jnp.int32)
    gate_rows = jnp.pad(routed[1:3].T, ((0, 0), (0, GATE_LANES - 2)))
    pos, ea, eb, n_used, zero_tiles, n_tiles = _moe_plan(bucket, tg)
    xs = _moe_dispatch(pos, zero_tiles, h2, gate_rows, n_tiles, tg)
    ys = _moe_grouped(xs, ea, eb, n_used, w1, w3, w2, tg)
    return _moe_combine(lay, pos, ys, x, mod, ln_g, ln_b, row_off, tg)


def kernel(x, c, ctx, c_ctx, router_w, router_b, ada_w, ada_b, ln1_g, ln1_b, ln2_g, ln2_b, moe_w1, moe_w3, moe_w2, ev_w_in, ev_w_out, ev_a_mu, ev_a_w0, ev_a_w2, ev_a_a0, ev_a_a2, ev_a_g2, ev_a_kk, ev_a_ka, ev_a_rk, ev_a_lnx_g, ev_a_lnx_b, ev_b_lam, ev_b_subln_g, od_w_in, od_w_out, od_qn_g, od_kn_g):
    bsz, seq, d = x.shape
    ctx_len = ctx.shape[1]
    assert d == D_MODEL and ada_w.shape[0] == DEPTH and seq % GRID_W == 0
    lay = _Layout(bsz, ctx_len, seq)

    cvec = jnp.zeros((lay.mod_rows, d), F32).at[:bsz].set(c).at[bsz].set(c_ctx)
    mods = _ada_mods(cvec, ada_w, ada_b).reshape(DEPTH, lay.mod_rows, 6, d)

    xs = (ctx.reshape(lay.rows_c, d), x.reshape(lay.rows_l, d))
    router_wt = router_w.T

    cos_b, sin_b = _rope_tables(lay, B_HEAD_DIM, B_QK)
    cos_c, sin_c = _rope_tables(lay, C_HEAD_DIM, C_HEAD_DIM)

    for i in range(DEPTH):
        last = i == DEPTH - 1
        j = i // 2
        mod = mods[i]
        if i % 2 == 0:
            lambda_init = 0.8 - 0.6 * math.exp(-0.3 * i)
            x_ctx, x_lat = xs if isinstance(xs, tuple) else (xs[:lay.rows_c], xs[lay.rows_c:])
            pa, q, k, v = _even_in_proj(lay, x_ctx, x_lat, mod, ev_w_in[j].astype(BF16), cos_b, sin_b)
            decay, scan_ops, g_, bonus = _rwkv_features(
                lay, pa, ev_a_mu[j], ev_a_w0[j], ev_a_w2[j], ev_a_a0[j], ev_a_a2[j], ev_a_g2[j],
                ev_a_kk[j], ev_a_ka[j], ev_a_rk[j].reshape(-1))
            chain_decay, chain_ops = _to_chains(lay, decay[None]), _to_chains(lay, scan_ops)
            yd = _diff_attention(lay, q, k, v, ev_b_lam[j], ev_b_subln_g[j], lambda_init)
            y_f, y_b = _wkv_scan(lay, chain_decay, chain_ops)
            ya = _rwkv_readout(lay, _from_chains(lay, y_f, y_b), bonus, g_, ev_a_lnx_g[j], ev_a_lnx_b[j])
            mixes, w_out = [ya, yd], ev_w_out[j]
            n_rows, tile_off = (lay.rows_l, lay.ntiles_c) if last else (lay.rows, 0)
            if last:
                mixes = [mx[lay.rows_c:] for mx in mixes]
        else:
            xs = jnp.concatenate(xs, axis=0) if isinstance(xs, tuple) else xs
            q, k, v = _odd_in_proj(lay, xs, mod, od_w_in[j].astype(BF16), cos_c, sin_c, od_qn_g[j], od_kn_g[j])
            assert last, "an odd layer that must also update the context stream is not supported"
            o = _gqa_attention(lay, q, k, v)
            mixes, w_out = [o], od_w_out[j]
            n_rows, tile_off = lay.rows_l, lay.ntiles_c
        row_off = tile_off * lay.tm
        if isinstance(xs, tuple) and row_off:
            xs = jnp.concatenate(xs, axis=0)
        x_new, h2, logits_t = _out_proj(lay, mixes, w_out.astype(BF16), xs, row_off, mod,
                                        ln1_g[i], ln1_b[i], router_wt, n_rows)
        routed = _router(logits_t, router_b)
        xs = _moe(lay, h2, routed, moe_w1[i].astype(BF16), moe_w3[i].astype(BF16), moe_w2[i].astype(BF16),
                  x_new, mod, ln2_g[i], ln2_b[i], row_off)
    return xs.reshape(bsz, seq, d)
```

```python
import functools
import math

import jax
import jax.numpy as jnp
from jax import lax
from jax.experimental import pallas as pl
from jax.experimental.pallas import tpu as pltpu

F32 = jnp.float32
BF16 = jnp.bfloat16

D_MODEL = 1024
DEPTH = 2
GRID_W = 64
ROPE_THETA = 10000.0
LN_EPS = 1e-5
LOG2_E = 1.4426950408889634
DEEPNORM_ALPHA = (2 * DEPTH) ** 0.25

A_HEAD_DIM = 64
A_HEADS = 8
A_WIDTH = 512
A_LORA = 64
A_GATE_LORA = 128
A_GN_EPS = 64e-5
A_IN = 3 * A_WIDTH + 4 * A_LORA + A_GATE_LORA
_SCAN_R, _SCAN_K, _SCAN_V, _SCAN_A, _SCAN_B = range(5)
N_SCAN_OPS = 5
SCAN_ROW_GROUPS = 2

B_HEAD_DIM = 64
B_V_DIM = 128
B_HEADS = 4
B_WIDTH = 512
B_QK = 512
B_QK_PAD = (B_QK // B_HEAD_DIM) * 128
B_SUBLN_EPS = 1e-5
EVEN_IN = A_IN + 2 * B_QK + B_WIDTH

C_HEAD_DIM = 128
C_HEADS = 8
C_KV_HEADS = 2
C_GROUP = 4
C_Q = 1024
C_KV = 256
ODD_IN = C_Q + 2 * C_KV
QK_NORM_EPS = 1e-6

N_EXPERTS = 16
N_GROUPS = 4
EXPERTS_PER_GROUP = 4
EXPERT_FF = 512

VMEM_LIMIT_BYTES = 56 * 1024 * 1024
LANES = 128
SUBLANES = 8


def _cparams(sem):
    return pltpu.CompilerParams(dimension_semantics=sem, vmem_limit_bytes=VMEM_LIMIT_BYTES)


def _dot(a, b):
    return jnp.dot(a.astype(BF16), b.astype(BF16), preferred_element_type=F32)


def _dot_nt(a, b):
    return lax.dot_general(a.astype(BF16), b.astype(BF16), (((1,), (1,)), ((), ())),
                           preferred_element_type=F32)


def _split(a):
    hi = a.astype(BF16)
    lo = (a - hi.astype(F32)).astype(BF16)
    return hi, lo


def _dot3(a, b):
    ah, al = _split(a)
    bh, bl = _split(b)
    return (jnp.dot(ah, bh, preferred_element_type=F32)
            + (jnp.dot(ah, bl, preferred_element_type=F32)
               + jnp.dot(al, bh, preferred_element_type=F32)))


def _dot2_exact_rhs(a, b_bf16):
    ah, al = _split(a)
    return jnp.dot(ah, b_bf16, preferred_element_type=F32) + jnp.dot(al, b_bf16, preferred_element_type=F32)


def _dot3_nt(a, b):
    ah, al = _split(a)
    bh, bl = _split(b)
    dn = (((1,), (1,)), ((), ()))
    return (lax.dot_general(ah, bh, dn, preferred_element_type=F32)
            + (lax.dot_general(ah, bl, dn, preferred_element_type=F32)
               + lax.dot_general(al, bh, dn, preferred_element_type=F32)))


def _sigmoid(x):
    return 1.0 / (1.0 + jnp.exp(-x))


def _layer_norm_rows(z, g, b):
    mu = jnp.mean(z, axis=-1, keepdims=True)
    zc = z - mu
    var = jnp.mean(zc * zc, axis=-1, keepdims=True)
    return zc * lax.rsqrt(var + LN_EPS) * g + b


class _Layout:
    def __init__(self, bsz, ctx_len, seq):
        self.B, self.CTX, self.S = bsz, ctx_len, seq
        self.T = ctx_len + seq
        self.tm = math.gcd(256, math.gcd(ctx_len, seq))
        self.nct = ctx_len // self.tm
        self.nlt = seq // self.tm
        self.rows_c = bsz * ctx_len
        self.rows_l = bsz * seq
        self.rows = self.rows_c + self.rows_l
        self.ntiles_c = bsz * self.nct
        self.ntiles = self.rows // self.tm
        assert self.rows_c % seq == 0, "latent K/V blocks are addressed in units of S rows"
        self.tb = math.gcd(32, math.gcd(ctx_len, seq))
        self.tmo = math.gcd(512, math.gcd(self.rows_c, seq))
        self.mod_rows = -(-(bsz + 1) // SUBLANES) * SUBLANES

    def seq_tile(self, b, j):
        return jnp.where(j < self.nct, b * self.nct + j, self.ntiles_c + b * self.nlt + (j - self.nct))

    def seq_block(self, i):
        il = i - self.ntiles_c
        return (jnp.where(i < self.ntiles_c, i // self.nct, il // self.nlt),
                jnp.where(i < self.ntiles_c, i % self.nct, self.nct + il % self.nlt))

    def mod_row(self, i, tile, row_offset=0):
        r = i * tile + row_offset
        return jnp.where(r < self.rows_c, self.B, (r - self.rows_c) // self.S)

    def pos_tile(self, i):
        il = i - self.ntiles_c
        return jnp.where(i < self.ntiles_c, i % self.nct, self.nct + il % self.nlt)


def _ada_body(cv_ref, w_ref, b_ref, o_ref):
    cv = cv_ref[...]
    s = cv * _sigmoid(cv)
    o_ref[0] = _dot3(s, w_ref[0]) + b_ref[0]


def _ada_mods(cvec, ada_w, ada_b):
    depth, d, n = ada_w.shape
    r = cvec.shape[0]
    tn = 512
    return pl.pallas_call(
        _ada_body,
        out_shape=jax.ShapeDtypeStruct((depth, r, n), F32),
        grid=(depth, n // tn),
        in_specs=[pl.BlockSpec((r, d), lambda l, j: (0, 0)),
                  pl.BlockSpec((1, d, tn), lambda l, j: (l, 0, j)),
                  pl.BlockSpec((1, 1, tn), lambda l, j: (l, 0, j))],
        out_specs=pl.BlockSpec((1, r, tn), lambda l, j: (l, 0, j)),
        compiler_params=_cparams(("arbitrary", "arbitrary")),
        name="ada_mods",
    )(cvec, ada_w, ada_b.reshape(depth, 1, n))


def _rope_tables(lay, head_dim, width):
    rows = lay.S // GRID_W
    rr, cc = jnp.meshgrid(jnp.arange(rows), jnp.arange(GRID_W), indexing="ij")
    row_pos = rr.reshape(-1).astype(F32)
    col_pos = cc.reshape(-1).astype(F32)
    axis_dim = head_dim // 2
    inv = ROPE_THETA ** (-jnp.arange(0, axis_dim, 2, dtype=F32) / axis_dim)
    ang = jnp.concatenate([row_pos[:, None] * inv, col_pos[:, None] * inv], -1)
    cos, sin = jnp.cos(ang), jnp.sin(ang)
    cos = jnp.concatenate([jnp.ones((lay.CTX, head_dim // 2), F32), cos], 0)
    sin = jnp.concatenate([jnp.zeros((lay.CTX, head_dim // 2), F32), sin], 0)
    cos_h = jnp.concatenate([cos, cos], -1)
    sin_h = jnp.concatenate([-sin, sin], -1)
    reps = width // head_dim
    return jnp.tile(cos_h, (1, reps)), jnp.tile(sin_h, (1, reps))


def _rope_lanes(x, cos, sin, head_dim):
    w = x.shape[-1]
    half = head_dim // 2
    if head_dim == LANES and w == LANES:
        rot = pltpu.roll(x, half, 1)
    else:
        fwd = pltpu.roll(x, w - half, 1)
        bwd = pltpu.roll(x, half, 1)
        lane = lax.broadcasted_iota(jnp.int32, x.shape, 1)
        rot = jnp.where((lane % head_dim) < half, fwd, bwd)
    return x * cos + rot * sin


def _row_tile(n_ctx_tiles, xc_ref, xl_ref):
    return jnp.where(pl.program_id(0) < n_ctx_tiles, xc_ref[...], xl_ref[...])


def _row_tile_specs(tile, n_ctx_tiles, d):
    return [pl.BlockSpec((tile, d), lambda i: (jnp.minimum(i, n_ctx_tiles - 1), 0)),
            pl.BlockSpec((tile, d), lambda i: (jnp.maximum(i - n_ctx_tiles, 0), 0))]


def _even_in_body(n_ctx_tiles, xc_ref, xl_ref, mod_ref, w_ref, cos_ref, sin_ref, pa_ref, q_ref, k_ref, v_ref):
    m = mod_ref[0]
    h = (_row_tile(n_ctx_tiles, xc_ref, xl_ref) * (1.0 + m[1:2]) + m[0:1]).astype(BF16)
    pa_ref[...] = jnp.dot(h, w_ref[:, :A_IN], preferred_element_type=F32)
    cos, sin = cos_ref[...], sin_ref[...]
    o = A_IN

    def put_maps(dst_ref, val):
        pad = jnp.zeros((val.shape[0], LANES - B_HEAD_DIM), BF16)
        for mp in range(B_QK // B_HEAD_DIM):
            piece = val[:, mp * B_HEAD_DIM:(mp + 1) * B_HEAD_DIM].astype(BF16)
            dst_ref[:, mp * LANES:(mp + 1) * LANES] = jnp.concatenate([piece, pad], axis=1)

    q = jnp.dot(h, w_ref[:, o:o + B_QK], preferred_element_type=F32)
    put_maps(q_ref, _rope_lanes(q, cos, sin, B_HEAD_DIM) * (LOG2_E * B_HEAD_DIM ** -0.5))
    o += B_QK
    k = jnp.dot(h, w_ref[:, o:o + B_QK], preferred_element_type=F32)
    put_maps(k_ref, _rope_lanes(k, cos, sin, B_HEAD_DIM))
    o += B_QK
    v_ref[...] = jnp.dot(h, w_ref[:, o:o + B_WIDTH], preferred_element_type=F32).astype(BF16)


def _even_in_proj(lay, x_ctx, x_lat, mod, w_bf16, cos, sin):
    tm, d = lay.tm, D_MODEL
    row = lambda i: (i, 0)
    return pl.pallas_call(
        functools.partial(_even_in_body, lay.ntiles_c),
        out_shape=(jax.ShapeDtypeStruct((lay.rows, A_IN), F32),
                   jax.ShapeDtypeStruct((lay.rows, B_QK_PAD), BF16),
                   jax.ShapeDtypeStruct((lay.rows, B_QK_PAD), BF16),
                   jax.ShapeDtypeStruct((lay.rows, B_WIDTH), BF16)),
        grid=(lay.ntiles,),
        in_specs=_row_tile_specs(tm, lay.ntiles_c, d) + [
                  pl.BlockSpec((1, 6, d), lambda i: (lay.mod_row(i, tm), 0, 0)),
                  pl.BlockSpec((d, EVEN_IN), lambda i: (0, 0)),
                  pl.BlockSpec((tm, B_QK), lambda i: (lay.pos_tile(i), 0)),
                  pl.BlockSpec((tm, B_QK), lambda i: (lay.pos_tile(i), 0))],
        out_specs=(pl.BlockSpec((tm, A_IN), row), pl.BlockSpec((tm, B_QK_PAD), row),
                   pl.BlockSpec((tm, B_QK_PAD), row), pl.BlockSpec((tm, B_WIDTH), row)),
        compiler_params=_cparams(("arbitrary",)),
        name="even_in_proj",
    )(x_ctx, x_lat, mod, w_bf16, cos, sin)


def _rwkv_feat_body(lay, pa_ref, prev_ref, next_ref, mu_ref, w0_ref, w2_ref, a0_ref, a2_ref, g2_ref,
                    kk_ref, ka_ref, rk_ref, bd_ref, tri_ref, blk_ref,
                    pend_ref, ops_ref, g_ref, bonus_ref):
    def put(idx, d, val):
        ops_ref[idx, d, 0] = val.astype(BF16)

    i = pl.program_id(0)
    tm = lay.tm
    il = i - lay.ntiles_c
    in_ctx = i < lay.ntiles_c
    seg_first = jnp.where(in_ctx, i % lay.nct == 0, il % lay.nlt == 0)
    seg_last = jnp.where(in_ctx, i % lay.nct == lay.nct - 1, il % lay.nlt == lay.nlt - 1)

    pa = pa_ref[...]
    row = lax.broadcasted_iota(jnp.int32, pa.shape, 0)
    prev_edge = jnp.where(seg_first, 0.0, 1.0) * prev_ref[SUBLANES - 1:SUBLANES, :]
    next_edge = jnp.where(seg_last, 0.0, 1.0) * next_ref[0:1, :]
    prev = jnp.where(row == 0, prev_edge, pltpu.roll(pa, 1, 0))
    nxt = jnp.where(row == tm - 1, next_edge, pltpu.roll(pa, tm - 1, 0))
    u = pa + (0.5 * (prev + nxt) - pa) * mu_ref[...]

    o1, o2, o3 = A_WIDTH, 2 * A_WIDTH, 3 * A_WIDTH
    o4 = o3 + 2 * A_LORA
    o5 = o4 + 2 * A_LORA
    r, k, v = u[:, :o1], u[:, o1:o2], u[:, o2:o3]
    bd = bd_ref[...]

    kk = k * kk_ref[...]
    ss = _dot2_exact_rhs(kk * kk, bd)
    kkn = kk / jnp.maximum(jnp.sqrt(ss), 1e-12)
    g = _dot3(_sigmoid(u[:, o5:]), g2_ref[...])

    g_ref[0] = g.astype(BF16)

    kd_sum = jnp.zeros_like(k)
    for d in range(2):
        wd = u[:, o3 + d * A_LORA:o3 + (d + 1) * A_LORA]
        ad = u[:, o4 + d * A_LORA:o4 + (d + 1) * A_LORA]
        log_decay = -math.exp(-0.5) * _sigmoid(w0_ref[d:d + 1, :] + _dot3(jnp.tanh(wd), w2_ref[d]))
        ld_hi, ld_lo = _split(log_decay)
        cum = (jnp.dot(tri_ref[d], ld_hi, preferred_element_type=F32)
               + jnp.dot(tri_ref[d], ld_lo, preferred_element_type=F32))
        tot = (jnp.dot(blk_ref[...], ld_hi, preferred_element_type=F32)
               + jnp.dot(blk_ref[...], ld_lo, preferred_element_type=F32))
        pend_ref[d, 0] = jnp.exp(tot)
        p = jnp.exp(cum)
        p_inv = 1.0 / p
        a = _sigmoid(a0_ref[d:d + 1, :] + _dot3(ad, a2_ref[d]))
        kd = k * (1.0 + (a - 1.0) * ka_ref[...])
        put(_SCAN_R, d, r * p)
        put(_SCAN_V, d, v)
        put(_SCAN_A, d, -kkn * jnp.exp(cum - log_decay))
        put(_SCAN_K, d, kd * p_inv)
        put(_SCAN_B, d, kkn * a * p_inv)
        kd_sum = kd_sum + kd
    bonus_ref[0] = (_dot2_exact_rhs(r * kd_sum * rk_ref[...], bd) * v).astype(BF16)


def _head_block_diag(width, head_dim):
    h = jnp.arange(width) // head_dim
    return (h[:, None] == h[None, :]).astype(BF16)


def _rwkv_features(lay, pa, mu, w0, w2, a0, a2, g2, k_k, k_a, r_k):
    tm = lay.tm
    hb = tm // SUBLANES
    nb8 = lay.rows // SUBLANES
    row = lambda i: (i, 0)
    full2 = lambda i: (0, 0)
    full3 = lambda i: (0, 0, 0)
    w = A_WIDTH
    out = jax.ShapeDtypeStruct((lay.B, lay.T, w), BF16)
    seq = lambda i: lay.seq_block(i) + (0,)
    tb = lay.tb
    nblk = tm // tb
    pos = jnp.arange(tm)
    same_blk = (pos[:, None] // tb) == (pos[None, :] // tb)
    tri = jnp.stack([same_blk & (pos[None, :] <= pos[:, None]),
                     same_blk & (pos[None, :] >= pos[:, None])]).astype(BF16)
    blk = (jnp.arange(nblk)[:, None] == (pos[None, :] // tb)).astype(BF16)
    return pl.pallas_call(
        functools.partial(_rwkv_feat_body, lay),
        out_shape=(jax.ShapeDtypeStruct((2, lay.B, lay.T // tb, w), F32),
                   jax.ShapeDtypeStruct((N_SCAN_OPS, 2, lay.B, lay.T, w), BF16), out, out),
        grid=(lay.ntiles,),
        in_specs=[pl.BlockSpec((tm, A_IN), row),
                  pl.BlockSpec((SUBLANES, A_IN), lambda i: (jnp.maximum(i * hb - 1, 0), 0)),
                  pl.BlockSpec((SUBLANES, A_IN), lambda i: (jnp.minimum((i + 1) * hb, nb8 - 1), 0)),
                  pl.BlockSpec((1, A_IN), full2),
                  pl.BlockSpec((2, w), full2),
                  pl.BlockSpec((2, A_LORA, w), full3),
                  pl.BlockSpec((2, w), full2),
                  pl.BlockSpec((2, A_LORA, w), full3),
                  pl.BlockSpec((A_GATE_LORA, w), full2),
                  pl.BlockSpec((1, w), full2),
                  pl.BlockSpec((1, w), full2),
                  pl.BlockSpec((1, w), full2),
                  pl.BlockSpec((w, w), full2),
                  pl.BlockSpec((2, tm, tm), full3),
                  pl.BlockSpec((nblk, tm), full2)],
        out_specs=(pl.BlockSpec((2, 1, nblk, w), lambda i: (0,) + lay.seq_block(i) + (0,)),
                   pl.BlockSpec((N_SCAN_OPS, 2, 1, tm, w), lambda i: (0, 0) + lay.seq_block(i) + (0,)),
                   pl.BlockSpec((1, tm, w), seq), pl.BlockSpec((1, tm, w), seq)),
        compiler_params=_cparams(("arbitrary",)),
        name="rwkv_features",
    )(pa, pa, pa, mu.reshape(1, A_IN), w0, w2, a0, a2, g2, k_k.reshape(1, w), k_a.reshape(1, w),
      r_k.reshape(1, w), _head_block_diag(w, A_HEAD_DIM), tri, blk)


def _wkv_scan_body(tb, pf_ref, pb_ref, fwd_ref, bwd_ref, yf_ref, yb_ref, s_ref, m_ref):
    n = A_HEAD_DIM
    chains = s_ref.shape[-1]
    is_fwd = lax.broadcasted_iota(jnp.int32, (n, chains), 1) < chains // 2

    @pl.when(pl.program_id(0) == 0)
    def _():
        s_ref[...] = jnp.zeros_like(s_ref)

    def step(t, carry):
        tr = tb - 1 - t
        for idx in range(N_SCAN_OPS):
            m_ref[idx] = jnp.where(is_fwd, fwd_ref[t, idx], bwd_ref[tr, idx]).astype(F32)
        ys = []
        for g in range(SCAN_ROW_GROUPS):
            rows = slice(g * (n // SCAN_ROW_GROUPS), (g + 1) * (n // SCAN_ROW_GROUPS))
            sa = jnp.zeros((n // SCAN_ROW_GROUPS, chains), F32)
            for kk in range(n):
                sa = sa + s_ref[kk, rows, :] * m_ref[_SCAN_A, kk:kk + 1, :]
            v_t = m_ref[_SCAN_V, rows, :]
            y = jnp.zeros((n // SCAN_ROW_GROUPS, chains), F32)
            for kk in range(n):
                s_new = (s_ref[kk, rows, :] + sa * m_ref[_SCAN_B, kk:kk + 1, :]
                         + v_t * m_ref[_SCAN_K, kk:kk + 1, :])
                s_ref[kk, rows, :] = s_new
                y = y + s_new * m_ref[_SCAN_R, kk:kk + 1, :]
            ys.append(y)
        y = jnp.concatenate(ys, axis=0).astype(BF16)
        yf_ref[t] = y
        yb_ref[tr] = y
        return carry

    lax.fori_loop(0, tb, step, 0)
    m_ref[0] = jnp.where(is_fwd, pf_ref[0, 0], pb_ref[0, 0])
    for kk in range(n):
        s_ref[kk] = s_ref[kk] * m_ref[0, kk:kk + 1, :]


def _wkv_scan(lay, block_decay, ops):
    t, nops, n, lanes = ops.shape
    tb = lay.tb
    nctb, nt = lay.CTX // tb, t // tb
    bwd_tile = lambda g: jnp.where(g < nctb, nctb - 1 - g, nt - 1 - (g - nctb))
    out = jax.ShapeDtypeStruct((t, n, lanes), BF16)
    fwd_map = lambda g: (g, 0, 0, 0)
    bwd_map = lambda g: (bwd_tile(g), 0, 0, 0)
    return pl.pallas_call(
        functools.partial(_wkv_scan_body, tb),
        out_shape=(out, out),
        grid=(nt,),
        in_specs=[pl.BlockSpec((1, 1, n, lanes), fwd_map), pl.BlockSpec((1, 1, n, lanes), bwd_map),
                  pl.BlockSpec((tb, nops, n, lanes), fwd_map), pl.BlockSpec((tb, nops, n, lanes), bwd_map)],
        out_specs=(pl.BlockSpec((tb, n, lanes), lambda g: (g, 0, 0)),
                   pl.BlockSpec((tb, n, lanes), lambda g: (bwd_tile(g), 0, 0))),
        scratch_shapes=[pltpu.VMEM((n, n, lanes), F32), pltpu.VMEM((nops, n, lanes), F32)],
        compiler_params=_cparams(("arbitrary",)),
        name="wkv7_scan",
    )(block_decay, block_decay, ops, ops)


def _to_chains(lay, ops):
    h, n = A_HEADS, A_HEAD_DIM
    nops, length = ops.shape[0], ops.shape[3]
    x = ops.reshape(nops, 2, lay.B, length, h, n).transpose(3, 0, 5, 1, 2, 4)
    return x.reshape(length, nops, n, 2 * lay.B * h)


def _from_chains(lay, y_f, y_b):
    h, n = A_HEADS, A_HEAD_DIM
    low = jnp.arange(2 * lay.B * h) < lay.B * h
    y = jnp.where(low, y_f, y_b)
    return y.reshape(lay.T, n, 2, lay.B, h).transpose(2, 3, 0, 4, 1).reshape(2, lay.B, lay.T, h * n)


def _rwkv_readout_body(yf_ref, yb_ref, bonus_ref, g_ref, lg_ref, lb_ref, bd_ref, o_ref):
    y = yf_ref[0, 0].astype(F32) + yb_ref[0, 0].astype(F32)
    bd = bd_ref[...]
    inv_n = 1.0 / A_HEAD_DIM
    mu = _dot2_exact_rhs(y, bd) * inv_n
    yc = y - mu
    var = _dot2_exact_rhs(yc * yc, bd) * inv_n
    yn = yc * lax.rsqrt(var + A_GN_EPS) * lg_ref[...] + lb_ref[...]
    o_ref[...] = ((yn + bonus_ref[0].astype(F32)) * g_ref[0].astype(F32)).astype(BF16)


def _rwkv_readout(lay, y2, bonus, g, lnx_g, lnx_b):
    tm, w = lay.tm, A_WIDTH
    row = lambda i: (i, 0)
    full = lambda i: (0, 0)
    seq = lambda i: lay.seq_block(i) + (0,)
    return pl.pallas_call(
        _rwkv_readout_body,
        out_shape=jax.ShapeDtypeStruct((lay.rows, w), BF16),
        grid=(lay.ntiles,),
        in_specs=[pl.BlockSpec((1, 1, tm, w), lambda i: (0,) + seq(i)),
                  pl.BlockSpec((1, 1, tm, w), lambda i: (1,) + seq(i)),
                  pl.BlockSpec((1, tm, w), seq), pl.BlockSpec((1, tm, w), seq),
                  pl.BlockSpec((1, w), full), pl.BlockSpec((1, w), full), pl.BlockSpec((w, w), full)],
        out_specs=pl.BlockSpec((tm, w), row),
        compiler_params=_cparams(("arbitrary",)),
        name="rwkv_readout",
    )(y2, y2, bonus, g, lnx_g.reshape(1, w), lnx_b.reshape(1, w), _head_block_diag(w, A_HEAD_DIM))


def _exp2_scores(q, keys):
    scores = [_dot_nt(q, kk) for kk in keys]
    m = scores[0].max(axis=-1, keepdims=True)
    for s in scores[1:]:
        m = jnp.maximum(m, s.max(axis=-1, keepdims=True))
    return [jnp.exp2(s - m) for s in scores]


def _with_ones_column(v):
    lane = lax.broadcasted_iota(jnp.int32, (v.shape[0], LANES), 1)
    return jnp.concatenate([v, jnp.where(lane == 0, 1.0, 0.0).astype(BF16)], axis=1)


def _softmax_pv(q, keys, vals_aug):
    e_dim = vals_aug[0].shape[-1] - LANES
    o = None
    for p, vv in zip(_exp2_scores(q, keys), vals_aug):
        part = jnp.dot(p.astype(BF16), vv, preferred_element_type=F32)
        o = part if o is None else o + part
    return o[:, :e_dim] / o[:, e_dim:e_dim + 1]


def _diff_attn_body(lay, lambda_init, q_ref, kc_ref, kl_ref, vc_ref, vl_ref, lam_ref, g_ref, o_ref):
    j = pl.program_id(1)
    lv = lam_ref[...]
    lam = (jnp.exp(jnp.sum(lv[0:1] * lv[1:2], axis=1, keepdims=True))
           - jnp.exp(jnp.sum(lv[2:3] * lv[3:4], axis=1, keepdims=True)) + lambda_init)

    def run(with_latent):
        for h in range(B_HEADS):
            vs = slice(h * B_V_DIM, (h + 1) * B_V_DIM)
            vals = [_with_ones_column(vc_ref[:, vs])] + ([_with_ones_column(vl_ref[:, vs])] if with_latent else [])
            outs = []
            for mi in range(2):
                cs = slice((2 * h + mi) * LANES, (2 * h + mi + 1) * LANES)
                keys = [kc_ref[:, cs]] + ([kl_ref[:, cs]] if with_latent else [])
                outs.append(_softmax_pv(q_ref[:, cs], keys, vals))
            o = outs[0] - lam * outs[1]
            ms = jnp.mean(o * o, axis=-1, keepdims=True)
            o = o * lax.rsqrt(ms + B_SUBLN_EPS) * g_ref[...] * (1.0 - lambda_init)
            o_ref[:, vs] = o.astype(BF16)

    @pl.when(j < lay.nct)
    def _():
        run(False)

    @pl.when(j >= lay.nct)
    def _():
        run(True)


def _diff_attention(lay, q, k, v, lam_vecs, subln_g, lambda_init):
    tm = lay.tm
    w = B_WIDTH
    lat0 = lay.rows_c // lay.S
    return pl.pallas_call(
        functools.partial(_diff_attn_body, lay, lambda_init),
        out_shape=jax.ShapeDtypeStruct((lay.rows, w), BF16),
        grid=(lay.B, lay.nct + lay.nlt),
        in_specs=[pl.BlockSpec((tm, B_QK_PAD), lambda b, j: (lay.seq_tile(b, j), 0)),
                  pl.BlockSpec((lay.CTX, B_QK_PAD), lambda b, j: (b, 0)),
                  pl.BlockSpec((lay.S, B_QK_PAD), lambda b, j: (lat0 + b, 0)),
                  pl.BlockSpec((lay.CTX, w), lambda b, j: (b, 0)),
                  pl.BlockSpec((lay.S, w), lambda b, j: (lat0 + b, 0)),
                  pl.BlockSpec((4, B_HEAD_DIM), lambda b, j: (0, 0)),
                  pl.BlockSpec((1, B_V_DIM), lambda b, j: (0, 0))],
        out_specs=pl.BlockSpec((tm, w), lambda b, j: (lay.seq_tile(b, j), 0)),
        compiler_params=_cparams(("arbitrary", "arbitrary")),
        name="diff_attention",
    )(q, k, k, v, v, lam_vecs, subln_g.reshape(1, B_V_DIM))


def _odd_in_body(x_ref, mod_ref, w_ref, cos_ref, sin_ref, qn_ref, kn_ref, q_ref, k_ref, v_ref):
    m = mod_ref[0]
    h = (x_ref[...] * (1.0 + m[1:2]) + m[0:1]).astype(BF16)
    cos, sin = cos_ref[...], sin_ref[...]

    def norm_rope(p, g, scale):
        ms = jnp.mean(p * p, axis=-1, keepdims=True)
        y = p * lax.rsqrt(ms + QK_NORM_EPS) * g
        return (_rope_lanes(y, cos, sin, C_HEAD_DIM) * scale).astype(BF16)

    p = jnp.dot(h, w_ref[...], preferred_element_type=F32)
    for hd in range(C_HEADS):
        cs = slice(hd * C_HEAD_DIM, (hd + 1) * C_HEAD_DIM)
        q_ref[:, cs] = norm_rope(p[:, cs], qn_ref[...], LOG2_E * C_HEAD_DIM ** -0.5)
    for hd in range(C_KV_HEADS):
        cs = slice(hd * C_HEAD_DIM, (hd + 1) * C_HEAD_DIM)
        k_ref[:, cs] = norm_rope(p[:, C_Q + hd * C_HEAD_DIM:C_Q + (hd + 1) * C_HEAD_DIM], kn_ref[...], 1.0)
    v_ref[...] = p[:, C_Q + C_KV:].astype(BF16)


def _odd_in_proj(lay, x, mod, w_bf16, cos, sin, qn_g, kn_g):
    tm, d = lay.tm, D_MODEL
    row = lambda i: (i, 0)
    full = lambda i: (0, 0)
    return pl.pallas_call(
        _odd_in_body,
        out_shape=(jax.ShapeDtypeStruct((lay.rows, C_Q), BF16),
                   jax.ShapeDtypeStruct((lay.rows, C_KV), BF16),
                   jax.ShapeDtypeStruct((lay.rows, C_KV), BF16)),
        grid=(lay.ntiles,),
        in_specs=[pl.BlockSpec((tm, d), row),
                  pl.BlockSpec((1, 6, d), lambda i: (lay.mod_row(i, tm), 0, 0)),
                  pl.BlockSpec((d, ODD_IN), full),
                  pl.BlockSpec((tm, C_HEAD_DIM), lambda i: (lay.pos_tile(i), 0)),
                  pl.BlockSpec((tm, C_HEAD_DIM), lambda i: (lay.pos_tile(i), 0)),
                  pl.BlockSpec((1, C_HEAD_DIM), full),
                  pl.BlockSpec((1, C_HEAD_DIM), full)],
        out_specs=(pl.BlockSpec((tm, C_Q), row), pl.BlockSpec((tm, C_KV), row), pl.BlockSpec((tm, C_KV), row)),
        compiler_params=_cparams(("arbitrary",)),
        name="odd_in_proj",
    )(x, mod, w_bf16, cos, sin, qn_g.reshape(1, C_HEAD_DIM), kn_g.reshape(1, C_HEAD_DIM))


def _gqa_body(q_ref, kc_ref, kl_ref, vc_ref, vl_ref, o_ref):
    for kvh in range(C_KV_HEADS):
        ks = slice(kvh * C_HEAD_DIM, (kvh + 1) * C_HEAD_DIM)
        keys = [kc_ref[:, ks], kl_ref[:, ks]]
        vals = [_with_ones_column(vc_ref[:, ks]), _with_ones_column(vl_ref[:, ks])]
        for g in range(C_GROUP):
            hd = kvh * C_GROUP + g
            cs = slice(hd * C_HEAD_DIM, (hd + 1) * C_HEAD_DIM)
            o_ref[:, cs] = _softmax_pv(q_ref[:, cs], keys, vals).astype(BF16)


def _gqa_attention(lay, q, k, v):
    tm = lay.tm
    lat0 = lay.rows_c // lay.S
    return pl.pallas_call(
        _gqa_body,
        out_shape=jax.ShapeDtypeStruct((lay.rows_l, C_Q), BF16),
        grid=(lay.B, lay.nlt),
        in_specs=[pl.BlockSpec((tm, C_Q), lambda b, j: (lay.ntiles_c + b * lay.nlt + j, 0)),
                  pl.BlockSpec((lay.CTX, C_KV), lambda b, j: (b, 0)),
                  pl.BlockSpec((lay.S, C_KV), lambda b, j: (lat0 + b, 0)),
                  pl.BlockSpec((lay.CTX, C_KV), lambda b, j: (b, 0)),
                  pl.BlockSpec((lay.S, C_KV), lambda b, j: (lat0 + b, 0))],
        out_specs=pl.BlockSpec((tm, C_Q), lambda b, j: (b * lay.nlt + j, 0)),
        compiler_params=_cparams(("arbitrary", "arbitrary")),
        name="gqa_attention",
    )(q, k, k, v, v)


def _out_proj_body(n_mix, n_ctx_tiles, *refs):
    mix_refs = refs[:n_mix]
    n_x = 1 if n_ctx_tiles is None else 2
    x_refs = refs[n_mix + 1:n_mix + 1 + n_x]
    w_ref = refs[n_mix]
    mod_ref, lg_ref, lb_ref, rw_ref, xo_ref, h_ref, lt_ref = refs[n_mix + 1 + n_x:]
    x = x_refs[0][...] if n_ctx_tiles is None else _row_tile(n_ctx_tiles, *x_refs)
    m = mod_ref[0]
    off = 0
    mix = None
    for mr in mix_refs:
        kw = mr.shape[-1]
        part = jnp.dot(mr[...], w_ref[off:off + kw, :], preferred_element_type=F32)
        mix = part if mix is None else mix + part
        off += kw
    z = DEEPNORM_ALPHA * x + m[2:3] * mix
    xn = _layer_norm_rows(z, lg_ref[...], lb_ref[...])
    xo_ref[...] = xn
    h2 = xn * (1.0 + m[4:5]) + m[3:4]
    h_ref[...] = h2
    lt_ref[...] = _dot3_nt(rw_ref[...], h2)


def _out_proj(lay, mixes, w_bf16, x, row_off, mod, ln_g, ln_b, router_wt, n_rows):
    tm, d = lay.tmo, D_MODEL
    row = lambda i: (i, 0)
    full = lambda i: (0, 0)
    x_tile_off = row_off // tm
    if isinstance(x, tuple):
        assert row_off == 0
        n_ctx_tiles = lay.rows_c // tm
        x_specs = _row_tile_specs(tm, n_ctx_tiles, d)
    else:
        n_ctx_tiles, x = None, (x,)
        x_specs = [pl.BlockSpec((tm, d), lambda i: (i + x_tile_off, 0))]
    in_specs = [pl.BlockSpec((tm, mx.shape[-1]), row) for mx in mixes]
    in_specs += [pl.BlockSpec((d, d), full)] + x_specs + [
                 pl.BlockSpec((1, 6, d), lambda i: (lay.mod_row(i, tm, row_off), 0, 0)),
                 pl.BlockSpec((1, d), full), pl.BlockSpec((1, d), full),
                 pl.BlockSpec((N_EXPERTS, d), full)]
    return pl.pallas_call(
        functools.partial(_out_proj_body, len(mixes), n_ctx_tiles),
        out_shape=(jax.ShapeDtypeStruct((n_rows, d), F32),
                   jax.ShapeDtypeStruct((n_rows, d), F32),
                   jax.ShapeDtypeStruct((N_EXPERTS, n_rows), F32)),
        grid=(n_rows // tm,),
        in_specs=in_specs,
        out_specs=(pl.BlockSpec((tm, d), row), pl.BlockSpec((tm, d), row),
                   pl.BlockSpec((N_EXPERTS, tm), lambda i: (0, i))),
        compiler_params=_cparams(("arbitrary",)),
        name="out_proj_ln",
    )(*mixes, w_bf16, *x, mod, ln_g.reshape(1, d), ln_b.reshape(1, d), router_wt)


_EXPERT_PAIRS = ((0, 1), (0, 2), (0, 3), (1, 2), (1, 3), (2, 3))
N_BUCKETS = N_GROUPS * len(_EXPERT_PAIRS)


def _router_body(lt_ref, rb_ref, o_ref):
    logits = lt_ref[...] + rb_ref[...]
    rows = [logits[e] for e in range(N_EXPERTS)]
    m = rows[0]
    for x in rows[1:]:
        m = jnp.maximum(m, x)
    ex = [jnp.exp(x - m) for x in rows]
    z = ex[0]
    for x in ex[1:]:
        z = z + x
    p = [x / z for x in ex]

    gscore = []
    for g in range(N_GROUPS):
        a, b, c, d = p[4 * g:4 * g + 4]
        hi1, lo1 = jnp.maximum(a, b), jnp.minimum(a, b)
        hi2, lo2 = jnp.maximum(c, d), jnp.minimum(c, d)
        top1 = jnp.maximum(hi1, hi2)
        top2 = jnp.maximum(jnp.minimum(hi1, hi2), jnp.maximum(lo1, lo2))
        gscore.append(top1 + top2)
    best = []
    for g in range(N_GROUPS):
        ok = None
        for o in range(N_GROUPS):
            if o == g:
                continue
            c = (gscore[g] > gscore[o]) if o < g else (gscore[g] >= gscore[o])
            ok = c if ok is None else jnp.logical_and(ok, c)
        best.append(ok)
    won = []
    for e in range(N_EXPERTS):
        g = e // EXPERTS_PER_GROUP
        rank = jnp.zeros_like(p[e])
        for o in range(4 * g, 4 * g + 4):
            if o == e:
                continue
            ahead = (p[o] > p[e]) if o > e else (p[o] >= p[e])
            rank = rank + jnp.where(ahead, 1.0, 0.0)
        won.append(jnp.where(jnp.logical_and(best[g], rank < 1.5), 1.0, 0.0))
    tot = won[0] * p[0]
    for e in range(1, N_EXPERTS):
        tot = tot + won[e] * p[e]
    bucket = jnp.zeros_like(tot)
    gate_a = jnp.zeros_like(tot)
    gate_b = jnp.zeros_like(tot)
    for g in range(N_GROUPS):
        for pid, (a, b) in enumerate(_EXPERT_PAIRS):
            ind = won[4 * g + a] * won[4 * g + b]
            bucket = bucket + ind * float(len(_EXPERT_PAIRS) * g + pid)
            gate_a = gate_a + ind * p[4 * g + a]
            gate_b = gate_b + ind * p[4 * g + b]
    o_ref[0] = bucket
    o_ref[1] = gate_a / tot
    o_ref[2] = gate_b / tot


def _router(logits_t, router_b):
    e, n = logits_t.shape
    tr = math.gcd(2048, n)
    blk = (tr // LANES, LANES)
    routed = pl.pallas_call(
        _router_body,
        out_shape=jax.ShapeDtypeStruct((3, n // LANES, LANES), F32),
        grid=(n // tr,),
        in_specs=[pl.BlockSpec((e,) + blk, lambda i: (0, i, 0)), pl.BlockSpec((e, 1, 1), lambda i: (0, 0, 0))],
        out_specs=pl.BlockSpec((3,) + blk, lambda i: (0, i, 0)),
        compiler_params=_cparams(("arbitrary",)),
        name="router_gates",
    )(logits_t.reshape(e, n // LANES, LANES), router_b.reshape(e, 1, 1))
    return routed.reshape(3, n)


GATE_LANES = LANES


def _moe_plan(bucket, tg):
    n = bucket.shape[0]
    n_tiles = n // tg + N_BUCKETS
    ids = jnp.arange(N_BUCKETS, dtype=jnp.int32)
    onehot = (bucket[:, None] == ids[None, :]).astype(jnp.int32)
    csum = jnp.cumsum(onehot, axis=0)
    counts = csum[-1]
    tiles = (counts + tg - 1) // tg
    tile_end = jnp.cumsum(tiles)
    row_start = (tile_end - tiles) * tg
    pos = jnp.sum(onehot * (row_start[None, :] + csum - 1), axis=1).astype(jnp.int32)
    n_used = tile_end[-1]
    tile_ids = jnp.minimum(jnp.arange(n_tiles, dtype=jnp.int32), n_used - 1)
    tile_bucket = jnp.sum((tile_ids[:, None] >= tile_end[None, :]).astype(jnp.int32), axis=1)
    pair_a = jnp.asarray([a for a, _ in _EXPERT_PAIRS], jnp.int32)
    pair_b = jnp.asarray([b for _, b in _EXPERT_PAIRS], jnp.int32)
    grp, pid = tile_bucket // len(_EXPERT_PAIRS), tile_bucket % len(_EXPERT_PAIRS)
    ea = grp * EXPERTS_PER_GROUP + pair_a[pid]
    eb = grp * EXPERTS_PER_GROUP + pair_b[pid]
    spare = n_used + ids
    zero_tiles = jnp.concatenate([jnp.where(tiles > 0, tile_end - 1, -1),
                                  jnp.where(spare < n_tiles, spare, -1)]).astype(jnp.int32)
    return (pos, ea.astype(jnp.int32), eb.astype(jnp.int32), n_used.reshape(1).astype(jnp.int32),
            zero_tiles, n_tiles)


def _row_copy(src_ref, src_row, dst_ref, dst_row, sem):
    return pltpu.make_async_copy(src_ref.at[pl.ds(src_row, 1), :], dst_ref.at[pl.ds(dst_row, 1), :], sem)


ROW_DMA_UNROLL = 8


def _row_copies(n_rows, row_copy, whole_tile_copy):
    def issue(blk, carry):
        for u in range(ROW_DMA_UNROLL):
            row_copy(blk * ROW_DMA_UNROLL + u).start(priority=u % 2)
        return carry

    lax.fori_loop(0, n_rows // ROW_DMA_UNROLL, issue, 0)
    whole_tile_copy.wait()


def _dispatch_body(tg, pos_ref, zero_tiles_ref, h_ref, g_ref, xs_ref, aug_ref, sem):
    base = pl.program_id(0) * tg

    @pl.when(pl.program_id(0) == 0)
    def _():
        aug_ref[...] = jnp.zeros_like(aug_ref)

        def fill(k):
            row0 = pl.multiple_of(zero_tiles_ref[k] * tg, tg)
            return pltpu.make_async_copy(aug_ref, xs_ref.at[pl.ds(row0, tg), :], sem)

        for k in range(2 * N_BUCKETS):
            pl.when(zero_tiles_ref[k] >= 0)(lambda k=k: fill(k).start())
        for k in range(2 * N_BUCKETS):
            pl.when(zero_tiles_ref[k] >= 0)(lambda k=k: fill(k).wait())

    aug_ref[:, :D_MODEL] = h_ref[...]
    aug_ref[:, D_MODEL:] = g_ref[...]

    _row_copies(tg, lambda r: _row_copy(aug_ref, r, xs_ref, pos_ref[base + r], sem),
                pltpu.make_async_copy(aug_ref, xs_ref.at[pl.ds(0, tg), :], sem))


def _moe_dispatch(pos, zero_tiles, h2, gate_rows, n_tiles, tg):
    n, d = h2.shape
    wide = d + GATE_LANES
    grid_spec = pltpu.PrefetchScalarGridSpec(
        num_scalar_prefetch=2,
        grid=(n // tg,),
        in_specs=[pl.BlockSpec((tg, d), lambda i, pos_ref, zt_ref: (i, 0)),
                  pl.BlockSpec((tg, GATE_LANES), lambda i, pos_ref, zt_ref: (i, 0))],
        out_specs=pl.BlockSpec(memory_space=pl.ANY),
        scratch_shapes=[pltpu.VMEM((tg, wide), F32), pltpu.SemaphoreType.DMA(())],
    )
    return pl.pallas_call(
        functools.partial(_dispatch_body, tg),
        out_shape=jax.ShapeDtypeStruct((n_tiles * tg, wide), F32),
        grid_spec=grid_spec,
        compiler_params=_cparams(("arbitrary",)),
        name="moe_dispatch",
    )(pos, zero_tiles, h2, gate_rows)


def _grouped_body(ea_ref, eb_ref, nu_ref, xs_ref, w1a_ref, w3a_ref, w2a_ref, w1b_ref, w3b_ref, w2b_ref, y_ref):
    del ea_ref, eb_ref

    @pl.when(pl.program_id(0) < nu_ref[0])
    def _():
        x = xs_ref[:, :D_MODEL].astype(BF16)

        def expert(w1_ref, w3_ref, w2_ref, gate):
            h1 = jnp.dot(x, w1_ref[0], preferred_element_type=F32)
            h3 = jnp.dot(x, w3_ref[0], preferred_element_type=F32)
            hid = (h1 * _sigmoid(h1) * h3 * gate).astype(BF16)
            return jnp.dot(hid, w2_ref[0], preferred_element_type=F32)

        y_ref[...] = (expert(w1a_ref, w3a_ref, w2a_ref, xs_ref[:, D_MODEL:D_MODEL + 1])
                      + expert(w1b_ref, w3b_ref, w2b_ref, xs_ref[:, D_MODEL + 1:D_MODEL + 2]))

    @pl.when(pl.program_id(0) >= nu_ref[0])
    def _():
        y_ref[...] = jnp.zeros_like(y_ref)


def _moe_grouped(xs, ea, eb, n_used, w1, w3, w2, tg):
    rows, wide = xs.shape
    d, ff = D_MODEL, EXPERT_FF
    n_tiles = rows // tg
    tile = lambda i, ea_r, eb_r, nu_r: (jnp.minimum(i, nu_r[0] - 1), 0)
    up_a = pl.BlockSpec((1, d, ff), lambda i, ea_r, eb_r, nu_r: (ea_r[i], 0, 0))
    up_b = pl.BlockSpec((1, d, ff), lambda i, ea_r, eb_r, nu_r: (eb_r[i], 0, 0))
    dn_a = pl.BlockSpec((1, ff, d), lambda i, ea_r, eb_r, nu_r: (ea_r[i], 0, 0))
    dn_b = pl.BlockSpec((1, ff, d), lambda i, ea_r, eb_r, nu_r: (eb_r[i], 0, 0))
    grid_spec = pltpu.PrefetchScalarGridSpec(
        num_scalar_prefetch=3,
        grid=(n_tiles,),
        in_specs=[pl.BlockSpec((tg, wide), tile), up_a, up_a, dn_a, up_b, up_b, dn_b],
        out_specs=pl.BlockSpec((tg, d), lambda i, ea_r, eb_r, nu_r: (i, 0)),
    )
    return pl.pallas_call(
        _grouped_body,
        out_shape=jax.ShapeDtypeStruct((rows, d), F32),
        grid_spec=grid_spec,
        compiler_params=_cparams(("arbitrary",)),
        name="moe_grouped",
    )(ea, eb, n_used, xs, w1, w3, w2, w1, w3, w2)


def _combine_body(tg, pos_ref, ys_ref, x_ref, mod_ref, lg_ref, lb_ref, o_ref, buf_ref, sem):
    base = pl.program_id(0) * tg

    _row_copies(tg, lambda r: _row_copy(ys_ref, pos_ref[base + r], buf_ref, r, sem),
                pltpu.make_async_copy(ys_ref.at[pl.ds(0, tg), :], buf_ref, sem))
    m = mod_ref[0]
    z = DEEPNORM_ALPHA * x_ref[...] + m[5:6] * buf_ref[...]
    o_ref[...] = _layer_norm_rows(z, lg_ref[...], lb_ref[...])


def _moe_combine(lay, pos, ys, x, mod, ln_g, ln_b, row_off, tg):
    n, d = x.shape
    grid_spec = pltpu.PrefetchScalarGridSpec(
        num_scalar_prefetch=1,
        grid=(n // tg,),
        in_specs=[pl.BlockSpec(memory_space=pl.ANY),
                  pl.BlockSpec((tg, d), lambda i, pos_ref: (i, 0)),
                  pl.BlockSpec((1, 6, d), lambda i, pos_ref: (lay.mod_row(i, tg, row_off), 0, 0)),
                  pl.BlockSpec((1, d), lambda i, pos_ref: (0, 0)),
                  pl.BlockSpec((1, d), lambda i, pos_ref: (0, 0))],
        out_specs=pl.BlockSpec((tg, d), lambda i, pos_ref: (i, 0)),
        scratch_shapes=[pltpu.VMEM((tg, d), F32), pltpu.SemaphoreType.DMA(())],
    )
    return pl.pallas_call(
        functools.partial(_combine_body, tg),
        out_shape=jax.ShapeDtypeStruct((n, d), F32),
        grid_spec=grid_spec,
        compiler_params=_cparams(("arbitrary",)),
        name="moe_combine_ln",
    )(pos, ys, x, mod, ln_g.reshape(1, d), ln_b.reshape(1, d))


def _moe(lay, h2, routed, w1, w3, w2, x, mod, ln_g, ln_b, row_off):
    tg = lay.tm
    bucket = routed[0].astype(jnp.int32)
    gate_rows = jnp.pad(routed[1:3].T, ((0, 0), (0, GATE_LANES - 2)))
    pos, ea, eb, n_used, zero_tiles, n_tiles = _moe_plan(bucket, tg)
    xs = _moe_dispatch(pos, zero_tiles, h2, gate_rows, n_tiles, tg)
    ys = _moe_grouped(xs, ea, eb, n_used, w1, w3, w2, tg)
    return _moe_combine(lay, pos, ys, x, mod, ln_g, ln_b, row_off, tg)


def kernel(x, c, ctx, c_ctx, router_w, router_b, ada_w, ada_b, ln1_g, ln1_b, ln2_g, ln2_b, moe_w1, moe_w3, moe_w2, ev_w_in, ev_w_out, ev_a_mu, ev_a_w0, ev_a_w2, ev_a_a0, ev_a_a2, ev_a_g2, ev_a_kk, ev_a_ka, ev_a_rk, ev_a_lnx_g, ev_a_lnx_b, ev_b_lam, ev_b_subln_g, od_w_in, od_w_out, od_qn_g, od_kn_g):
    bsz, seq, d = x.shape
    ctx_len = ctx.shape[1]
    assert d == D_MODEL and ada_w.shape[0] == DEPTH and seq % GRID_W == 0
    lay = _Layout(bsz, ctx_len, seq)

    cvec = jnp.zeros((lay.mod_rows, d), F32).at[:bsz].set(c).at[bsz].set(c_ctx)
    mods = _ada_mods(cvec, ada_w, ada_b).reshape(DEPTH, lay.mod_rows, 6, d)

    xs = (ctx.reshape(lay.rows_c, d), x.reshape(lay.rows_l, d))
    router_wt = router_w.T

    cos_b, sin_b = _rope_tables(lay, B_HEAD_DIM, B_QK)
    cos_c, sin_c = _rope_tables(lay, C_HEAD_DIM, C_HEAD_DIM)

    for i in range(DEPTH):
        last = i == DEPTH - 1
        j = i // 2
        mod = mods[i]
        if i % 2 == 0:
            lambda_init = 0.8 - 0.6 * math.exp(-0.3 * i)
            x_ctx, x_lat = xs if isinstance(xs, tuple) else (xs[:lay.rows_c], xs[lay.rows_c:])
            pa, q, k, v = _even_in_proj(lay, x_ctx, x_lat, mod, ev_w_in[j].astype(BF16), cos_b, sin_b)
            decay, scan_ops, g_, bonus = _rwkv_features(
                lay, pa, ev_a_mu[j], ev_a_w0[j], ev_a_w2[j], ev_a_a0[j], ev_a_a2[j], ev_a_g2[j],
                ev_a_kk[j], ev_a_ka[j], ev_a_rk[j].reshape(-1))
            chain_decay, chain_ops = _to_chains(lay, decay[None]), _to_chains(lay, scan_ops)
            yd = _diff_attention(lay, q, k, v, ev_b_lam[j], ev_b_subln_g[j], lambda_init)
            y_f, y_b = _wkv_scan(lay, chain_decay, chain_ops)
            ya = _rwkv_readout(lay, _from_chains(lay, y_f, y_b), bonus, g_, ev_a_lnx_g[j], ev_a_lnx_b[j])
            mixes, w_out = [ya, yd], ev_w_out[j]
            n_rows, tile_off = (lay.rows_l, lay.ntiles_c) if last else (lay.rows, 0)
            if last:
                mixes = [mx[lay.rows_c:] for mx in mixes]
        else:
            xs = jnp.concatenate(xs, axis=0) if isinstance(xs, tuple) else xs
            q, k, v = _odd_in_proj(lay, xs, mod, od_w_in[j].astype(BF16), cos_c, sin_c, od_qn_g[j], od_kn_g[j])
            assert last, "an odd layer that must also update the context stream is not supported"
            o = _gqa_attention(lay, q, k, v)
            mixes, w_out = [o], od_w_out[j]
            n_rows, tile_off = lay.rows_l, lay.ntiles_c
        row_off = tile_off * lay.tm
        if isinstance(xs, tuple) and row_off:
            xs = jnp.concatenate(xs, axis=0)
        x_new, h2, logits_t = _out_proj(lay, mixes, w_out.astype(BF16), xs, row_off, mod,
                                        ln1_g[i], ln1_b[i], router_wt, n_rows)
        routed = _router(logits_t, router_b)
        xs = _moe(lay, h2, routed, moe_w1[i].astype(BF16), moe_w3[i].astype(BF16), moe_w2[i].astype(BF16),
                  x_new, mod, ln2_g[i], ln2_b[i], row_off)
    return xs.reshape(bsz, seq, d)
```
